```python
import jax, jax.numpy as jnp
from jax import lax
import numpy as np

D_MODEL = 1024
BATCH = 32
SEQ = 2048
DEPTH = 1

MEM_LEN = 256
GDN_HEADS = 8
GDN_DK = 128
GDN_DV = 128
GDN_CONV = 4
GDN_CHUNK = 64
GDN_QK_W = GDN_HEADS * GDN_DK
GDN_V_W = GDN_HEADS * GDN_DV
GDN_QKV_W = 2 * GDN_QK_W + GDN_V_W
SB_HEADS = 8
SB_DH = 128
SB_W = SB_HEADS * SB_DH
SB_BLOCK = 128
X_HEADS = 4
X_DH = D_MODEL // X_HEADS
D_FF = 2816
FFN_CONV = 3
EPS = 1e-6
REST_WIDTHS = (GDN_HEADS, GDN_HEADS, GDN_V_W, SB_W, SB_W, SB_W, D_MODEL, D_MODEL)
IN_W = GDN_QKV_W + sum(REST_WIDTHS)

kernel_name = "hybrid_gdn_stickbreak_memxattn_convffn"


def rmsnorm(x, g):
    xf = x.astype(jnp.float32)
    y = xf * lax.rsqrt(jnp.mean(xf * xf, axis=-1, keepdims=True) + EPS)
    return (y * g.astype(jnp.float32)).astype(x.dtype)


def l2norm(x):
    return x * lax.rsqrt(jnp.sum(x * x, axis=-1, keepdims=True) + EPS)


def causal_dwconv(x, w):
    k = w.shape[0]
    return lax.conv_general_dilated(
        x, w[:, None, :].astype(x.dtype), window_strides=(1,), padding=[(k - 1, 0)],
        dimension_numbers=("NWC", "WIO", "NWC"), feature_group_count=x.shape[-1])


def split_heads(t, h):
    return t.reshape(t.shape[:-1] + (h, t.shape[-1] // h))


def to_chunks(t):
    b, s, h = t.shape[:3]
    t = jnp.moveaxis(t, 2, 1)
    return t.reshape((b, h, s // GDN_CHUNK, GDN_CHUNK) + t.shape[3:])


def gated_deltanet(q, k, v, a, bgate, z, a_log, dt_bias, out_norm):
    b, s, h, _ = q.shape
    f32 = jnp.float32
    q = l2norm(q.astype(f32)) * (GDN_DK ** -0.5)
    k = l2norm(k.astype(f32))
    v = v.astype(f32)
    beta = jax.nn.sigmoid(bgate.astype(f32))
    g = -jnp.exp(a_log.astype(f32)) * jax.nn.softplus(a.astype(f32) + dt_bias.astype(f32))

    qc, kc, vc = to_chunks(q), to_chunks(k), to_chunks(v)
    betac, gc = to_chunks(beta), jnp.cumsum(to_chunks(g), axis=-1)

    idx = jnp.arange(GDN_CHUNK)
    incl = idx[:, None] >= idx[None, :]
    strict = idx[:, None] > idx[None, :]
    decay = jnp.exp(jnp.where(incl, gc[..., :, None] - gc[..., None, :], -jnp.inf))

    kk = jnp.einsum("bhncd,bhnmd->bhncm", kc, kc)
    m_strict = jnp.where(strict, betac[..., :, None] * kk * decay, 0.0)
    rhs = jnp.concatenate([vc * betac[..., None],
                           kc * (betac * jnp.exp(gc))[..., None]], axis=-1)
    sol = lax.linalg.triangular_solve(m_strict, rhs, left_side=True, lower=True,
                                      unit_diagonal=True)
    u, w = sol[..., :GDN_DV], sol[..., GDN_DV:]

    qk = jnp.einsum("bhncd,bhnmd->bhncm", qc, kc) * decay
    q_dec = qc * jnp.exp(gc)[..., None]
    k_dec = kc * jnp.exp(gc[..., -1:] - gc)[..., None]
    chunk_decay = jnp.exp(gc[..., -1])

    def step(state, inp):
        u_n, w_n, qk_n, qd_n, kd_n, cd_n = inp
        v_new = u_n - jnp.einsum("bhck,bhkv->bhcv", w_n, state)
        o_n = (jnp.einsum("bhck,bhkv->bhcv", qd_n, state)
               + jnp.einsum("bhcm,bhmv->bhcv", qk_n, v_new))
        state = (state * cd_n[..., None, None]
                 + jnp.einsum("bhck,bhcv->bhkv", kd_n, v_new))
        return state, o_n

    xs = tuple(jnp.moveaxis(t, 2, 0) for t in (u, w, qk, q_dec, k_dec, chunk_decay))
    state0 = jnp.zeros((b, h, GDN_DK, GDN_DV), f32)
    _, o = lax.scan(step, state0, xs)
    o = jnp.transpose(o, (1, 0, 3, 2, 4)).reshape(b, s, h, GDN_DV)
    o = rmsnorm(o, out_norm) * jax.nn.silu(z.astype(f32))
    return o.reshape(b, s, h * GDN_DV)


def stick_breaking_attention(q, k, v):
    b, s, h, d = q.shape
    q = q * (d ** -0.5)
    outs = []
    for blk in range(s // SB_BLOCK):
        t0 = blk * SB_BLOCK
        end = t0 + SB_BLOCK
        z = jnp.einsum("bthd,bshd->bhts", q[:, t0:end], k[:, :end]).astype(jnp.float32)
        t_idx = t0 + jnp.arange(SB_BLOCK)
        s_idx = jnp.arange(end)
        causal = s_idx[None, :] < t_idx[:, None]
        log_fail = jnp.where(causal, jax.nn.log_sigmoid(-z), 0.0)
        suffix = lax.cumsum(log_fail, axis=3, reverse=True)
        suffix_excl = jnp.pad(suffix[..., 1:], ((0, 0), (0, 0), (0, 0), (0, 1)))
        weights = jnp.where(causal, jnp.exp(jax.nn.log_sigmoid(z) + suffix_excl), 0.0)
        outs.append(jnp.einsum("bhts,bshd->bthd", weights.astype(v.dtype), v[:, :end]))
    return jnp.concatenate(outs, axis=1).reshape(b, s, h * d)


def hybrid_mixer(x, norm_mix, w_in, conv_gdn, a_log, dt_bias, gdn_out_norm,
                 w_proj_gdn, w_proj_sb, w_out):
    xn = rmsnorm(x, norm_mix)
    proj = xn @ w_in
    qkv = jax.nn.silu(causal_dwconv(proj[..., :GDN_QKV_W], conv_gdn))
    gq = split_heads(qkv[..., :GDN_QK_W], GDN_HEADS)
    gk = split_heads(qkv[..., GDN_QK_W:2 * GDN_QK_W], GDN_HEADS)
    gv = split_heads(qkv[..., 2 * GDN_QK_W:], GDN_HEADS)
    splits = tuple(int(i) for i in np.cumsum(REST_WIDTHS)[:-1])
    a, bg, z, sq, sk, sv, gate_a, gate_b = jnp.split(proj[..., GDN_QKV_W:], splits, axis=-1)
    o_a = gated_deltanet(gq, gk, gv, a, bg, split_heads(z, GDN_HEADS),
                         a_log, dt_bias, gdn_out_norm).astype(x.dtype)
    o_b = stick_breaking_attention(split_heads(sq, SB_HEADS), split_heads(sk, SB_HEADS),
                                   split_heads(sv, SB_HEADS))
    merged = (jax.nn.sigmoid(gate_a) * (o_a @ w_proj_gdn)
              + jax.nn.sigmoid(gate_b) * (o_b @ w_proj_sb))
    return merged @ w_out


def memory_cross_attention(h, mem, norm_x, norm_mem, w_xq, w_xkv, xq_norm, xk_norm, w_xo):
    b, s, _ = h.shape
    hn = rmsnorm(h, norm_x)
    mn = rmsnorm(mem, norm_mem)
    q = rmsnorm(split_heads(hn @ w_xq, X_HEADS), xq_norm)
    kv = mn @ w_xkv
    k = rmsnorm(split_heads(kv[..., :D_MODEL], X_HEADS), xk_norm)
    v = split_heads(kv[..., D_MODEL:], X_HEADS)
    scores = jnp.einsum("bshd,bmhd->bhsm", q, k).astype(jnp.float32) * (X_DH ** -0.5)
    p = jax.nn.softmax(scores, axis=-1)
    o = jnp.einsum("bhsm,bmhd->bshd", p.astype(v.dtype), v).reshape(b, s, D_MODEL)
    return o @ w_xo


def conv_gated_mlp(h, norm_ffn, w_up, conv_ffn, w_down):
    hn = rmsnorm(h, norm_ffn)
    u = causal_dwconv(hn @ w_up, conv_ffn)
    return (jax.nn.silu(u[..., :D_FF]) * u[..., D_FF:]) @ w_down


def _fwd_setup_inputs(seed: int = 0) -> dict:
    key = jax.random.key(seed)
    ks = jax.random.split(key, 24)
    L = DEPTH

    def dense(k, fan_in, fan_out):
        return jax.random.normal(k, (L, fan_in, fan_out), jnp.float32) * fan_in ** -0.5

    def gain(k, n):
        return 1.0 + 0.02 * jax.random.normal(k, (L, n), jnp.float32)

    dt = jnp.exp(jax.random.uniform(ks[5], (L, GDN_HEADS), jnp.float32,
                                    np.log(1e-3), np.log(1e-1)))
    return {
        "x": jax.random.normal(ks[0], (BATCH, SEQ, D_MODEL), jnp.float32),
        "mem": jax.random.normal(ks[1], (BATCH, MEM_LEN, D_MODEL), jnp.float32),
        "norm_mix": gain(ks[2], D_MODEL),
        "w_in": dense(ks[3], D_MODEL, IN_W),
        "conv_gdn": jax.random.normal(ks[4], (L, GDN_CONV, GDN_QKV_W), jnp.float32) * GDN_CONV ** -0.5,
        "a_log": jnp.log(jax.random.uniform(ks[6], (L, GDN_HEADS), jnp.float32, 1.0, 16.0)),
        "dt_bias": dt + jnp.log(-jnp.expm1(-dt)),
        "gdn_out_norm": gain(ks[7], GDN_DV),
        "w_proj_gdn": dense(ks[8], GDN_V_W, D_MODEL),
        "w_proj_sb": dense(ks[9], SB_W, D_MODEL),
        "w_out": dense(ks[10], D_MODEL, D_MODEL),
        "norm_x": gain(ks[11], D_MODEL),
        "norm_mem": gain(ks[12], D_MODEL),
        "w_xq": dense(ks[13], D_MODEL, D_MODEL),
        "w_xkv": dense(ks[14], D_MODEL, 2 * D_MODEL),
        "xq_norm": gain(ks[15], X_DH),
        "xk_norm": gain(ks[16], X_DH),
        "w_xo": dense(ks[17], D_MODEL, D_MODEL),
        "norm_ffn": gain(ks[18], D_MODEL),
        "w_up": dense(ks[19], D_MODEL, 2 * D_FF),
        "conv_ffn": jax.random.normal(ks[20], (L, FFN_CONV, 2 * D_FF), jnp.float32) * FFN_CONV ** -0.5,
        "w_down": dense(ks[21], D_FF, D_MODEL),
    }


def _fwd_reference(x, mem, norm_mix, w_in, conv_gdn, a_log, dt_bias, gdn_out_norm,
              w_proj_gdn, w_proj_sb, w_out, norm_x, norm_mem, w_xq, w_xkv,
              xq_norm, xk_norm, w_xo, norm_ffn, w_up, conv_ffn, w_down):
    h = x
    for l in range(DEPTH):
        h = h + hybrid_mixer(h, norm_mix[l], w_in[l], conv_gdn[l], a_log[l], dt_bias[l],
                             gdn_out_norm[l], w_proj_gdn[l], w_proj_sb[l], w_out[l])
        h = h + memory_cross_attention(h, mem, norm_x[l], norm_mem[l], w_xq[l], w_xkv[l],
                                       xq_norm[l], xk_norm[l], w_xo[l])
        h = h + conv_gated_mlp(h, norm_ffn[l], w_up[l], conv_ffn[l], w_down[l])
    return h


import jax as _jax
import jax.numpy as _jnp

TWIN_FORMAT = 'train_step'
FWD_PARAMS = ['x', 'mem', 'norm_mix', 'w_in', 'conv_gdn', 'a_log', 'dt_bias', 'gdn_out_norm', 'w_proj_gdn', 'w_proj_sb', 'w_out', 'norm_x', 'norm_mem', 'w_xq', 'w_xkv', 'xq_norm', 'xk_norm', 'w_xo', 'norm_ffn', 'w_up', 'conv_ffn', 'w_down']
TWIN_WEIGHTS = ['norm_mix', 'w_in', 'conv_gdn', 'a_log', 'dt_bias', 'gdn_out_norm', 'w_proj_gdn', 'w_proj_sb', 'w_out', 'norm_x', 'norm_mem', 'w_xq', 'w_xkv', 'xq_norm', 'xk_norm', 'w_xo', 'norm_ffn', 'w_up', 'conv_ffn', 'w_down']
TWIN_DIFF_INPUT = 'x'
TWIN_INPUTS = ['x', 'mem', 'norm_mix', 'w_in', 'conv_gdn', 'a_log', 'dt_bias', 'gdn_out_norm', 'w_proj_gdn', 'w_proj_sb', 'w_out', 'norm_x', 'norm_mem', 'w_xq', 'w_xkv', 'xq_norm', 'xk_norm', 'w_xo', 'norm_ffn', 'w_up', 'conv_ffn', 'w_down', 'loss_target', 'm_norm_mix', 'm_w_in', 'm_conv_gdn', 'm_a_log', 'm_dt_bias', 'm_gdn_out_norm', 'm_w_proj_gdn', 'm_w_proj_sb', 'm_w_out', 'm_norm_x', 'm_norm_mem', 'm_w_xq', 'm_w_xkv', 'm_xq_norm', 'm_xk_norm', 'm_w_xo', 'm_norm_ffn', 'm_w_up', 'm_conv_ffn', 'm_w_down', 'v_norm_mix', 'v_w_in', 'v_conv_gdn', 'v_a_log', 'v_dt_bias', 'v_gdn_out_norm', 'v_w_proj_gdn', 'v_w_proj_sb', 'v_w_out', 'v_norm_x', 'v_norm_mem', 'v_w_xq', 'v_w_xkv', 'v_xq_norm', 'v_xk_norm', 'v_w_xo', 'v_norm_ffn', 'v_w_up', 'v_conv_ffn', 'v_w_down']
TWIN_OUTPUTS = ['loss', 'grad_x', 'grad_norm_mix', 'grad_w_in', 'grad_conv_gdn', 'grad_a_log', 'grad_dt_bias', 'grad_gdn_out_norm', 'grad_w_proj_gdn', 'grad_w_proj_sb', 'grad_w_out', 'grad_norm_x', 'grad_norm_mem', 'grad_w_xq', 'grad_w_xkv', 'grad_xq_norm', 'grad_xk_norm', 'grad_w_xo', 'grad_norm_ffn', 'grad_w_up', 'grad_conv_ffn', 'grad_w_down', 'delta_norm_mix', 'delta_w_in', 'delta_conv_gdn', 'delta_a_log', 'delta_dt_bias', 'delta_gdn_out_norm', 'delta_w_proj_gdn', 'delta_w_proj_sb', 'delta_w_out', 'delta_norm_x', 'delta_norm_mem', 'delta_w_xq', 'delta_w_xkv', 'delta_xq_norm', 'delta_xk_norm', 'delta_w_xo', 'delta_norm_ffn', 'delta_w_up', 'delta_conv_ffn', 'delta_w_down', 'new_m_norm_mix', 'new_m_w_in', 'new_m_conv_gdn', 'new_m_a_log', 'new_m_dt_bias', 'new_m_gdn_out_norm', 'new_m_w_proj_gdn', 'new_m_w_proj_sb', 'new_m_w_out', 'new_m_norm_x', 'new_m_norm_mem', 'new_m_w_xq', 'new_m_w_xkv', 'new_m_xq_norm', 'new_m_xk_norm', 'new_m_w_xo', 'new_m_norm_ffn', 'new_m_w_up', 'new_m_conv_ffn', 'new_m_w_down', 'new_v_norm_mix', 'new_v_w_in', 'new_v_conv_gdn', 'new_v_a_log', 'new_v_dt_bias', 'new_v_gdn_out_norm', 'new_v_w_proj_gdn', 'new_v_w_proj_sb', 'new_v_w_out', 'new_v_norm_x', 'new_v_norm_mem', 'new_v_w_xq', 'new_v_w_xkv', 'new_v_xq_norm', 'new_v_xk_norm', 'new_v_w_xo', 'new_v_norm_ffn', 'new_v_w_up', 'new_v_conv_ffn', 'new_v_w_down']
TWIN_LEAF_KINDS = {'loss': 'loss', 'grad_x': 'grad_x', 'grad_norm_mix': 'grad_w', 'grad_w_in': 'grad_w', 'grad_conv_gdn': 'grad_w', 'grad_a_log': 'grad_w', 'grad_dt_bias': 'grad_w', 'grad_gdn_out_norm': 'grad_w', 'grad_w_proj_gdn': 'grad_w', 'grad_w_proj_sb': 'grad_w', 'grad_w_out': 'grad_w', 'grad_norm_x': 'grad_w', 'grad_norm_mem': 'grad_w', 'grad_w_xq': 'grad_w', 'grad_w_xkv': 'grad_w', 'grad_xq_norm': 'grad_w', 'grad_xk_norm': 'grad_w', 'grad_w_xo': 'grad_w', 'grad_norm_ffn': 'grad_w', 'grad_w_up': 'grad_w', 'grad_conv_ffn': 'grad_w', 'grad_w_down': 'grad_w', 'delta_norm_mix': 'delta_w', 'delta_w_in': 'delta_w', 'delta_conv_gdn': 'delta_w', 'delta_a_log': 'delta_w', 'delta_dt_bias': 'delta_w', 'delta_gdn_out_norm': 'delta_w', 'delta_w_proj_gdn': 'delta_w', 'delta_w_proj_sb': 'delta_w', 'delta_w_out': 'delta_w', 'delta_norm_x': 'delta_w', 'delta_norm_mem': 'delta_w', 'delta_w_xq': 'delta_w', 'delta_w_xkv': 'delta_w', 'delta_xq_norm': 'delta_w', 'delta_xk_norm': 'delta_w', 'delta_w_xo': 'delta_w', 'delta_norm_ffn': 'delta_w', 'delta_w_up': 'delta_w', 'delta_conv_ffn': 'delta_w', 'delta_w_down': 'delta_w', 'new_m_norm_mix': 'new_m', 'new_m_w_in': 'new_m', 'new_m_conv_gdn': 'new_m', 'new_m_a_log': 'new_m', 'new_m_dt_bias': 'new_m', 'new_m_gdn_out_norm': 'new_m', 'new_m_w_proj_gdn': 'new_m', 'new_m_w_proj_sb': 'new_m', 'new_m_w_out': 'new_m', 'new_m_norm_x': 'new_m', 'new_m_norm_mem': 'new_m', 'new_m_w_xq': 'new_m', 'new_m_w_xkv': 'new_m', 'new_m_xq_norm': 'new_m', 'new_m_xk_norm': 'new_m', 'new_m_w_xo': 'new_m', 'new_m_norm_ffn': 'new_m', 'new_m_w_up': 'new_m', 'new_m_conv_ffn': 'new_m', 'new_m_w_down': 'new_m', 'new_v_norm_mix': 'new_v', 'new_v_w_in': 'new_v', 'new_v_conv_gdn': 'new_v', 'new_v_a_log': 'new_v', 'new_v_dt_bias': 'new_v', 'new_v_gdn_out_norm': 'new_v', 'new_v_w_proj_gdn': 'new_v', 'new_v_w_proj_sb': 'new_v', 'new_v_w_out': 'new_v', 'new_v_norm_x': 'new_v', 'new_v_norm_mem': 'new_v', 'new_v_w_xq': 'new_v', 'new_v_w_xkv': 'new_v', 'new_v_xq_norm': 'new_v', 'new_v_xk_norm': 'new_v', 'new_v_w_xo': 'new_v', 'new_v_norm_ffn': 'new_v', 'new_v_w_up': 'new_v', 'new_v_conv_ffn': 'new_v', 'new_v_w_down': 'new_v'}


def _forward(args):
    return _fwd_reference(*[args[k] for k in FWD_PARAMS])


def _output_shape():
    out = _jax.eval_shape(lambda: _forward(_fwd_setup_inputs(0)))
    return out.shape, out.dtype

N_MICROBATCH = 1
ADAM_LR = 0.001
ADAM_B1 = 0.9
ADAM_B2 = 0.999
ADAM_EPS = 1e-08
ADAM_WD = 0.01
ADAM_STEP = 10
PER_EXAMPLE_BATCH_AXIS = {'x': 0, 'mem': 0, 'loss_target': 0}
SHARED_INPUTS = []
_WEIGHT_DTYPES = {'norm_mix': _jnp.float32, 'w_in': _jnp.float32, 'conv_gdn': _jnp.float32, 'a_log': _jnp.float32, 'dt_bias': _jnp.float32, 'gdn_out_norm': _jnp.float32, 'w_proj_gdn': _jnp.float32, 'w_proj_sb': _jnp.float32, 'w_out': _jnp.float32, 'norm_x': _jnp.float32, 'norm_mem': _jnp.float32, 'w_xq': _jnp.float32, 'w_xkv': _jnp.float32, 'xq_norm': _jnp.float32, 'xk_norm': _jnp.float32, 'w_xo': _jnp.float32, 'norm_ffn': _jnp.float32, 'w_up': _jnp.float32, 'conv_ffn': _jnp.float32, 'w_down': _jnp.float32}
MOMENT_SCALE = {'norm_mix': 1.898409e+01, 'w_in': 1.886764e-01, 'conv_gdn': 3.162818e-01, 'a_log': 3.233392e+01, 'dt_bias': 3.097433e+01, 'gdn_out_norm': 5.100397e+01, 'w_proj_gdn': 6.626490e-01, 'w_proj_sb': 2.954042e-01, 'w_out': 7.071432e-01, 'norm_x': 8.130400e-02, 'norm_mem': 6.914453e-01, 'w_xq': 8.336395e-02, 'w_xkv': 2.673158e-01, 'xq_norm': 2.404610e+00, 'xk_norm': 2.404974e+00, 'w_xo': 3.996265e-01, 'norm_ffn': 5.028474e+01, 'w_up': 4.093907e-01, 'conv_ffn': 6.811144e+00, 'w_down': 4.706473e-01}


def _to_microbatches(a, axis):
    t = _jnp.moveaxis(a, axis, 0)
    t = t.reshape((N_MICROBATCH, t.shape[0] // N_MICROBATCH) + t.shape[1:])
    return _jnp.moveaxis(t, 1, axis + 1)


def setup_inputs(seed: int = 0) -> dict:
    inp = _fwd_setup_inputs(seed)
    key = _jax.random.fold_in(_jax.random.key(seed), 7919)
    shape, _ = _output_shape()
    out = dict(inp)
    out["loss_target"] = _jax.random.normal(_jax.random.fold_in(key, 0), shape, _jnp.float32)
    for i, name in enumerate(TWIN_WEIGHTS):
        w = inp[name].astype(_jnp.float32)
        if MOMENT_SCALE is None:
            s = _jnp.sqrt(_jnp.mean(_jnp.square(w)) + 1e-30)
        else:
            s = MOMENT_SCALE[name]
        km, kv = _jax.random.split(_jax.random.fold_in(key, i + 1))
        out[name] = w
        out["m_" + name] = s * _jax.random.normal(km, w.shape, _jnp.float32)
        out["v_" + name] = (s * s) * _jax.random.uniform(kv, w.shape, _jnp.float32, 0.5, 1.5)
    if N_MICROBATCH > 1:
        for name, axis in PER_EXAMPLE_BATCH_AXIS.items():
            out[name] = _to_microbatches(out[name], axis)
    return {'x': out['x'], 'mem': out['mem'], 'norm_mix': out['norm_mix'], 'w_in': out['w_in'], 'conv_gdn': out['conv_gdn'], 'a_log': out['a_log'], 'dt_bias': out['dt_bias'], 'gdn_out_norm': out['gdn_out_norm'], 'w_proj_gdn': out['w_proj_gdn'], 'w_proj_sb': out['w_proj_sb'], 'w_out': out['w_out'], 'norm_x': out['norm_x'], 'norm_mem': out['norm_mem'], 'w_xq': out['w_xq'], 'w_xkv': out['w_xkv'], 'xq_norm': out['xq_norm'], 'xk_norm': out['xk_norm'], 'w_xo': out['w_xo'], 'norm_ffn': out['norm_ffn'], 'w_up': out['w_up'], 'conv_ffn': out['conv_ffn'], 'w_down': out['w_down'], 'loss_target': out['loss_target'], 'm_norm_mix': out['m_norm_mix'], 'm_w_in': out['m_w_in'], 'm_conv_gdn': out['m_conv_gdn'], 'm_a_log': out['m_a_log'], 'm_dt_bias': out['m_dt_bias'], 'm_gdn_out_norm': out['m_gdn_out_norm'], 'm_w_proj_gdn': out['m_w_proj_gdn'], 'm_w_proj_sb': out['m_w_proj_sb'], 'm_w_out': out['m_w_out'], 'm_norm_x': out['m_norm_x'], 'm_norm_mem': out['m_norm_mem'], 'm_w_xq': out['m_w_xq'], 'm_w_xkv': out['m_w_xkv'], 'm_xq_norm': out['m_xq_norm'], 'm_xk_norm': out['m_xk_norm'], 'm_w_xo': out['m_w_xo'], 'm_norm_ffn': out['m_norm_ffn'], 'm_w_up': out['m_w_up'], 'm_conv_ffn': out['m_conv_ffn'], 'm_w_down': out['m_w_down'], 'v_norm_mix': out['v_norm_mix'], 'v_w_in': out['v_w_in'], 'v_conv_gdn': out['v_conv_gdn'], 'v_a_log': out['v_a_log'], 'v_dt_bias': out['v_dt_bias'], 'v_gdn_out_norm': out['v_gdn_out_norm'], 'v_w_proj_gdn': out['v_w_proj_gdn'], 'v_w_proj_sb': out['v_w_proj_sb'], 'v_w_out': out['v_w_out'], 'v_norm_x': out['v_norm_x'], 'v_norm_mem': out['v_norm_mem'], 'v_w_xq': out['v_w_xq'], 'v_w_xkv': out['v_w_xkv'], 'v_xq_norm': out['v_xq_norm'], 'v_xk_norm': out['v_xk_norm'], 'v_w_xo': out['v_w_xo'], 'v_norm_ffn': out['v_norm_ffn'], 'v_w_up': out['v_w_up'], 'v_conv_ffn': out['v_conv_ffn'], 'v_w_down': out['v_w_down']}


def _loss(weights, diff, rest, loss_target):
    with _jax.named_scope("forward"):
        args = {**rest, TWIN_DIFF_INPUT: diff, **{k: w.astype(_WEIGHT_DTYPES[k]) for k, w in weights.items()}}
        y = _forward(args)
    with _jax.named_scope("loss_head"):
        err = _jnp.square(y.astype(_jnp.float32) - loss_target)
        return 0.5 * _jnp.sum(_jnp.mean(err, axis=-1)) if err.ndim else 0.5 * err


def _adamw(w, g, m, v):
    m = ADAM_B1 * m + (1.0 - ADAM_B1) * g
    v = ADAM_B2 * v + (1.0 - ADAM_B2) * _jnp.square(g)
    m_hat = m / (1.0 - ADAM_B1 ** ADAM_STEP)
    v_hat = v / (1.0 - ADAM_B2 ** ADAM_STEP)
    delta = -ADAM_LR * (m_hat / (_jnp.sqrt(v_hat) + ADAM_EPS) + ADAM_WD * w)
    return delta, m, v


def reference(x, mem, norm_mix, w_in, conv_gdn, a_log, dt_bias, gdn_out_norm, w_proj_gdn, w_proj_sb, w_out, norm_x, norm_mem, w_xq, w_xkv, xq_norm, xk_norm, w_xo, norm_ffn, w_up, conv_ffn, w_down, loss_target, m_norm_mix, m_w_in, m_conv_gdn, m_a_log, m_dt_bias, m_gdn_out_norm, m_w_proj_gdn, m_w_proj_sb, m_w_out, m_norm_x, m_norm_mem, m_w_xq, m_w_xkv, m_xq_norm, m_xk_norm, m_w_xo, m_norm_ffn, m_w_up, m_conv_ffn, m_w_down, v_norm_mix, v_w_in, v_conv_gdn, v_a_log, v_dt_bias, v_gdn_out_norm, v_w_proj_gdn, v_w_proj_sb, v_w_out, v_norm_x, v_norm_mem, v_w_xq, v_w_xkv, v_xq_norm, v_xk_norm, v_w_xo, v_norm_ffn, v_w_up, v_conv_ffn, v_w_down):
    given = dict(x=x, mem=mem, norm_mix=norm_mix, w_in=w_in, conv_gdn=conv_gdn, a_log=a_log, dt_bias=dt_bias, gdn_out_norm=gdn_out_norm, w_proj_gdn=w_proj_gdn, w_proj_sb=w_proj_sb, w_out=w_out, norm_x=norm_x, norm_mem=norm_mem, w_xq=w_xq, w_xkv=w_xkv, xq_norm=xq_norm, xk_norm=xk_norm, w_xo=w_xo, norm_ffn=norm_ffn, w_up=w_up, conv_ffn=conv_ffn, w_down=w_down, loss_target=loss_target, m_norm_mix=m_norm_mix, m_w_in=m_w_in, m_conv_gdn=m_conv_gdn, m_a_log=m_a_log, m_dt_bias=m_dt_bias, m_gdn_out_norm=m_gdn_out_norm, m_w_proj_gdn=m_w_proj_gdn, m_w_proj_sb=m_w_proj_sb, m_w_out=m_w_out, m_norm_x=m_norm_x, m_norm_mem=m_norm_mem, m_w_xq=m_w_xq, m_w_xkv=m_w_xkv, m_xq_norm=m_xq_norm, m_xk_norm=m_xk_norm, m_w_xo=m_w_xo, m_norm_ffn=m_norm_ffn, m_w_up=m_w_up, m_conv_ffn=m_conv_ffn, m_w_down=m_w_down, v_norm_mix=v_norm_mix, v_w_in=v_w_in, v_conv_gdn=v_conv_gdn, v_a_log=v_a_log, v_dt_bias=v_dt_bias, v_gdn_out_norm=v_gdn_out_norm, v_w_proj_gdn=v_w_proj_gdn, v_w_proj_sb=v_w_proj_sb, v_w_out=v_w_out, v_norm_x=v_norm_x, v_norm_mem=v_norm_mem, v_w_xq=v_w_xq, v_w_xkv=v_w_xkv, v_xq_norm=v_xq_norm, v_xk_norm=v_xk_norm, v_w_xo=v_w_xo, v_norm_ffn=v_norm_ffn, v_w_up=v_w_up, v_conv_ffn=v_conv_ffn, v_w_down=v_w_down)
    weights = {n: given[n] for n in TWIN_WEIGHTS}
    shared = {n: given[n] for n in SHARED_INPUTS}
    per_example = {n: given[n] for n in ['x', 'mem']}
    grad_fn = _jax.value_and_grad(_loss, argnums=(0, 1))

    def one_microbatch(ex, loss_target):
        ex = dict(ex)
        diff = ex.pop(TWIN_DIFF_INPUT)
        return grad_fn(weights, diff, {**shared, **ex}, loss_target)

    if N_MICROBATCH == 1:
        loss, (grad_w, grad_x) = one_microbatch(per_example, given["loss_target"])
    else:
        def body(carry, xs):
            loss_sum, grad_sum = carry
            l_k, (gw_k, gx_k) = one_microbatch(xs[0], xs[1])
            with _jax.named_scope("update"):
                return (loss_sum + l_k, _jax.tree.map(_jnp.add, grad_sum, gw_k)), gx_k

        init = (_jnp.zeros((), _jnp.float32), _jax.tree.map(_jnp.zeros_like, weights))
        (loss, grad_w), grad_x = _jax.lax.scan(body, init, (per_example, given["loss_target"]))
    with _jax.named_scope("update"):
        delta_w, new_m, new_v = {}, {}, {}
        for n in TWIN_WEIGHTS:
            delta_w[n], new_m[n], new_v[n] = _adamw(weights[n], grad_w[n], given["m_" + n], given["v_" + n])
    return (loss, grad_x, *[grad_w[n] for n in TWIN_WEIGHTS], *[delta_w[n] for n in TWIN_WEIGHTS],
            *[new_m[n] for n in TWIN_WEIGHTS], *[new_v[n] for n in TWIN_WEIGHTS])
```

```python
import functools

import jax
import jax.numpy as jnp
from jax import lax
from jax.experimental import pallas as pl
from jax.experimental.pallas import tpu as pltpu

F32 = jnp.float32
BF16 = jnp.bfloat16
HIGHEST = lax.Precision.HIGHEST

LANE = 128
SUBLANE = 8
VMEM_LIMIT = 56 * 2 ** 20
N_DEV = 8
MESH = pl.DeviceIdType.MESH

EPS = 1e-6
ADAM_LR = 0.001
ADAM_B1 = 0.9
ADAM_B2 = 0.999
ADAM_EPS = 1e-08
ADAM_WD = 0.01
ADAM_STEP = 10


class _Cfg:
    d = 1024
    b = 4
    s = 2048
    mem = 256
    gh = 8
    gch = 64
    sbh = 8
    xh = 4
    dff = 2816
    pack_tile = 240


CFG = _Cfg()
HD = 128
SB_BLK = 128
CONV_CB = 256
XQ_TILE = 256


def _tile(n, prefs):
    for t in prefs:
        if n % t == 0:
            return t
    raise ValueError(f"no tile for {n}")


def _cp(sem, **kw):
    return pltpu.CompilerParams(dimension_semantics=sem, vmem_limit_bytes=VMEM_LIMIT, **kw)


def _dims(kind):
    return {"nn": (((1,), (0,)), ((), ())), "nt": (((1,), (1,)), ((), ())), "tn": (((0,), (0,)), ((), ()))}[kind]


def _raw_bdot(a, b, kind):
    return lax.dot_general(a.astype(BF16), b.astype(BF16), _dims(kind), preferred_element_type=F32)


def _raw_fdot(a, b, kind):
    return lax.dot_general(a.astype(F32), b.astype(F32), _dims(kind), precision=HIGHEST, preferred_element_type=F32)


def _make_dot(raw):
    @functools.partial(jax.custom_vjp, nondiff_argnums=(2,))
    def dot(a, b, kind):
        return raw(a, b, kind)

    def fwd(a, b, kind):
        return raw(a, b, kind), (a, b)

    def bwd(kind, res, g):
        a, b = res
        if kind == "nn":
            return raw(g, b, "nt").astype(a.dtype), raw(a, g, "tn").astype(b.dtype)
        if kind == "nt":
            return raw(g, b, "nn").astype(a.dtype), raw(g, a, "tn").astype(b.dtype)
        return raw(b, g, "nt").astype(a.dtype), raw(a, g, "nn").astype(b.dtype)

    dot.defvjp(fwd, bwd)
    return dot


_bdot = _make_dot(_raw_bdot)
_fdot = _make_dot(_raw_fdot)


def _split_dot(x, m01):
    hi = x.astype(BF16)
    lo = (x - hi.astype(F32)).astype(BF16)
    return (lax.dot_general(hi, m01, _dims("nn"), preferred_element_type=F32)
            + lax.dot_general(lo, m01, _dims("nn"), preferred_element_type=F32))


_sigmoid = jax.nn.sigmoid


def _silu(x):
    return x * _sigmoid(x)


def _softplus(x):
    return jnp.maximum(x, 0.0) + jnp.log1p(jnp.exp(-jnp.abs(x)))


def _rms(x, g):
    return x * lax.rsqrt(jnp.mean(x * x, axis=-1, keepdims=True) + EPS) * g


def _iota2(shape, dim):
    return lax.broadcasted_iota(jnp.int32, shape, dim)


def _mm(a, b, *, ta=False, tb=False, add=None, out_dtype=F32, name):
    if ta:
        kd, m = a.shape
    else:
        m, kd = a.shape
    if tb:
        n, kb = b.shape
    else:
        kb, n = b.shape
    assert kd == kb, (a.shape, b.shape, ta, tb)
    tm = _tile(m, (1024, 512, 256, 128))
    tn = _tile(n, (1024, 512, 256, 128))
    tk = _tile(kd, (512, 256, 128))
    nk = kd // tk
    kind_dims = (((0 if ta else 1,), (1 if tb else 0,)), ((), ()))

    def body(*refs):
        if add is None:
            a_ref, b_ref, o_ref, acc = refs
        else:
            a_ref, b_ref, add_ref, o_ref, acc = refs
        k = pl.program_id(2)

        @pl.when(k == 0)
        def _():
            acc[...] = jnp.zeros_like(acc)

        acc[...] += lax.dot_general(a_ref[...].astype(BF16), b_ref[...].astype(BF16), kind_dims,
                                    preferred_element_type=F32)

        @pl.when(k == nk - 1)
        def _():
            r = acc[...]
            if add is not None:
                r = r + add_ref[...].astype(F32)
            o_ref[...] = r.astype(o_ref.dtype)

    a_spec = pl.BlockSpec((tk, tm), lambda i, j, k: (k, i)) if ta else pl.BlockSpec((tm, tk), lambda i, j, k: (i, k))
    b_spec = pl.BlockSpec((tn, tk), lambda i, j, k: (j, k)) if tb else pl.BlockSpec((tk, tn), lambda i, j, k: (k, j))
    in_specs = [a_spec, b_spec]
    args = [a, b]
    if add is not None:
        in_specs.append(pl.BlockSpec((tm, tn), lambda i, j, k: (i, j)))
        args.append(add)
    return pl.pallas_call(
        body, name=name, grid=(m // tm, n // tn, nk),
        in_specs=in_specs, out_specs=pl.BlockSpec((tm, tn), lambda i, j, k: (i, j)),
        out_shape=jax.ShapeDtypeStruct((m, n), out_dtype),
        scratch_shapes=[pltpu.VMEM((tm, tn), F32)],
        compiler_params=_cp(("parallel", "parallel", "arbitrary")),
    )(*args)


def _rowwise(fn, rows, pars, out_rows, out_accs, *, name, tm=None):
    t = rows[0].shape[0]
    if tm is None:
        tm = _tile(t, (256, 128, 64, 32, 16))
    assert t % tm == 0, (t, tm)
    n_r, n_p, n_or, n_oa = len(rows), len(pars), len(out_rows), len(out_accs)

    def body(*refs):
        r_in = refs[:n_r]
        p_in = refs[n_r:n_r + n_p]
        o_r = refs[n_r + n_p:n_r + n_p + n_or]
        o_a = refs[n_r + n_p + n_or:]
        outs = fn(*[r[...] for r in r_in], *[p[...] for p in p_in])
        if not isinstance(outs, (tuple, list)):
            outs = (outs,)
        assert len(outs) == n_or + n_oa, (name, len(outs))
        for ref, val in zip(o_r, outs[:n_or]):
            ref[...] = val.astype(ref.dtype)
        if n_oa:
            @pl.when(pl.program_id(0) == 0)
            def _():
                for ref in o_a:
                    ref[...] = jnp.zeros_like(ref)

            for ref, val in zip(o_a, outs[n_or:]):
                ref[...] += val.astype(F32)

    in_specs = [pl.BlockSpec((tm, r.shape[1]), lambda i: (i, 0)) for r in rows]
    in_specs += [pl.BlockSpec(p.shape, lambda i: (0, 0)) for p in pars]
    out_specs = [pl.BlockSpec((tm, c), lambda i: (i, 0)) for c, _ in out_rows]
    out_specs += [pl.BlockSpec(s, lambda i: (0, 0)) for s in out_accs]
    out_shape = [jax.ShapeDtypeStruct((t, c), dt) for c, dt in out_rows]
    out_shape += [jax.ShapeDtypeStruct(s, F32) for s in out_accs]
    return pl.pallas_call(
        body, name=name, grid=(t // tm,), in_specs=in_specs, out_specs=out_specs, out_shape=out_shape,
        compiler_params=_cp(("arbitrary",)),
    )(*rows, *pars)


def _shift_down(x, sh):
    rolled = pltpu.roll(x, sh, 0)
    return jnp.where(_iota2(x.shape, 0) >= sh, rolled, 0.0)


def _shift_up(x, sh):
    s = x.shape[0]
    rolled = pltpu.roll(x, s - sh, 0)
    return jnp.where(_iota2(x.shape, 0) < s - sh, rolled, 0.0)


def _conv(x, w):
    k = w.shape[0]
    y = x * w[k - 1:k, :]
    for i in range(k - 1):
        y = y + _shift_down(x, k - 1 - i) * w[i:i + 1, :]
    return y


def _conv_bwd(x, w, dy):
    k = w.shape[0]
    dx = dy * w[k - 1:k, :]
    dws = []
    for i in range(k - 1):
        dx = dx + _shift_up(dy, k - 1 - i) * w[i:i + 1, :]
        dws.append(jnp.sum(dy * _shift_down(x, k - 1 - i), axis=0, keepdims=True))
    dws.append(jnp.sum(dy * x, axis=0, keepdims=True))
    return dx, dws


def _gdn_post(y, j, nqb):
    a = _silu(y)
    sc = jnp.where(j < nqb, HD ** -0.5, 1.0).astype(F32)
    outs = []
    for h in range(y.shape[1] // HD):
        ah = a[:, h * HD:(h + 1) * HD]
        l2 = ah * lax.rsqrt(jnp.sum(ah * ah, axis=-1, keepdims=True) + EPS)
        outs.append(jnp.where(j < 2 * nqb, l2 * sc, ah))
    return jnp.concatenate(outs, axis=1) if len(outs) > 1 else outs[0]


def _gdn_conv_fwd(x, w):
    bsz, s, c3 = x.shape
    k = w.shape[0]
    nb = c3 // CONV_CB
    nqb = nb // 3

    def body(x_ref, w_ref, o_ref):
        j = pl.program_id(1)
        o_ref[0] = _gdn_post(_conv(x_ref[0], w_ref[...]), j, nqb)

    return pl.pallas_call(
        body, name="gdn_conv_fwd", grid=(bsz, nb),
        in_specs=[pl.BlockSpec((1, s, CONV_CB), lambda b, j: (b, 0, j)), pl.BlockSpec((k, CONV_CB), lambda b, j: (0, j))],
        out_specs=pl.BlockSpec((1, s, CONV_CB), lambda b, j: (b, 0, j)),
        out_shape=jax.ShapeDtypeStruct(x.shape, F32),
        compiler_params=_cp(("parallel", "parallel")),
    )(x, w)


def _gdn_conv_bwd(x, w, dout):
    bsz, s, c3 = x.shape
    k = w.shape[0]
    nb = c3 // CONV_CB
    nqb = nb // 3

    def body(x_ref, w_ref, d_ref, dx_ref, dw_ref):
        j = pl.program_id(0)
        b = pl.program_id(1)
        xv, wv = x_ref[0], w_ref[...]
        y = _conv(xv, wv)
        _, f = jax.vjp(lambda yy: _gdn_post(yy, j, nqb), y)
        (dy,) = f(d_ref[0])
        dx, dws = _conv_bwd(xv, wv, dy)
        dx_ref[0] = dx.astype(dx_ref.dtype)

        @pl.when(b == 0)
        def _():
            dw_ref[...] = jnp.zeros_like(dw_ref)

        for i in range(k):
            dw_ref[i:i + 1, :] += dws[i]

    return pl.pallas_call(
        body, name="gdn_conv_bwd", grid=(nb, bsz),
        in_specs=[pl.BlockSpec((1, s, CONV_CB), lambda j, b: (b, 0, j)), pl.BlockSpec((k, CONV_CB), lambda j, b: (0, j)),
                  pl.BlockSpec((1, s, CONV_CB), lambda j, b: (b, 0, j))],
        out_specs=[pl.BlockSpec((1, s, CONV_CB), lambda j, b: (b, 0, j)), pl.BlockSpec((k, CONV_CB), lambda j, b: (0, j))],
        out_shape=[jax.ShapeDtypeStruct(x.shape, BF16), jax.ShapeDtypeStruct(w.shape, F32)],
        compiler_params=_cp(("parallel", "arbitrary")),
    )(x, w, dout)


def _ffn_conv_fwd(up, w):
    bsz, s, c2 = up.shape
    k = w.shape[0]
    nb = (c2 // 2) // CONV_CB

    def body(x1_ref, x2_ref, w1_ref, w2_ref, o_ref):
        u1 = _conv(x1_ref[0], w1_ref[...])
        u2 = _conv(x2_ref[0], w2_ref[...])
        o_ref[0] = (_silu(u1) * u2).astype(o_ref.dtype)

    return pl.pallas_call(
        body, name="ffn_conv_fwd", grid=(bsz, nb),
        in_specs=[pl.BlockSpec((1, s, CONV_CB), lambda b, j: (b, 0, j)), pl.BlockSpec((1, s, CONV_CB), lambda b, j: (b, 0, j + nb)),
                  pl.BlockSpec((k, CONV_CB), lambda b, j: (0, j)), pl.BlockSpec((k, CONV_CB), lambda b, j: (0, j + nb))],
        out_specs=pl.BlockSpec((1, s, CONV_CB), lambda b, j: (b, 0, j)),
        out_shape=jax.ShapeDtypeStruct((bsz, s, c2 // 2), BF16),
        compiler_params=_cp(("parallel", "parallel")),
    )(up, up, w, w)


def _ffn_conv_bwd(up, w, dact):
    bsz, s, c2 = up.shape
    k = w.shape[0]
    half = c2 // 2
    nb = half // CONV_CB

    def body(x1_ref, x2_ref, w1_ref, w2_ref, d_ref, dx1_ref, dx2_ref, dw1_ref, dw2_ref):
        b = pl.program_id(1)
        x1, x2, w1, w2 = x1_ref[0], x2_ref[0], w1_ref[...], w2_ref[...]
        u1 = _conv(x1, w1)
        u2 = _conv(x2, w2)
        _, f = jax.vjp(lambda p, q: _silu(p) * q, u1, u2)
        du1, du2 = f(d_ref[0])
        dx1, dws1 = _conv_bwd(x1, w1, du1)
        dx2, dws2 = _conv_bwd(x2, w2, du2)
        dx1_ref[0] = dx1.astype(dx1_ref.dtype)
        dx2_ref[0] = dx2.astype(dx2_ref.dtype)

        @pl.when(b == 0)
        def _():
            dw1_ref[...] = jnp.zeros_like(dw1_ref)
            dw2_ref[...] = jnp.zeros_like(dw2_ref)

        for i in range(k):
            dw1_ref[i:i + 1, :] += dws1[i]
            dw2_ref[i:i + 1, :] += dws2[i]

    def blk(off):
        return pl.BlockSpec((1, s, CONV_CB), lambda j, b: (b, 0, j + off))

    def wblk(off):
        return pl.BlockSpec((k, CONV_CB), lambda j, b: (0, j + off))

    return pl.pallas_call(
        body, name="ffn_conv_bwd", grid=(nb, bsz),
        in_specs=[blk(0), blk(nb), wblk(0), wblk(nb), blk(0)],
        out_specs=[blk(0), blk(0), wblk(0), wblk(0)],
        out_shape=[jax.ShapeDtypeStruct((bsz, s, half), BF16), jax.ShapeDtypeStruct((bsz, s, half), BF16),
                   jax.ShapeDtypeStruct((k, half), F32), jax.ShapeDtypeStruct((k, half), F32)],
        compiler_params=_cp(("parallel", "arbitrary")),
    )(up, up, w, w, dact)


@jax.custom_vjp
def _inv_unit_lower(a):
    c = a.shape[0]
    eye = (_iota2((c, c), 0) == _iota2((c, c), 1)).astype(F32)
    p = -a
    t = eye + p
    n = 2
    while n < c:
        p = _raw_fdot(p, p, "nn")
        t = t + _raw_fdot(t, p, "nn")
        n *= 2
    return t


def _inv_fwd(a):
    t = _inv_unit_lower(a)
    return t, t


def _inv_bwd(t, g):
    return (-_raw_fdot(t, _raw_fdot(g, t, "nt"), "tn"),)


_inv_unit_lower.defvjp(_inv_fwd, _inv_bwd)


def _gdn_chunk(q, k, v, z, g_row, beta_row, state, onorm):
    c = q.shape[0]
    ii, jj = _iota2((c, c), 0), _iota2((c, c), 1)
    incl, strict, eye = ii >= jj, ii > jj, ii == jj
    g_b = jnp.broadcast_to(g_row, (c, c))
    gc_col = jnp.sum(jnp.where(incl, g_b, 0.0), axis=1, keepdims=True)
    gc_row = jnp.sum(jnp.where(eye, jnp.broadcast_to(gc_col, (c, c)), 0.0), axis=0, keepdims=True)
    beta_col = jnp.sum(jnp.where(eye, jnp.broadcast_to(beta_row, (c, c)), 0.0), axis=1, keepdims=True)
    gc_last = jnp.sum(g_row, axis=1, keepdims=True)
    decay = jnp.where(incl, jnp.exp(jnp.where(incl, gc_col - gc_row, 0.0)), 0.0)
    kk = _bdot(k, k, "nt")
    a = jnp.where(strict, beta_col * kk * decay, 0.0)
    tinv = _inv_unit_lower(a)
    u = _fdot(tinv, v * beta_col, "nn")
    w = _fdot(tinv, k * (beta_col * jnp.exp(gc_col)), "nn")
    qk = _bdot(q, k, "nt") * decay
    q_dec = q * jnp.exp(gc_col)
    k_dec = k * jnp.exp(gc_last - gc_col)
    v_new = u - _bdot(w, state, "nn")
    o = _bdot(q_dec, state, "nn") + _bdot(qk, v_new, "nn")
    new_state = state * jnp.exp(gc_last) + _bdot(k_dec, v_new, "tn")
    y = _rms(o, onorm) * _silu(z)
    return y, new_state


def _gdn_specs(s, c, reverse):
    n = s // c
    nn = (lambda i: n - 1 - i) if reverse else (lambda i: i)

    def qkv(off):
        return pl.BlockSpec((1, c, HD), lambda b, h, i: (b, nn(i), h + off))

    def gate(off):
        return pl.BlockSpec((1, 1, 1, 1, c), lambda b, h, i: (b, h + off, nn(i), 0, 0))

    st = pl.BlockSpec((1, 1, 1, HD, HD), lambda b, h, i: (b, h, nn(i), 0, 0))
    onorm = pl.BlockSpec((1, HD), lambda b, h, i: (0, 0))
    return n, qkv, gate, st, onorm


def _gdn_fwd(qkv, z, gbt, onorm):
    bsz, s, _ = qkv.shape
    gh, c = CFG.gh, CFG.gch
    n, qs, gs, st, on = _gdn_specs(s, c, False)

    def body(q_ref, k_ref, v_ref, z_ref, g_ref, b_ref, on_ref, y_ref, st_ref, state):
        @pl.when(pl.program_id(2) == 0)
        def _():
            state[...] = jnp.zeros_like(state)

        s_in = state[...]
        st_ref[0, 0, 0] = s_in
        y, s_out = _gdn_chunk(q_ref[0], k_ref[0], v_ref[0], z_ref[0], g_ref[0, 0, 0], b_ref[0, 0, 0], s_in, on_ref[...])
        y_ref[0] = y
        state[...] = s_out

    return pl.pallas_call(
        body, name="gdn_fwd", grid=(bsz, gh, n),
        in_specs=[qs(0), qs(gh), qs(2 * gh), qs(0), gs(0), gs(gh), on],
        out_specs=[qs(0), st],
        out_shape=[jax.ShapeDtypeStruct((bsz, s, gh * HD), F32), jax.ShapeDtypeStruct((bsz, gh, n, HD, HD), F32)],
        scratch_shapes=[pltpu.VMEM((HD, HD), F32)],
        compiler_params=_cp(("parallel", "parallel", "arbitrary")),
    )(qkv, qkv, qkv, z, gbt, gbt, onorm)


def _gdn_bwd(qkv, z, gbt, onorm, states, dy):
    bsz, s, _ = qkv.shape
    gh, c = CFG.gh, CFG.gch
    n, qs, gs, st, on = _gdn_specs(s, c, True)

    def body(q_ref, k_ref, v_ref, z_ref, g_ref, b_ref, on_ref, st_ref, dy_ref,
             dq_ref, dk_ref, dv_ref, dz_ref, dg_ref, db_ref, don_ref, dstate):
        first = (pl.program_id(0) == 0) & (pl.program_id(1) == 0) & (pl.program_id(2) == 0)

        @pl.when(first)
        def _():
            don_ref[...] = jnp.zeros_like(don_ref)

        @pl.when(pl.program_id(2) == 0)
        def _():
            dstate[...] = jnp.zeros_like(dstate)

        _, f = jax.vjp(_gdn_chunk, q_ref[0], k_ref[0], v_ref[0], z_ref[0], g_ref[0, 0, 0], b_ref[0, 0, 0],
                       st_ref[0, 0, 0], on_ref[...])
        dq, dk, dv, dz, dg, db, ds, don = f((dy_ref[0], dstate[...]))
        dq_ref[0] = dq
        dk_ref[0] = dk
        dv_ref[0] = dv
        dz_ref[0] = dz
        dg_ref[0, 0, 0] = dg
        db_ref[0, 0, 0] = db
        don_ref[...] += don
        dstate[...] = ds

    act = jax.ShapeDtypeStruct((bsz, s, gh * HD), F32)
    gshape = jax.ShapeDtypeStruct((bsz, gh, n, 1, c), F32)
    return pl.pallas_call(
        body, name="gdn_bwd", grid=(bsz, gh, n),
        in_specs=[qs(0), qs(gh), qs(2 * gh), qs(0), gs(0), gs(gh), on, st, qs(0)],
        out_specs=[qs(0), qs(0), qs(0), qs(0), gs(0), gs(0), on],
        out_shape=[act, act, act, act, gshape, gshape, jax.ShapeDtypeStruct((1, HD), F32)],
        scratch_shapes=[pltpu.VMEM((HD, HD), F32)],
        compiler_params=_cp(("arbitrary", "arbitrary", "arbitrary")),
    )(qkv, qkv, qkv, z, gbt, gbt, onorm, states, dy)


def _gates_fn(ab, alog, dtb):
    lane = _iota2(ab.shape, 1)
    g = -jnp.exp(alog) * _softplus(ab + dtb)
    beta = _sigmoid(ab)
    return jnp.where(lane < CFG.gh, g, jnp.where(lane < 2 * CFG.gh, beta, 0.0))


def _sb_weights(qs, kj, mask, run, tri_su):
    z = lax.dot_general(qs, kj, _dims("nt"), preferred_element_type=F32)
    l1p = jnp.log1p(jnp.exp(-jnp.abs(z)))
    ls = jnp.minimum(z, 0.0) - l1p
    lf = jnp.where(mask, jnp.minimum(-z, 0.0) - l1p, 0.0)
    sfx = _split_dot(lf, tri_su) + run
    w = jnp.where(mask, jnp.exp(ls + sfx), 0.0)
    return z, lf, w


def _sb_specs(s):
    def qb(off):
        return pl.BlockSpec((1, SB_BLK, HD), lambda b, h, i: (b, i, h + off))

    def full(off):
        return pl.BlockSpec((1, s, HD), lambda b, h, i: (b, 0, h + off))

    return qb, full


def _sb_fwd(qkv):
    bsz, s, _ = qkv.shape
    sbh = CFG.sbh
    scale = HD ** -0.5
    qb, full = _sb_specs(s)

    def body(q_ref, k_ref, v_ref, o_ref):
        i = pl.program_id(2)
        qs = (q_ref[0] * scale).astype(BF16)
        r, c = _iota2((SB_BLK, SB_BLK), 0), _iota2((SB_BLK, SB_BLK), 1)
        tri_su = (r > c).astype(BF16)

        def step(jj, carry):
            acc, run = carry
            j = i - jj
            off = pl.multiple_of(j * SB_BLK, SB_BLK)
            kj = k_ref[0, pl.ds(off, SB_BLK), :].astype(BF16)
            vj = v_ref[0, pl.ds(off, SB_BLK), :].astype(BF16)
            mask = (j < i) | (c < r)
            _, lf, w = _sb_weights(qs, kj, mask, run, tri_su)
            acc = acc + lax.dot_general(w.astype(BF16), vj, _dims("nn"), preferred_element_type=F32)
            return acc, run + jnp.sum(lf, axis=1, keepdims=True)

        acc, _ = lax.fori_loop(0, i + 1, step, (jnp.zeros((SB_BLK, HD), F32), jnp.zeros((SB_BLK, 1), F32)))
        o_ref[0] = acc

    return pl.pallas_call(
        body, name="sb_fwd", grid=(bsz, sbh, s // SB_BLK),
        in_specs=[qb(0), full(sbh), full(2 * sbh)], out_specs=qb(0),
        out_shape=jax.ShapeDtypeStruct((bsz, s, sbh * HD), F32),
        compiler_params=_cp(("parallel", "parallel", "arbitrary")),
    )(qkv, qkv, qkv)


def _sb_bwd(qkv, do):
    bsz, s, _ = qkv.shape
    sbh = CFG.sbh
    nblk = s // SB_BLK
    scale = HD ** -0.5
    qb, full = _sb_specs(s)

    def body(q_ref, k_ref, v_ref, do_ref, dq_ref, dk_ref, dv_ref, dk_acc, dv_acc, dl_pan, z_pan):
        i = pl.program_id(2)

        @pl.when(i == 0)
        def _():
            dk_acc[...] = jnp.zeros_like(dk_acc)
            dv_acc[...] = jnp.zeros_like(dv_acc)

        qs = (q_ref[0] * scale).astype(BF16)
        dob = do_ref[0].astype(BF16)
        r, c = _iota2((SB_BLK, SB_BLK), 0), _iota2((SB_BLK, SB_BLK), 1)
        tri_su = (r > c).astype(BF16)
        tri_pre = (r < c).astype(BF16)

        def step_a(jj, run):
            j = i - jj
            off = pl.multiple_of(j * SB_BLK, SB_BLK)
            kj = k_ref[0, pl.ds(off, SB_BLK), :].astype(BF16)
            vj = v_ref[0, pl.ds(off, SB_BLK), :].astype(BF16)
            mask = (j < i) | (c < r)
            z, lf, w = _sb_weights(qs, kj, mask, run, tri_su)
            dw = lax.dot_general(dob, vj, _dims("nt"), preferred_element_type=F32)
            dl_pan[j] = dw * w
            z_pan[j] = z
            dv_acc[pl.ds(off, SB_BLK), :] += lax.dot_general(w.astype(BF16), dob, _dims("tn"), preferred_element_type=F32)
            return run + jnp.sum(lf, axis=1, keepdims=True)

        lax.fori_loop(0, i + 1, step_a, jnp.zeros((SB_BLK, 1), F32))

        def step_b(j, carry):
            dq_acc, pre = carry
            off = pl.multiple_of(j * SB_BLK, SB_BLK)
            kj = k_ref[0, pl.ds(off, SB_BLK), :].astype(BF16)
            mask = (j < i) | (c < r)
            dl = dl_pan[j]
            sg = _sigmoid(z_pan[j])
            pfx = _split_dot(dl, tri_pre) + pre
            dz = jnp.where(mask, dl * (1.0 - sg) - sg * pfx, 0.0).astype(BF16)
            dq_acc = dq_acc + lax.dot_general(dz, kj, _dims("nn"), preferred_element_type=F32)
            dk_acc[pl.ds(off, SB_BLK), :] += lax.dot_general(dz, qs, _dims("tn"), preferred_element_type=F32)
            return dq_acc, pre + jnp.sum(dl, axis=1, keepdims=True)

        dq_acc, _ = lax.fori_loop(0, i + 1, step_b, (jnp.zeros((SB_BLK, HD), F32), jnp.zeros((SB_BLK, 1), F32)))
        dq_ref[0] = (dq_acc * scale).astype(dq_ref.dtype)

        @pl.when(i == nblk - 1)
        def _():
            dk_ref[0] = dk_acc[...].astype(dk_ref.dtype)
            dv_ref[0] = dv_acc[...].astype(dv_ref.dtype)

    out = jax.ShapeDtypeStruct((bsz, s, sbh * HD), BF16)
    return pl.pallas_call(
        body, name="sb_bwd", grid=(bsz, sbh, nblk),
        in_specs=[qb(0), full(sbh), full(2 * sbh), qb(0)],
        out_specs=[qb(0), full(0), full(0)],
        out_shape=[out, out, out],
        scratch_shapes=[pltpu.VMEM((s, HD), F32), pltpu.VMEM((s, HD), F32),
                        pltpu.VMEM((nblk, SB_BLK, SB_BLK), F32), pltpu.VMEM((nblk, SB_BLK, SB_BLK), F32)],
        compiler_params=_cp(("parallel", "parallel", "arbitrary")),
    )(qkv, qkv, qkv, do)


def _xattn_fn(q_raw, kv, qn, kn):
    d = q_raw.shape[1]
    dh = d // CFG.xh
    outs = []
    for h in range(CFG.xh):
        qh = _rms(q_raw[:, h * dh:(h + 1) * dh], qn)
        kh = _rms(kv[:, h * dh:(h + 1) * dh], kn)
        vh = kv[:, d + h * dh:d + (h + 1) * dh]
        sc = _bdot(qh, kh, "nt") * (dh ** -0.5)
        sc = sc - lax.stop_gradient(jnp.max(sc, axis=-1, keepdims=True))
        e = jnp.exp(sc)
        p = e / jnp.sum(e, axis=-1, keepdims=True)
        outs.append(_bdot(p, vh, "nn"))
    return jnp.concatenate(outs, axis=1)


def _xattn_fwd(q_raw, kv, qn, kn):
    bsz, s, d = q_raw.shape
    m = kv.shape[1]
    tq = _tile(s, (XQ_TILE, 128))

    def body(q_ref, kv_ref, qn_ref, kn_ref, o_ref):
        o_ref[0] = _xattn_fn(q_ref[0], kv_ref[0], qn_ref[...], kn_ref[...]).astype(o_ref.dtype)

    return pl.pallas_call(
        body, name="xattn_fwd", grid=(bsz, s // tq),
        in_specs=[pl.BlockSpec((1, tq, d), lambda b, i: (b, i, 0)), pl.BlockSpec((1, m, 2 * d), lambda b, i: (b, 0, 0)),
                  pl.BlockSpec(qn.shape, lambda b, i: (0, 0)), pl.BlockSpec(kn.shape, lambda b, i: (0, 0))],
        out_specs=pl.BlockSpec((1, tq, d), lambda b, i: (b, i, 0)),
        out_shape=jax.ShapeDtypeStruct((bsz, s, d), BF16),
        compiler_params=_cp(("parallel", "parallel")),
    )(q_raw, kv, qn, kn)


def _xattn_bwd(q_raw, kv, qn, kn, do):
    bsz, s, d = q_raw.shape
    m = kv.shape[1]
    tq = _tile(s, (XQ_TILE, 128))

    def body(q_ref, kv_ref, qn_ref, kn_ref, do_ref, dq_ref, dkv_ref, dqn_ref, dkn_ref):
        b, i = pl.program_id(0), pl.program_id(1)

        @pl.when((b == 0) & (i == 0))
        def _():
            dqn_ref[...] = jnp.zeros_like(dqn_ref)
            dkn_ref[...] = jnp.zeros_like(dkn_ref)

        @pl.when(i == 0)
        def _():
            dkv_ref[...] = jnp.zeros_like(dkv_ref)

        _, f = jax.vjp(_xattn_fn, q_ref[0], kv_ref[0], qn_ref[...], kn_ref[...])
        dq, dkv, dqn, dkn = f(do_ref[0].astype(F32))
        dq_ref[0] = dq.astype(dq_ref.dtype)
        dkv_ref[0] += dkv
        dqn_ref[...] += dqn
        dkn_ref[...] += dkn

    return pl.pallas_call(
        body, name="xattn_bwd", grid=(bsz, s // tq),
        in_specs=[pl.BlockSpec((1, tq, d), lambda b, i: (b, i, 0)), pl.BlockSpec((1, m, 2 * d), lambda b, i: (b, 0, 0)),
                  pl.BlockSpec(qn.shape, lambda b, i: (0, 0)), pl.BlockSpec(kn.shape, lambda b, i: (0, 0)),
                  pl.BlockSpec((1, tq, d), lambda b, i: (b, i, 0))],
        out_specs=[pl.BlockSpec((1, tq, d), lambda b, i: (b, i, 0)), pl.BlockSpec((1, m, 2 * d), lambda b, i: (b, 0, 0)),
                   pl.BlockSpec(qn.shape, lambda b, i: (0, 0)), pl.BlockSpec(kn.shape, lambda b, i: (0, 0))],
        out_shape=[jax.ShapeDtypeStruct((bsz, s, d), BF16), jax.ShapeDtypeStruct(kv.shape, F32),
                   jax.ShapeDtypeStruct(qn.shape, F32), jax.ShapeDtypeStruct(kn.shape, F32)],
        compiler_params=_cp(("arbitrary", "arbitrary")),
    )(q_raw, kv, qn, kn, do)


def _my_pos():
    return lax.axis_index("x"), lax.axis_index("y"), lax.axis_index("c")


def _all_gather_big(shard):
    r, d = shard.shape

    def body(x_ref, out_ref, send_sems, recv_sems, local_sem):
        x, y, c = _my_pos()
        me, sibling = (x, y, c), (x, y, 1 - c)
        chips = [(1 - x, y), (x, 1 - y), (1 - x, 1 - y)]

        def slot(px, py, pc):
            return out_ref.at[4 * px + 2 * py + pc]

        def copy(k, block, to, src=None):
            return pltpu.make_async_remote_copy(
                src_ref=slot(*block) if src is None else src, dst_ref=slot(*block),
                send_sem=send_sems.at[k], recv_sem=recv_sems.at[k], device_id=to, device_id_type=MESH)

        mine = pltpu.make_async_copy(x_ref, slot(*me), local_sem)
        mine.start()
        first = [copy(0, me, sibling, src=x_ref)]
        first += [copy(1 + j, me, (*chip, c), src=x_ref) for j, chip in enumerate(chips)]
        for cp in first:
            cp.start()
        passed = [copy(4 + j, (*chip, c), sibling) for j, chip in enumerate(chips)]
        for j, chip in enumerate(chips):
            copy(1 + j, (*chip, c), me).wait_recv()
            passed[j].start()
        copy(0, sibling, me).wait_recv()
        for j, chip in enumerate(chips):
            copy(4 + j, (*chip, 1 - c), me).wait_recv()
        for cp in first + passed:
            cp.wait_send()
        mine.wait()

    return pl.pallas_call(
        body, name="all_gather_weights",
        out_shape=jax.ShapeDtypeStruct((N_DEV, r, d), shard.dtype),
        in_specs=[pl.BlockSpec(memory_space=pl.ANY)], out_specs=pl.BlockSpec(memory_space=pl.ANY),
        scratch_shapes=[pltpu.SemaphoreType.DMA((7,)), pltpu.SemaphoreType.DMA((7,)), pltpu.SemaphoreType.DMA],
    )(shard)


def _exchange_sibling(g):
    _, r, d = g.shape

    def body(g_ref, out_ref, send_sems, recv_sems):
        x, y, c = _my_pos()
        copies = [pltpu.make_async_remote_copy(
            src_ref=g_ref.at[2 * k + (1 - c)], dst_ref=out_ref.at[k],
            send_sem=send_sems.at[k], recv_sem=recv_sems.at[k], device_id=(x, y, 1 - c), device_id_type=MESH)
            for k in range(4)]
        for cp in copies:
            cp.start()
        for cp in copies:
            cp.wait_recv()
        for cp in copies:
            cp.wait_send()

    return pl.pallas_call(
        body, name="grads_to_sibling",
        out_shape=jax.ShapeDtypeStruct((4, r, d), g.dtype),
        in_specs=[pl.BlockSpec(memory_space=pl.ANY)], out_specs=pl.BlockSpec(memory_space=pl.ANY),
        scratch_shapes=[pltpu.SemaphoreType.DMA((4,)), pltpu.SemaphoreType.DMA((4,))],
    )(g)


def _exchange_chips(s1):
    _, r, d = s1.shape

    def body(s_ref, out_ref, send_sems, recv_sems):
        x, y, c = _my_pos()
        copies = []
        for rel in (1, 2, 3):
            px = jnp.bitwise_xor(x, rel >> 1)
            py = jnp.bitwise_xor(y, rel & 1)
            copies.append(pltpu.make_async_remote_copy(
                src_ref=s_ref.at[2 * px + py], dst_ref=out_ref.at[rel - 1],
                send_sem=send_sems.at[rel - 1], recv_sem=recv_sems.at[rel - 1],
                device_id=(px, py, c), device_id_type=MESH))
        for cp in copies:
            cp.start()
        for cp in copies:
            cp.wait_recv()
        for cp in copies:
            cp.wait_send()

    return pl.pallas_call(
        body, name="grads_to_chips",
        out_shape=jax.ShapeDtypeStruct((3, r, d), s1.dtype),
        in_specs=[pl.BlockSpec(memory_space=pl.ANY)], out_specs=pl.BlockSpec(memory_space=pl.ANY),
        scratch_shapes=[pltpu.SemaphoreType.DMA((3,)), pltpu.SemaphoreType.DMA((3,))],
    )(s1)


def _all_reduce_small(blk, name):
    rows, d = blk.shape

    def body(x_ref, out_ref, land, send_sems, recv_sems):
        x, y, c = _my_pos()
        me = 4 * x + 2 * y + c
        copies = []
        for rel in range(1, N_DEV):
            peer = (jnp.bitwise_xor(x, rel >> 2), jnp.bitwise_xor(y, (rel >> 1) & 1), jnp.bitwise_xor(c, rel & 1))
            copies.append(pltpu.make_async_remote_copy(
                src_ref=x_ref, dst_ref=land.at[rel - 1], send_sem=send_sems.at[rel - 1], recv_sem=recv_sems.at[rel - 1],
                device_id=peer, device_id_type=MESH))
        for cp in copies:
            cp.start()
        for cp in copies:
            cp.wait_recv()
        acc = jnp.zeros((rows, d), F32)
        for dev in range(N_DEV):
            rel = jnp.bitwise_xor(me, dev)
            got = land[jnp.maximum(rel - 1, 0)]
            acc = acc + jnp.where(rel == 0, x_ref[...], got)
        out_ref[...] = acc
        for cp in copies:
            cp.wait_send()

    return pl.pallas_call(
        body, name=name,
        out_shape=jax.ShapeDtypeStruct((rows, d), F32),
        in_specs=[pl.BlockSpec(memory_space=pltpu.VMEM)], out_specs=pl.BlockSpec(memory_space=pltpu.VMEM),
        scratch_shapes=[pltpu.VMEM((N_DEV - 1, rows, d), F32), pltpu.SemaphoreType.DMA((N_DEV - 1,)),
                        pltpu.SemaphoreType.DMA((N_DEV - 1,))],
    )(blk)


def _cast_rows(x, dtype, name):
    return _rowwise(lambda v: v, [x], [], [(x.shape[1], dtype)], [], name=name, tm=CFG.pack_tile)[0]


def _sum_sibling(g, recv1, c_idx):
    _, r, d = g.shape
    tm = CFG.pack_tile

    def body(c_ref, g_ref, r_ref, o_ref):
        o_ref[0] = (g_ref[0] + r_ref[0]).astype(o_ref.dtype)

    grid_spec = pltpu.PrefetchScalarGridSpec(
        num_scalar_prefetch=1, grid=(4, r // tm),
        in_specs=[pl.BlockSpec((1, tm, d), lambda k, i, c_ref: (2 * k + c_ref[0], i, 0)),
                  pl.BlockSpec((1, tm, d), lambda k, i, c_ref: (k, i, 0))],
        out_specs=pl.BlockSpec((1, tm, d), lambda k, i, c_ref: (k, i, 0)))
    return pl.pallas_call(
        body, name="sum_sibling", grid_spec=grid_spec,
        out_shape=jax.ShapeDtypeStruct((4, r, d), BF16),
        compiler_params=_cp(("parallel", "parallel")),
    )(c_idx, g, recv1)


def _adamw_math(w, g, m, v):
    m2 = ADAM_B1 * m + (1.0 - ADAM_B1) * g
    v2 = ADAM_B2 * v + (1.0 - ADAM_B2) * (g * g)
    m_hat = m2 / (1.0 - ADAM_B1 ** ADAM_STEP)
    v_hat = v2 / (1.0 - ADAM_B2 ** ADAM_STEP)
    delta = -ADAM_LR * (m_hat / (jnp.sqrt(v_hat) + ADAM_EPS) + ADAM_WD * w)
    return delta, m2, v2


def _adamw_big(g, recv1, recv2, w, m, v, idx):
    _, r, d = g.shape
    tm = CFG.pack_tile

    def body(idx_ref, g_ref, r1_ref, ra_ref, rb_ref, rc_ref, w_ref, m_ref, v_ref, og, od, om, ov):
        grad = (g_ref[0] + r1_ref[0]) + ra_ref[0].astype(F32) + rb_ref[0].astype(F32) + rc_ref[0].astype(F32)
        delta, m2, v2 = _adamw_math(w_ref[...], grad, m_ref[...], v_ref[...])
        og[...] = grad
        od[...] = delta
        om[...] = m2
        ov[...] = v2

    flat = pl.BlockSpec((tm, d), lambda i, idx_ref: (i, 0))
    grid_spec = pltpu.PrefetchScalarGridSpec(
        num_scalar_prefetch=1, grid=(r // tm,),
        in_specs=[pl.BlockSpec((1, tm, d), lambda i, idx_ref: (idx_ref[0], i, 0)),
                  pl.BlockSpec((1, tm, d), lambda i, idx_ref: (idx_ref[1], i, 0)),
                  pl.BlockSpec((1, tm, d), lambda i, idx_ref: (0, i, 0)),
                  pl.BlockSpec((1, tm, d), lambda i, idx_ref: (1, i, 0)),
                  pl.BlockSpec((1, tm, d), lambda i, idx_ref: (2, i, 0)),
                  flat, flat, flat],
        out_specs=[flat, flat, flat, flat])
    shp = jax.ShapeDtypeStruct((r, d), F32)
    return pl.pallas_call(
        body, name="adamw_big", grid_spec=grid_spec, out_shape=[shp, shp, shp, shp],
        compiler_params=_cp(("parallel",)),
    )(idx, g, recv1, recv2, recv2, recv2, w, m, v)


def _rows_of(v, d):
    flat = v.reshape(-1)
    pad = (-flat.shape[0]) % d
    if pad:
        flat = jnp.concatenate([flat, jnp.zeros((pad,), flat.dtype)])
    return flat.reshape(-1, d)


def _pad_rows(a, mult):
    pad = (-a.shape[0]) % mult
    if pad:
        a = jnp.concatenate([a, jnp.zeros((pad,) + a.shape[1:], a.dtype)], axis=0)
    return a


_BIG = ("w_in", "w_xkv", "w_up", "w_proj_gdn", "w_proj_sb", "w_out", "w_xq", "w_xo", "w_down")
_COL_SHARDED = ("w_in", "w_xkv", "w_up")
_SMALL_REP = ("norm_mix", "norm_x", "norm_mem", "norm_ffn", "a_log", "dt_bias", "gdn_out_norm", "xq_norm", "xk_norm")
_SMALL_CONV = ("conv_gdn", "conv_ffn")


def _pack_big_shards(shards):
    d = CFG.d
    return _pad_rows(jnp.concatenate([shards[n].reshape(-1, d) for n in _BIG], axis=0), CFG.pack_tile)


def _big_row_counts(shapes):
    d = CFG.d
    return [shapes[n][0] * shapes[n][1] // d for n in _BIG]


def _unpack_gathered(gath, shapes):
    out, r0 = {}, 0
    for n, cnt in zip(_BIG, _big_row_counts(shapes)):
        rows, cols = shapes[n]
        part = gath[:, r0:r0 + cnt, :]
        if n in _COL_SHARDED:
            out[n] = part.reshape(N_DEV, rows, cols).transpose(1, 0, 2).reshape(rows, N_DEV * cols)
        else:
            out[n] = part.reshape(N_DEV * rows, cols)
        r0 += cnt
    return out


def _pack_full_grads(grads, shapes):
    d = CFG.d
    parts = []
    for n in _BIG:
        rows, cols = shapes[n]
        g = grads[n]
        if n in _COL_SHARDED:
            g = g.reshape(rows, N_DEV, cols).transpose(1, 0, 2)
        parts.append(g.reshape(N_DEV, rows * cols // d, d))
    full = jnp.concatenate(parts, axis=1)
    pad = (-full.shape[1]) % CFG.pack_tile
    if pad:
        full = jnp.concatenate([full, jnp.zeros((N_DEV, pad, d), full.dtype)], axis=1)
    return full


def _unpack_shard(packed, shapes):
    out, r0 = {}, 0
    for n, cnt in zip(_BIG, _big_row_counts(shapes)):
        out[n] = packed[r0:r0 + cnt].reshape((1,) + tuple(shapes[n]))
        r0 += cnt
    return out


def kernel(x, mem, norm_mix, w_in, conv_gdn, a_log, dt_bias, gdn_out_norm, w_proj_gdn, w_proj_sb, w_out, norm_x, norm_mem, w_xq, w_xkv, xq_norm, xk_norm, w_xo, norm_ffn, w_up, conv_ffn, w_down, loss_target, m_norm_mix, m_w_in, m_conv_gdn, m_a_log, m_dt_bias, m_gdn_out_norm, m_w_proj_gdn, m_w_proj_sb, m_w_out, m_norm_x, m_norm_mem, m_w_xq, m_w_xkv, m_xq_norm, m_xk_norm, m_w_xo, m_norm_ffn, m_w_up, m_conv_ffn, m_w_down, v_norm_mix, v_w_in, v_conv_gdn, v_a_log, v_dt_bias, v_gdn_out_norm, v_w_proj_gdn, v_w_proj_sb, v_w_out, v_norm_x, v_norm_mem, v_w_xq, v_w_xkv, v_xq_norm, v_xk_norm, v_w_xo, v_norm_ffn, v_w_up, v_conv_ffn, v_w_down):
    names = ("norm_mix", "w_in", "conv_gdn", "a_log", "dt_bias", "gdn_out_norm", "w_proj_gdn", "w_proj_sb", "w_out",
             "norm_x", "norm_mem", "w_xq", "w_xkv", "xq_norm", "xk_norm", "w_xo", "norm_ffn", "w_up", "conv_ffn", "w_down")
    wts = dict(zip(names, (norm_mix, w_in, conv_gdn, a_log, dt_bias, gdn_out_norm, w_proj_gdn, w_proj_sb, w_out,
                           norm_x, norm_mem, w_xq, w_xkv, xq_norm, xk_norm, w_xo, norm_ffn, w_up, conv_ffn, w_down)))
    mom = dict(zip(names, (m_norm_mix, m_w_in, m_conv_gdn, m_a_log, m_dt_bias, m_gdn_out_norm, m_w_proj_gdn, m_w_proj_sb,
                           m_w_out, m_norm_x, m_norm_mem, m_w_xq, m_w_xkv, m_xq_norm, m_xk_norm, m_w_xo, m_norm_ffn, m_w_up,
                           m_conv_ffn, m_w_down)))
    vel = dict(zip(names, (v_norm_mix, v_w_in, v_conv_gdn, v_a_log, v_dt_bias, v_gdn_out_norm, v_w_proj_gdn, v_w_proj_sb,
                           v_w_out, v_norm_x, v_norm_mem, v_w_xq, v_w_xkv, v_xq_norm, v_xk_norm, v_w_xo, v_norm_ffn, v_w_up,
                           v_conv_ffn, v_w_down)))
    cfg = CFG
    d, bsz, s = cfg.d, cfg.b, cfg.s
    t = bsz * s
    gh, sbh = cfg.gh, cfg.sbh
    gw, sw = gh * HD, sbh * HD
    nchunk = s // cfg.gch
    mx, my, mc = _my_pos()
    me = 4 * mx + 2 * my + mc

    shard_shapes = {n: tuple(wts[n].shape[1:]) for n in _BIG}

    packed_w = _pack_big_shards({n: wts[n][0] for n in _BIG})
    gathered = _all_gather_big(_cast_rows(packed_w, BF16, "cast_weights"))
    full = _unpack_gathered(gathered, shard_shapes)

    conv_rows = {n: _rows_of(wts[n][0], d) for n in _SMALL_CONV}
    conv_cnt = {n: conv_rows[n].shape[0] for n in _SMALL_CONV}
    conv_blk = _pad_rows(jnp.concatenate([conv_rows[n] for n in _SMALL_CONV], axis=0), SUBLANE)
    conv_all = jnp.zeros((N_DEV,) + conv_blk.shape, F32)
    conv_all = lax.dynamic_update_slice(conv_all, conv_blk[None], (me, 0, 0))
    conv_all = _all_reduce_small(conv_all.reshape(-1, d), "gather_conv_taps").reshape((N_DEV,) + conv_blk.shape)

    def full_conv(n, r0):
        k, cols = wts[n].shape[1], wts[n].shape[2]
        part = conv_all[:, r0:r0 + conv_cnt[n], :].reshape(N_DEV, -1)[:, :k * cols].reshape(N_DEV, k, cols)
        return part.transpose(1, 0, 2).reshape(k, N_DEV * cols)

    cgdn = full_conv("conv_gdn", 0)
    cffn = full_conv("conv_ffn", conv_cnt["conv_gdn"])

    win = full["w_in"]
    o_ab = 3 * gw
    o_z = o_ab + 2 * gh
    o_sb = o_z + gw
    o_gate = o_sb + 3 * sw
    w_qkv = win[:, :o_ab]
    w_ab = jnp.concatenate([win[:, o_ab:o_z], jnp.zeros((d, LANE - 2 * gh), win.dtype)], axis=1)
    w_z = win[:, o_z:o_sb]
    w_sb = win[:, o_sb:o_gate]
    w_gate = win[:, o_gate:]

    alog_p = jnp.concatenate([a_log.reshape(1, -1), jnp.zeros((1, LANE - gh), F32)], axis=1)
    dtb_p = jnp.concatenate([dt_bias.reshape(1, -1), jnp.zeros((1, LANE - gh), F32)], axis=1)
    onorm = gdn_out_norm.reshape(1, HD)

    x2 = x.reshape(t, d)
    tgt2 = loss_target.reshape(t, d)
    mem2 = mem.reshape(bsz * cfg.mem, d)

    (xn,) = _rowwise(_rms, [x2], [norm_mix], [(d, BF16)], [], name="norm_mix_fwd")
    p_qkv = _mm(xn, w_qkv, name="proj_qkv")
    p_ab = _mm(xn, w_ab, name="proj_ab")
    p_z = _mm(xn, w_z, name="proj_z")
    p_sb = _mm(xn, w_sb, name="proj_sb")
    p_gate = _mm(xn, w_gate, name="proj_gate")

    qkv_c = _gdn_conv_fwd(p_qkv.reshape(bsz, s, 3 * gw), cgdn)
    (gb,) = _rowwise(_gates_fn, [p_ab], [alog_p, dtb_p], [(LANE, F32)], [], name="gdn_gates_fwd")
    gbt = gb.reshape(bsz, s, LANE)[:, :, :2 * gh].transpose(0, 2, 1).reshape(bsz, 2 * gh, nchunk, 1, cfg.gch)
    o_a, states = _gdn_fwd(qkv_c, p_z.reshape(bsz, s, gw), gbt, onorm)
    o_b = _sb_fwd(p_sb.reshape(bsz, s, 3 * sw))

    pa = _mm(o_a.reshape(t, gw), full["w_proj_gdn"], name="proj_gdn_out")
    pb = _mm(o_b.reshape(t, sw), full["w_proj_sb"], name="proj_sb_out")

    def merge_fn(pa_, pb_, gate_):
        return _sigmoid(gate_[:, :d]) * pa_ + _sigmoid(gate_[:, d:]) * pb_

    (merged,) = _rowwise(merge_fn, [pa, pb, p_gate], [], [(d, BF16)], [], name="merge_fwd")
    h1 = _mm(merged, full["w_out"], add=x2, name="mixer_out")

    (hn_x,) = _rowwise(_rms, [h1], [norm_x], [(d, BF16)], [], name="norm_x_fwd")
    (mn,) = _rowwise(_rms, [mem2], [norm_mem], [(d, BF16)], [], name="norm_mem_fwd")
    q_raw = _mm(hn_x, full["w_xq"], name="xattn_q")
    kv = _mm(mn, full["w_xkv"], name="xattn_kv")
    xo = _xattn_fwd(q_raw.reshape(bsz, s, d), kv.reshape(bsz, cfg.mem, 2 * d), xq_norm, xk_norm)
    h2 = _mm(xo.reshape(t, d), full["w_xo"], add=h1, name="xattn_out")

    (hn_f,) = _rowwise(_rms, [h2], [norm_ffn], [(d, BF16)], [], name="norm_ffn_fwd")
    up = _mm(hn_f, full["w_up"], name="ffn_up")
    act = _ffn_conv_fwd(up.reshape(bsz, s, 2 * cfg.dff), cffn)
    y = _mm(act.reshape(t, cfg.dff), full["w_down"], add=h2, name="ffn_down")

    def loss_fn(y_, tg_):
        err = y_ - tg_
        part = 0.5 * jnp.sum(err * err) / d
        return err / d, jnp.full((1, LANE), part, F32)

    dy, loss_part = _rowwise(loss_fn, [y, tgt2], [], [(d, F32)], [(1, LANE)], name="loss")

    grads = {}
    dact = _mm(dy, full["w_down"], tb=True, name="d_act")
    grads["w_down"] = _mm(act.reshape(t, cfg.dff), dy, ta=True, name="dw_down")
    dup1, dup2, dcf1, dcf2 = _ffn_conv_bwd(up.reshape(bsz, s, 2 * cfg.dff), cffn, dact.reshape(bsz, s, cfg.dff))
    dup = jnp.concatenate([dup1, dup2], axis=2).reshape(t, 2 * cfg.dff)
    g_conv_ffn = jnp.concatenate([dcf1, dcf2], axis=1)
    dhn_f = _mm(dup, full["w_up"], tb=True, name="d_hn_ffn")
    grads["w_up"] = _mm(hn_f, dup, ta=True, name="dw_up")

    def norm_bwd_fn(h_, res_, dn_, g_):
        _, f = jax.vjp(_rms, h_, g_)
        dh, dg = f(dn_)
        return res_ + dh, dg

    dh2, g_norm_ffn = _rowwise(norm_bwd_fn, [h2, dy, dhn_f], [norm_ffn], [(d, F32)], [(1, d)], name="norm_ffn_bwd")

    dxo = _mm(dh2, full["w_xo"], tb=True, out_dtype=BF16, name="d_xo")
    grads["w_xo"] = _mm(xo.reshape(t, d), dh2, ta=True, name="dw_xo")
    dq_raw, dkv, g_xq_norm, g_xk_norm = _xattn_bwd(q_raw.reshape(bsz, s, d), kv.reshape(bsz, cfg.mem, 2 * d),
                                                   xq_norm, xk_norm, dxo.reshape(bsz, s, d))
    dq_raw2 = dq_raw.reshape(t, d)
    dkv2 = dkv.reshape(bsz * cfg.mem, 2 * d)
    dhn_x = _mm(dq_raw2, full["w_xq"], tb=True, name="d_hn_x")
    grads["w_xq"] = _mm(hn_x, dq_raw2, ta=True, name="dw_xq")
    dmn = _mm(dkv2, full["w_xkv"], tb=True, name="d_mn")
    grads["w_xkv"] = _mm(mn, dkv2, ta=True, name="dw_xkv")

    def norm_w_bwd_fn(h_, dn_, g_):
        _, f = jax.vjp(lambda gg: _rms(h_, gg), g_)
        return f(dn_)[0]

    (g_norm_mem,) = _rowwise(norm_w_bwd_fn, [mem2, dmn], [norm_mem], [], [(1, d)], name="norm_mem_bwd")
    dh1, g_norm_x = _rowwise(norm_bwd_fn, [h1, dh2, dhn_x], [norm_x], [(d, F32)], [(1, d)], name="norm_x_bwd")

    dmerged = _mm(dh1, full["w_out"], tb=True, name="d_merged")
    grads["w_out"] = _mm(merged, dh1, ta=True, name="dw_out")

    def merge_bwd_fn(pa_, pb_, gate_, dm_):
        _, f = jax.vjp(merge_fn, pa_, pb_, gate_)
        return f(dm_)

    dpa, dpb, dgate = _rowwise(merge_bwd_fn, [pa, pb, p_gate, dmerged], [], [(d, BF16), (d, BF16), (2 * d, BF16)], [],
                               name="merge_bwd")
    do_a = _mm(dpa, full["w_proj_gdn"], tb=True, name="d_o_gdn")
    grads["w_proj_gdn"] = _mm(o_a.reshape(t, gw), dpa, ta=True, name="dw_proj_gdn")
    do_b = _mm(dpb, full["w_proj_sb"], tb=True, name="d_o_sb")
    grads["w_proj_sb"] = _mm(o_b.reshape(t, sw), dpb, ta=True, name="dw_proj_sb")

    dsq, dsk, dsv = _sb_bwd(p_sb.reshape(bsz, s, 3 * sw), do_b.reshape(bsz, s, sw))
    dp_sb = jnp.concatenate([dsq, dsk, dsv], axis=2).reshape(t, 3 * sw)

    dgq, dgk, dgv, dz, dg, dbeta, g_onorm = _gdn_bwd(qkv_c, p_z.reshape(bsz, s, gw), gbt, onorm, states,
                                                     do_a.reshape(bsz, s, gw))
    dgb = jnp.concatenate([dg, dbeta], axis=1).reshape(bsz, 2 * gh, s).transpose(0, 2, 1)
    dgb = jnp.concatenate([dgb, jnp.zeros((bsz, s, LANE - 2 * gh), F32)], axis=2).reshape(t, LANE)

    def gates_bwd_fn(ab_, dgb_, alog_, dtb_):
        _, f = jax.vjp(_gates_fn, ab_, alog_, dtb_)
        return f(dgb_)

    dp_ab, g_alog, g_dtb = _rowwise(gates_bwd_fn, [p_ab, dgb], [alog_p, dtb_p], [(LANE, BF16)], [(1, LANE), (1, LANE)],
                                    name="gdn_gates_bwd")
    dqkv_c = jnp.concatenate([dgq, dgk, dgv], axis=2)
    dp_qkv, g_conv_gdn = _gdn_conv_bwd(p_qkv.reshape(bsz, s, 3 * gw), cgdn, dqkv_c)
    dp_qkv = dp_qkv.reshape(t, 3 * gw)
    dp_z = dz.reshape(t, gw)

    dxn = _mm(dp_qkv, w_qkv, tb=True, name="d_xn_qkv")
    dxn = _mm(dp_ab, w_ab, tb=True, add=dxn, name="d_xn_ab")
    dxn = _mm(dp_z, w_z, tb=True, add=dxn, name="d_xn_z")
    dxn = _mm(dp_sb, w_sb, tb=True, add=dxn, name="d_xn_sb")
    dxn = _mm(dgate, w_gate, tb=True, add=dxn, name="d_xn_gate")
    grads["w_in"] = jnp.concatenate([
        _mm(xn, dp_qkv, ta=True, name="dw_in_qkv"),
        _mm(xn, dp_ab, ta=True, name="dw_in_ab")[:, :2 * gh],
        _mm(xn, dp_z, ta=True, name="dw_in_z"),
        _mm(xn, dp_sb, ta=True, name="dw_in_sb"),
        _mm(xn, dgate, ta=True, name="dw_in_gate")], axis=1)
    grad_x, g_norm_mix = _rowwise(norm_bwd_fn, [x2, dh1, dxn], [norm_mix], [(d, F32)], [(1, d)], name="norm_mix_bwd")

    small_g = {"norm_mix": g_norm_mix, "norm_x": g_norm_x, "norm_mem": g_norm_mem, "norm_ffn": g_norm_ffn,
               "a_log": g_alog[:, :gh], "dt_bias": g_dtb[:, :gh], "gdn_out_norm": g_onorm,
               "xq_norm": g_xq_norm, "xk_norm": g_xk_norm}
    sm_rows = [_rows_of(small_g[n], d) for n in _SMALL_REP] + [_rows_of(loss_part, d)]
    sm_rows += [_rows_of(g_conv_gdn, d), _rows_of(g_conv_ffn, d)]
    sm_cnt = [r.shape[0] for r in sm_rows]
    sm_sum = _all_reduce_small(_pad_rows(jnp.concatenate(sm_rows, axis=0), SUBLANE), "all_reduce_small_grads")
    offs = [0]
    for cnt in sm_cnt:
        offs.append(offs[-1] + cnt)
    small_grad = {}
    for i, n in enumerate(_SMALL_REP):
        small_grad[n] = sm_sum[offs[i]:offs[i + 1]].reshape(-1)[:wts[n].size].reshape(wts[n].shape)
    loss = sm_sum[offs[len(_SMALL_REP)], 0]
    for i, n in enumerate(_SMALL_CONV):
        k, cols = wts[n].shape[1], wts[n].shape[2]
        o = offs[len(_SMALL_REP) + 1 + i]
        fullg = sm_sum[o:o + sm_cnt[len(_SMALL_REP) + 1 + i]].reshape(-1)[:k * cols * N_DEV].reshape(k, N_DEV * cols)
        small_grad[n] = lax.dynamic_slice(fullg, (0, me * cols), (k, cols)).reshape(wts[n].shape)

    small_names = _SMALL_REP + _SMALL_CONV

    def pack_small(src):
        return _pad_rows(jnp.concatenate([_rows_of(src[n], d) for n in small_names], axis=0), SUBLANE)

    sw_, sg_, sm_, sv_ = pack_small(wts), pack_small(small_grad), pack_small(mom), pack_small(vel)
    sd_, snm_, snv_ = _rowwise(_adamw_math, [sw_, sg_, sm_, sv_], [], [(d, F32)] * 3, [], name="adamw_small", tm=sw_.shape[0])

    def unpack_small(packed):
        out, r0 = {}, 0
        for n in small_names:
            cnt = _rows_of(wts[n], d).shape[0]
            out[n] = packed[r0:r0 + cnt].reshape(-1)[:wts[n].size].reshape(wts[n].shape)
            r0 += cnt
        return out

    small_delta, small_m, small_v = unpack_small(sd_), unpack_small(snm_), unpack_small(snv_)

    gpack = _pack_full_grads(grads, shard_shapes)
    recv1 = _exchange_sibling(gpack)
    s1 = _sum_sibling(gpack, recv1, jnp.reshape(mc, (1,)).astype(jnp.int32))
    recv2 = _exchange_chips(s1)
    idx = jnp.stack([me, 2 * mx + my]).astype(jnp.int32)
    pm = _pack_big_shards({n: mom[n][0] for n in _BIG})
    pv = _pack_big_shards({n: vel[n][0] for n in _BIG})
    bg, bd, bm, bv = _adamw_big(gpack, recv1, recv2, packed_w, pm, pv, idx)
    big_grad, big_delta, big_m, big_v = (_unpack_shard(a, shard_shapes) for a in (bg, bd, bm, bv))

    def pick(big, small, n):
        return big[n] if n in big else small[n]

    outs = [loss, grad_x.reshape(bsz, s, d)]
    outs += [pick(big_grad, small_grad, n) for n in names]
    outs += [pick(big_delta, small_delta, n) for n in names]
    outs += [pick(big_m, small_m, n) for n in names]
    outs += [pick(big_v, small_v, n) for n in names]
    return tuple(outs)
```

```python
import functools

import jax
import jax.numpy as jnp
from jax import lax
from jax.experimental import pallas as pl
from jax.experimental.pallas import tpu as pltpu

F32 = jnp.float32
BF16 = jnp.bfloat16
HIGHEST = lax.Precision.HIGHEST

LANE = 128
SUBLANE = 8
VMEM_LIMIT = 56 * 2 ** 20
N_DEV = 8
MESH = pl.DeviceIdType.MESH

EPS = 1e-6
ADAM_LR = 0.001
ADAM_B1 = 0.9
ADAM_B2 = 0.999
ADAM_EPS = 1e-08
ADAM_WD = 0.01
ADAM_STEP = 10


class _Cfg:
    d = 1024
    b = 4
    s = 2048
    mem = 256
    gh = 8
    gch = 64
    sbh = 8
    xh = 4
    dff = 2816
    pack_tile = 240


CFG = _Cfg()
HD = 128
SB_BLK = 128
SB_HP = 4
GDN_HP = 4
CONV_CB = 256
XQ_TILE = 256


def _tile(n, prefs):
    for t in prefs:
        if n % t == 0:
            return t
    raise ValueError(f"no tile for {n}")


def _cp(sem, **kw):
    return pltpu.CompilerParams(dimension_semantics=sem, vmem_limit_bytes=VMEM_LIMIT, **kw)


def _dims(kind):
    return {"nn": (((1,), (0,)), ((), ())), "nt": (((1,), (1,)), ((), ())), "tn": (((0,), (0,)), ((), ()))}[kind]


def _raw_bdot(a, b, kind):
    return lax.dot_general(a.astype(BF16), b.astype(BF16), _dims(kind), preferred_element_type=F32)


def _raw_fdot(a, b, kind):
    return lax.dot_general(a.astype(F32), b.astype(F32), _dims(kind), precision=HIGHEST, preferred_element_type=F32)


def _make_dot(raw):
    @functools.partial(jax.custom_vjp, nondiff_argnums=(2,))
    def dot(a, b, kind):
        return raw(a, b, kind)

    def fwd(a, b, kind):
        return raw(a, b, kind), (a, b)

    def bwd(kind, res, g):
        a, b = res
        if kind == "nn":
            return raw(g, b, "nt").astype(a.dtype), raw(a, g, "tn").astype(b.dtype)
        if kind == "nt":
            return raw(g, b, "nn").astype(a.dtype), raw(g, a, "tn").astype(b.dtype)
        return raw(b, g, "nt").astype(a.dtype), raw(a, g, "nn").astype(b.dtype)

    dot.defvjp(fwd, bwd)
    return dot


_bdot = _make_dot(_raw_bdot)
_fdot = _make_dot(_raw_fdot)


def _split_dot(x, m01):
    hi = x.astype(BF16)
    lo = (x - hi.astype(F32)).astype(BF16)
    return (lax.dot_general(hi, m01, _dims("nn"), preferred_element_type=F32)
            + lax.dot_general(lo, m01, _dims("nn"), preferred_element_type=F32))


_sigmoid = jax.nn.sigmoid


def _silu(x):
    return x * _sigmoid(x)


def _softplus(x):
    return jnp.maximum(x, 0.0) + jnp.log1p(jnp.exp(-jnp.abs(x)))


def _rms(x, g):
    return x * lax.rsqrt(jnp.mean(x * x, axis=-1, keepdims=True) + EPS) * g


def _iota2(shape, dim):
    return lax.broadcasted_iota(jnp.int32, shape, dim)


def _mm(a, b, *, ta=False, tb=False, add=None, out_dtype=F32, name):
    if ta:
        kd, m = a.shape
    else:
        m, kd = a.shape
    if tb:
        n, kb = b.shape
    else:
        kb, n = b.shape
    assert kd == kb, (a.shape, b.shape, ta, tb)
    tm = _tile(m, (1024, 512, 256, 128))
    tn = _tile(n, (1024, 512, 256, 128))
    tk = _tile(kd, (512, 256, 128))
    nk = kd // tk
    kind_dims = (((0 if ta else 1,), (1 if tb else 0,)), ((), ()))

    def body(*refs):
        if add is None:
            a_ref, b_ref, o_ref, acc = refs
        else:
            a_ref, b_ref, add_ref, o_ref, acc = refs
        k = pl.program_id(2)

        @pl.when(k == 0)
        def _():
            acc[...] = jnp.zeros_like(acc)

        acc[...] += lax.dot_general(a_ref[...].astype(BF16), b_ref[...].astype(BF16), kind_dims,
                                    preferred_element_type=F32)

        @pl.when(k == nk - 1)
        def _():
            r = acc[...]
            if add is not None:
                r = r + add_ref[...].astype(F32)
            o_ref[...] = r.astype(o_ref.dtype)

    a_spec = pl.BlockSpec((tk, tm), lambda i, j, k: (k, i)) if ta else pl.BlockSpec((tm, tk), lambda i, j, k: (i, k))
    b_spec = pl.BlockSpec((tn, tk), lambda i, j, k: (j, k)) if tb else pl.BlockSpec((tk, tn), lambda i, j, k: (k, j))
    in_specs = [a_spec, b_spec]
    args = [a, b]
    if add is not None:
        in_specs.append(pl.BlockSpec((tm, tn), lambda i, j, k: (i, j)))
        args.append(add)
    return pl.pallas_call(
        body, name=name, grid=(m // tm, n // tn, nk),
        in_specs=in_specs, out_specs=pl.BlockSpec((tm, tn), lambda i, j, k: (i, j)),
        out_shape=jax.ShapeDtypeStruct((m, n), out_dtype),
        scratch_shapes=[pltpu.VMEM((tm, tn), F32)],
        compiler_params=_cp(("parallel", "parallel", "arbitrary")),
    )(*args)


def _rowwise(fn, rows, pars, out_rows, out_accs, *, name, tm=None):
    t = rows[0].shape[0]
    if tm is None:
        tm = _tile(t, (256, 128, 64, 32, 16))
    assert t % tm == 0, (t, tm)
    n_r, n_p, n_or, n_oa = len(rows), len(pars), len(out_rows), len(out_accs)

    def body(*refs):
        r_in = refs[:n_r]
        p_in = refs[n_r:n_r + n_p]
        o_r = refs[n_r + n_p:n_r + n_p + n_or]
        o_a = refs[n_r + n_p + n_or:]
        outs = fn(*[r[...] for r in r_in], *[p[...] for p in p_in])
        if not isinstance(outs, (tuple, list)):
            outs = (outs,)
        assert len(outs) == n_or + n_oa, (name, len(outs))
        for ref, val in zip(o_r, outs[:n_or]):
            ref[...] = val.astype(ref.dtype)
        if n_oa:
            @pl.when(pl.program_id(0) == 0)
            def _():
                for ref in o_a:
                    ref[...] = jnp.zeros_like(ref)

            for ref, val in zip(o_a, outs[n_or:]):
                ref[...] += val.astype(F32)

    in_specs = [pl.BlockSpec((tm, r.shape[1]), lambda i: (i, 0)) for r in rows]
    in_specs += [pl.BlockSpec(p.shape, lambda i: (0, 0)) for p in pars]
    out_specs = [pl.BlockSpec((tm, c), lambda i: (i, 0)) for c, _ in out_rows]
    out_specs += [pl.BlockSpec(s, lambda i: (0, 0)) for s in out_accs]
    out_shape = [jax.ShapeDtypeStruct((t, c), dt) for c, dt in out_rows]
    out_shape += [jax.ShapeDtypeStruct(s, F32) for s in out_accs]
    return pl.pallas_call(
        body, name=name, grid=(t // tm,), in_specs=in_specs, out_specs=out_specs, out_shape=out_shape,
        compiler_params=_cp(("arbitrary",)),
    )(*rows, *pars)


def _shift_down(x, sh):
    rolled = pltpu.roll(x, sh, 0)
    return jnp.where(_iota2(x.shape, 0) >= sh, rolled, 0.0)


def _shift_up(x, sh):
    s = x.shape[0]
    rolled = pltpu.roll(x, s - sh, 0)
    return jnp.where(_iota2(x.shape, 0) < s - sh, rolled, 0.0)


def _conv(x, w):
    k = w.shape[0]
    y = x * w[k - 1:k, :]
    for i in range(k - 1):
        y = y + _shift_down(x, k - 1 - i) * w[i:i + 1, :]
    return y


def _conv_bwd(x, w, dy):
    k = w.shape[0]
    dx = dy * w[k - 1:k, :]
    dws = []
    for i in range(k - 1):
        dx = dx + _shift_up(dy, k - 1 - i) * w[i:i + 1, :]
        dws.append(jnp.sum(dy * _shift_down(x, k - 1 - i), axis=0, keepdims=True))
    dws.append(jnp.sum(dy * x, axis=0, keepdims=True))
    return dx, dws


def _gdn_post(y, j, nqb):
    a = _silu(y)
    sc = jnp.where(j < nqb, HD ** -0.5, 1.0).astype(F32)
    outs = []
    for h in range(y.shape[1] // HD):
        ah = a[:, h * HD:(h + 1) * HD]
        l2 = ah * lax.rsqrt(jnp.sum(ah * ah, axis=-1, keepdims=True) + EPS)
        outs.append(jnp.where(j < 2 * nqb, l2 * sc, ah))
    return jnp.concatenate(outs, axis=1) if len(outs) > 1 else outs[0]


def _gdn_conv_fwd(x, w):
    bsz, s, c3 = x.shape
    k = w.shape[0]
    nb = c3 // CONV_CB
    nqb = nb // 3

    def body(x_ref, w_ref, o_ref):
        j = pl.program_id(1)
        o_ref[0] = _gdn_post(_conv(x_ref[0], w_ref[...]), j, nqb)

    return pl.pallas_call(
        body, name="gdn_conv_fwd", grid=(bsz, nb),
        in_specs=[pl.BlockSpec((1, s, CONV_CB), lambda b, j: (b, 0, j)), pl.BlockSpec((k, CONV_CB), lambda b, j: (0, j))],
        out_specs=pl.BlockSpec((1, s, CONV_CB), lambda b, j: (b, 0, j)),
        out_shape=jax.ShapeDtypeStruct(x.shape, F32),
        compiler_params=_cp(("parallel", "parallel")),
    )(x, w)


def _gdn_conv_bwd(x, w, dout):
    bsz, s, c3 = x.shape
    k = w.shape[0]
    nb = c3 // CONV_CB
    nqb = nb // 3

    def body(x_ref, w_ref, d_ref, dx_ref, dw_ref):
        j = pl.program_id(0)
        b = pl.program_id(1)
        xv, wv = x_ref[0], w_ref[...]
        y = _conv(xv, wv)
        _, f = jax.vjp(lambda yy: _gdn_post(yy, j, nqb), y)
        (dy,) = f(d_ref[0])
        dx, dws = _conv_bwd(xv, wv, dy)
        dx_ref[0] = dx.astype(dx_ref.dtype)

        @pl.when(b == 0)
        def _():
            dw_ref[...] = jnp.zeros_like(dw_ref)

        for i in range(k):
            dw_ref[i:i + 1, :] += dws[i]

    return pl.pallas_call(
        body, name="gdn_conv_bwd", grid=(nb, bsz),
        in_specs=[pl.BlockSpec((1, s, CONV_CB), lambda j, b: (b, 0, j)), pl.BlockSpec((k, CONV_CB), lambda j, b: (0, j)),
                  pl.BlockSpec((1, s, CONV_CB), lambda j, b: (b, 0, j))],
        out_specs=[pl.BlockSpec((1, s, CONV_CB), lambda j, b: (b, 0, j)), pl.BlockSpec((k, CONV_CB), lambda j, b: (0, j))],
        out_shape=[jax.ShapeDtypeStruct(x.shape, BF16), jax.ShapeDtypeStruct(w.shape, F32)],
        compiler_params=_cp(("parallel", "arbitrary")),
    )(x, w, dout)


def _ffn_conv_fwd(up, w):
    bsz, s, c2 = up.shape
    k = w.shape[0]
    nb = (c2 // 2) // CONV_CB

    def body(x1_ref, x2_ref, w1_ref, w2_ref, o_ref):
        u1 = _conv(x1_ref[0], w1_ref[...])
        u2 = _conv(x2_ref[0], w2_ref[...])
        o_ref[0] = (_silu(u1) * u2).astype(o_ref.dtype)

    return pl.pallas_call(
        body, name="ffn_conv_fwd", grid=(bsz, nb),
        in_specs=[pl.BlockSpec((1, s, CONV_CB), lambda b, j: (b, 0, j)), pl.BlockSpec((1, s, CONV_CB), lambda b, j: (b, 0, j + nb)),
                  pl.BlockSpec((k, CONV_CB), lambda b, j: (0, j)), pl.BlockSpec((k, CONV_CB), lambda b, j: (0, j + nb))],
        out_specs=pl.BlockSpec((1, s, CONV_CB), lambda b, j: (b, 0, j)),
        out_shape=jax.ShapeDtypeStruct((bsz, s, c2 // 2), BF16),
        compiler_params=_cp(("parallel", "parallel")),
    )(up, up, w, w)


def _ffn_conv_bwd(up, w, dact):
    bsz, s, c2 = up.shape
    k = w.shape[0]
    half = c2 // 2
    nb = half // CONV_CB

    def body(x1_ref, x2_ref, w1_ref, w2_ref, d_ref, dx1_ref, dx2_ref, dw1_ref, dw2_ref):
        b = pl.program_id(1)
        x1, x2, w1, w2 = x1_ref[0], x2_ref[0], w1_ref[...], w2_ref[...]
        u1 = _conv(x1, w1)
        u2 = _conv(x2, w2)
        _, f = jax.vjp(lambda p, q: _silu(p) * q, u1, u2)
        du1, du2 = f(d_ref[0])
        dx1, dws1 = _conv_bwd(x1, w1, du1)
        dx2, dws2 = _conv_bwd(x2, w2, du2)
        dx1_ref[0] = dx1.astype(dx1_ref.dtype)
        dx2_ref[0] = dx2.astype(dx2_ref.dtype)

        @pl.when(b == 0)
        def _():
            dw1_ref[...] = jnp.zeros_like(dw1_ref)
            dw2_ref[...] = jnp.zeros_like(dw2_ref)

        for i in range(k):
            dw1_ref[i:i + 1, :] += dws1[i]
            dw2_ref[i:i + 1, :] += dws2[i]

    def blk(off):
        return pl.BlockSpec((1, s, CONV_CB), lambda j, b: (b, 0, j + off))

    def wblk(off):
        return pl.BlockSpec((k, CONV_CB), lambda j, b: (0, j + off))

    return pl.pallas_call(
        body, name="ffn_conv_bwd", grid=(nb, bsz),
        in_specs=[blk(0), blk(nb), wblk(0), wblk(nb), blk(0)],
        out_specs=[blk(0), blk(0), wblk(0), wblk(0)],
        out_shape=[jax.ShapeDtypeStruct((bsz, s, half), BF16), jax.ShapeDtypeStruct((bsz, s, half), BF16),
                   jax.ShapeDtypeStruct((k, half), F32), jax.ShapeDtypeStruct((k, half), F32)],
        compiler_params=_cp(("parallel", "arbitrary")),
    )(up, up, w, w, dact)


@jax.custom_vjp
def _inv_unit_lower(mats):
    c = mats[0].shape[0]
    eye = (_iota2((c, c), 0) == _iota2((c, c), 1)).astype(F32)
    ps = [-a for a in mats]
    ts = [eye + p for p in ps]
    n = 2
    while n < c:
        ps = [_raw_fdot(p, p, "nn") for p in ps]
        ts = [t + _raw_fdot(t, p, "nn") for t, p in zip(ts, ps)]
        n *= 2
    return ts


def _inv_fwd(mats):
    ts = _inv_unit_lower(mats)
    return ts, ts


def _inv_bwd(ts, gs):
    xs = [_raw_fdot(g, t, "nt") for g, t in zip(gs, ts)]
    return ([-_raw_fdot(t, x, "tn") for t, x in zip(ts, xs)],)


_inv_unit_lower.defvjp(_inv_fwd, _inv_bwd)


def _gdn_chunk(q, k, v, z, g_row, beta_row, state, onorm):
    nh = range(len(q))
    c = q[0].shape[0]
    ii, jj = _iota2((c, c), 0), _iota2((c, c), 1)
    incl, strict, eye = ii >= jj, ii > jj, ii == jj

    def to_col(row):
        return jnp.sum(jnp.where(eye, jnp.broadcast_to(row, (c, c)), 0.0), axis=1, keepdims=True)

    gc_col = [jnp.sum(jnp.where(incl, jnp.broadcast_to(g_row[h], (c, c)), 0.0), axis=1, keepdims=True) for h in nh]
    gc_row = [jnp.sum(jnp.where(eye, jnp.broadcast_to(gc_col[h], (c, c)), 0.0), axis=0, keepdims=True) for h in nh]
    beta_col = [to_col(beta_row[h]) for h in nh]
    gc_last = [jnp.sum(g_row[h], axis=1, keepdims=True) for h in nh]
    decay = [jnp.where(incl, jnp.exp(jnp.where(incl, gc_col[h] - gc_row[h], 0.0)), 0.0) for h in nh]
    kk = [_bdot(k[h], k[h], "nt") for h in nh]
    qk = [_bdot(q[h], k[h], "nt") * decay[h] for h in nh]
    tinv = _inv_unit_lower([jnp.where(strict, beta_col[h] * kk[h] * decay[h], 0.0) for h in nh])
    rhs = [jnp.concatenate([v[h] * beta_col[h], k[h] * (beta_col[h] * jnp.exp(gc_col[h]))], axis=1) for h in nh]
    uw = [_fdot(tinv[h], rhs[h], "nn") for h in nh]
    dv = v[0].shape[1]
    ws = [_bdot(uw[h][:, dv:], state[h], "nn") for h in nh]
    qs = [_bdot(q[h] * jnp.exp(gc_col[h]), state[h], "nn") for h in nh]
    v_new = [uw[h][:, :dv] - ws[h] for h in nh]
    o = [qs[h] + _bdot(qk[h], v_new[h], "nn") for h in nh]
    kv = [_bdot(k[h] * jnp.exp(gc_last[h] - gc_col[h]), v_new[h], "tn") for h in nh]
    new_state = [state[h] * jnp.exp(gc_last[h]) + kv[h] for h in nh]
    y = [_rms(o[h], onorm) * _silu(z[h]) for h in nh]
    return y, new_state


def _gdn_specs(s, c, reverse):
    n = s // c
    nn = (lambda i: n - 1 - i) if reverse else (lambda i: i)

    def qkv(off):
        return pl.BlockSpec((1, c, GDN_HP * HD), lambda b, h, i: (b, nn(i), h + off))

    def gate(off):
        return pl.BlockSpec((1, GDN_HP, 1, 1, c), lambda b, h, i: (b, h + off, nn(i), 0, 0))

    st = pl.BlockSpec((1, GDN_HP, 1, HD, HD), lambda b, h, i: (b, h, nn(i), 0, 0))
    onorm = pl.BlockSpec((1, HD), lambda b, h, i: (0, 0))
    return n, qkv, gate, st, onorm


def _gdn_fwd(qkv, z, gbt, onorm):
    bsz, s, _ = qkv.shape
    gh, c = CFG.gh, CFG.gch
    ng = gh // GDN_HP
    n, qs, gs, st, on = _gdn_specs(s, c, False)

    def body(q_ref, k_ref, v_ref, z_ref, g_ref, b_ref, on_ref, y_ref, st_ref, state):
        @pl.when(pl.program_id(2) == 0)
        def _():
            state[...] = jnp.zeros_like(state)

        nh = range(GDN_HP)
        hs = [slice(h * HD, (h + 1) * HD) for h in nh]
        s_in = [state[h] for h in nh]
        for h in nh:
            st_ref[0, h, 0] = s_in[h]
        y, s_out = _gdn_chunk([q_ref[0, :, hs[h]] for h in nh], [k_ref[0, :, hs[h]] for h in nh],
                              [v_ref[0, :, hs[h]] for h in nh], [z_ref[0, :, hs[h]] for h in nh],
                              [g_ref[0, h, 0] for h in nh], [b_ref[0, h, 0] for h in nh], s_in, on_ref[...])
        for h in nh:
            y_ref[0, :, hs[h]] = y[h]
            state[h] = s_out[h]

    return pl.pallas_call(
        body, name="gdn_fwd", grid=(bsz, ng, n),
        in_specs=[qs(0), qs(ng), qs(2 * ng), qs(0), gs(0), gs(ng), on],
        out_specs=[qs(0), st],
        out_shape=[jax.ShapeDtypeStruct((bsz, s, gh * HD), F32), jax.ShapeDtypeStruct((bsz, gh, n, HD, HD), F32)],
        scratch_shapes=[pltpu.VMEM((GDN_HP, HD, HD), F32)],
        compiler_params=_cp(("parallel", "parallel", "arbitrary")),
    )(qkv, qkv, qkv, z, gbt, gbt, onorm)


def _gdn_bwd(qkv, z, gbt, onorm, states, dy):
    bsz, s, _ = qkv.shape
    gh, c = CFG.gh, CFG.gch
    ng = gh // GDN_HP
    n, qs, gs, st, on = _gdn_specs(s, c, True)

    def body(q_ref, k_ref, v_ref, z_ref, g_ref, b_ref, on_ref, st_ref, dy_ref,
             dq_ref, dk_ref, dv_ref, dz_ref, dg_ref, db_ref, don_ref, dstate):
        first = (pl.program_id(0) == 0) & (pl.program_id(1) == 0) & (pl.program_id(2) == 0)

        @pl.when(first)
        def _():
            don_ref[...] = jnp.zeros_like(don_ref)

        @pl.when(pl.program_id(2) == 0)
        def _():
            dstate[...] = jnp.zeros_like(dstate)

        nh = range(GDN_HP)
        hs = [slice(h * HD, (h + 1) * HD) for h in nh]
        _, f = jax.vjp(_gdn_chunk, [q_ref[0, :, hs[h]] for h in nh], [k_ref[0, :, hs[h]] for h in nh],
                       [v_ref[0, :, hs[h]] for h in nh], [z_ref[0, :, hs[h]] for h in nh],
                       [g_ref[0, h, 0] for h in nh], [b_ref[0, h, 0] for h in nh],
                       [st_ref[0, h, 0] for h in nh], on_ref[...])
        dq, dk, dv, dz, dg, db, ds, don = f(([dy_ref[0, :, hs[h]] for h in nh], [dstate[h] for h in nh]))
        for h in nh:
            dq_ref[0, :, hs[h]] = dq[h]
            dk_ref[0, :, hs[h]] = dk[h]
            dv_ref[0, :, hs[h]] = dv[h]
            dz_ref[0, :, hs[h]] = dz[h]
            dg_ref[0, h, 0] = dg[h]
            db_ref[0, h, 0] = db[h]
            dstate[h] = ds[h]
        don_ref[...] += don

    act = jax.ShapeDtypeStruct((bsz, s, gh * HD), F32)
    gshape = jax.ShapeDtypeStruct((bsz, gh, n, 1, c), F32)
    return pl.pallas_call(
        body, name="gdn_bwd", grid=(bsz, ng, n),
        in_specs=[qs(0), qs(ng), qs(2 * ng), qs(0), gs(0), gs(ng), on, st, qs(0)],
        out_specs=[qs(0), qs(0), qs(0), qs(0), gs(0), gs(0), on],
        out_shape=[act, act, act, act, gshape, gshape, jax.ShapeDtypeStruct((1, HD), F32)],
        scratch_shapes=[pltpu.VMEM((GDN_HP, HD, HD), F32)],
        compiler_params=_cp(("arbitrary", "arbitrary", "arbitrary")),
    )(qkv, qkv, qkv, z, gbt, gbt, onorm, states, dy)


def _gates_fn(ab, alog, dtb):
    lane = _iota2(ab.shape, 1)
    g = -jnp.exp(alog) * _softplus(ab + dtb)
    beta = _sigmoid(ab)
    return jnp.where(lane < CFG.gh, g, jnp.where(lane < 2 * CFG.gh, beta, 0.0))


def _heads_cumsum(xs, tri):
    n = xs[0].shape[0]
    y = _split_dot(jnp.concatenate(xs, axis=0), tri)
    return [y[h * n:(h + 1) * n] for h in range(len(xs))]


def _sb_weights(qs, kjs, mask, runs, tri_su):
    nh = len(qs)
    zs = [lax.dot_general(qs[h], kjs[h], _dims("nt"), preferred_element_type=F32) for h in range(nh)]
    l1p = [jnp.log1p(jnp.exp(-jnp.abs(z))) for z in zs]
    lss = [jnp.minimum(zs[h], 0.0) - l1p[h] for h in range(nh)]
    lfs = [jnp.where(mask, jnp.minimum(-zs[h], 0.0) - l1p[h], 0.0) for h in range(nh)]
    sfx = _heads_cumsum(lfs, tri_su)
    ws = [jnp.where(mask, jnp.exp(lss[h] + sfx[h] + runs[h]), 0.0) for h in range(nh)]
    return zs, lfs, ws


def _sb_specs(s, w):
    def qb(off):
        return pl.BlockSpec((1, SB_BLK, w), lambda b, h, i: (b, i, h + off))

    def full(off):
        return pl.BlockSpec((1, s, w), lambda b, h, i: (b, 0, h + off))

    return qb, full


def _sb_fwd(qkv):
    bsz, s, _ = qkv.shape
    ng = CFG.sbh // SB_HP
    w = SB_HP * HD
    scale = HD ** -0.5
    qb, full = _sb_specs(s, w)

    def body(q_ref, k_ref, v_ref, o_ref):
        i = pl.program_id(2)
        r, c = _iota2((SB_BLK, SB_BLK), 0), _iota2((SB_BLK, SB_BLK), 1)
        tri_su = (r > c).astype(BF16)
        hs = [slice(h * HD, (h + 1) * HD) for h in range(SB_HP)]
        qs = [(q_ref[0, :, hs[h]] * scale).astype(BF16) for h in range(SB_HP)]

        def step(jj, carry):
            j = i - jj
            off = pl.multiple_of(j * SB_BLK, SB_BLK)
            mask = (j < i) | (c < r)
            kjs = [k_ref[0, pl.ds(off, SB_BLK), hs[h]].astype(BF16) for h in range(SB_HP)]
            vjs = [v_ref[0, pl.ds(off, SB_BLK), hs[h]].astype(BF16) for h in range(SB_HP)]
            _, lfs, ws = _sb_weights(qs, kjs, mask, [cr[1] for cr in carry], tri_su)
            pv = [lax.dot_general(ws[h].astype(BF16), vjs[h], _dims("nn"), preferred_element_type=F32)
                  for h in range(SB_HP)]
            return tuple((carry[h][0] + pv[h], carry[h][1] + jnp.sum(lfs[h], axis=1, keepdims=True))
                         for h in range(SB_HP))

        init = tuple((jnp.zeros((SB_BLK, HD), F32), jnp.zeros((SB_BLK, 1), F32)) for _ in range(SB_HP))
        res = lax.fori_loop(0, i + 1, step, init)
        for h in range(SB_HP):
            o_ref[0, :, hs[h]] = res[h][0]

    return pl.pallas_call(
        body, name="sb_fwd", grid=(bsz, ng, s // SB_BLK),
        in_specs=[qb(0), full(ng), full(2 * ng)], out_specs=qb(0),
        out_shape=jax.ShapeDtypeStruct((bsz, s, CFG.sbh * HD), F32),
        compiler_params=_cp(("parallel", "parallel", "arbitrary")),
    )(qkv, qkv, qkv)


def _sb_bwd(qkv, do):
    bsz, s, _ = qkv.shape
    ng = CFG.sbh // SB_HP
    w = SB_HP * HD
    nblk = s // SB_BLK
    scale = HD ** -0.5
    qb, full = _sb_specs(s, w)

    def body(q_ref, k_ref, v_ref, do_ref, dq_ref, dk_ref, dv_ref, dk_acc, dv_acc, dl_pan, z_pan):
        i = pl.program_id(2)

        @pl.when(i == 0)
        def _():
            dk_acc[...] = jnp.zeros_like(dk_acc)
            dv_acc[...] = jnp.zeros_like(dv_acc)

        r, c = _iota2((SB_BLK, SB_BLK), 0), _iota2((SB_BLK, SB_BLK), 1)
        tri_su = (r > c).astype(BF16)
        tri_pre = (r < c).astype(BF16)
        hs = [slice(h * HD, (h + 1) * HD) for h in range(SB_HP)]
        qs = [(q_ref[0, :, hs[h]] * scale).astype(BF16) for h in range(SB_HP)]
        dob = [do_ref[0, :, hs[h]].astype(BF16) for h in range(SB_HP)]

        def step_a(jj, runs):
            j = i - jj
            off = pl.multiple_of(j * SB_BLK, SB_BLK)
            mask = (j < i) | (c < r)
            nh = range(SB_HP)
            kjs = [k_ref[0, pl.ds(off, SB_BLK), hs[h]].astype(BF16) for h in nh]
            vjs = [v_ref[0, pl.ds(off, SB_BLK), hs[h]].astype(BF16) for h in nh]
            dws = [lax.dot_general(dob[h], vjs[h], _dims("nt"), preferred_element_type=F32) for h in nh]
            zs, lfs, ws = _sb_weights(qs, kjs, mask, runs, tri_su)
            dvs = [lax.dot_general(ws[h].astype(BF16), dob[h], _dims("tn"), preferred_element_type=F32) for h in nh]
            for h in nh:
                dl_pan[h, j] = dws[h] * ws[h]
                z_pan[h, j] = zs[h]
                dv_acc[pl.ds(off, SB_BLK), hs[h]] += dvs[h]
            return tuple(runs[h] + jnp.sum(lfs[h], axis=1, keepdims=True) for h in nh)

        lax.fori_loop(0, i + 1, step_a, tuple(jnp.zeros((SB_BLK, 1), F32) for _ in range(SB_HP)))

        def step_b(j, carry):
            off = pl.multiple_of(j * SB_BLK, SB_BLK)
            mask = (j < i) | (c < r)
            nh = range(SB_HP)
            kjs = [k_ref[0, pl.ds(off, SB_BLK), hs[h]].astype(BF16) for h in nh]
            dls = [dl_pan[h, j] for h in nh]
            sgs = [_sigmoid(z_pan[h, j]) for h in nh]
            pfx = _heads_cumsum(dls, tri_pre)
            dzs = [jnp.where(mask, dls[h] * (1.0 - sgs[h]) - sgs[h] * (pfx[h] + carry[h][1]), 0.0).astype(BF16)
                   for h in nh]
            dqs = [lax.dot_general(dzs[h], kjs[h], _dims("nn"), preferred_element_type=F32) for h in nh]
            dks = [lax.dot_general(dzs[h], qs[h], _dims("tn"), preferred_element_type=F32) for h in nh]
            for h in nh:
                dk_acc[pl.ds(off, SB_BLK), hs[h]] += dks[h]
            return tuple((carry[h][0] + dqs[h], carry[h][1] + jnp.sum(dls[h], axis=1, keepdims=True)) for h in nh)

        init = tuple((jnp.zeros((SB_BLK, HD), F32), jnp.zeros((SB_BLK, 1), F32)) for _ in range(SB_HP))
        res = lax.fori_loop(0, i + 1, step_b, init)
        for h in range(SB_HP):
            dq_ref[0, :, hs[h]] = (res[h][0] * scale).astype(dq_ref.dtype)

        @pl.when(i == nblk - 1)
        def _():
            dk_ref[0] = dk_acc[...].astype(dk_ref.dtype)
            dv_ref[0] = dv_acc[...].astype(dv_ref.dtype)

    out = jax.ShapeDtypeStruct((bsz, s, CFG.sbh * HD), BF16)
    return pl.pallas_call(
        body, name="sb_bwd", grid=(bsz, ng, nblk),
        in_specs=[qb(0), full(ng), full(2 * ng), qb(0)],
        out_specs=[qb(0), full(0), full(0)],
        out_shape=[out, out, out],
        scratch_shapes=[pltpu.VMEM((s, w), F32), pltpu.VMEM((s, w), F32),
                        pltpu.VMEM((SB_HP, nblk, SB_BLK, SB_BLK), F32), pltpu.VMEM((SB_HP, nblk, SB_BLK, SB_BLK), F32)],
        compiler_params=_cp(("parallel", "parallel", "arbitrary")),
    )(qkv, qkv, qkv, do)


def _xattn_fn(q_raw, kv, qn, kn):
    d = q_raw.shape[1]
    dh = d // CFG.xh
    outs = []
    for h in range(CFG.xh):
        qh = _rms(q_raw[:, h * dh:(h + 1) * dh], qn)
        kh = _rms(kv[:, h * dh:(h + 1) * dh], kn)
        vh = kv[:, d + h * dh:d + (h + 1) * dh]
        sc = _bdot(qh, kh, "nt") * (dh ** -0.5)
        sc = sc - lax.stop_gradient(jnp.max(sc, axis=-1, keepdims=True))
        e = jnp.exp(sc)
        p = e / jnp.sum(e, axis=-1, keepdims=True)
        outs.append(_bdot(p, vh, "nn"))
    return jnp.concatenate(outs, axis=1)


def _xattn_fwd(q_raw, kv, qn, kn):
    bsz, s, d = q_raw.shape
    m = kv.shape[1]
    tq = _tile(s, (XQ_TILE, 128))

    def body(q_ref, kv_ref, qn_ref, kn_ref, o_ref):
        o_ref[0] = _xattn_fn(q_ref[0], kv_ref[0], qn_ref[...], kn_ref[...]).astype(o_ref.dtype)

    return pl.pallas_call(
        body, name="xattn_fwd", grid=(bsz, s // tq),
        in_specs=[pl.BlockSpec((1, tq, d), lambda b, i: (b, i, 0)), pl.BlockSpec((1, m, 2 * d), lambda b, i: (b, 0, 0)),
                  pl.BlockSpec(qn.shape, lambda b, i: (0, 0)), pl.BlockSpec(kn.shape, lambda b, i: (0, 0))],
        out_specs=pl.BlockSpec((1, tq, d), lambda b, i: (b, i, 0)),
        out_shape=jax.ShapeDtypeStruct((bsz, s, d), BF16),
        compiler_params=_cp(("parallel", "parallel")),
    )(q_raw, kv, qn, kn)


def _xattn_bwd(q_raw, kv, qn, kn, do):
    bsz, s, d = q_raw.shape
    m = kv.shape[1]
    tq = _tile(s, (XQ_TILE, 128))

    def body(q_ref, kv_ref, qn_ref, kn_ref, do_ref, dq_ref, dkv_ref, dqn_ref, dkn_ref):
        b, i = pl.program_id(0), pl.program_id(1)

        @pl.when((b == 0) & (i == 0))
        def _():
            dqn_ref[...] = jnp.zeros_like(dqn_ref)
            dkn_ref[...] = jnp.zeros_like(dkn_ref)

        @pl.when(i == 0)
        def _():
            dkv_ref[...] = jnp.zeros_like(dkv_ref)

        _, f = jax.vjp(_xattn_fn, q_ref[0], kv_ref[0], qn_ref[...], kn_ref[...])
        dq, dkv, dqn, dkn = f(do_ref[0].astype(F32))
        dq_ref[0] = dq.astype(dq_ref.dtype)
        dkv_ref[0] += dkv
        dqn_ref[...] += dqn
        dkn_ref[...] += dkn

    return pl.pallas_call(
        body, name="xattn_bwd", grid=(bsz, s // tq),
        in_specs=[pl.BlockSpec((1, tq, d), lambda b, i: (b, i, 0)), pl.BlockSpec((1, m, 2 * d), lambda b, i: (b, 0, 0)),
                  pl.BlockSpec(qn.shape, lambda b, i: (0, 0)), pl.BlockSpec(kn.shape, lambda b, i: (0, 0)),
                  pl.BlockSpec((1, tq, d), lambda b, i: (b, i, 0))],
        out_specs=[pl.BlockSpec((1, tq, d), lambda b, i: (b, i, 0)), pl.BlockSpec((1, m, 2 * d), lambda b, i: (b, 0, 0)),
                   pl.BlockSpec(qn.shape, lambda b, i: (0, 0)), pl.BlockSpec(kn.shape, lambda b, i: (0, 0))],
        out_shape=[jax.ShapeDtypeStruct((bsz, s, d), BF16), jax.ShapeDtypeStruct(kv.shape, F32),
                   jax.ShapeDtypeStruct(qn.shape, F32), jax.ShapeDtypeStruct(kn.shape, F32)],
        compiler_params=_cp(("arbitrary", "arbitrary")),
    )(q_raw, kv, qn, kn, do)


def _my_pos():
    return lax.axis_index("x"), lax.axis_index("y"), lax.axis_index("c")


def _all_gather_big(shard):
    r, d = shard.shape

    def body(x_ref, out_ref, send_sems, recv_sems, local_sem):
        x, y, c = _my_pos()
        me, sibling = (x, y, c), (x, y, 1 - c)
        chips = [(1 - x, y), (x, 1 - y), (1 - x, 1 - y)]

        def slot(px, py, pc):
            return out_ref.at[4 * px + 2 * py + pc]

        def copy(k, block, to, src=None):
            return pltpu.make_async_remote_copy(
                src_ref=slot(*block) if src is None else src, dst_ref=slot(*block),
                send_sem=send_sems.at[k], recv_sem=recv_sems.at[k], device_id=to, device_id_type=MESH)

        mine = pltpu.make_async_copy(x_ref, slot(*me), local_sem)
        mine.start()
        first = [copy(0, me, sibling, src=x_ref)]
        first += [copy(1 + j, me, (*chip, c), src=x_ref) for j, chip in enumerate(chips)]
        for cp in first:
            cp.start()
        passed = [copy(4 + j, (*chip, c), sibling) for j, chip in enumerate(chips)]
        for j, chip in enumerate(chips):
            copy(1 + j, (*chip, c), me).wait_recv()
            passed[j].start()
        copy(0, sibling, me).wait_recv()
        for j, chip in enumerate(chips):
            copy(4 + j, (*chip, 1 - c), me).wait_recv()
        for cp in first + passed:
            cp.wait_send()
        mine.wait()

    return pl.pallas_call(
        body, name="all_gather_weights",
        out_shape=jax.ShapeDtypeStruct((N_DEV, r, d), shard.dtype),
        in_specs=[pl.BlockSpec(memory_space=pl.ANY)], out_specs=pl.BlockSpec(memory_space=pl.ANY),
        scratch_shapes=[pltpu.SemaphoreType.DMA((7,)), pltpu.SemaphoreType.DMA((7,)), pltpu.SemaphoreType.DMA],
    )(shard)


def _exchange_sibling(g):
    _, r, d = g.shape

    def body(g_ref, out_ref, send_sems, recv_sems):
        x, y, c = _my_pos()
        copies = [pltpu.make_async_remote_copy(
            src_ref=g_ref.at[2 * k + (1 - c)], dst_ref=out_ref.at[k],
            send_sem=send_sems.at[k], recv_sem=recv_sems.at[k], device_id=(x, y, 1 - c), device_id_type=MESH)
            for k in range(4)]
        for cp in copies:
            cp.start()
        for cp in copies:
            cp.wait_recv()
        for cp in copies:
            cp.wait_send()

    return pl.pallas_call(
        body, name="grads_to_sibling",
        out_shape=jax.ShapeDtypeStruct((4, r, d), g.dtype),
        in_specs=[pl.BlockSpec(memory_space=pl.ANY)], out_specs=pl.BlockSpec(memory_space=pl.ANY),
        scratch_shapes=[pltpu.SemaphoreType.DMA((4,)), pltpu.SemaphoreType.DMA((4,))],
    )(g)


def _exchange_chips(s1):
    _, r, d = s1.shape

    def body(s_ref, out_ref, send_sems, recv_sems):
        x, y, c = _my_pos()
        copies = []
        for rel in (1, 2, 3):
            px = jnp.bitwise_xor(x, rel >> 1)
            py = jnp.bitwise_xor(y, rel & 1)
            copies.append(pltpu.make_async_remote_copy(
                src_ref=s_ref.at[2 * px + py], dst_ref=out_ref.at[rel - 1],
                send_sem=send_sems.at[rel - 1], recv_sem=recv_sems.at[rel - 1],
                device_id=(px, py, c), device_id_type=MESH))
        for cp in copies:
            cp.start()
        for cp in copies:
            cp.wait_recv()
        for cp in copies:
            cp.wait_send()

    return pl.pallas_call(
        body, name="grads_to_chips",
        out_shape=jax.ShapeDtypeStruct((3, r, d), s1.dtype),
        in_specs=[pl.BlockSpec(memory_space=pl.ANY)], out_specs=pl.BlockSpec(memory_space=pl.ANY),
        scratch_shapes=[pltpu.SemaphoreType.DMA((3,)), pltpu.SemaphoreType.DMA((3,))],
    )(s1)


def _all_reduce_small(blk, name):
    rows, d = blk.shape

    def body(x_ref, out_ref, land, send_sems, recv_sems):
        x, y, c = _my_pos()
        me = 4 * x + 2 * y + c
        copies = []
        for rel in range(1, N_DEV):
            peer = (jnp.bitwise_xor(x, rel >> 2), jnp.bitwise_xor(y, (rel >> 1) & 1), jnp.bitwise_xor(c, rel & 1))
            copies.append(pltpu.make_async_remote_copy(
                src_ref=x_ref, dst_ref=land.at[rel - 1], send_sem=send_sems.at[rel - 1], recv_sem=recv_sems.at[rel - 1],
                device_id=peer, device_id_type=MESH))
        for cp in copies:
            cp.start()
        for cp in copies:
            cp.wait_recv()
        acc = jnp.zeros((rows, d), F32)
        for dev in range(N_DEV):
            rel = jnp.bitwise_xor(me, dev)
            got = land[jnp.maximum(rel - 1, 0)]
            acc = acc + jnp.where(rel == 0, x_ref[...], got)
        out_ref[...] = acc
        for cp in copies:
            cp.wait_send()

    return pl.pallas_call(
        body, name=name,
        out_shape=jax.ShapeDtypeStruct((rows, d), F32),
        in_specs=[pl.BlockSpec(memory_space=pltpu.VMEM)], out_specs=pl.BlockSpec(memory_space=pltpu.VMEM),
        scratch_shapes=[pltpu.VMEM((N_DEV - 1, rows, d), F32), pltpu.SemaphoreType.DMA((N_DEV - 1,)),
                        pltpu.SemaphoreType.DMA((N_DEV - 1,))],
    )(blk)


def _cast_rows(x, dtype, name):
    return _rowwise(lambda v: v, [x], [], [(x.shape[1], dtype)], [], name=name, tm=CFG.pack_tile)[0]


def _sum_sibling(g, recv1, c_idx):
    _, r, d = g.shape
    tm = CFG.pack_tile

    def body(c_ref, g_ref, r_ref, o_ref):
        o_ref[0] = (g_ref[0] + r_ref[0]).astype(o_ref.dtype)

    grid_spec = pltpu.PrefetchScalarGridSpec(
        num_scalar_prefetch=1, grid=(4, r // tm),
        in_specs=[pl.BlockSpec((1, tm, d), lambda k, i, c_ref: (2 * k + c_ref[0], i, 0)),
                  pl.BlockSpec((1, tm, d), lambda k, i, c_ref: (k, i, 0))],
        out_specs=pl.BlockSpec((1, tm, d), lambda k, i, c_ref: (k, i, 0)))
    return pl.pallas_call(
        body, name="sum_sibling", grid_spec=grid_spec,
        out_shape=jax.ShapeDtypeStruct((4, r, d), BF16),
        compiler_params=_cp(("parallel", "parallel")),
    )(c_idx, g, recv1)


def _adamw_math(w, g, m, v):
    m2 = ADAM_B1 * m + (1.0 - ADAM_B1) * g
    v2 = ADAM_B2 * v + (1.0 - ADAM_B2) * (g * g)
    m_hat = m2 / (1.0 - ADAM_B1 ** ADAM_STEP)
    v_hat = v2 / (1.0 - ADAM_B2 ** ADAM_STEP)
    delta = -ADAM_LR * (m_hat / (jnp.sqrt(v_hat) + ADAM_EPS) + ADAM_WD * w)
    return delta, m2, v2


def _adamw_big(g, recv1, recv2, w, m, v, idx):
    _, r, d = g.shape
    tm = CFG.pack_tile

    def body(idx_ref, g_ref, r1_ref, ra_ref, rb_ref, rc_ref, w_ref, m_ref, v_ref, og, od, om, ov):
        grad = (g_ref[0] + r1_ref[0]) + ra_ref[0].astype(F32) + rb_ref[0].astype(F32) + rc_ref[0].astype(F32)
        delta, m2, v2 = _adamw_math(w_ref[...], grad, m_ref[...], v_ref[...])
        og[...] = grad
        od[...] = delta
        om[...] = m2
        ov[...] = v2

    flat = pl.BlockSpec((tm, d), lambda i, idx_ref: (i, 0))
    grid_spec = pltpu.PrefetchScalarGridSpec(
        num_scalar_prefetch=1, grid=(r // tm,),
        in_specs=[pl.BlockSpec((1, tm, d), lambda i, idx_ref: (idx_ref[0], i, 0)),
                  pl.BlockSpec((1, tm, d), lambda i, idx_ref: (idx_ref[1], i, 0)),
                  pl.BlockSpec((1, tm, d), lambda i, idx_ref: (0, i, 0)),
                  pl.BlockSpec((1, tm, d), lambda i, idx_ref: (1, i, 0)),
                  pl.BlockSpec((1, tm, d), lambda i, idx_ref: (2, i, 0)),
                  flat, flat, flat],
        out_specs=[flat, flat, flat, flat])
    shp = jax.ShapeDtypeStruct((r, d), F32)
    return pl.pallas_call(
        body, name="adamw_big", grid_spec=grid_spec, out_shape=[shp, shp, shp, shp],
        compiler_params=_cp(("parallel",)),
    )(idx, g, recv1, recv2, recv2, recv2, w, m, v)


def _rows_of(v, d):
    flat = v.reshape(-1)
    pad = (-flat.shape[0]) % d
    if pad:
        flat = jnp.concatenate([flat, jnp.zeros((pad,), flat.dtype)])
    return flat.reshape(-1, d)


def _pad_rows(a, mult):
    pad = (-a.shape[0]) % mult
    if pad:
        a = jnp.concatenate([a, jnp.zeros((pad,) + a.shape[1:], a.dtype)], axis=0)
    return a


_BIG = ("w_in", "w_xkv", "w_up", "w_proj_gdn", "w_proj_sb", "w_out", "w_xq", "w_xo", "w_down")
_COL_SHARDED = ("w_in", "w_xkv", "w_up")
_SMALL_REP = ("norm_mix", "norm_x", "norm_mem", "norm_ffn", "a_log", "dt_bias", "gdn_out_norm", "xq_norm", "xk_norm")
_SMALL_CONV = ("conv_gdn", "conv_ffn")


def _pack_big_shards(shards):
    d = CFG.d
    return _pad_rows(jnp.concatenate([shards[n].reshape(-1, d) for n in _BIG], axis=0), CFG.pack_tile)


def _big_row_counts(shapes):
    d = CFG.d
    return [shapes[n][0] * shapes[n][1] // d for n in _BIG]


def _unpack_gathered(gath, shapes):
    out, r0 = {}, 0
    for n, cnt in zip(_BIG, _big_row_counts(shapes)):
        rows, cols = shapes[n]
        part = gath[:, r0:r0 + cnt, :]
        if n in _COL_SHARDED:
            out[n] = part.reshape(N_DEV, rows, cols).transpose(1, 0, 2).reshape(rows, N_DEV * cols)
        else:
            out[n] = part.reshape(N_DEV * rows, cols)
        r0 += cnt
    return out


def _pack_full_grads(grads, shapes):
    d = CFG.d
    parts = []
    for n in _BIG:
        rows, cols = shapes[n]
        g = grads[n]
        if n in _COL_SHARDED:
            g = g.reshape(rows, N_DEV, cols).transpose(1, 0, 2)
        parts.append(g.reshape(N_DEV, rows * cols // d, d))
    full = jnp.concatenate(parts, axis=1)
    pad = (-full.shape[1]) % CFG.pack_tile
    if pad:
        full = jnp.concatenate([full, jnp.zeros((N_DEV, pad, d), full.dtype)], axis=1)
    return full


def _unpack_shard(packed, shapes):
    out, r0 = {}, 0
    for n, cnt in zip(_BIG, _big_row_counts(shapes)):
        out[n] = packed[r0:r0 + cnt].reshape((1,) + tuple(shapes[n]))
        r0 += cnt
    return out


def kernel(x, mem, norm_mix, w_in, conv_gdn, a_log, dt_bias, gdn_out_norm, w_proj_gdn, w_proj_sb, w_out, norm_x, norm_mem, w_xq, w_xkv, xq_norm, xk_norm, w_xo, norm_ffn, w_up, conv_ffn, w_down, loss_target, m_norm_mix, m_w_in, m_conv_gdn, m_a_log, m_dt_bias, m_gdn_out_norm, m_w_proj_gdn, m_w_proj_sb, m_w_out, m_norm_x, m_norm_mem, m_w_xq, m_w_xkv, m_xq_norm, m_xk_norm, m_w_xo, m_norm_ffn, m_w_up, m_conv_ffn, m_w_down, v_norm_mix, v_w_in, v_conv_gdn, v_a_log, v_dt_bias, v_gdn_out_norm, v_w_proj_gdn, v_w_proj_sb, v_w_out, v_norm_x, v_norm_mem, v_w_xq, v_w_xkv, v_xq_norm, v_xk_norm, v_w_xo, v_norm_ffn, v_w_up, v_conv_ffn, v_w_down):
    names = ("norm_mix", "w_in", "conv_gdn", "a_log", "dt_bias", "gdn_out_norm", "w_proj_gdn", "w_proj_sb", "w_out",
             "norm_x", "norm_mem", "w_xq", "w_xkv", "xq_norm", "xk_norm", "w_xo", "norm_ffn", "w_up", "conv_ffn", "w_down")
    wts = dict(zip(names, (norm_mix, w_in, conv_gdn, a_log, dt_bias, gdn_out_norm, w_proj_gdn, w_proj_sb, w_out,
                           norm_x, norm_mem, w_xq, w_xkv, xq_norm, xk_norm, w_xo, norm_ffn, w_up, conv_ffn, w_down)))
    mom = dict(zip(names, (m_norm_mix, m_w_in, m_conv_gdn, m_a_log, m_dt_bias, m_gdn_out_norm, m_w_proj_gdn, m_w_proj_sb,
                           m_w_out, m_norm_x, m_norm_mem, m_w_xq, m_w_xkv, m_xq_norm, m_xk_norm, m_w_xo, m_norm_ffn, m_w_up,
                           m_conv_ffn, m_w_down)))
    vel = dict(zip(names, (v_norm_mix, v_w_in, v_conv_gdn, v_a_log, v_dt_bias, v_gdn_out_norm, v_w_proj_gdn, v_w_proj_sb,
                           v_w_out, v_norm_x, v_norm_mem, v_w_xq, v_w_xkv, v_xq_norm, v_xk_norm, v_w_xo, v_norm_ffn, v_w_up,
                           v_conv_ffn, v_w_down)))
    cfg = CFG
    d, bsz, s = cfg.d, cfg.b, cfg.s
    t = bsz * s
    gh, sbh = cfg.gh, cfg.sbh
    gw, sw = gh * HD, sbh * HD
    nchunk = s // cfg.gch
    mx, my, mc = _my_pos()
    me = 4 * mx + 2 * my + mc

    shard_shapes = {n: tuple(wts[n].shape[1:]) for n in _BIG}

    packed_w = _pack_big_shards({n: wts[n][0] for n in _BIG})
    gathered = _all_gather_big(_cast_rows(packed_w, BF16, "cast_weights"))
    full = _unpack_gathered(gathered, shard_shapes)

    conv_rows = {n: _rows_of(wts[n][0], d) for n in _SMALL_CONV}
    conv_cnt = {n: conv_rows[n].shape[0] for n in _SMALL_CONV}
    conv_blk = _pad_rows(jnp.concatenate([conv_rows[n] for n in _SMALL_CONV], axis=0), SUBLANE)
    conv_all = jnp.zeros((N_DEV,) + conv_blk.shape, F32)
    conv_all = lax.dynamic_update_slice(conv_all, conv_blk[None], (me, 0, 0))
    conv_all = _all_reduce_small(conv_all.reshape(-1, d), "gather_conv_taps").reshape((N_DEV,) + conv_blk.shape)

    def full_conv(n, r0):
        k, cols = wts[n].shape[1], wts[n].shape[2]
        part = conv_all[:, r0:r0 + conv_cnt[n], :].reshape(N_DEV, -1)[:, :k * cols].reshape(N_DEV, k, cols)
        return part.transpose(1, 0, 2).reshape(k, N_DEV * cols)

    cgdn = full_conv("conv_gdn", 0)
    cffn = full_conv("conv_ffn", conv_cnt["conv_gdn"])

    win = full["w_in"]
    o_ab = 3 * gw
    o_z = o_ab + 2 * gh
    o_sb = o_z + gw
    o_gate = o_sb + 3 * sw
    w_qkv = win[:, :o_ab]
    w_ab = jnp.concatenate([win[:, o_ab:o_z], jnp.zeros((d, LANE - 2 * gh), win.dtype)], axis=1)
    w_z = win[:, o_z:o_sb]
    w_sb = win[:, o_sb:o_gate]
    w_gate = win[:, o_gate:]

    alog_p = jnp.concatenate([a_log.reshape(1, -1), jnp.zeros((1, LANE - gh), F32)], axis=1)
    dtb_p = jnp.concatenate([dt_bias.reshape(1, -1), jnp.zeros((1, LANE - gh), F32)], axis=1)
    onorm = gdn_out_norm.reshape(1, HD)

    x2 = x.reshape(t, d)
    tgt2 = loss_target.reshape(t, d)
    mem2 = mem.reshape(bsz * cfg.mem, d)

    (xn,) = _rowwise(_rms, [x2], [norm_mix], [(d, BF16)], [], name="norm_mix_fwd")
    p_qkv = _mm(xn, w_qkv, name="proj_qkv")
    p_ab = _mm(xn, w_ab, name="proj_ab")
    p_z = _mm(xn, w_z, name="proj_z")
    p_sb = _mm(xn, w_sb, name="proj_sb")
    p_gate = _mm(xn, w_gate, name="proj_gate")

    qkv_c = _gdn_conv_fwd(p_qkv.reshape(bsz, s, 3 * gw), cgdn)
    (gb,) = _rowwise(_gates_fn, [p_ab], [alog_p, dtb_p], [(LANE, F32)], [], name="gdn_gates_fwd")
    gbt = gb.reshape(bsz, s, LANE)[:, :, :2 * gh].transpose(0, 2, 1).reshape(bsz, 2 * gh, nchunk, 1, cfg.gch)
    o_a, states = _gdn_fwd(qkv_c, p_z.reshape(bsz, s, gw), gbt, onorm)
    o_b = _sb_fwd(p_sb.reshape(bsz, s, 3 * sw))

    pa = _mm(o_a.reshape(t, gw), full["w_proj_gdn"], name="proj_gdn_out")
    pb = _mm(o_b.reshape(t, sw), full["w_proj_sb"], name="proj_sb_out")

    def merge_fn(pa_, pb_, gate_):
        return _sigmoid(gate_[:, :d]) * pa_ + _sigmoid(gate_[:, d:]) * pb_

    (merged,) = _rowwise(merge_fn, [pa, pb, p_gate], [], [(d, BF16)], [], name="merge_fwd")
    h1 = _mm(merged, full["w_out"], add=x2, name="mixer_out")

    (hn_x,) = _rowwise(_rms, [h1], [norm_x], [(d, BF16)], [], name="norm_x_fwd")
    (mn,) = _rowwise(_rms, [mem2], [norm_mem], [(d, BF16)], [], name="norm_mem_fwd")
    q_raw = _mm(hn_x, full["w_xq"], name="xattn_q")
    kv = _mm(mn, full["w_xkv"], name="xattn_kv")
    xo = _xattn_fwd(q_raw.reshape(bsz, s, d), kv.reshape(bsz, cfg.mem, 2 * d), xq_norm, xk_norm)
    h2 = _mm(xo.reshape(t, d), full["w_xo"], add=h1, name="xattn_out")

    (hn_f,) = _rowwise(_rms, [h2], [norm_ffn], [(d, BF16)], [], name="norm_ffn_fwd")
    up = _mm(hn_f, full["w_up"], name="ffn_up")
    act = _ffn_conv_fwd(up.reshape(bsz, s, 2 * cfg.dff), cffn)
    y = _mm(act.reshape(t, cfg.dff), full["w_down"], add=h2, name="ffn_down")

    def loss_fn(y_, tg_):
        err = y_ - tg_
        part = 0.5 * jnp.sum(err * err) / d
        return err / d, jnp.full((1, LANE), part, F32)

    dy, loss_part = _rowwise(loss_fn, [y, tgt2], [], [(d, F32)], [(1, LANE)], name="loss")

    grads = {}
    dact = _mm(dy, full["w_down"], tb=True, name="d_act")
    grads["w_down"] = _mm(act.reshape(t, cfg.dff), dy, ta=True, name="dw_down")
    dup1, dup2, dcf1, dcf2 = _ffn_conv_bwd(up.reshape(bsz, s, 2 * cfg.dff), cffn, dact.reshape(bsz, s, cfg.dff))
    dup = jnp.concatenate([dup1, dup2], axis=2).reshape(t, 2 * cfg.dff)
    g_conv_ffn = jnp.concatenate([dcf1, dcf2], axis=1)
    dhn_f = _mm(dup, full["w_up"], tb=True, name="d_hn_ffn")
    grads["w_up"] = _mm(hn_f, dup, ta=True, name="dw_up")

    def norm_bwd_fn(h_, res_, dn_, g_):
        _, f = jax.vjp(_rms, h_, g_)
        dh, dg = f(dn_)
        return res_ + dh, dg

    dh2, g_norm_ffn = _rowwise(norm_bwd_fn, [h2, dy, dhn_f], [norm_ffn], [(d, F32)], [(1, d)], name="norm_ffn_bwd")

    dxo = _mm(dh2, full["w_xo"], tb=True, out_dtype=BF16, name="d_xo")
    grads["w_xo"] = _mm(xo.reshape(t, d), dh2, ta=True, name="dw_xo")
    dq_raw, dkv, g_xq_norm, g_xk_norm = _xattn_bwd(q_raw.reshape(bsz, s, d), kv.reshape(bsz, cfg.mem, 2 * d),
                                                   xq_norm, xk_norm, dxo.reshape(bsz, s, d))
    dq_raw2 = dq_raw.reshape(t, d)
    dkv2 = dkv.reshape(bsz * cfg.mem, 2 * d)
    dhn_x = _mm(dq_raw2, full["w_xq"], tb=True, name="d_hn_x")
    grads["w_xq"] = _mm(hn_x, dq_raw2, ta=True, name="dw_xq")
    dmn = _mm(dkv2, full["w_xkv"], tb=True, name="d_mn")
    grads["w_xkv"] = _mm(mn, dkv2, ta=True, name="dw_xkv")

    def norm_w_bwd_fn(h_, dn_, g_):
        _, f = jax.vjp(lambda gg: _rms(h_, gg), g_)
        return f(dn_)[0]

    (g_norm_mem,) = _rowwise(norm_w_bwd_fn, [mem2, dmn], [norm_mem], [], [(1, d)], name="norm_mem_bwd")
    dh1, g_norm_x = _rowwise(norm_bwd_fn, [h1, dh2, dhn_x], [norm_x], [(d, F32)], [(1, d)], name="norm_x_bwd")

    dmerged = _mm(dh1, full["w_out"], tb=True, name="d_merged")
    grads["w_out"] = _mm(merged, dh1, ta=True, name="dw_out")

    def merge_bwd_fn(pa_, pb_, gate_, dm_):
        _, f = jax.vjp(merge_fn, pa_, pb_, gate_)
        return f(dm_)

    dpa, dpb, dgate = _rowwise(merge_bwd_fn, [pa, pb, p_gate, dmerged], [], [(d, BF16), (d, BF16), (2 * d, BF16)], [],
                               name="merge_bwd")
    do_a = _mm(dpa, full["w_proj_gdn"], tb=True, name="d_o_gdn")
    grads["w_proj_gdn"] = _mm(o_a.reshape(t, gw), dpa, ta=True, name="dw_proj_gdn")
    do_b = _mm(dpb, full["w_proj_sb"], tb=True, name="d_o_sb")
    grads["w_proj_sb"] = _mm(o_b.reshape(t, sw), dpb, ta=True, name="dw_proj_sb")

    dsq, dsk, dsv = _sb_bwd(p_sb.reshape(bsz, s, 3 * sw), do_b.reshape(bsz, s, sw))
    dp_sb = jnp.concatenate([dsq, dsk, dsv], axis=2).reshape(t, 3 * sw)

    dgq, dgk, dgv, dz, dg, dbeta, g_onorm = _gdn_bwd(qkv_c, p_z.reshape(bsz, s, gw), gbt, onorm, states,
                                                     do_a.reshape(bsz, s, gw))
    dgb = jnp.concatenate([dg, dbeta], axis=1).reshape(bsz, 2 * gh, s).transpose(0, 2, 1)
    dgb = jnp.concatenate([dgb, jnp.zeros((bsz, s, LANE - 2 * gh), F32)], axis=2).reshape(t, LANE)

    def gates_bwd_fn(ab_, dgb_, alog_, dtb_):
        _, f = jax.vjp(_gates_fn, ab_, alog_, dtb_)
        return f(dgb_)

    dp_ab, g_alog, g_dtb = _rowwise(gates_bwd_fn, [p_ab, dgb], [alog_p, dtb_p], [(LANE, BF16)], [(1, LANE), (1, LANE)],
                                    name="gdn_gates_bwd")
    dqkv_c = jnp.concatenate([dgq, dgk, dgv], axis=2)
    dp_qkv, g_conv_gdn = _gdn_conv_bwd(p_qkv.reshape(bsz, s, 3 * gw), cgdn, dqkv_c)
    dp_qkv = dp_qkv.reshape(t, 3 * gw)
    dp_z = dz.reshape(t, gw)

    dxn = _mm(dp_qkv, w_qkv, tb=True, name="d_xn_qkv")
    dxn = _mm(dp_ab, w_ab, tb=True, add=dxn, name="d_xn_ab")
    dxn = _mm(dp_z, w_z, tb=True, add=dxn, name="d_xn_z")
    dxn = _mm(dp_sb, w_sb, tb=True, add=dxn, name="d_xn_sb")
    dxn = _mm(dgate, w_gate, tb=True, add=dxn, name="d_xn_gate")
    grads["w_in"] = jnp.concatenate([
        _mm(xn, dp_qkv, ta=True, name="dw_in_qkv"),
        _mm(xn, dp_ab, ta=True, name="dw_in_ab")[:, :2 * gh],
        _mm(xn, dp_z, ta=True, name="dw_in_z"),
        _mm(xn, dp_sb, ta=True, name="dw_in_sb"),
        _mm(xn, dgate, ta=True, name="dw_in_gate")], axis=1)
    grad_x, g_norm_mix = _rowwise(norm_bwd_fn, [x2, dh1, dxn], [norm_mix], [(d, F32)], [(1, d)], name="norm_mix_bwd")

    small_g = {"norm_mix": g_norm_mix, "norm_x": g_norm_x, "norm_mem": g_norm_mem, "norm_ffn": g_norm_ffn,
               "a_log": g_alog[:, :gh], "dt_bias": g_dtb[:, :gh], "gdn_out_norm": g_onorm,
               "xq_norm": g_xq_norm, "xk_norm": g_xk_norm}
    sm_rows = [_rows_of(small_g[n], d) for n in _SMALL_REP] + [_rows_of(loss_part, d)]
    sm_rows += [_rows_of(g_conv_gdn, d), _rows_of(g_conv_ffn, d)]
    sm_cnt = [r.shape[0] for r in sm_rows]
    sm_sum = _all_reduce_small(_pad_rows(jnp.concatenate(sm_rows, axis=0), SUBLANE), "all_reduce_small_grads")
    offs = [0]
    for cnt in sm_cnt:
        offs.append(offs[-1] + cnt)
    small_grad = {}
    for i, n in enumerate(_SMALL_REP):
        small_grad[n] = sm_sum[offs[i]:offs[i + 1]].reshape(-1)[:wts[n].size].reshape(wts[n].shape)
    loss = sm_sum[offs[len(_SMALL_REP)], 0]
    for i, n in enumerate(_SMALL_CONV):
        k, cols = wts[n].shape[1], wts[n].shape[2]
        o = offs[len(_SMALL_REP) + 1 + i]
        fullg = sm_sum[o:o + sm_cnt[len(_SMALL_REP) + 1 + i]].reshape(-1)[:k * cols * N_DEV].reshape(k, N_DEV * cols)
        small_grad[n] = lax.dynamic_slice(fullg, (0, me * cols), (k, cols)).reshape(wts[n].shape)

    small_names = _SMALL_REP + _SMALL_CONV

    def pack_small(src):
        return _pad_rows(jnp.concatenate([_rows_of(src[n], d) for n in small_names], axis=0), SUBLANE)

    sw_, sg_, sm_, sv_ = pack_small(wts), pack_small(small_grad), pack_small(mom), pack_small(vel)
    sd_, snm_, snv_ = _rowwise(_adamw_math, [sw_, sg_, sm_, sv_], [], [(d, F32)] * 3, [], name="adamw_small", tm=sw_.shape[0])

    def unpack_small(packed):
        out, r0 = {}, 0
        for n in small_names:
            cnt = _rows_of(wts[n], d).shape[0]
            out[n] = packed[r0:r0 + cnt].reshape(-1)[:wts[n].size].reshape(wts[n].shape)
            r0 += cnt
        return out

    small_delta, small_m, small_v = unpack_small(sd_), unpack_small(snm_), unpack_small(snv_)

    gpack = _pack_full_grads(grads, shard_shapes)
    recv1 = _exchange_sibling(gpack)
    s1 = _sum_sibling(gpack, recv1, jnp.reshape(mc, (1,)).astype(jnp.int32))
    recv2 = _exchange_chips(s1)
    idx = jnp.stack([me, 2 * mx + my]).astype(jnp.int32)
    pm = _pack_big_shards({n: mom[n][0] for n in _BIG})
    pv = _pack_big_shards({n: vel[n][0] for n in _BIG})
    bg, bd, bm, bv = _adamw_big(gpack, recv1, recv2, packed_w, pm, pv, idx)
    big_grad, big_delta, big_m, big_v = (_unpack_shard(a, shard_shapes) for a in (bg, bd, bm, bv))

    def pick(big, small, n):
        return big[n] if n in big else small[n]

    outs = [loss, grad_x.reshape(bsz, s, d)]
    outs += [pick(big_grad, small_grad, n) for n in names]
    outs += [pick(big_delta, small_delta, n) for n in names]
    outs += [pick(big_m, small_m, n) for n in names]
    outs += [pick(big_v, small_v, n) for n in names]
    return tuple(outs)
```

```python
import functools

import jax
import jax.numpy as jnp
from jax import lax
from jax.experimental import pallas as pl
from jax.experimental.pallas import tpu as pltpu

F32 = jnp.float32
BF16 = jnp.bfloat16
FDOT_PRECISION = lax.Precision.HIGH

LANE = 128
SUBLANE = 8
PACK_ROW_ALIGN = 16
VMEM_LIMIT = 56 * 2 ** 20
N_DEV = 8
MESH = pl.DeviceIdType.MESH

EPS = 1e-6
ADAM_LR = 0.001
ADAM_B1 = 0.9
ADAM_B2 = 0.999
ADAM_EPS = 1e-08
ADAM_WD = 0.01
ADAM_STEP = 10


class _Cfg:
    d = 1024
    b = 4
    s = 2048
    mem = 256
    gh = 8
    gch = 64
    sbh = 8
    xh = 4
    dff = 2816
    pack_tile = 240


CFG = _Cfg()
HD = 128
SB_BLK = 128
SB_HP = 4
GDN_HP = 8
MM_TILES = (1024, 1408, 704, 512, 256, 128)
MM_K_TILES = (1024, 1408, 704, 512, 256, 128)
MM_K_TILES_F32 = (512, 704, 256, 128)
CONV_CB = 256
XQ_TILE = 256


def _tile(n, prefs):
    for t in prefs:
        if n % t == 0:
            return t
    raise ValueError(f"no tile for {n}")


def _cp(sem, **kw):
    return pltpu.CompilerParams(dimension_semantics=sem, vmem_limit_bytes=VMEM_LIMIT, **kw)


def _dims(kind):
    return {"nn": (((1,), (0,)), ((), ())), "nt": (((1,), (1,)), ((), ())), "tn": (((0,), (0,)), ((), ()))}[kind]


def _raw_bdot(a, b, kind):
    return lax.dot_general(a.astype(BF16), b.astype(BF16), _dims(kind), preferred_element_type=F32)


def _raw_fdot(a, b, kind):
    return lax.dot_general(a.astype(F32), b.astype(F32), _dims(kind), precision=FDOT_PRECISION,
                           preferred_element_type=F32)


def _make_dot(raw):
    @functools.partial(jax.custom_vjp, nondiff_argnums=(2,))
    def dot(a, b, kind):
        return raw(a, b, kind)

    def fwd(a, b, kind):
        return raw(a, b, kind), (a, b)

    def bwd(kind, res, g):
        a, b = res
        if kind == "nn":
            return raw(g, b, "nt").astype(a.dtype), raw(a, g, "tn").astype(b.dtype)
        if kind == "nt":
            return raw(g, b, "nn").astype(a.dtype), raw(g, a, "tn").astype(b.dtype)
        return raw(b, g, "nt").astype(a.dtype), raw(a, g, "nn").astype(b.dtype)

    dot.defvjp(fwd, bwd)
    return dot


_bdot = _make_dot(_raw_bdot)
_fdot = _make_dot(_raw_fdot)


def _split_dot(x, m01):
    hi = x.astype(BF16)
    lo = (x - hi.astype(F32)).astype(BF16)
    return (lax.dot_general(hi, m01, _dims("nn"), preferred_element_type=F32)
            + lax.dot_general(lo, m01, _dims("nn"), preferred_element_type=F32))


_sigmoid = jax.nn.sigmoid


def _silu(x):
    return x * _sigmoid(x)


def _softplus(x):
    return jnp.maximum(x, 0.0) + jnp.log1p(jnp.exp(-jnp.abs(x)))


def _rms(x, g):
    return x * lax.rsqrt(jnp.mean(x * x, axis=-1, keepdims=True) + EPS) * g


def _iota2(shape, dim):
    return lax.broadcasted_iota(jnp.int32, shape, dim)


def _mm(a, b, *, ta=False, tb=False, add=None, out_dtype=F32, name):
    if ta:
        kd, m = a.shape
    else:
        m, kd = a.shape
    if tb:
        n, kb = b.shape
    else:
        kb, n = b.shape
    assert kd == kb, (a.shape, b.shape, ta, tb)
    tm = _tile(m, MM_TILES)
    tn = _tile(n, MM_TILES)
    wide = max(a.dtype.itemsize, b.dtype.itemsize) > 2
    tk = _tile(kd, MM_K_TILES_F32 if wide else MM_K_TILES)
    nk = kd // tk
    kind_dims = (((0 if ta else 1,), (1 if tb else 0,)), ((), ()))

    def body(*refs):
        a_ref, b_ref = refs[:2]
        add_ref = refs[2] if add is not None else None
        o_ref = refs[3 if add is not None else 2]
        part = lax.dot_general(a_ref[...].astype(BF16), b_ref[...].astype(BF16), kind_dims,
                               preferred_element_type=F32)

        def finish(r):
            if add is not None:
                r = r + add_ref[...].astype(F32)
            o_ref[...] = r.astype(o_ref.dtype)

        if nk == 1:
            finish(part)
            return
        acc = refs[-1]
        k = pl.program_id(2)

        @pl.when(k == 0)
        def _():
            acc[...] = part

        @pl.when((k > 0) & (k < nk - 1))
        def _():
            acc[...] += part

        @pl.when(k == nk - 1)
        def _():
            finish(acc[...] + part)

    a_spec = pl.BlockSpec((tk, tm), lambda i, j, k: (k, i)) if ta else pl.BlockSpec((tm, tk), lambda i, j, k: (i, k))
    b_spec = pl.BlockSpec((tn, tk), lambda i, j, k: (j, k)) if tb else pl.BlockSpec((tk, tn), lambda i, j, k: (k, j))
    in_specs = [a_spec, b_spec]
    args = [a, b]
    if add is not None:
        in_specs.append(pl.BlockSpec((tm, tn), lambda i, j, k: (i, j)))
        args.append(add)
    return pl.pallas_call(
        body, name=name, grid=(m // tm, n // tn, nk),
        in_specs=in_specs, out_specs=pl.BlockSpec((tm, tn), lambda i, j, k: (i, j)),
        out_shape=jax.ShapeDtypeStruct((m, n), out_dtype),
        scratch_shapes=[pltpu.VMEM((tm, tn), F32)] if nk > 1 else [],
        compiler_params=_cp(("parallel", "parallel", "arbitrary")),
    )(*args)


def _rowwise(fn, rows, pars, out_rows, out_accs, *, name, tm=None):
    t = rows[0].shape[0]
    if tm is None:
        tm = _tile(t, (256, 128, 64, 32, 16))
    assert t % tm == 0, (t, tm)
    n_r, n_p, n_or, n_oa = len(rows), len(pars), len(out_rows), len(out_accs)

    def body(*refs):
        r_in = refs[:n_r]
        p_in = refs[n_r:n_r + n_p]
        o_r = refs[n_r + n_p:n_r + n_p + n_or]
        o_a = refs[n_r + n_p + n_or:]
        outs = fn(*[r[...] for r in r_in], *[p[...] for p in p_in])
        if not isinstance(outs, (tuple, list)):
            outs = (outs,)
        assert len(outs) == n_or + n_oa, (name, len(outs))
        for ref, val in zip(o_r, outs[:n_or]):
            ref[...] = val.astype(ref.dtype)
        if n_oa:
            @pl.when(pl.program_id(0) == 0)
            def _():
                for ref in o_a:
                    ref[...] = jnp.zeros_like(ref)

            for ref, val in zip(o_a, outs[n_or:]):
                ref[...] += val.astype(F32)

    in_specs = [pl.BlockSpec((tm, r.shape[1]), lambda i: (i, 0)) for r in rows]
    in_specs += [pl.BlockSpec(p.shape, lambda i: (0, 0)) for p in pars]
    out_specs = [pl.BlockSpec((tm, c), lambda i: (i, 0)) for c, _ in out_rows]
    out_specs += [pl.BlockSpec(s, lambda i: (0, 0)) for s in out_accs]
    out_shape = [jax.ShapeDtypeStruct((t, c), dt) for c, dt in out_rows]
    out_shape += [jax.ShapeDtypeStruct(s, F32) for s in out_accs]
    return pl.pallas_call(
        body, name=name, grid=(t // tm,), in_specs=in_specs, out_specs=out_specs, out_shape=out_shape,
        compiler_params=_cp(("arbitrary",)),
    )(*rows, *pars)


def _shift_down(x, sh):
    rolled = pltpu.roll(x, sh, 0)
    return jnp.where(_iota2(x.shape, 0) >= sh, rolled, 0.0)


def _shift_up(x, sh):
    s = x.shape[0]
    rolled = pltpu.roll(x, s - sh, 0)
    return jnp.where(_iota2(x.shape, 0) < s - sh, rolled, 0.0)


def _conv(x, w):
    k = w.shape[0]
    y = x * w[k - 1:k, :]
    for i in range(k - 1):
        y = y + _shift_down(x, k - 1 - i) * w[i:i + 1, :]
    return y


def _conv_bwd(x, w, dy):
    k = w.shape[0]
    dx = dy * w[k - 1:k, :]
    dws = []
    for i in range(k - 1):
        dx = dx + _shift_up(dy, k - 1 - i) * w[i:i + 1, :]
        dws.append(jnp.sum(dy * _shift_down(x, k - 1 - i), axis=0, keepdims=True))
    dws.append(jnp.sum(dy * x, axis=0, keepdims=True))
    return dx, dws


def _gdn_post(y, j, nqb):
    a = _silu(y)
    sc = jnp.where(j < nqb, HD ** -0.5, 1.0).astype(F32)
    outs = []
    for h in range(y.shape[1] // HD):
        ah = a[:, h * HD:(h + 1) * HD]
        l2 = ah * lax.rsqrt(jnp.sum(ah * ah, axis=-1, keepdims=True) + EPS)
        outs.append(jnp.where(j < 2 * nqb, l2 * sc, ah))
    return jnp.concatenate(outs, axis=1) if len(outs) > 1 else outs[0]


def _gdn_conv_fwd(x, w):
    bsz, s, c3 = x.shape
    k = w.shape[0]
    nb = c3 // CONV_CB
    nqb = nb // 3

    def body(x_ref, w_ref, o_ref):
        j = pl.program_id(1)
        o_ref[0] = _gdn_post(_conv(x_ref[0], w_ref[...]), j, nqb)

    return pl.pallas_call(
        body, name="gdn_conv_fwd", grid=(bsz, nb),
        in_specs=[pl.BlockSpec((1, s, CONV_CB), lambda b, j: (b, 0, j)), pl.BlockSpec((k, CONV_CB), lambda b, j: (0, j))],
        out_specs=pl.BlockSpec((1, s, CONV_CB), lambda b, j: (b, 0, j)),
        out_shape=jax.ShapeDtypeStruct(x.shape, F32),
        compiler_params=_cp(("parallel", "parallel")),
    )(x, w)


def _gdn_conv_bwd(x, w, dout):
    bsz, s, c3 = x.shape
    k = w.shape[0]
    nb = c3 // CONV_CB
    nqb = nb // 3

    def body(x_ref, w_ref, d_ref, dx_ref, dw_ref):
        j = pl.program_id(0)
        b = pl.program_id(1)
        xv, wv = x_ref[0], w_ref[...]
        y = _conv(xv, wv)
        _, f = jax.vjp(lambda yy: _gdn_post(yy, j, nqb), y)
        (dy,) = f(d_ref[0])
        dx, dws = _conv_bwd(xv, wv, dy)
        dx_ref[0] = dx.astype(dx_ref.dtype)

        @pl.when(b == 0)
        def _():
            dw_ref[...] = jnp.zeros_like(dw_ref)

        for i in range(k):
            dw_ref[i:i + 1, :] += dws[i]

    return pl.pallas_call(
        body, name="gdn_conv_bwd", grid=(nb, bsz),
        in_specs=[pl.BlockSpec((1, s, CONV_CB), lambda j, b: (b, 0, j)), pl.BlockSpec((k, CONV_CB), lambda j, b: (0, j)),
                  pl.BlockSpec((1, s, CONV_CB), lambda j, b: (b, 0, j))],
        out_specs=[pl.BlockSpec((1, s, CONV_CB), lambda j, b: (b, 0, j)), pl.BlockSpec((k, CONV_CB), lambda j, b: (0, j))],
        out_shape=[jax.ShapeDtypeStruct(x.shape, BF16), jax.ShapeDtypeStruct(w.shape, F32)],
        compiler_params=_cp(("parallel", "arbitrary")),
    )(x, w, dout)


def _ffn_conv_fwd(up, w):
    bsz, s, c2 = up.shape
    k = w.shape[0]
    nb = (c2 // 2) // CONV_CB

    def body(x1_ref, x2_ref, w1_ref, w2_ref, o_ref):
        u1 = _conv(x1_ref[0], w1_ref[...])
        u2 = _conv(x2_ref[0], w2_ref[...])
        o_ref[0] = (_silu(u1) * u2).astype(o_ref.dtype)

    return pl.pallas_call(
        body, name="ffn_conv_fwd", grid=(bsz, nb),
        in_specs=[pl.BlockSpec((1, s, CONV_CB), lambda b, j: (b, 0, j)), pl.BlockSpec((1, s, CONV_CB), lambda b, j: (b, 0, j + nb)),
                  pl.BlockSpec((k, CONV_CB), lambda b, j: (0, j)), pl.BlockSpec((k, CONV_CB), lambda b, j: (0, j + nb))],
        out_specs=pl.BlockSpec((1, s, CONV_CB), lambda b, j: (b, 0, j)),
        out_shape=jax.ShapeDtypeStruct((bsz, s, c2 // 2), BF16),
        compiler_params=_cp(("parallel", "parallel")),
    )(up, up, w, w)


def _ffn_conv_bwd(up, w, dact):
    bsz, s, c2 = up.shape
    k = w.shape[0]
    half = c2 // 2
    nb = half // CONV_CB

    def body(x1_ref, x2_ref, w1_ref, w2_ref, d_ref, dx1_ref, dx2_ref, dw1_ref, dw2_ref):
        b = pl.program_id(1)
        x1, x2, w1, w2 = x1_ref[0], x2_ref[0], w1_ref[...], w2_ref[...]
        u1 = _conv(x1, w1)
        u2 = _conv(x2, w2)
        _, f = jax.vjp(lambda p, q: _silu(p) * q, u1, u2)
        du1, du2 = f(d_ref[0])
        dx1, dws1 = _conv_bwd(x1, w1, du1)
        dx2, dws2 = _conv_bwd(x2, w2, du2)
        dx1_ref[0] = dx1.astype(dx1_ref.dtype)
        dx2_ref[0] = dx2.astype(dx2_ref.dtype)

        @pl.when(b == 0)
        def _():
            dw1_ref[...] = jnp.zeros_like(dw1_ref)
            dw2_ref[...] = jnp.zeros_like(dw2_ref)

        for i in range(k):
            dw1_ref[i:i + 1, :] += dws1[i]
            dw2_ref[i:i + 1, :] += dws2[i]

    def blk(off):
        return pl.BlockSpec((1, s, CONV_CB), lambda j, b: (b, 0, j + off))

    def wblk(off):
        return pl.BlockSpec((k, CONV_CB), lambda j, b: (0, j + off))

    return pl.pallas_call(
        body, name="ffn_conv_bwd", grid=(nb, bsz),
        in_specs=[blk(0), blk(nb), wblk(0), wblk(nb), blk(0)],
        out_specs=[blk(0), blk(0), wblk(0), wblk(0)],
        out_shape=[jax.ShapeDtypeStruct((bsz, s, half), BF16), jax.ShapeDtypeStruct((bsz, s, half), BF16),
                   jax.ShapeDtypeStruct((k, half), F32), jax.ShapeDtypeStruct((k, half), F32)],
        compiler_params=_cp(("parallel", "arbitrary")),
    )(up, up, w, w, dact)


@jax.custom_vjp
def _inv_unit_lower(mats):
    c = mats[0].shape[0]
    eye = (_iota2((c, c), 0) == _iota2((c, c), 1)).astype(F32)
    ps = [-a for a in mats]
    ts = [eye + p for p in ps]
    n = 2
    while n < c:
        ps = [_raw_fdot(p, p, "nn") for p in ps]
        ts = [t + _raw_fdot(t, p, "nn") for t, p in zip(ts, ps)]
        n *= 2
    return ts


def _inv_fwd(mats):
    ts = _inv_unit_lower(mats)
    return ts, ts


def _inv_bwd(ts, gs):
    xs = [_raw_fdot(g, t, "nt") for g, t in zip(gs, ts)]
    return ([-_raw_fdot(t, x, "tn") for t, x in zip(ts, xs)],)


_inv_unit_lower.defvjp(_inv_fwd, _inv_bwd)


def _gdn_chunk(q, k, v, z, g_row, beta_row, state, onorm):
    nh = range(len(q))
    c = q[0].shape[0]
    ii, jj = _iota2((c, c), 0), _iota2((c, c), 1)
    incl, strict, eye = ii >= jj, ii > jj, ii == jj

    def to_col(row):
        return jnp.sum(jnp.where(eye, jnp.broadcast_to(row, (c, c)), 0.0), axis=1, keepdims=True)

    gc_col = [jnp.sum(jnp.where(incl, jnp.broadcast_to(g_row[h], (c, c)), 0.0), axis=1, keepdims=True) for h in nh]
    gc_row = [jnp.sum(jnp.where(eye, jnp.broadcast_to(gc_col[h], (c, c)), 0.0), axis=0, keepdims=True) for h in nh]
    beta_col = [to_col(beta_row[h]) for h in nh]
    gc_last = [jnp.sum(g_row[h], axis=1, keepdims=True) for h in nh]
    decay = [jnp.where(incl, jnp.exp(jnp.where(incl, gc_col[h] - gc_row[h], 0.0)), 0.0) for h in nh]
    kk = [_bdot(k[h], k[h], "nt") for h in nh]
    qk = [_bdot(q[h], k[h], "nt") * decay[h] for h in nh]
    tinv = _inv_unit_lower([jnp.where(strict, beta_col[h] * kk[h] * decay[h], 0.0) for h in nh])
    rhs = [jnp.concatenate([v[h] * beta_col[h], k[h] * (beta_col[h] * jnp.exp(gc_col[h]))], axis=1) for h in nh]
    uw = [_fdot(tinv[h], rhs[h], "nn") for h in nh]
    dv = v[0].shape[1]
    ws = [_bdot(uw[h][:, dv:], state[h], "nn") for h in nh]
    qs = [_bdot(q[h] * jnp.exp(gc_col[h]), state[h], "nn") for h in nh]
    v_new = [uw[h][:, :dv] - ws[h] for h in nh]
    o = [qs[h] + _bdot(qk[h], v_new[h], "nn") for h in nh]
    kv = [_bdot(k[h] * jnp.exp(gc_last[h] - gc_col[h]), v_new[h], "tn") for h in nh]
    new_state = [state[h] * jnp.exp(gc_last[h]) + kv[h] for h in nh]
    y = [_rms(o[h], onorm) * _silu(z[h]) for h in nh]
    return y, new_state


def _gdn_specs(s, c, reverse):
    n = s // c
    nn = (lambda i: n - 1 - i) if reverse else (lambda i: i)

    def qkv(off):
        return pl.BlockSpec((1, c, GDN_HP * HD), lambda b, h, i: (b, nn(i), h + off))

    def gate(off):
        return pl.BlockSpec((1, GDN_HP, 1, 1, c), lambda b, h, i: (b, h + off, nn(i), 0, 0))

    st = pl.BlockSpec((1, GDN_HP, 1, HD, HD), lambda b, h, i: (b, h, nn(i), 0, 0))
    onorm = pl.BlockSpec((1, HD), lambda b, h, i: (0, 0))
    return n, qkv, gate, st, onorm


def _gdn_fwd(qkv, z, gbt, onorm):
    bsz, s, _ = qkv.shape
    gh, c = CFG.gh, CFG.gch
    ng = gh // GDN_HP
    n, qs, gs, st, on = _gdn_specs(s, c, False)

    def body(q_ref, k_ref, v_ref, z_ref, g_ref, b_ref, on_ref, y_ref, st_ref, state):
        @pl.when(pl.program_id(2) == 0)
        def _():
            state[...] = jnp.zeros_like(state)

        nh = range(GDN_HP)
        hs = [slice(h * HD, (h + 1) * HD) for h in nh]
        s_in = [state[h] for h in nh]
        for h in nh:
            st_ref[0, h, 0] = s_in[h]
        y, s_out = _gdn_chunk([q_ref[0, :, hs[h]] for h in nh], [k_ref[0, :, hs[h]] for h in nh],
                              [v_ref[0, :, hs[h]] for h in nh], [z_ref[0, :, hs[h]] for h in nh],
                              [g_ref[0, h, 0] for h in nh], [b_ref[0, h, 0] for h in nh], s_in, on_ref[...])
        for h in nh:
            y_ref[0, :, hs[h]] = y[h].astype(y_ref.dtype)
            state[h] = s_out[h]

    return pl.pallas_call(
        body, name="gdn_fwd", grid=(bsz, ng, n),
        in_specs=[qs(0), qs(ng), qs(2 * ng), qs(0), gs(0), gs(ng), on],
        out_specs=[qs(0), st],
        out_shape=[jax.ShapeDtypeStruct((bsz, s, gh * HD), BF16), jax.ShapeDtypeStruct((bsz, gh, n, HD, HD), F32)],
        scratch_shapes=[pltpu.VMEM((GDN_HP, HD, HD), F32)],
        compiler_params=_cp(("parallel", "parallel", "arbitrary")),
    )(qkv, qkv, qkv, z, gbt, gbt, onorm)


def _gdn_bwd(qkv, z, gbt, onorm, states, dy):
    bsz, s, _ = qkv.shape
    gh, c = CFG.gh, CFG.gch
    ng = gh // GDN_HP
    n, qs, gs, st, on = _gdn_specs(s, c, True)

    def body(q_ref, k_ref, v_ref, z_ref, g_ref, b_ref, on_ref, st_ref, dy_ref,
             dq_ref, dk_ref, dv_ref, dz_ref, dg_ref, db_ref, don_ref, dstate):
        first = (pl.program_id(0) == 0) & (pl.program_id(1) == 0) & (pl.program_id(2) == 0)

        @pl.when(first)
        def _():
            don_ref[...] = jnp.zeros_like(don_ref)

        @pl.when(pl.program_id(2) == 0)
        def _():
            dstate[...] = jnp.zeros_like(dstate)

        nh = range(GDN_HP)
        hs = [slice(h * HD, (h + 1) * HD) for h in nh]
        _, f = jax.vjp(_gdn_chunk, [q_ref[0, :, hs[h]] for h in nh], [k_ref[0, :, hs[h]] for h in nh],
                       [v_ref[0, :, hs[h]] for h in nh], [z_ref[0, :, hs[h]] for h in nh],
                       [g_ref[0, h, 0] for h in nh], [b_ref[0, h, 0] for h in nh],
                       [st_ref[0, h, 0] for h in nh], on_ref[...])
        dq, dk, dv, dz, dg, db, ds, don = f(([dy_ref[0, :, hs[h]] for h in nh], [dstate[h] for h in nh]))
        for h in nh:
            dq_ref[0, :, hs[h]] = dq[h]
            dk_ref[0, :, hs[h]] = dk[h]
            dv_ref[0, :, hs[h]] = dv[h]
            dz_ref[0, :, hs[h]] = dz[h].astype(dz_ref.dtype)
            dg_ref[0, h, 0] = dg[h]
            db_ref[0, h, 0] = db[h]
            dstate[h] = ds[h]
        don_ref[...] += don

    act = jax.ShapeDtypeStruct((bsz, s, gh * HD), F32)
    gshape = jax.ShapeDtypeStruct((bsz, gh, n, 1, c), F32)
    return pl.pallas_call(
        body, name="gdn_bwd", grid=(bsz, ng, n),
        in_specs=[qs(0), qs(ng), qs(2 * ng), qs(0), gs(0), gs(ng), on, st, qs(0)],
        out_specs=[qs(0), qs(0), qs(0), qs(0), gs(0), gs(0), on],
        out_shape=[act, act, act, jax.ShapeDtypeStruct(act.shape, BF16), gshape, gshape, jax.ShapeDtypeStruct((1, HD), F32)],
        scratch_shapes=[pltpu.VMEM((GDN_HP, HD, HD), F32)],
        compiler_params=_cp(("arbitrary", "arbitrary", "arbitrary")),
    )(qkv, qkv, qkv, z, gbt, gbt, onorm, states, dy)


def _gates_fn(ab, alog, dtb):
    lane = _iota2(ab.shape, 1)
    g = -jnp.exp(alog) * _softplus(ab + dtb)
    beta = _sigmoid(ab)
    return jnp.where(lane < CFG.gh, g, jnp.where(lane < 2 * CFG.gh, beta, 0.0))


def _heads_cumsum(xs, tri):
    n = xs[0].shape[0]
    y = _split_dot(jnp.concatenate(xs, axis=0), tri)
    return [y[h * n:(h + 1) * n] for h in range(len(xs))]


def _blk_off(jblk):
    return jblk * SB_BLK if isinstance(jblk, int) else pl.multiple_of(jblk * SB_BLK, SB_BLK)


def _sb_span(qs, k_spans, mask, runs, tri_su):
    nh = range(len(qs))
    nb = k_spans[0].shape[0] // SB_BLK
    zs = [lax.dot_general(qs[h], k_spans[h], _dims("nt"), preferred_element_type=F32) for h in nh]
    l1p = [jnp.log1p(jnp.exp(-jnp.abs(z))) for z in zs]
    lss = [jnp.minimum(zs[h], 0.0) - l1p[h] for h in nh]
    lfs = [jnp.minimum(-zs[h], 0.0) - l1p[h] for h in nh]
    if mask is not None:
        lfs = [jnp.where(mask, lf, 0.0) for lf in lfs]
    units = [lfs[h][:, b * SB_BLK:(b + 1) * SB_BLK] for h in nh for b in range(nb)]
    cums = _heads_cumsum(units, tri_su)
    sfx, new_runs = [], []
    for h in nh:
        run, parts = runs[h], [None] * nb
        for b in reversed(range(nb)):
            parts[b] = cums[h * nb + b] + run
            run = run + jnp.sum(units[h * nb + b], axis=1, keepdims=True)
        sfx.append(jnp.concatenate(parts, axis=1) if nb > 1 else parts[0])
        new_runs.append(run)
    ws = [jnp.exp(lss[h] + sfx[h]) for h in nh]
    if mask is not None:
        ws = [jnp.where(mask, w, 0.0) for w in ws]
    return zs, lfs, ws, new_runs


def _sb_specs(s, w):
    def qb(off):
        return pl.BlockSpec((1, SB_BLK, w), lambda b, h, i: (b, i, h + off))

    def full(off):
        return pl.BlockSpec((1, s, w), lambda b, h, i: (b, 0, h + off))

    return qb, full


def _sb_fwd(qkv):
    bsz, s, _ = qkv.shape
    ng = CFG.sbh // SB_HP
    w = SB_HP * HD
    scale = HD ** -0.5
    qb, full = _sb_specs(s, w)

    def body(q_ref, k_ref, v_ref, o_ref):
        i = pl.program_id(2)
        r, c = _iota2((SB_BLK, SB_BLK), 0), _iota2((SB_BLK, SB_BLK), 1)
        tri_su = (r > c).astype(BF16)
        nh = range(SB_HP)
        hs = [slice(h * HD, (h + 1) * HD) for h in nh]
        qs = [(q_ref[0, :, hs[h]] * scale).astype(BF16) for h in nh]

        def span(off, nb, mask, carry):
            ks = [k_ref[0, pl.ds(off, nb * SB_BLK), hs[h]].astype(BF16) for h in nh]
            vs = [v_ref[0, pl.ds(off, nb * SB_BLK), hs[h]].astype(BF16) for h in nh]
            _, _, ws, runs = _sb_span(qs, ks, mask, [cr[1] for cr in carry], tri_su)
            pv = [lax.dot_general(ws[h].astype(BF16), vs[h], _dims("nn"), preferred_element_type=F32) for h in nh]
            return tuple((carry[h][0] + pv[h], runs[h]) for h in nh)

        carry = tuple((jnp.zeros((SB_BLK, HD), F32), jnp.zeros((SB_BLK, 1), F32)) for _ in nh)
        carry = span(pl.multiple_of(i * SB_BLK, SB_BLK), 1, c < r, carry)
        carry = lax.fori_loop(
            0, lax.shift_right_logical(i, 1),
            lambda p, cr: span(pl.multiple_of((i - 2 - 2 * p) * SB_BLK, SB_BLK), 2, None, cr), carry)
        carry = lax.fori_loop(0, jnp.bitwise_and(i, 1), lambda _, cr: span(0, 1, None, cr), carry)
        for h in nh:
            o_ref[0, :, hs[h]] = carry[h][0].astype(o_ref.dtype)

    return pl.pallas_call(
        body, name="sb_fwd", grid=(bsz, ng, s // SB_BLK),
        in_specs=[qb(0), full(ng), full(2 * ng)], out_specs=qb(0),
        out_shape=jax.ShapeDtypeStruct((bsz, s, CFG.sbh * HD), BF16),
        compiler_params=_cp(("parallel", "parallel", "arbitrary")),
    )(qkv, qkv, qkv)


def _sb_bwd(qkv, do):
    bsz, s, _ = qkv.shape
    ng = CFG.sbh // SB_HP
    w = SB_HP * HD
    nblk = s // SB_BLK
    scale = HD ** -0.5
    qb, full = _sb_specs(s, w)

    def body(q_ref, k_ref, v_ref, do_ref, dq_ref, dk_ref, dv_ref, dk_acc, dv_acc, dl_pan, z_pan):
        i = pl.program_id(2)

        @pl.when(i == 0)
        def _():
            dk_acc[...] = jnp.zeros_like(dk_acc)
            dv_acc[...] = jnp.zeros_like(dv_acc)

        r, c = _iota2((SB_BLK, SB_BLK), 0), _iota2((SB_BLK, SB_BLK), 1)
        tri_su = (r > c).astype(BF16)
        tri_pre = (r < c).astype(BF16)
        nh = range(SB_HP)
        hs = [slice(h * HD, (h + 1) * HD) for h in nh]
        qs = [(q_ref[0, :, hs[h]] * scale).astype(BF16) for h in nh]
        dob = [do_ref[0, :, hs[h]].astype(BF16) for h in nh]
        half = lax.shift_right_logical(i, 1)
        odd = jnp.bitwise_and(i, 1)

        def span_a(jblk, nb, mask, runs):
            rows = pl.ds(_blk_off(jblk), nb * SB_BLK)
            ks = [k_ref[0, rows, hs[h]].astype(BF16) for h in nh]
            vs = [v_ref[0, rows, hs[h]].astype(BF16) for h in nh]
            dws = [lax.dot_general(dob[h], vs[h], _dims("nt"), preferred_element_type=F32) for h in nh]
            zs, _, ws, runs = _sb_span(qs, ks, mask, runs, tri_su)
            dvs = [lax.dot_general(ws[h].astype(BF16), dob[h], _dims("tn"), preferred_element_type=F32) for h in nh]
            for h in nh:
                dl = dws[h] * ws[h]
                for b in range(nb):
                    dl_pan[h, jblk + b] = dl[:, b * SB_BLK:(b + 1) * SB_BLK]
                    z_pan[h, jblk + b] = zs[h][:, b * SB_BLK:(b + 1) * SB_BLK]
                dv_acc[rows, hs[h]] += dvs[h]
            return tuple(runs)

        runs = tuple(jnp.zeros((SB_BLK, 1), F32) for _ in nh)
        runs = span_a(i, 1, c < r, runs)
        runs = lax.fori_loop(0, half, lambda p, rn: span_a(i - 2 - 2 * p, 2, None, rn), runs)
        lax.fori_loop(0, odd, lambda _, rn: span_a(0, 1, None, rn), runs)

        def span_b(jblk, nb, mask, carry):
            rows = pl.ds(_blk_off(jblk), nb * SB_BLK)
            ks = [k_ref[0, rows, hs[h]].astype(BF16) for h in nh]
            units = [dl_pan[h, jblk + b] for h in nh for b in range(nb)]
            sgs = [_sigmoid(z_pan[h, jblk + b]) for h in nh for b in range(nb)]
            cums = _heads_cumsum(units, tri_pre)
            dzs, pres = [], []
            for h in nh:
                pre, parts = carry[h][1], []
                for b in range(nb):
                    u, sg = units[h * nb + b], sgs[h * nb + b]
                    parts.append(u * (1.0 - sg) - sg * (cums[h * nb + b] + pre))
                    pre = pre + jnp.sum(u, axis=1, keepdims=True)
                dz = jnp.concatenate(parts, axis=1) if nb > 1 else parts[0]
                if mask is not None:
                    dz = jnp.where(mask, dz, 0.0)
                dzs.append(dz.astype(BF16))
                pres.append(pre)
            dqs = [lax.dot_general(dzs[h], ks[h], _dims("nn"), preferred_element_type=F32) for h in nh]
            dks = [lax.dot_general(dzs[h], qs[h], _dims("tn"), preferred_element_type=F32) for h in nh]
            for h in nh:
                dk_acc[rows, hs[h]] += dks[h]
            return tuple((carry[h][0] + dqs[h], pres[h]) for h in nh)

        carry = tuple((jnp.zeros((SB_BLK, HD), F32), jnp.zeros((SB_BLK, 1), F32)) for _ in nh)
        carry = lax.fori_loop(0, odd, lambda _, cr: span_b(0, 1, None, cr), carry)
        carry = lax.fori_loop(0, half, lambda p, cr: span_b(odd + 2 * p, 2, None, cr), carry)
        carry = span_b(i, 1, c < r, carry)
        for h in nh:
            dq_ref[0, :, hs[h]] = (carry[h][0] * scale).astype(dq_ref.dtype)

        @pl.when(i == nblk - 1)
        def _():
            dk_ref[0] = dk_acc[...].astype(dk_ref.dtype)
            dv_ref[0] = dv_acc[...].astype(dv_ref.dtype)

    out = jax.ShapeDtypeStruct((bsz, s, CFG.sbh * HD), BF16)
    return pl.pallas_call(
        body, name="sb_bwd", grid=(bsz, ng, nblk),
        in_specs=[qb(0), full(ng), full(2 * ng), qb(0)],
        out_specs=[qb(0), full(0), full(0)],
        out_shape=[out, out, out],
        scratch_shapes=[pltpu.VMEM((s, w), F32), pltpu.VMEM((s, w), F32),
                        pltpu.VMEM((SB_HP, nblk, SB_BLK, SB_BLK), F32), pltpu.VMEM((SB_HP, nblk, SB_BLK, SB_BLK), F32)],
        compiler_params=_cp(("parallel", "parallel", "arbitrary")),
    )(qkv, qkv, qkv, do)


def _xattn_fn(q_raw, kv, qn, kn):
    d = q_raw.shape[1]
    dh = d // CFG.xh
    outs = []
    for h in range(CFG.xh):
        qh = _rms(q_raw[:, h * dh:(h + 1) * dh], qn)
        kh = _rms(kv[:, h * dh:(h + 1) * dh], kn)
        vh = kv[:, d + h * dh:d + (h + 1) * dh]
        sc = _bdot(qh, kh, "nt") * (dh ** -0.5)
        sc = sc - lax.stop_gradient(jnp.max(sc, axis=-1, keepdims=True))
        e = jnp.exp(sc)
        p = e / jnp.sum(e, axis=-1, keepdims=True)
        outs.append(_bdot(p, vh, "nn"))
    return jnp.concatenate(outs, axis=1)


def _xattn_fwd(q_raw, kv, qn, kn):
    bsz, s, d = q_raw.shape
    m = kv.shape[1]
    tq = _tile(s, (XQ_TILE, 128))

    def body(q_ref, kv_ref, qn_ref, kn_ref, o_ref):
        o_ref[0] = _xattn_fn(q_ref[0], kv_ref[0], qn_ref[...], kn_ref[...]).astype(o_ref.dtype)

    return pl.pallas_call(
        body, name="xattn_fwd", grid=(bsz, s // tq),
        in_specs=[pl.BlockSpec((1, tq, d), lambda b, i: (b, i, 0)), pl.BlockSpec((1, m, 2 * d), lambda b, i: (b, 0, 0)),
                  pl.BlockSpec(qn.shape, lambda b, i: (0, 0)), pl.BlockSpec(kn.shape, lambda b, i: (0, 0))],
        out_specs=pl.BlockSpec((1, tq, d), lambda b, i: (b, i, 0)),
        out_shape=jax.ShapeDtypeStruct((bsz, s, d), BF16),
        compiler_params=_cp(("parallel", "parallel")),
    )(q_raw, kv, qn, kn)


def _xattn_bwd(q_raw, kv, qn, kn, do):
    bsz, s, d = q_raw.shape
    m = kv.shape[1]
    tq = _tile(s, (XQ_TILE, 128))

    def body(q_ref, kv_ref, qn_ref, kn_ref, do_ref, dq_ref, dkv_ref, dqn_ref, dkn_ref):
        b, i = pl.program_id(0), pl.program_id(1)

        @pl.when((b == 0) & (i == 0))
        def _():
            dqn_ref[...] = jnp.zeros_like(dqn_ref)
            dkn_ref[...] = jnp.zeros_like(dkn_ref)

        @pl.when(i == 0)
        def _():
            dkv_ref[...] = jnp.zeros_like(dkv_ref)

        _, f = jax.vjp(_xattn_fn, q_ref[0], kv_ref[0], qn_ref[...], kn_ref[...])
        dq, dkv, dqn, dkn = f(do_ref[0].astype(F32))
        dq_ref[0] = dq.astype(dq_ref.dtype)
        dkv_ref[0] += dkv
        dqn_ref[...] += dqn
        dkn_ref[...] += dkn

    return pl.pallas_call(
        body, name="xattn_bwd", grid=(bsz, s // tq),
        in_specs=[pl.BlockSpec((1, tq, d), lambda b, i: (b, i, 0)), pl.BlockSpec((1, m, 2 * d), lambda b, i: (b, 0, 0)),
                  pl.BlockSpec(qn.shape, lambda b, i: (0, 0)), pl.BlockSpec(kn.shape, lambda b, i: (0, 0)),
                  pl.BlockSpec((1, tq, d), lambda b, i: (b, i, 0))],
        out_specs=[pl.BlockSpec((1, tq, d), lambda b, i: (b, i, 0)), pl.BlockSpec((1, m, 2 * d), lambda b, i: (b, 0, 0)),
                   pl.BlockSpec(qn.shape, lambda b, i: (0, 0)), pl.BlockSpec(kn.shape, lambda b, i: (0, 0))],
        out_shape=[jax.ShapeDtypeStruct((bsz, s, d), BF16), jax.ShapeDtypeStruct(kv.shape, F32),
                   jax.ShapeDtypeStruct(qn.shape, F32), jax.ShapeDtypeStruct(kn.shape, F32)],
        compiler_params=_cp(("arbitrary", "arbitrary")),
    )(q_raw, kv, qn, kn, do)


def _my_pos():
    return lax.axis_index("x"), lax.axis_index("y"), lax.axis_index("c")


def _all_gather_big(shard):
    r, d = shard.shape

    def body(x_ref, out_ref, send_sems, recv_sems, local_sem):
        x, y, c = _my_pos()
        me, sibling = (x, y, c), (x, y, 1 - c)
        chips = [(1 - x, y), (x, 1 - y), (1 - x, 1 - y)]

        def slot(px, py, pc):
            return out_ref.at[4 * px + 2 * py + pc]

        def copy(k, block, to, src=None):
            return pltpu.make_async_remote_copy(
                src_ref=slot(*block) if src is None else src, dst_ref=slot(*block),
                send_sem=send_sems.at[k], recv_sem=recv_sems.at[k], device_id=to, device_id_type=MESH)

        mine = pltpu.make_async_copy(x_ref, slot(*me), local_sem)
        mine.start()
        first = [copy(0, me, sibling, src=x_ref)]
        first += [copy(1 + j, me, (*chip, c), src=x_ref) for j, chip in enumerate(chips)]
        for cp in first:
            cp.start()
        passed = [copy(4 + j, (*chip, c), sibling) for j, chip in enumerate(chips)]
        for j, chip in enumerate(chips):
            copy(1 + j, (*chip, c), me).wait_recv()
            passed[j].start()
        copy(0, sibling, me).wait_recv()
        for j, chip in enumerate(chips):
            copy(4 + j, (*chip, 1 - c), me).wait_recv()
        for cp in first + passed:
            cp.wait_send()
        mine.wait()

    return pl.pallas_call(
        body, name="all_gather_weights",
        out_shape=jax.ShapeDtypeStruct((N_DEV, r, d), shard.dtype),
        in_specs=[pl.BlockSpec(memory_space=pl.ANY)], out_specs=pl.BlockSpec(memory_space=pl.ANY),
        scratch_shapes=[pltpu.SemaphoreType.DMA((7,)), pltpu.SemaphoreType.DMA((7,)), pltpu.SemaphoreType.DMA],
    )(shard)


def _exchange_sibling(g):
    _, r, d = g.shape

    def body(g_ref, out_ref, send_sems, recv_sems):
        x, y, c = _my_pos()
        copies = [pltpu.make_async_remote_copy(
            src_ref=g_ref.at[2 * k + (1 - c)], dst_ref=out_ref.at[k],
            send_sem=send_sems.at[k], recv_sem=recv_sems.at[k], device_id=(x, y, 1 - c), device_id_type=MESH)
            for k in range(4)]
        for cp in copies:
            cp.start()
        for cp in copies:
            cp.wait_recv()
        for cp in copies:
            cp.wait_send()

    return pl.pallas_call(
        body, name="grads_to_sibling",
        out_shape=jax.ShapeDtypeStruct((4, r, d), g.dtype),
        in_specs=[pl.BlockSpec(memory_space=pl.ANY)], out_specs=pl.BlockSpec(memory_space=pl.ANY),
        scratch_shapes=[pltpu.SemaphoreType.DMA((4,)), pltpu.SemaphoreType.DMA((4,))],
    )(g)


def _exchange_chips(s1):
    _, r, d = s1.shape

    def body(s_ref, out_ref, send_sems, recv_sems):
        x, y, c = _my_pos()
        copies = []
        for rel in (1, 2, 3):
            px = jnp.bitwise_xor(x, rel >> 1)
            py = jnp.bitwise_xor(y, rel & 1)
            copies.append(pltpu.make_async_remote_copy(
                src_ref=s_ref.at[2 * px + py], dst_ref=out_ref.at[rel - 1],
                send_sem=send_sems.at[rel - 1], recv_sem=recv_sems.at[rel - 1],
                device_id=(px, py, c), device_id_type=MESH))
        for cp in copies:
            cp.start()
        for cp in copies:
            cp.wait_recv()
        for cp in copies:
            cp.wait_send()

    return pl.pallas_call(
        body, name="grads_to_chips",
        out_shape=jax.ShapeDtypeStruct((3, r, d), s1.dtype),
        in_specs=[pl.BlockSpec(memory_space=pl.ANY)], out_specs=pl.BlockSpec(memory_space=pl.ANY),
        scratch_shapes=[pltpu.SemaphoreType.DMA((3,)), pltpu.SemaphoreType.DMA((3,))],
    )(s1)


def _all_reduce_small(blk, name):
    rows, d = blk.shape

    def body(x_ref, out_ref, land, send_sems, recv_sems):
        x, y, c = _my_pos()
        me = 4 * x + 2 * y + c
        copies = []
        for rel in range(1, N_DEV):
            peer = (jnp.bitwise_xor(x, rel >> 2), jnp.bitwise_xor(y, (rel >> 1) & 1), jnp.bitwise_xor(c, rel & 1))
            copies.append(pltpu.make_async_remote_copy(
                src_ref=x_ref, dst_ref=land.at[rel - 1], send_sem=send_sems.at[rel - 1], recv_sem=recv_sems.at[rel - 1],
                device_id=peer, device_id_type=MESH))
        for cp in copies:
            cp.start()
        for cp in copies:
            cp.wait_recv()
        acc = jnp.zeros((rows, d), F32)
        for dev in range(N_DEV):
            rel = jnp.bitwise_xor(me, dev)
            got = land[jnp.maximum(rel - 1, 0)]
            acc = acc + jnp.where(rel == 0, x_ref[...], got)
        out_ref[...] = acc
        for cp in copies:
            cp.wait_send()

    return pl.pallas_call(
        body, name=name,
        out_shape=jax.ShapeDtypeStruct((rows, d), F32),
        in_specs=[pl.BlockSpec(memory_space=pltpu.VMEM)], out_specs=pl.BlockSpec(memory_space=pltpu.VMEM),
        scratch_shapes=[pltpu.VMEM((N_DEV - 1, rows, d), F32), pltpu.SemaphoreType.DMA((N_DEV - 1,)),
                        pltpu.SemaphoreType.DMA((N_DEV - 1,))],
    )(blk)


def _cast_rows(x, dtype, name):
    return _rowwise(lambda v: v, [x], [], [(x.shape[1], dtype)], [], name=name, tm=CFG.pack_tile)[0]


def _sum_sibling(g, recv1, c_idx):
    _, r, d = g.shape
    tm = CFG.pack_tile

    def body(c_ref, g_ref, r_ref, o_ref):
        o_ref[0] = (g_ref[0] + r_ref[0]).astype(o_ref.dtype)

    grid_spec = pltpu.PrefetchScalarGridSpec(
        num_scalar_prefetch=1, grid=(4, r // tm),
        in_specs=[pl.BlockSpec((1, tm, d), lambda k, i, c_ref: (2 * k + c_ref[0], i, 0)),
                  pl.BlockSpec((1, tm, d), lambda k, i, c_ref: (k, i, 0))],
        out_specs=pl.BlockSpec((1, tm, d), lambda k, i, c_ref: (k, i, 0)))
    return pl.pallas_call(
        body, name="sum_sibling", grid_spec=grid_spec,
        out_shape=jax.ShapeDtypeStruct((4, r, d), BF16),
        compiler_params=_cp(("parallel", "parallel")),
    )(c_idx, g, recv1)


def _adamw_math(w, g, m, v):
    m2 = ADAM_B1 * m + (1.0 - ADAM_B1) * g
    v2 = ADAM_B2 * v + (1.0 - ADAM_B2) * (g * g)
    m_hat = m2 / (1.0 - ADAM_B1 ** ADAM_STEP)
    v_hat = v2 / (1.0 - ADAM_B2 ** ADAM_STEP)
    delta = -ADAM_LR * (m_hat / (jnp.sqrt(v_hat) + ADAM_EPS) + ADAM_WD * w)
    return delta, m2, v2


def _adamw_big(g, recv1, recv2, w, m, v, idx):
    _, r, d = g.shape
    tm = CFG.pack_tile

    def body(idx_ref, g_ref, r1_ref, ra_ref, rb_ref, rc_ref, w_ref, m_ref, v_ref, og, od, om, ov):
        grad = (g_ref[0] + r1_ref[0]) + ra_ref[0].astype(F32) + rb_ref[0].astype(F32) + rc_ref[0].astype(F32)
        delta, m2, v2 = _adamw_math(w_ref[...], grad, m_ref[...], v_ref[...])
        og[...] = grad
        od[...] = delta
        om[...] = m2
        ov[...] = v2

    flat = pl.BlockSpec((tm, d), lambda i, idx_ref: (i, 0))
    grid_spec = pltpu.PrefetchScalarGridSpec(
        num_scalar_prefetch=1, grid=(r // tm,),
        in_specs=[pl.BlockSpec((1, tm, d), lambda i, idx_ref: (idx_ref[0], i, 0)),
                  pl.BlockSpec((1, tm, d), lambda i, idx_ref: (idx_ref[1], i, 0)),
                  pl.BlockSpec((1, tm, d), lambda i, idx_ref: (0, i, 0)),
                  pl.BlockSpec((1, tm, d), lambda i, idx_ref: (1, i, 0)),
                  pl.BlockSpec((1, tm, d), lambda i, idx_ref: (2, i, 0)),
                  flat, flat, flat],
        out_specs=[flat, flat, flat, flat])
    shp = jax.ShapeDtypeStruct((r, d), F32)
    return pl.pallas_call(
        body, name="adamw_big", grid_spec=grid_spec, out_shape=[shp, shp, shp, shp],
        compiler_params=_cp(("parallel",)),
    )(idx, g, recv1, recv2, recv2, recv2, w, m, v)


def _rows_of(v, d):
    flat = v.reshape(-1)
    pad = (-flat.shape[0]) % d
    if pad:
        flat = jnp.concatenate([flat, jnp.zeros((pad,), flat.dtype)])
    return flat.reshape(-1, d)


def _pad_rows(a, mult):
    pad = (-a.shape[0]) % mult
    if pad:
        a = jnp.concatenate([a, jnp.zeros((pad,) + a.shape[1:], a.dtype)], axis=0)
    return a


_BIG = ("w_in", "w_xkv", "w_up", "w_proj_gdn", "w_proj_sb", "w_out", "w_xq", "w_xo", "w_down")
_COL_SHARDED = ("w_in", "w_xkv", "w_up")
_SMALL_REP = ("norm_mix", "norm_x", "norm_mem", "norm_ffn", "a_log", "dt_bias", "gdn_out_norm", "xq_norm", "xk_norm")
_SMALL_CONV = ("conv_gdn", "conv_ffn")


def _part_rows(shapes):
    out = []
    for n in _BIG:
        rows, cols = shapes[n]
        cnt = cols if n in _COL_SHARDED else rows
        out.append((cnt, cnt + (-cnt) % PACK_ROW_ALIGN))
    return out


def _pack_big_shards(shards, shapes):
    parts = []
    for n, (_, padded) in zip(_BIG, _part_rows(shapes)):
        parts.append(_pad_rows(shards[n].T if n in _COL_SHARDED else shards[n], padded))
    return _pad_rows(jnp.concatenate(parts, axis=0), CFG.pack_tile)


def _unpack_gathered(gath, shapes):
    out, r0 = {}, 0
    for n, (cnt, padded) in zip(_BIG, _part_rows(shapes)):
        out[n] = gath[:, r0:r0 + cnt, :].reshape(N_DEV * cnt, gath.shape[2])
        r0 += padded
    return out


def _pack_full_grads(grads, shapes):
    d = CFG.d
    parts = []
    for n, (cnt, padded) in zip(_BIG, _part_rows(shapes)):
        g = grads[n].reshape(N_DEV, cnt, d)
        if padded > cnt:
            g = jnp.concatenate([g, jnp.zeros((N_DEV, padded - cnt, d), g.dtype)], axis=1)
        parts.append(g)
    full = jnp.concatenate(parts, axis=1)
    pad = (-full.shape[1]) % CFG.pack_tile
    if pad:
        full = jnp.concatenate([full, jnp.zeros((N_DEV, pad, d), full.dtype)], axis=1)
    return full


def _unpack_shard(packed, shapes):
    out, r0 = {}, 0
    for n, (cnt, padded) in zip(_BIG, _part_rows(shapes)):
        part = packed[r0:r0 + cnt]
        out[n] = (part.T if n in _COL_SHARDED else part).reshape((1,) + tuple(shapes[n]))
        r0 += padded
    return out


def kernel(x, mem, norm_mix, w_in, conv_gdn, a_log, dt_bias, gdn_out_norm, w_proj_gdn, w_proj_sb, w_out, norm_x, norm_mem, w_xq, w_xkv, xq_norm, xk_norm, w_xo, norm_ffn, w_up, conv_ffn, w_down, loss_target, m_norm_mix, m_w_in, m_conv_gdn, m_a_log, m_dt_bias, m_gdn_out_norm, m_w_proj_gdn, m_w_proj_sb, m_w_out, m_norm_x, m_norm_mem, m_w_xq, m_w_xkv, m_xq_norm, m_xk_norm, m_w_xo, m_norm_ffn, m_w_up, m_conv_ffn, m_w_down, v_norm_mix, v_w_in, v_conv_gdn, v_a_log, v_dt_bias, v_gdn_out_norm, v_w_proj_gdn, v_w_proj_sb, v_w_out, v_norm_x, v_norm_mem, v_w_xq, v_w_xkv, v_xq_norm, v_xk_norm, v_w_xo, v_norm_ffn, v_w_up, v_conv_ffn, v_w_down):
    names = ("norm_mix", "w_in", "conv_gdn", "a_log", "dt_bias", "gdn_out_norm", "w_proj_gdn", "w_proj_sb", "w_out",
             "norm_x", "norm_mem", "w_xq", "w_xkv", "xq_norm", "xk_norm", "w_xo", "norm_ffn", "w_up", "conv_ffn", "w_down")
    wts = dict(zip(names, (norm_mix, w_in, conv_gdn, a_log, dt_bias, gdn_out_norm, w_proj_gdn, w_proj_sb, w_out,
                           norm_x, norm_mem, w_xq, w_xkv, xq_norm, xk_norm, w_xo, norm_ffn, w_up, conv_ffn, w_down)))
    mom = dict(zip(names, (m_norm_mix, m_w_in, m_conv_gdn, m_a_log, m_dt_bias, m_gdn_out_norm, m_w_proj_gdn, m_w_proj_sb,
                           m_w_out, m_norm_x, m_norm_mem, m_w_xq, m_w_xkv, m_xq_norm, m_xk_norm, m_w_xo, m_norm_ffn, m_w_up,
                           m_conv_ffn, m_w_down)))
    vel = dict(zip(names, (v_norm_mix, v_w_in, v_conv_gdn, v_a_log, v_dt_bias, v_gdn_out_norm, v_w_proj_gdn, v_w_proj_sb,
                           v_w_out, v_norm_x, v_norm_mem, v_w_xq, v_w_xkv, v_xq_norm, v_xk_norm, v_w_xo, v_norm_ffn, v_w_up,
                           v_conv_ffn, v_w_down)))
    cfg = CFG
    d, bsz, s = cfg.d, cfg.b, cfg.s
    t = bsz * s
    gh, sbh = cfg.gh, cfg.sbh
    gw, sw = gh * HD, sbh * HD
    nchunk = s // cfg.gch
    mx, my, mc = _my_pos()
    me = 4 * mx + 2 * my + mc

    shard_shapes = {n: tuple(wts[n].shape[1:]) for n in _BIG}

    packed_w = _pack_big_shards({n: wts[n][0] for n in _BIG}, shard_shapes)
    gathered = _all_gather_big(_cast_rows(packed_w, BF16, "cast_weights"))
    full = _unpack_gathered(gathered, shard_shapes)

    conv_rows = {n: _rows_of(wts[n][0], d) for n in _SMALL_CONV}
    conv_cnt = {n: conv_rows[n].shape[0] for n in _SMALL_CONV}
    conv_blk = _pad_rows(jnp.concatenate([conv_rows[n] for n in _SMALL_CONV], axis=0), SUBLANE)
    conv_all = jnp.zeros((N_DEV,) + conv_blk.shape, F32)
    conv_all = lax.dynamic_update_slice(conv_all, conv_blk[None], (me, 0, 0))
    conv_all = _all_reduce_small(conv_all.reshape(-1, d), "gather_conv_taps").reshape((N_DEV,) + conv_blk.shape)

    def full_conv(n, r0):
        k, cols = wts[n].shape[1], wts[n].shape[2]
        part = conv_all[:, r0:r0 + conv_cnt[n], :].reshape(N_DEV, -1)[:, :k * cols].reshape(N_DEV, k, cols)
        return part.transpose(1, 0, 2).reshape(k, N_DEV * cols)

    cgdn = full_conv("conv_gdn", 0)
    cffn = full_conv("conv_ffn", conv_cnt["conv_gdn"])

    win = full["w_in"]
    o_ab = 3 * gw
    o_z = o_ab + 2 * gh
    o_sb = o_z + gw
    o_gate = o_sb + 3 * sw
    w_qkv = win[:o_ab]
    w_ab = jnp.concatenate([win[o_ab:o_z], jnp.zeros((LANE - 2 * gh, d), win.dtype)], axis=0)
    w_z = win[o_z:o_sb]
    w_sb = win[o_sb:o_gate]
    w_gate = win[o_gate:]

    alog_p = jnp.concatenate([a_log.reshape(1, -1), jnp.zeros((1, LANE - gh), F32)], axis=1)
    dtb_p = jnp.concatenate([dt_bias.reshape(1, -1), jnp.zeros((1, LANE - gh), F32)], axis=1)
    onorm = gdn_out_norm.reshape(1, HD)

    x2 = x.reshape(t, d)
    tgt2 = loss_target.reshape(t, d)
    mem2 = mem.reshape(bsz * cfg.mem, d)

    (xn,) = _rowwise(_rms, [x2], [norm_mix], [(d, BF16)], [], name="norm_mix_fwd")
    p_qkv = _mm(xn, w_qkv, tb=True, name="proj_qkv")
    p_ab = _mm(xn, w_ab, tb=True, name="proj_ab")
    p_z = _mm(xn, w_z, tb=True, name="proj_z")
    p_sb = _mm(xn, w_sb, tb=True, name="proj_sb")
    p_gate = _mm(xn, w_gate, tb=True, name="proj_gate")

    qkv_c = _gdn_conv_fwd(p_qkv.reshape(bsz, s, 3 * gw), cgdn)
    (gb,) = _rowwise(_gates_fn, [p_ab], [alog_p, dtb_p], [(LANE, F32)], [], name="gdn_gates_fwd")
    gbt = gb.reshape(bsz, s, LANE)[:, :, :2 * gh].transpose(0, 2, 1).reshape(bsz, 2 * gh, nchunk, 1, cfg.gch)
    o_a, states = _gdn_fwd(qkv_c, p_z.reshape(bsz, s, gw), gbt, onorm)
    o_b = _sb_fwd(p_sb.reshape(bsz, s, 3 * sw))

    pa = _mm(o_a.reshape(t, gw), full["w_proj_gdn"], name="proj_gdn_out")
    pb = _mm(o_b.reshape(t, sw), full["w_proj_sb"], name="proj_sb_out")

    def merge_fn(pa_, pb_, gate_):
        return _sigmoid(gate_[:, :d]) * pa_ + _sigmoid(gate_[:, d:]) * pb_

    (merged,) = _rowwise(merge_fn, [pa, pb, p_gate], [], [(d, BF16)], [], name="merge_fwd")
    h1 = _mm(merged, full["w_out"], add=x2, name="mixer_out")

    (hn_x,) = _rowwise(_rms, [h1], [norm_x], [(d, BF16)], [], name="norm_x_fwd")
    (mn,) = _rowwise(_rms, [mem2], [norm_mem], [(d, BF16)], [], name="norm_mem_fwd")
    q_raw = _mm(hn_x, full["w_xq"], name="xattn_q")
    kv = _mm(mn, full["w_xkv"], tb=True, name="xattn_kv")
    xo = _xattn_fwd(q_raw.reshape(bsz, s, d), kv.reshape(bsz, cfg.mem, 2 * d), xq_norm, xk_norm)
    h2 = _mm(xo.reshape(t, d), full["w_xo"], add=h1, name="xattn_out")

    (hn_f,) = _rowwise(_rms, [h2], [norm_ffn], [(d, BF16)], [], name="norm_ffn_fwd")
    up = _mm(hn_f, full["w_up"], tb=True, name="ffn_up")
    act = _ffn_conv_fwd(up.reshape(bsz, s, 2 * cfg.dff), cffn)
    y = _mm(act.reshape(t, cfg.dff), full["w_down"], add=h2, name="ffn_down")

    def loss_fn(y_, tg_):
        err = y_ - tg_
        part = 0.5 * jnp.sum(err * err) / d
        return err / d, err / d, jnp.full((1, LANE), part, F32)

    dy, dy_b, loss_part = _rowwise(loss_fn, [y, tgt2], [], [(d, F32), (d, BF16)], [(1, LANE)], name="loss")

    grads = {}
    dact = _mm(dy_b, full["w_down"], tb=True, name="d_act")
    grads["w_down"] = _mm(act.reshape(t, cfg.dff), dy_b, ta=True, name="dw_down")
    dup1, dup2, dcf1, dcf2 = _ffn_conv_bwd(up.reshape(bsz, s, 2 * cfg.dff), cffn, dact.reshape(bsz, s, cfg.dff))
    dup = jnp.concatenate([dup1, dup2], axis=2).reshape(t, 2 * cfg.dff)
    g_conv_ffn = jnp.concatenate([dcf1, dcf2], axis=1)
    dhn_f = _mm(dup, full["w_up"], name="d_hn_ffn")
    grads["w_up"] = _mm(dup, hn_f, ta=True, name="dw_up")

    def norm_bwd_fn(h_, res_, dn_, g_):
        _, f = jax.vjp(_rms, h_, g_)
        dh, dg = f(dn_)
        return res_ + dh, dg

    def norm_bwd_copy_fn(h_, res_, dn_, g_):
        dres, dg = norm_bwd_fn(h_, res_, dn_, g_)
        return dres, dres, dg

    dh2, dh2_b, g_norm_ffn = _rowwise(norm_bwd_copy_fn, [h2, dy, dhn_f], [norm_ffn], [(d, F32), (d, BF16)], [(1, d)],
                                      name="norm_ffn_bwd")

    dxo = _mm(dh2_b, full["w_xo"], tb=True, out_dtype=BF16, name="d_xo")
    grads["w_xo"] = _mm(xo.reshape(t, d), dh2_b, ta=True, name="dw_xo")
    dq_raw, dkv, g_xq_norm, g_xk_norm = _xattn_bwd(q_raw.reshape(bsz, s, d), kv.reshape(bsz, cfg.mem, 2 * d),
                                                   xq_norm, xk_norm, dxo.reshape(bsz, s, d))
    dq_raw2 = dq_raw.reshape(t, d)
    dkv2 = dkv.reshape(bsz * cfg.mem, 2 * d)
    dhn_x = _mm(dq_raw2, full["w_xq"], tb=True, name="d_hn_x")
    grads["w_xq"] = _mm(hn_x, dq_raw2, ta=True, name="dw_xq")
    dmn = _mm(dkv2, full["w_xkv"], name="d_mn")
    grads["w_xkv"] = _mm(dkv2, mn, ta=True, name="dw_xkv")

    def norm_w_bwd_fn(h_, dn_, g_):
        _, f = jax.vjp(lambda gg: _rms(h_, gg), g_)
        return f(dn_)[0]

    (g_norm_mem,) = _rowwise(norm_w_bwd_fn, [mem2, dmn], [norm_mem], [], [(1, d)], name="norm_mem_bwd")
    dh1, dh1_b, g_norm_x = _rowwise(norm_bwd_copy_fn, [h1, dh2, dhn_x], [norm_x], [(d, F32), (d, BF16)], [(1, d)],
                                    name="norm_x_bwd")

    dmerged = _mm(dh1_b, full["w_out"], tb=True, name="d_merged")
    grads["w_out"] = _mm(merged, dh1_b, ta=True, name="dw_out")

    def merge_bwd_fn(pa_, pb_, gate_, dm_):
        _, f = jax.vjp(merge_fn, pa_, pb_, gate_)
        return f(dm_)

    dpa, dpb, dgate = _rowwise(merge_bwd_fn, [pa, pb, p_gate, dmerged], [], [(d, BF16), (d, BF16), (2 * d, BF16)], [],
                               name="merge_bwd")
    do_a = _mm(dpa, full["w_proj_gdn"], tb=True, name="d_o_gdn")
    grads["w_proj_gdn"] = _mm(o_a.reshape(t, gw), dpa, ta=True, name="dw_proj_gdn")
    do_b = _mm(dpb, full["w_proj_sb"], tb=True, name="d_o_sb")
    grads["w_proj_sb"] = _mm(o_b.reshape(t, sw), dpb, ta=True, name="dw_proj_sb")

    dsq, dsk, dsv = _sb_bwd(p_sb.reshape(bsz, s, 3 * sw), do_b.reshape(bsz, s, sw))
    dp_sb = jnp.concatenate([dsq, dsk, dsv], axis=2).reshape(t, 3 * sw)

    dgq, dgk, dgv, dz, dg, dbeta, g_onorm = _gdn_bwd(qkv_c, p_z.reshape(bsz, s, gw), gbt, onorm, states,
                                                     do_a.reshape(bsz, s, gw))
    dgb = jnp.concatenate([dg, dbeta], axis=1).reshape(bsz, 2 * gh, s).transpose(0, 2, 1)
    dgb = jnp.concatenate([dgb, jnp.zeros((bsz, s, LANE - 2 * gh), F32)], axis=2).reshape(t, LANE)

    def gates_bwd_fn(ab_, dgb_, alog_, dtb_):
        _, f = jax.vjp(_gates_fn, ab_, alog_, dtb_)
        return f(dgb_)

    dp_ab, g_alog, g_dtb = _rowwise(gates_bwd_fn, [p_ab, dgb], [alog_p, dtb_p], [(LANE, BF16)], [(1, LANE), (1, LANE)],
                                    name="gdn_gates_bwd")
    dqkv_c = jnp.concatenate([dgq, dgk, dgv], axis=2)
    dp_qkv, g_conv_gdn = _gdn_conv_bwd(p_qkv.reshape(bsz, s, 3 * gw), cgdn, dqkv_c)
    dp_qkv = dp_qkv.reshape(t, 3 * gw)
    dp_z = dz.reshape(t, gw)

    dxn = _mm(dp_qkv, w_qkv, name="d_xn_qkv")
    dxn = _mm(dp_ab, w_ab, add=dxn, name="d_xn_ab")
    dxn = _mm(dp_z, w_z, add=dxn, name="d_xn_z")
    dxn = _mm(dp_sb, w_sb, add=dxn, name="d_xn_sb")
    dxn = _mm(dgate, w_gate, add=dxn, name="d_xn_gate")
    grads["w_in"] = jnp.concatenate([
        _mm(dp_qkv, xn, ta=True, name="dw_in_qkv"),
        _mm(dp_ab, xn, ta=True, name="dw_in_ab")[:2 * gh],
        _mm(dp_z, xn, ta=True, name="dw_in_z"),
        _mm(dp_sb, xn, ta=True, name="dw_in_sb"),
        _mm(dgate, xn, ta=True, name="dw_in_gate")], axis=0)
    grad_x, g_norm_mix = _rowwise(norm_bwd_fn, [x2, dh1, dxn], [norm_mix], [(d, F32)], [(1, d)], name="norm_mix_bwd")

    small_g = {"norm_mix": g_norm_mix, "norm_x": g_norm_x, "norm_mem": g_norm_mem, "norm_ffn": g_norm_ffn,
               "a_log": g_alog[:, :gh], "dt_bias": g_dtb[:, :gh], "gdn_out_norm": g_onorm,
               "xq_norm": g_xq_norm, "xk_norm": g_xk_norm}
    sm_rows = [_rows_of(small_g[n], d) for n in _SMALL_REP] + [_rows_of(loss_part, d)]
    sm_rows += [_rows_of(g_conv_gdn, d), _rows_of(g_conv_ffn, d)]
    sm_cnt = [r.shape[0] for r in sm_rows]
    sm_sum = _all_reduce_small(_pad_rows(jnp.concatenate(sm_rows, axis=0), SUBLANE), "all_reduce_small_grads")
    offs = [0]
    for cnt in sm_cnt:
        offs.append(offs[-1] + cnt)
    small_grad = {}
    for i, n in enumerate(_SMALL_REP):
        small_grad[n] = sm_sum[offs[i]:offs[i + 1]].reshape(-1)[:wts[n].size].reshape(wts[n].shape)
    loss = sm_sum[offs[len(_SMALL_REP)], 0]
    for i, n in enumerate(_SMALL_CONV):
        k, cols = wts[n].shape[1], wts[n].shape[2]
        o = offs[len(_SMALL_REP) + 1 + i]
        fullg = sm_sum[o:o + sm_cnt[len(_SMALL_REP) + 1 + i]].reshape(-1)[:k * cols * N_DEV].reshape(k, N_DEV * cols)
        small_grad[n] = lax.dynamic_slice(fullg, (0, me * cols), (k, cols)).reshape(wts[n].shape)

    small_names = _SMALL_REP + _SMALL_CONV

    def pack_small(src):
        return _pad_rows(jnp.concatenate([_rows_of(src[n], d) for n in small_names], axis=0), SUBLANE)

    sw_, sg_, sm_, sv_ = pack_small(wts), pack_small(small_grad), pack_small(mom), pack_small(vel)
    sd_, snm_, snv_ = _rowwise(_adamw_math, [sw_, sg_, sm_, sv_], [], [(d, F32)] * 3, [], name="adamw_small", tm=sw_.shape[0])

    def unpack_small(packed):
        out, r0 = {}, 0
        for n in small_names:
            cnt = _rows_of(wts[n], d).shape[0]
            out[n] = packed[r0:r0 + cnt].reshape(-1)[:wts[n].size].reshape(wts[n].shape)
            r0 += cnt
        return out

    small_delta, small_m, small_v = unpack_small(sd_), unpack_small(snm_), unpack_small(snv_)

    gpack = _pack_full_grads(grads, shard_shapes)
    recv1 = _exchange_sibling(gpack)
    s1 = _sum_sibling(gpack, recv1, jnp.reshape(mc, (1,)).astype(jnp.int32))
    recv2 = _exchange_chips(s1)
    idx = jnp.stack([me, 2 * mx + my]).astype(jnp.int32)
    pm = _pack_big_shards({n: mom[n][0] for n in _BIG}, shard_shapes)
    pv = _pack_big_shards({n: vel[n][0] for n in _BIG}, shard_shapes)
    bg, bd, bm, bv = _adamw_big(gpack, recv1, recv2, packed_w, pm, pv, idx)
    big_grad, big_delta, big_m, big_v = (_unpack_shard(a, shard_shapes) for a in (bg, bd, bm, bv))

    def pick(big, small, n):
        return big[n] if n in big else small[n]

    outs = [loss, grad_x.reshape(bsz, s, d)]
    outs += [pick(big_grad, small_grad, n) for n in names]
    outs += [pick(big_delta, small_delta, n) for n in names]
    outs += [pick(big_m, small_m, n) for n in names]
    outs += [pick(big_v, small_v, n) for n in names]
    return tuple(outs)
```

```python
import functools

import jax
import jax.numpy as jnp
from jax import lax
from jax.experimental import pallas as pl
from jax.experimental.pallas import tpu as pltpu

F32 = jnp.float32
BF16 = jnp.bfloat16
FDOT_PRECISION = lax.Precision.HIGH

LANE = 128
SUBLANE = 8
PACK_ROW_ALIGN = 16
VMEM_LIMIT = 56 * 2 ** 20
N_DEV = 8
MESH = pl.DeviceIdType.MESH

EPS = 1e-6
ADAM_LR = 0.001
ADAM_B1 = 0.9
ADAM_B2 = 0.999
ADAM_EPS = 1e-08
ADAM_WD = 0.01
ADAM_STEP = 10


class _Cfg:
    d = 1024
    b = 4
    s = 2048
    mem = 256
    gh = 8
    gch = 64
    sbh = 8
    xh = 4
    dff = 2816
    pack_tile = 240


CFG = _Cfg()
HD = 128
SB_BLK = 128
SB_HP = 4
GDN_HP = 8
MM_TILES = (1024, 1408, 704, 512, 256, 128)
MM_K_TILES = (1024, 1408, 704, 512, 256, 128)
MM_K_TILES_F32 = (512, 704, 256, 128)
CONV_CB = 256
XQ_TILE = 256


def _tile(n, prefs):
    for t in prefs:
        if n % t == 0:
            return t
    raise ValueError(f"no tile for {n}")


def _cp(sem, **kw):
    return pltpu.CompilerParams(dimension_semantics=sem, vmem_limit_bytes=VMEM_LIMIT, **kw)


def _dims(kind):
    return {"nn": (((1,), (0,)), ((), ())), "nt": (((1,), (1,)), ((), ())), "tn": (((0,), (0,)), ((), ()))}[kind]


def _raw_bdot(a, b, kind):
    return lax.dot_general(a.astype(BF16), b.astype(BF16), _dims(kind), preferred_element_type=F32)


def _raw_fdot(a, b, kind):
    return lax.dot_general(a.astype(F32), b.astype(F32), _dims(kind), precision=FDOT_PRECISION,
                           preferred_element_type=F32)


def _make_dot(raw):
    @functools.partial(jax.custom_vjp, nondiff_argnums=(2,))
    def dot(a, b, kind):
        return raw(a, b, kind)

    def fwd(a, b, kind):
        return raw(a, b, kind), (a, b)

    def bwd(kind, res, g):
        a, b = res
        if kind == "nn":
            return raw(g, b, "nt").astype(a.dtype), raw(a, g, "tn").astype(b.dtype)
        if kind == "nt":
            return raw(g, b, "nn").astype(a.dtype), raw(g, a, "tn").astype(b.dtype)
        return raw(b, g, "nt").astype(a.dtype), raw(a, g, "nn").astype(b.dtype)

    dot.defvjp(fwd, bwd)
    return dot


_bdot = _make_dot(_raw_bdot)
_fdot = _make_dot(_raw_fdot)


def _split_dot(x, m01):
    hi = x.astype(BF16)
    lo = (x - hi.astype(F32)).astype(BF16)
    return (lax.dot_general(hi, m01, _dims("nn"), preferred_element_type=F32)
            + lax.dot_general(lo, m01, _dims("nn"), preferred_element_type=F32))


_sigmoid = jax.nn.sigmoid


def _silu(x):
    return x * _sigmoid(x)


def _softplus(x):
    return jnp.maximum(x, 0.0) + jnp.log1p(jnp.exp(-jnp.abs(x)))


def _rms(x, g):
    return x * lax.rsqrt(jnp.mean(x * x, axis=-1, keepdims=True) + EPS) * g


def _iota2(shape, dim):
    return lax.broadcasted_iota(jnp.int32, shape, dim)


def _mm(a, b, *, ta=False, tb=False, add=None, out_dtype=F32, name):
    if ta:
        kd, m = a.shape
    else:
        m, kd = a.shape
    if tb:
        n, kb = b.shape
    else:
        kb, n = b.shape
    assert kd == kb, (a.shape, b.shape, ta, tb)
    tm = _tile(m, MM_TILES)
    tn = _tile(n, MM_TILES)
    wide = max(a.dtype.itemsize, b.dtype.itemsize) > 2
    tk = _tile(kd, MM_K_TILES_F32 if wide else MM_K_TILES)
    nk = kd // tk
    kind_dims = (((0 if ta else 1,), (1 if tb else 0,)), ((), ()))

    def body(*refs):
        a_ref, b_ref = refs[:2]
        add_ref = refs[2] if add is not None else None
        o_ref = refs[3 if add is not None else 2]
        part = lax.dot_general(a_ref[...].astype(BF16), b_ref[...].astype(BF16), kind_dims,
                               preferred_element_type=F32)

        def finish(r):
            if add is not None:
                r = r + add_ref[...].astype(F32)
            o_ref[...] = r.astype(o_ref.dtype)

        if nk == 1:
            finish(part)
            return
        acc = refs[-1]
        k = pl.program_id(2)

        @pl.when(k == 0)
        def _():
            acc[...] = part

        @pl.when((k > 0) & (k < nk - 1))
        def _():
            acc[...] += part

        @pl.when(k == nk - 1)
        def _():
            finish(acc[...] + part)

    a_spec = pl.BlockSpec((tk, tm), lambda i, j, k: (k, i)) if ta else pl.BlockSpec((tm, tk), lambda i, j, k: (i, k))
    b_spec = pl.BlockSpec((tn, tk), lambda i, j, k: (j, k)) if tb else pl.BlockSpec((tk, tn), lambda i, j, k: (k, j))
    in_specs = [a_spec, b_spec]
    args = [a, b]
    if add is not None:
        in_specs.append(pl.BlockSpec((tm, tn), lambda i, j, k: (i, j)))
        args.append(add)
    return pl.pallas_call(
        body, name=name, grid=(m // tm, n // tn, nk),
        in_specs=in_specs, out_specs=pl.BlockSpec((tm, tn), lambda i, j, k: (i, j)),
        out_shape=jax.ShapeDtypeStruct((m, n), out_dtype),
        scratch_shapes=[pltpu.VMEM((tm, tn), F32)] if nk > 1 else [],
        compiler_params=_cp(("parallel", "parallel", "arbitrary")),
    )(*args)


def _rowwise(fn, rows, pars, out_rows, out_accs, *, name, tm=None):
    t = rows[0].shape[0]
    if tm is None:
        tm = _tile(t, (256, 128, 64, 32, 16))
    assert t % tm == 0, (t, tm)
    n_r, n_p, n_or, n_oa = len(rows), len(pars), len(out_rows), len(out_accs)

    def body(*refs):
        r_in = refs[:n_r]
        p_in = refs[n_r:n_r + n_p]
        o_r = refs[n_r + n_p:n_r + n_p + n_or]
        o_a = refs[n_r + n_p + n_or:]
        outs = fn(*[r[...] for r in r_in], *[p[...] for p in p_in])
        if not isinstance(outs, (tuple, list)):
            outs = (outs,)
        assert len(outs) == n_or + n_oa, (name, len(outs))
        for ref, val in zip(o_r, outs[:n_or]):
            ref[...] = val.astype(ref.dtype)
        if n_oa:
            @pl.when(pl.program_id(0) == 0)
            def _():
                for ref in o_a:
                    ref[...] = jnp.zeros_like(ref)

            for ref, val in zip(o_a, outs[n_or:]):
                ref[...] += val.astype(F32)

    in_specs = [pl.BlockSpec((tm, r.shape[1]), lambda i: (i, 0)) for r in rows]
    in_specs += [pl.BlockSpec(p.shape, lambda i: (0, 0)) for p in pars]
    out_specs = [pl.BlockSpec((tm, c), lambda i: (i, 0)) for c, _ in out_rows]
    out_specs += [pl.BlockSpec(s, lambda i: (0, 0)) for s in out_accs]
    out_shape = [jax.ShapeDtypeStruct((t, c), dt) for c, dt in out_rows]
    out_shape += [jax.ShapeDtypeStruct(s, F32) for s in out_accs]
    return pl.pallas_call(
        body, name=name, grid=(t // tm,), in_specs=in_specs, out_specs=out_specs, out_shape=out_shape,
        compiler_params=_cp(("arbitrary",)),
    )(*rows, *pars)


def _shift_down(x, sh):
    rolled = pltpu.roll(x, sh, 0)
    return jnp.where(_iota2(x.shape, 0) >= sh, rolled, 0.0)


def _shift_up(x, sh):
    s = x.shape[0]
    rolled = pltpu.roll(x, s - sh, 0)
    return jnp.where(_iota2(x.shape, 0) < s - sh, rolled, 0.0)


def _conv(x, w):
    k = w.shape[0]
    y = x * w[k - 1:k, :]
    for i in range(k - 1):
        y = y + _shift_down(x, k - 1 - i) * w[i:i + 1, :]
    return y


def _conv_bwd(x, w, dy):
    k = w.shape[0]
    dx = dy * w[k - 1:k, :]
    dws = []
    for i in range(k - 1):
        dx = dx + _shift_up(dy, k - 1 - i) * w[i:i + 1, :]
        dws.append(jnp.sum(dy * _shift_down(x, k - 1 - i), axis=0, keepdims=True))
    dws.append(jnp.sum(dy * x, axis=0, keepdims=True))
    return dx, dws


def _gdn_post(y, j, nqb):
    a = _silu(y)
    sc = jnp.where(j < nqb, HD ** -0.5, 1.0).astype(F32)
    outs = []
    for h in range(y.shape[1] // HD):
        ah = a[:, h * HD:(h + 1) * HD]
        l2 = ah * lax.rsqrt(jnp.sum(ah * ah, axis=-1, keepdims=True) + EPS)
        outs.append(jnp.where(j < 2 * nqb, l2 * sc, ah))
    return jnp.concatenate(outs, axis=1) if len(outs) > 1 else outs[0]


def _gdn_conv_fwd(x, w):
    bsz, s, c3 = x.shape
    k = w.shape[0]
    nb = c3 // CONV_CB
    nqb = nb // 3

    def body(x_ref, w_ref, o_ref):
        j = pl.program_id(1)
        o_ref[0] = _gdn_post(_conv(x_ref[0], w_ref[...]), j, nqb)

    return pl.pallas_call(
        body, name="gdn_conv_fwd", grid=(bsz, nb),
        in_specs=[pl.BlockSpec((1, s, CONV_CB), lambda b, j: (b, 0, j)), pl.BlockSpec((k, CONV_CB), lambda b, j: (0, j))],
        out_specs=pl.BlockSpec((1, s, CONV_CB), lambda b, j: (b, 0, j)),
        out_shape=jax.ShapeDtypeStruct(x.shape, F32),
        compiler_params=_cp(("parallel", "parallel")),
    )(x, w)


def _gdn_conv_bwd(x, w, dout):
    bsz, s, c3 = x.shape
    k = w.shape[0]
    nb = c3 // CONV_CB
    nqb = nb // 3

    def body(x_ref, w_ref, d_ref, dx_ref, dw_ref):
        j = pl.program_id(0)
        b = pl.program_id(1)
        xv, wv = x_ref[0], w_ref[...]
        y = _conv(xv, wv)
        _, f = jax.vjp(lambda yy: _gdn_post(yy, j, nqb), y)
        (dy,) = f(d_ref[0])
        dx, dws = _conv_bwd(xv, wv, dy)
        dx_ref[0] = dx.astype(dx_ref.dtype)

        @pl.when(b == 0)
        def _():
            dw_ref[...] = jnp.zeros_like(dw_ref)

        for i in range(k):
            dw_ref[i:i + 1, :] += dws[i]

    return pl.pallas_call(
        body, name="gdn_conv_bwd", grid=(nb, bsz),
        in_specs=[pl.BlockSpec((1, s, CONV_CB), lambda j, b: (b, 0, j)), pl.BlockSpec((k, CONV_CB), lambda j, b: (0, j)),
                  pl.BlockSpec((1, s, CONV_CB), lambda j, b: (b, 0, j))],
        out_specs=[pl.BlockSpec((1, s, CONV_CB), lambda j, b: (b, 0, j)), pl.BlockSpec((k, CONV_CB), lambda j, b: (0, j))],
        out_shape=[jax.ShapeDtypeStruct(x.shape, BF16), jax.ShapeDtypeStruct(w.shape, F32)],
        compiler_params=_cp(("parallel", "arbitrary")),
    )(x, w, dout)


def _ffn_conv_fwd(up, w):
    bsz, s, c2 = up.shape
    k = w.shape[0]
    nb = (c2 // 2) // CONV_CB

    def body(x1_ref, x2_ref, w1_ref, w2_ref, o_ref):
        u1 = _conv(x1_ref[0], w1_ref[...])
        u2 = _conv(x2_ref[0], w2_ref[...])
        o_ref[0] = (_silu(u1) * u2).astype(o_ref.dtype)

    return pl.pallas_call(
        body, name="ffn_conv_fwd", grid=(bsz, nb),
        in_specs=[pl.BlockSpec((1, s, CONV_CB), lambda b, j: (b, 0, j)), pl.BlockSpec((1, s, CONV_CB), lambda b, j: (b, 0, j + nb)),
                  pl.BlockSpec((k, CONV_CB), lambda b, j: (0, j)), pl.BlockSpec((k, CONV_CB), lambda b, j: (0, j + nb))],
        out_specs=pl.BlockSpec((1, s, CONV_CB), lambda b, j: (b, 0, j)),
        out_shape=jax.ShapeDtypeStruct((bsz, s, c2 // 2), BF16),
        compiler_params=_cp(("parallel", "parallel")),
    )(up, up, w, w)


def _ffn_conv_bwd(up, w, dact):
    bsz, s, c2 = up.shape
    k = w.shape[0]
    half = c2 // 2
    nb = half // CONV_CB

    def body(x1_ref, x2_ref, w1_ref, w2_ref, d_ref, dx1_ref, dx2_ref, dw1_ref, dw2_ref):
        b = pl.program_id(1)
        x1, x2, w1, w2 = x1_ref[0], x2_ref[0], w1_ref[...], w2_ref[...]
        u1 = _conv(x1, w1)
        u2 = _conv(x2, w2)
        _, f = jax.vjp(lambda p, q: _silu(p) * q, u1, u2)
        du1, du2 = f(d_ref[0])
        dx1, dws1 = _conv_bwd(x1, w1, du1)
        dx2, dws2 = _conv_bwd(x2, w2, du2)
        dx1_ref[0] = dx1.astype(dx1_ref.dtype)
        dx2_ref[0] = dx2.astype(dx2_ref.dtype)

        @pl.when(b == 0)
        def _():
            dw1_ref[...] = jnp.zeros_like(dw1_ref)
            dw2_ref[...] = jnp.zeros_like(dw2_ref)

        for i in range(k):
            dw1_ref[i:i + 1, :] += dws1[i]
            dw2_ref[i:i + 1, :] += dws2[i]

    def blk(off):
        return pl.BlockSpec((1, s, CONV_CB), lambda j, b: (b, 0, j + off))

    def wblk(off):
        return pl.BlockSpec((k, CONV_CB), lambda j, b: (0, j + off))

    return pl.pallas_call(
        body, name="ffn_conv_bwd", grid=(nb, bsz),
        in_specs=[blk(0), blk(nb), wblk(0), wblk(nb), blk(0)],
        out_specs=[blk(0), blk(0), wblk(0), wblk(0)],
        out_shape=[jax.ShapeDtypeStruct((bsz, s, half), BF16), jax.ShapeDtypeStruct((bsz, s, half), BF16),
                   jax.ShapeDtypeStruct((k, half), F32), jax.ShapeDtypeStruct((k, half), F32)],
        compiler_params=_cp(("parallel", "arbitrary")),
    )(up, up, w, w, dact)


@jax.custom_vjp
def _inv_unit_lower(mats):
    c = mats[0].shape[0]
    eye = (_iota2((c, c), 0) == _iota2((c, c), 1)).astype(F32)
    ps = [-a for a in mats]
    ts = [eye + p for p in ps]
    n = 2
    while n < c:
        ps = [_raw_fdot(p, p, "nn") for p in ps]
        ts = [t + _raw_fdot(t, p, "nn") for t, p in zip(ts, ps)]
        n *= 2
    return ts


def _inv_fwd(mats):
    ts = _inv_unit_lower(mats)
    return ts, ts


def _inv_bwd(ts, gs):
    xs = [_raw_fdot(g, t, "nt") for g, t in zip(gs, ts)]
    return ([-_raw_fdot(t, x, "tn") for t, x in zip(ts, xs)],)


_inv_unit_lower.defvjp(_inv_fwd, _inv_bwd)


def _gdn_chunk(q, k, v, z, g_row, beta_row, state, onorm):
    nh = range(len(q))
    c = q[0].shape[0]
    ii, jj = _iota2((c, c), 0), _iota2((c, c), 1)
    incl, strict, eye = ii >= jj, ii > jj, ii == jj

    def to_col(row):
        return jnp.sum(jnp.where(eye, jnp.broadcast_to(row, (c, c)), 0.0), axis=1, keepdims=True)

    gc_col = [jnp.sum(jnp.where(incl, jnp.broadcast_to(g_row[h], (c, c)), 0.0), axis=1, keepdims=True) for h in nh]
    gc_row = [jnp.sum(jnp.where(eye, jnp.broadcast_to(gc_col[h], (c, c)), 0.0), axis=0, keepdims=True) for h in nh]
    beta_col = [to_col(beta_row[h]) for h in nh]
    gc_last = [jnp.sum(g_row[h], axis=1, keepdims=True) for h in nh]
    decay = [jnp.where(incl, jnp.exp(jnp.where(incl, gc_col[h] - gc_row[h], 0.0)), 0.0) for h in nh]
    kk = [_bdot(k[h], k[h], "nt") for h in nh]
    qk = [_bdot(q[h], k[h], "nt") * decay[h] for h in nh]
    tinv = _inv_unit_lower([jnp.where(strict, beta_col[h] * kk[h] * decay[h], 0.0) for h in nh])
    rhs = [jnp.concatenate([v[h] * beta_col[h], k[h] * (beta_col[h] * jnp.exp(gc_col[h]))], axis=1) for h in nh]
    uw = [_fdot(tinv[h], rhs[h], "nn") for h in nh]
    dv = v[0].shape[1]
    ws = [_bdot(uw[h][:, dv:], state[h], "nn") for h in nh]
    qs = [_bdot(q[h] * jnp.exp(gc_col[h]), state[h], "nn") for h in nh]
    v_new = [uw[h][:, :dv] - ws[h] for h in nh]
    o = [qs[h] + _bdot(qk[h], v_new[h], "nn") for h in nh]
    kv = [_bdot(k[h] * jnp.exp(gc_last[h] - gc_col[h]), v_new[h], "tn") for h in nh]
    new_state = [state[h] * jnp.exp(gc_last[h]) + kv[h] for h in nh]
    y = [_rms(o[h], onorm) * _silu(z[h]) for h in nh]
    return y, new_state


def _gdn_specs(s, c, reverse):
    n = s // c
    nn = (lambda i: n - 1 - i) if reverse else (lambda i: i)

    def qkv(off):
        return pl.BlockSpec((1, c, GDN_HP * HD), lambda b, h, i: (b, nn(i), h + off))

    def gate(off):
        return pl.BlockSpec((1, GDN_HP, 1, 1, c), lambda b, h, i: (b, h + off, nn(i), 0, 0))

    st = pl.BlockSpec((1, GDN_HP, 1, HD, HD), lambda b, h, i: (b, h, nn(i), 0, 0))
    onorm = pl.BlockSpec((1, HD), lambda b, h, i: (0, 0))
    return n, qkv, gate, st, onorm


def _gdn_fwd(qkv, z, gbt, onorm):
    bsz, s, _ = qkv.shape
    gh, c = CFG.gh, CFG.gch
    ng = gh // GDN_HP
    n, qs, gs, st, on = _gdn_specs(s, c, False)

    def body(q_ref, k_ref, v_ref, z_ref, g_ref, b_ref, on_ref, y_ref, st_ref, state):
        @pl.when(pl.program_id(2) == 0)
        def _():
            state[...] = jnp.zeros_like(state)

        nh = range(GDN_HP)
        hs = [slice(h * HD, (h + 1) * HD) for h in nh]
        s_in = [state[h] for h in nh]
        for h in nh:
            st_ref[0, h, 0] = s_in[h]
        y, s_out = _gdn_chunk([q_ref[0, :, hs[h]] for h in nh], [k_ref[0, :, hs[h]] for h in nh],
                              [v_ref[0, :, hs[h]] for h in nh], [z_ref[0, :, hs[h]] for h in nh],
                              [g_ref[0, h, 0] for h in nh], [b_ref[0, h, 0] for h in nh], s_in, on_ref[...])
        for h in nh:
            y_ref[0, :, hs[h]] = y[h].astype(y_ref.dtype)
            state[h] = s_out[h]

    return pl.pallas_call(
        body, name="gdn_fwd", grid=(bsz, ng, n),
        in_specs=[qs(0), qs(ng), qs(2 * ng), qs(0), gs(0), gs(ng), on],
        out_specs=[qs(0), st],
        out_shape=[jax.ShapeDtypeStruct((bsz, s, gh * HD), BF16), jax.ShapeDtypeStruct((bsz, gh, n, HD, HD), F32)],
        scratch_shapes=[pltpu.VMEM((GDN_HP, HD, HD), F32)],
        compiler_params=_cp(("parallel", "parallel", "arbitrary")),
    )(qkv, qkv, qkv, z, gbt, gbt, onorm)


def _gdn_bwd(qkv, z, gbt, onorm, states, dy):
    bsz, s, _ = qkv.shape
    gh, c = CFG.gh, CFG.gch
    ng = gh // GDN_HP
    n, qs, gs, st, on = _gdn_specs(s, c, True)

    def body(q_ref, k_ref, v_ref, z_ref, g_ref, b_ref, on_ref, st_ref, dy_ref,
             dq_ref, dk_ref, dv_ref, dz_ref, dg_ref, db_ref, don_ref, dstate):
        first = (pl.program_id(0) == 0) & (pl.program_id(1) == 0) & (pl.program_id(2) == 0)

        @pl.when(first)
        def _():
            don_ref[...] = jnp.zeros_like(don_ref)

        @pl.when(pl.program_id(2) == 0)
        def _():
            dstate[...] = jnp.zeros_like(dstate)

        nh = range(GDN_HP)
        hs = [slice(h * HD, (h + 1) * HD) for h in nh]
        _, f = jax.vjp(_gdn_chunk, [q_ref[0, :, hs[h]] for h in nh], [k_ref[0, :, hs[h]] for h in nh],
                       [v_ref[0, :, hs[h]] for h in nh], [z_ref[0, :, hs[h]] for h in nh],
                       [g_ref[0, h, 0] for h in nh], [b_ref[0, h, 0] for h in nh],
                       [st_ref[0, h, 0] for h in nh], on_ref[...])
        dq, dk, dv, dz, dg, db, ds, don = f(([dy_ref[0, :, hs[h]] for h in nh], [dstate[h] for h in nh]))
        for h in nh:
            dq_ref[0, :, hs[h]] = dq[h]
            dk_ref[0, :, hs[h]] = dk[h]
            dv_ref[0, :, hs[h]] = dv[h]
            dz_ref[0, :, hs[h]] = dz[h].astype(dz_ref.dtype)
            dg_ref[0, h, 0] = dg[h]
            db_ref[0, h, 0] = db[h]
            dstate[h] = ds[h]
        don_ref[...] += don

    act = jax.ShapeDtypeStruct((bsz, s, gh * HD), F32)
    gshape = jax.ShapeDtypeStruct((bsz, gh, n, 1, c), F32)
    return pl.pallas_call(
        body, name="gdn_bwd", grid=(bsz, ng, n),
        in_specs=[qs(0), qs(ng), qs(2 * ng), qs(0), gs(0), gs(ng), on, st, qs(0)],
        out_specs=[qs(0), qs(0), qs(0), qs(0), gs(0), gs(0), on],
        out_shape=[act, act, act, jax.ShapeDtypeStruct(act.shape, BF16), gshape, gshape, jax.ShapeDtypeStruct((1, HD), F32)],
        scratch_shapes=[pltpu.VMEM((GDN_HP, HD, HD), F32)],
        compiler_params=_cp(("arbitrary", "arbitrary", "arbitrary")),
    )(qkv, qkv, qkv, z, gbt, gbt, onorm, states, dy)


def _gates_fn(ab, alog, dtb):
    lane = _iota2(ab.shape, 1)
    g = -jnp.exp(alog) * _softplus(ab + dtb)
    beta = _sigmoid(ab)
    return jnp.where(lane < CFG.gh, g, jnp.where(lane < 2 * CFG.gh, beta, 0.0))


def _heads_cumsum(xs, tri):
    n = xs[0].shape[0]
    y = _split_dot(jnp.concatenate(xs, axis=0), tri)
    return [y[h * n:(h + 1) * n] for h in range(len(xs))]


def _blk_off(jblk):
    return jblk * SB_BLK if isinstance(jblk, int) else pl.multiple_of(jblk * SB_BLK, SB_BLK)


def _sb_span(qs, k_spans, mask, runs, tri_su):
    nh = range(len(qs))
    nb = k_spans[0].shape[0] // SB_BLK
    zs = [lax.dot_general(qs[h], k_spans[h], _dims("nt"), preferred_element_type=F32) for h in nh]
    l1p = [jnp.log(1.0 + jnp.exp(-jnp.abs(z))) for z in zs]
    lss = [jnp.minimum(zs[h], 0.0) - l1p[h] for h in nh]
    lfs = [lss[h] - zs[h] for h in nh]
    if mask is not None:
        lfs = [jnp.where(mask, lf, 0.0) for lf in lfs]
    units = [lfs[h][:, b * SB_BLK:(b + 1) * SB_BLK] for h in nh for b in range(nb)]
    cums = _heads_cumsum(units, tri_su)
    sfx, new_runs = [], []
    for h in nh:
        run, parts = runs[h], [None] * nb
        for b in reversed(range(nb)):
            parts[b] = cums[h * nb + b] + run
            run = run + jnp.sum(units[h * nb + b], axis=1, keepdims=True)
        sfx.append(jnp.concatenate(parts, axis=1) if nb > 1 else parts[0])
        new_runs.append(run)
    ws = [jnp.exp(lss[h] + sfx[h]) for h in nh]
    if mask is not None:
        ws = [jnp.where(mask, w, 0.0) for w in ws]
    return zs, lfs, ws, new_runs


def _sb_specs(s, w):
    def qb(off):
        return pl.BlockSpec((1, SB_BLK, w), lambda b, h, i: (b, i, h + off))

    def full(off):
        return pl.BlockSpec((1, s, w), lambda b, h, i: (b, 0, h + off))

    return qb, full


def _sb_fwd(qkv):
    bsz, s, _ = qkv.shape
    ng = CFG.sbh // SB_HP
    w = SB_HP * HD
    scale = HD ** -0.5
    qb, full = _sb_specs(s, w)

    def body(q_ref, k_ref, v_ref, o_ref):
        i = pl.program_id(2)
        r, c = _iota2((SB_BLK, SB_BLK), 0), _iota2((SB_BLK, SB_BLK), 1)
        tri_su = (r > c).astype(BF16)
        nh = range(SB_HP)
        hs = [slice(h * HD, (h + 1) * HD) for h in nh]
        qs = [(q_ref[0, :, hs[h]] * scale).astype(BF16) for h in nh]

        def span(off, nb, mask, carry):
            ks = [k_ref[0, pl.ds(off, nb * SB_BLK), hs[h]].astype(BF16) for h in nh]
            vs = [v_ref[0, pl.ds(off, nb * SB_BLK), hs[h]].astype(BF16) for h in nh]
            _, _, ws, runs = _sb_span(qs, ks, mask, [cr[1] for cr in carry], tri_su)
            pv = [lax.dot_general(ws[h].astype(BF16), vs[h], _dims("nn"), preferred_element_type=F32) for h in nh]
            return tuple((carry[h][0] + pv[h], runs[h]) for h in nh)

        carry = tuple((jnp.zeros((SB_BLK, HD), F32), jnp.zeros((SB_BLK, 1), F32)) for _ in nh)
        carry = span(_blk_off(i), 1, c < r, carry)
        rem = jnp.bitwise_and(i, 3)
        carry = lax.fori_loop(0, lax.shift_right_logical(i, 2),
                              lambda p, cr: span(_blk_off(i - 4 - 4 * p), 4, None, cr), carry)
        carry = lax.fori_loop(0, lax.shift_right_logical(rem, 1),
                              lambda _, cr: span(_blk_off(jnp.bitwise_and(rem, 1)), 2, None, cr), carry)
        carry = lax.fori_loop(0, jnp.bitwise_and(rem, 1), lambda _, cr: span(0, 1, None, cr), carry)
        for h in nh:
            o_ref[0, :, hs[h]] = carry[h][0].astype(o_ref.dtype)

    return pl.pallas_call(
        body, name="sb_fwd", grid=(bsz, ng, s // SB_BLK),
        in_specs=[qb(0), full(ng), full(2 * ng)], out_specs=qb(0),
        out_shape=jax.ShapeDtypeStruct((bsz, s, CFG.sbh * HD), BF16),
        compiler_params=_cp(("parallel", "parallel", "arbitrary")),
    )(qkv, qkv, qkv)


def _sb_bwd(qkv, do):
    bsz, s, _ = qkv.shape
    ng = CFG.sbh // SB_HP
    w = SB_HP * HD
    nblk = s // SB_BLK
    scale = HD ** -0.5
    qb, full = _sb_specs(s, w)

    def body(q_ref, k_ref, v_ref, do_ref, dq_ref, dk_ref, dv_ref, dk_acc, dv_acc, dl_pan, z_pan):
        i = pl.program_id(2)

        @pl.when(i == 0)
        def _():
            dk_acc[...] = jnp.zeros_like(dk_acc)
            dv_acc[...] = jnp.zeros_like(dv_acc)

        r, c = _iota2((SB_BLK, SB_BLK), 0), _iota2((SB_BLK, SB_BLK), 1)
        tri_su = (r > c).astype(BF16)
        tri_pre = (r < c).astype(BF16)
        nh = range(SB_HP)
        hs = [slice(h * HD, (h + 1) * HD) for h in nh]
        qs = [(q_ref[0, :, hs[h]] * scale).astype(BF16) for h in nh]
        dob = [do_ref[0, :, hs[h]].astype(BF16) for h in nh]
        quads = lax.shift_right_logical(i, 2)
        rem = jnp.bitwise_and(i, 3)
        pair = lax.shift_right_logical(rem, 1)
        odd = jnp.bitwise_and(rem, 1)

        def span_a(jblk, nb, mask, runs):
            rows = pl.ds(_blk_off(jblk), nb * SB_BLK)
            ks = [k_ref[0, rows, hs[h]].astype(BF16) for h in nh]
            vs = [v_ref[0, rows, hs[h]].astype(BF16) for h in nh]
            dws = [lax.dot_general(dob[h], vs[h], _dims("nt"), preferred_element_type=F32) for h in nh]
            zs, _, ws, runs = _sb_span(qs, ks, mask, runs, tri_su)
            dvs = [lax.dot_general(ws[h].astype(BF16), dob[h], _dims("tn"), preferred_element_type=F32) for h in nh]
            for h in nh:
                dl = dws[h] * ws[h]
                for b in range(nb):
                    dl_pan[h, jblk + b] = dl[:, b * SB_BLK:(b + 1) * SB_BLK]
                    z_pan[h, jblk + b] = zs[h][:, b * SB_BLK:(b + 1) * SB_BLK]
                dv_acc[rows, hs[h]] += dvs[h]
            return tuple(runs)

        runs = tuple(jnp.zeros((SB_BLK, 1), F32) for _ in nh)
        runs = span_a(i, 1, c < r, runs)
        runs = lax.fori_loop(0, quads, lambda p, rn: span_a(i - 4 - 4 * p, 4, None, rn), runs)
        runs = lax.fori_loop(0, pair, lambda _, rn: span_a(odd, 2, None, rn), runs)
        lax.fori_loop(0, odd, lambda _, rn: span_a(0, 1, None, rn), runs)

        def span_b(jblk, nb, mask, carry):
            rows = pl.ds(_blk_off(jblk), nb * SB_BLK)
            ks = [k_ref[0, rows, hs[h]].astype(BF16) for h in nh]
            units = [dl_pan[h, jblk + b] for h in nh for b in range(nb)]
            sgs = [_sigmoid(z_pan[h, jblk + b]) for h in nh for b in range(nb)]
            cums = _heads_cumsum(units, tri_pre)
            dzs, pres = [], []
            for h in nh:
                pre, parts = carry[h][1], []
                for b in range(nb):
                    u, sg = units[h * nb + b], sgs[h * nb + b]
                    parts.append(u * (1.0 - sg) - sg * (cums[h * nb + b] + pre))
                    pre = pre + jnp.sum(u, axis=1, keepdims=True)
                dz = jnp.concatenate(parts, axis=1) if nb > 1 else parts[0]
                if mask is not None:
                    dz = jnp.where(mask, dz, 0.0)
                dzs.append(dz.astype(BF16))
                pres.append(pre)
            dqs = [lax.dot_general(dzs[h], ks[h], _dims("nn"), preferred_element_type=F32) for h in nh]
            dks = [lax.dot_general(dzs[h], qs[h], _dims("tn"), preferred_element_type=F32) for h in nh]
            for h in nh:
                dk_acc[rows, hs[h]] += dks[h]
            return tuple((carry[h][0] + dqs[h], pres[h]) for h in nh)

        carry = tuple((jnp.zeros((SB_BLK, HD), F32), jnp.zeros((SB_BLK, 1), F32)) for _ in nh)
        carry = lax.fori_loop(0, odd, lambda _, cr: span_b(0, 1, None, cr), carry)
        carry = lax.fori_loop(0, pair, lambda _, cr: span_b(odd, 2, None, cr), carry)
        carry = lax.fori_loop(0, quads, lambda p, cr: span_b(rem + 4 * p, 4, None, cr), carry)
        carry = span_b(i, 1, c < r, carry)
        for h in nh:
            dq_ref[0, :, hs[h]] = (carry[h][0] * scale).astype(dq_ref.dtype)

        @pl.when(i == nblk - 1)
        def _():
            dk_ref[0] = dk_acc[...].astype(dk_ref.dtype)
            dv_ref[0] = dv_acc[...].astype(dv_ref.dtype)

    out = jax.ShapeDtypeStruct((bsz, s, CFG.sbh * HD), BF16)
    return pl.pallas_call(
        body, name="sb_bwd", grid=(bsz, ng, nblk),
        in_specs=[qb(0), full(ng), full(2 * ng), qb(0)],
        out_specs=[qb(0), full(0), full(0)],
        out_shape=[out, out, out],
        scratch_shapes=[pltpu.VMEM((s, w), F32), pltpu.VMEM((s, w), F32),
                        pltpu.VMEM((SB_HP, nblk, SB_BLK, SB_BLK), F32), pltpu.VMEM((SB_HP, nblk, SB_BLK, SB_BLK), F32)],
        compiler_params=_cp(("parallel", "parallel", "arbitrary")),
    )(qkv, qkv, qkv, do)


def _xattn_fn(q_raw, kv, qn, kn):
    d = q_raw.shape[1]
    dh = d // CFG.xh
    outs = []
    for h in range(CFG.xh):
        qh = _rms(q_raw[:, h * dh:(h + 1) * dh], qn)
        kh = _rms(kv[:, h * dh:(h + 1) * dh], kn)
        vh = kv[:, d + h * dh:d + (h + 1) * dh]
        sc = _bdot(qh, kh, "nt") * (dh ** -0.5)
        sc = sc - lax.stop_gradient(jnp.max(sc, axis=-1, keepdims=True))
        e = jnp.exp(sc)
        p = e / jnp.sum(e, axis=-1, keepdims=True)
        outs.append(_bdot(p, vh, "nn"))
    return jnp.concatenate(outs, axis=1)


def _xattn_fwd(q_raw, kv, qn, kn):
    bsz, s, d = q_raw.shape
    m = kv.shape[1]
    tq = _tile(s, (XQ_TILE, 128))

    def body(q_ref, kv_ref, qn_ref, kn_ref, o_ref):
        o_ref[0] = _xattn_fn(q_ref[0], kv_ref[0], qn_ref[...], kn_ref[...]).astype(o_ref.dtype)

    return pl.pallas_call(
        body, name="xattn_fwd", grid=(bsz, s // tq),
        in_specs=[pl.BlockSpec((1, tq, d), lambda b, i: (b, i, 0)), pl.BlockSpec((1, m, 2 * d), lambda b, i: (b, 0, 0)),
                  pl.BlockSpec(qn.shape, lambda b, i: (0, 0)), pl.BlockSpec(kn.shape, lambda b, i: (0, 0))],
        out_specs=pl.BlockSpec((1, tq, d), lambda b, i: (b, i, 0)),
        out_shape=jax.ShapeDtypeStruct((bsz, s, d), BF16),
        compiler_params=_cp(("parallel", "parallel")),
    )(q_raw, kv, qn, kn)


def _xattn_bwd(q_raw, kv, qn, kn, do):
    bsz, s, d = q_raw.shape
    m = kv.shape[1]
    tq = _tile(s, (XQ_TILE, 128))

    def body(q_ref, kv_ref, qn_ref, kn_ref, do_ref, dq_ref, dkv_ref, dqn_ref, dkn_ref):
        b, i = pl.program_id(0), pl.program_id(1)

        @pl.when((b == 0) & (i == 0))
        def _():
            dqn_ref[...] = jnp.zeros_like(dqn_ref)
            dkn_ref[...] = jnp.zeros_like(dkn_ref)

        @pl.when(i == 0)
        def _():
            dkv_ref[...] = jnp.zeros_like(dkv_ref)

        _, f = jax.vjp(_xattn_fn, q_ref[0], kv_ref[0], qn_ref[...], kn_ref[...])
        dq, dkv, dqn, dkn = f(do_ref[0].astype(F32))
        dq_ref[0] = dq.astype(dq_ref.dtype)
        dkv_ref[0] += dkv
        dqn_ref[...] += dqn
        dkn_ref[...] += dkn

    return pl.pallas_call(
        body, name="xattn_bwd", grid=(bsz, s // tq),
        in_specs=[pl.BlockSpec((1, tq, d), lambda b, i: (b, i, 0)), pl.BlockSpec((1, m, 2 * d), lambda b, i: (b, 0, 0)),
                  pl.BlockSpec(qn.shape, lambda b, i: (0, 0)), pl.BlockSpec(kn.shape, lambda b, i: (0, 0)),
                  pl.BlockSpec((1, tq, d), lambda b, i: (b, i, 0))],
        out_specs=[pl.BlockSpec((1, tq, d), lambda b, i: (b, i, 0)), pl.BlockSpec((1, m, 2 * d), lambda b, i: (b, 0, 0)),
                   pl.BlockSpec(qn.shape, lambda b, i: (0, 0)), pl.BlockSpec(kn.shape, lambda b, i: (0, 0))],
        out_shape=[jax.ShapeDtypeStruct((bsz, s, d), BF16), jax.ShapeDtypeStruct(kv.shape, F32),
                   jax.ShapeDtypeStruct(qn.shape, F32), jax.ShapeDtypeStruct(kn.shape, F32)],
        compiler_params=_cp(("arbitrary", "arbitrary")),
    )(q_raw, kv, qn, kn, do)


def _my_pos():
    return lax.axis_index("x"), lax.axis_index("y"), lax.axis_index("c")


def _all_gather_big(shard):
    r, d = shard.shape

    def body(x_ref, out_ref, send_sems, recv_sems, local_sem):
        x, y, c = _my_pos()
        me, sibling = (x, y, c), (x, y, 1 - c)
        chips = [(1 - x, y), (x, 1 - y), (1 - x, 1 - y)]

        def slot(px, py, pc):
            return out_ref.at[4 * px + 2 * py + pc]

        def copy(k, block, to, src=None):
            return pltpu.make_async_remote_copy(
                src_ref=slot(*block) if src is None else src, dst_ref=slot(*block),
                send_sem=send_sems.at[k], recv_sem=recv_sems.at[k], device_id=to, device_id_type=MESH)

        mine = pltpu.make_async_copy(x_ref, slot(*me), local_sem)
        mine.start()
        first = [copy(0, me, sibling, src=x_ref)]
        first += [copy(1 + j, me, (*chip, c), src=x_ref) for j, chip in enumerate(chips)]
        for cp in first:
            cp.start()
        passed = [copy(4 + j, (*chip, c), sibling) for j, chip in enumerate(chips)]
        for j, chip in enumerate(chips):
            copy(1 + j, (*chip, c), me).wait_recv()
            passed[j].start()
        copy(0, sibling, me).wait_recv()
        for j, chip in enumerate(chips):
            copy(4 + j, (*chip, 1 - c), me).wait_recv()
        for cp in first + passed:
            cp.wait_send()
        mine.wait()

    return pl.pallas_call(
        body, name="all_gather_weights",
        out_shape=jax.ShapeDtypeStruct((N_DEV, r, d), shard.dtype),
        in_specs=[pl.BlockSpec(memory_space=pl.ANY)], out_specs=pl.BlockSpec(memory_space=pl.ANY),
        scratch_shapes=[pltpu.SemaphoreType.DMA((7,)), pltpu.SemaphoreType.DMA((7,)), pltpu.SemaphoreType.DMA],
    )(shard)


def _exchange_sibling(g):
    _, r, d = g.shape

    def body(g_ref, out_ref, send_sems, recv_sems):
        x, y, c = _my_pos()
        copies = [pltpu.make_async_remote_copy(
            src_ref=g_ref.at[2 * k + (1 - c)], dst_ref=out_ref.at[k],
            send_sem=send_sems.at[k], recv_sem=recv_sems.at[k], device_id=(x, y, 1 - c), device_id_type=MESH)
            for k in range(4)]
        for cp in copies:
            cp.start()
        for cp in copies:
            cp.wait_recv()
        for cp in copies:
            cp.wait_send()

    return pl.pallas_call(
        body, name="grads_to_sibling",
        out_shape=jax.ShapeDtypeStruct((4, r, d), g.dtype),
        in_specs=[pl.BlockSpec(memory_space=pl.ANY)], out_specs=pl.BlockSpec(memory_space=pl.ANY),
        scratch_shapes=[pltpu.SemaphoreType.DMA((4,)), pltpu.SemaphoreType.DMA((4,))],
    )(g)


def _exchange_chips(s1):
    _, r, d = s1.shape

    def body(s_ref, out_ref, send_sems, recv_sems):
        x, y, c = _my_pos()
        copies = []
        for rel in (1, 2, 3):
            px = jnp.bitwise_xor(x, rel >> 1)
            py = jnp.bitwise_xor(y, rel & 1)
            copies.append(pltpu.make_async_remote_copy(
                src_ref=s_ref.at[2 * px + py], dst_ref=out_ref.at[rel - 1],
                send_sem=send_sems.at[rel - 1], recv_sem=recv_sems.at[rel - 1],
                device_id=(px, py, c), device_id_type=MESH))
        for cp in copies:
            cp.start()
        for cp in copies:
            cp.wait_recv()
        for cp in copies:
            cp.wait_send()

    return pl.pallas_call(
        body, name="grads_to_chips",
        out_shape=jax.ShapeDtypeStruct((3, r, d), s1.dtype),
        in_specs=[pl.BlockSpec(memory_space=pl.ANY)], out_specs=pl.BlockSpec(memory_space=pl.ANY),
        scratch_shapes=[pltpu.SemaphoreType.DMA((3,)), pltpu.SemaphoreType.DMA((3,))],
    )(s1)


def _all_reduce_small(blk, name):
    rows, d = blk.shape

    def body(x_ref, out_ref, land, send_sems, recv_sems):
        x, y, c = _my_pos()
        me = 4 * x + 2 * y + c
        copies = []
        for rel in range(1, N_DEV):
            peer = (jnp.bitwise_xor(x, rel >> 2), jnp.bitwise_xor(y, (rel >> 1) & 1), jnp.bitwise_xor(c, rel & 1))
            copies.append(pltpu.make_async_remote_copy(
                src_ref=x_ref, dst_ref=land.at[rel - 1], send_sem=send_sems.at[rel - 1], recv_sem=recv_sems.at[rel - 1],
                device_id=peer, device_id_type=MESH))
        for cp in copies:
            cp.start()
        for cp in copies:
            cp.wait_recv()
        acc = jnp.zeros((rows, d), F32)
        for dev in range(N_DEV):
            rel = jnp.bitwise_xor(me, dev)
            got = land[jnp.maximum(rel - 1, 0)]
            acc = acc + jnp.where(rel == 0, x_ref[...], got)
        out_ref[...] = acc
        for cp in copies:
            cp.wait_send()

    return pl.pallas_call(
        body, name=name,
        out_shape=jax.ShapeDtypeStruct((rows, d), F32),
        in_specs=[pl.BlockSpec(memory_space=pltpu.VMEM)], out_specs=pl.BlockSpec(memory_space=pltpu.VMEM),
        scratch_shapes=[pltpu.VMEM((N_DEV - 1, rows, d), F32), pltpu.SemaphoreType.DMA((N_DEV - 1,)),
                        pltpu.SemaphoreType.DMA((N_DEV - 1,))],
    )(blk)


def _cast_rows(x, dtype, name):
    return _rowwise(lambda v: v, [x], [], [(x.shape[1], dtype)], [], name=name, tm=CFG.pack_tile)[0]


def _sum_sibling(g, recv1, c_idx):
    _, r, d = g.shape
    tm = CFG.pack_tile

    def body(c_ref, g_ref, r_ref, o_ref):
        o_ref[0] = (g_ref[0] + r_ref[0]).astype(o_ref.dtype)

    grid_spec = pltpu.PrefetchScalarGridSpec(
        num_scalar_prefetch=1, grid=(4, r // tm),
        in_specs=[pl.BlockSpec((1, tm, d), lambda k, i, c_ref: (2 * k + c_ref[0], i, 0)),
                  pl.BlockSpec((1, tm, d), lambda k, i, c_ref: (k, i, 0))],
        out_specs=pl.BlockSpec((1, tm, d), lambda k, i, c_ref: (k, i, 0)))
    return pl.pallas_call(
        body, name="sum_sibling", grid_spec=grid_spec,
        out_shape=jax.ShapeDtypeStruct((4, r, d), BF16),
        compiler_params=_cp(("parallel", "parallel")),
    )(c_idx, g, recv1)


def _adamw_math(w, g, m, v):
    m2 = ADAM_B1 * m + (1.0 - ADAM_B1) * g
    v2 = ADAM_B2 * v + (1.0 - ADAM_B2) * (g * g)
    m_hat = m2 / (1.0 - ADAM_B1 ** ADAM_STEP)
    v_hat = v2 / (1.0 - ADAM_B2 ** ADAM_STEP)
    delta = -ADAM_LR * (m_hat / (jnp.sqrt(v_hat) + ADAM_EPS) + ADAM_WD * w)
    return delta, m2, v2


def _adamw_big(g, recv1, recv2, w, m, v, idx):
    _, r, d = g.shape
    tm = CFG.pack_tile

    def body(idx_ref, g_ref, r1_ref, ra_ref, rb_ref, rc_ref, w_ref, m_ref, v_ref, og, od, om, ov):
        grad = (g_ref[0] + r1_ref[0]) + ra_ref[0].astype(F32) + rb_ref[0].astype(F32) + rc_ref[0].astype(F32)
        delta, m2, v2 = _adamw_math(w_ref[...], grad, m_ref[...], v_ref[...])
        og[...] = grad
        od[...] = delta
        om[...] = m2
        ov[...] = v2

    flat = pl.BlockSpec((tm, d), lambda i, idx_ref: (i, 0))
    grid_spec = pltpu.PrefetchScalarGridSpec(
        num_scalar_prefetch=1, grid=(r // tm,),
        in_specs=[pl.BlockSpec((1, tm, d), lambda i, idx_ref: (idx_ref[0], i, 0)),
                  pl.BlockSpec((1, tm, d), lambda i, idx_ref: (idx_ref[1], i, 0)),
                  pl.BlockSpec((1, tm, d), lambda i, idx_ref: (0, i, 0)),
                  pl.BlockSpec((1, tm, d), lambda i, idx_ref: (1, i, 0)),
                  pl.BlockSpec((1, tm, d), lambda i, idx_ref: (2, i, 0)),
                  flat, flat, flat],
        out_specs=[flat, flat, flat, flat])
    shp = jax.ShapeDtypeStruct((r, d), F32)
    return pl.pallas_call(
        body, name="adamw_big", grid_spec=grid_spec, out_shape=[shp, shp, shp, shp],
        compiler_params=_cp(("parallel",)),
    )(idx, g, recv1, recv2, recv2, recv2, w, m, v)


def _rows_of(v, d):
    flat = v.reshape(-1)
    pad = (-flat.shape[0]) % d
    if pad:
        flat = jnp.concatenate([flat, jnp.zeros((pad,), flat.dtype)])
    return flat.reshape(-1, d)


def _pad_rows(a, mult):
    pad = (-a.shape[0]) % mult
    if pad:
        a = jnp.concatenate([a, jnp.zeros((pad,) + a.shape[1:], a.dtype)], axis=0)
    return a


_BIG = ("w_in", "w_xkv", "w_up", "w_proj_gdn", "w_proj_sb", "w_out", "w_xq", "w_xo", "w_down")
_COL_SHARDED = ("w_in", "w_xkv", "w_up")
_SMALL_REP = ("norm_mix", "norm_x", "norm_mem", "norm_ffn", "a_log", "dt_bias", "gdn_out_norm", "xq_norm", "xk_norm")
_SMALL_CONV = ("conv_gdn", "conv_ffn")


def _part_rows(shapes):
    out = []
    for n in _BIG:
        rows, cols = shapes[n]
        cnt = cols if n in _COL_SHARDED else rows
        out.append((cnt, cnt + (-cnt) % PACK_ROW_ALIGN))
    return out


def _pack_big_shards(shards, shapes):
    parts = []
    for n, (_, padded) in zip(_BIG, _part_rows(shapes)):
        parts.append(_pad_rows(shards[n].T if n in _COL_SHARDED else shards[n], padded))
    return _pad_rows(jnp.concatenate(parts, axis=0), CFG.pack_tile)


def _unpack_gathered(gath, shapes):
    out, r0 = {}, 0
    for n, (cnt, padded) in zip(_BIG, _part_rows(shapes)):
        out[n] = gath[:, r0:r0 + cnt, :].reshape(N_DEV * cnt, gath.shape[2])
        r0 += padded
    return out


def _pack_full_grads(grads, shapes):
    d = CFG.d
    parts = []
    for n, (cnt, padded) in zip(_BIG, _part_rows(shapes)):
        g = grads[n].reshape(N_DEV, cnt, d)
        if padded > cnt:
            g = jnp.concatenate([g, jnp.zeros((N_DEV, padded - cnt, d), g.dtype)], axis=1)
        parts.append(g)
    full = jnp.concatenate(parts, axis=1)
    pad = (-full.shape[1]) % CFG.pack_tile
    if pad:
        full = jnp.concatenate([full, jnp.zeros((N_DEV, pad, d), full.dtype)], axis=1)
    return full


def _unpack_shard(packed, shapes):
    out, r0 = {}, 0
    for n, (cnt, padded) in zip(_BIG, _part_rows(shapes)):
        part = packed[r0:r0 + cnt]
        out[n] = (part.T if n in _COL_SHARDED else part).reshape((1,) + tuple(shapes[n]))
        r0 += padded
    return out


def kernel(x, mem, norm_mix, w_in, conv_gdn, a_log, dt_bias, gdn_out_norm, w_proj_gdn, w_proj_sb, w_out, norm_x, norm_mem, w_xq, w_xkv, xq_norm, xk_norm, w_xo, norm_ffn, w_up, conv_ffn, w_down, loss_target, m_norm_mix, m_w_in, m_conv_gdn, m_a_log, m_dt_bias, m_gdn_out_norm, m_w_proj_gdn, m_w_proj_sb, m_w_out, m_norm_x, m_norm_mem, m_w_xq, m_w_xkv, m_xq_norm, m_xk_norm, m_w_xo, m_norm_ffn, m_w_up, m_conv_ffn, m_w_down, v_norm_mix, v_w_in, v_conv_gdn, v_a_log, v_dt_bias, v_gdn_out_norm, v_w_proj_gdn, v_w_proj_sb, v_w_out, v_norm_x, v_norm_mem, v_w_xq, v_w_xkv, v_xq_norm, v_xk_norm, v_w_xo, v_norm_ffn, v_w_up, v_conv_ffn, v_w_down):
    names = ("norm_mix", "w_in", "conv_gdn", "a_log", "dt_bias", "gdn_out_norm", "w_proj_gdn", "w_proj_sb", "w_out",
             "norm_x", "norm_mem", "w_xq", "w_xkv", "xq_norm", "xk_norm", "w_xo", "norm_ffn", "w_up", "conv_ffn", "w_down")
    wts = dict(zip(names, (norm_mix, w_in, conv_gdn, a_log, dt_bias, gdn_out_norm, w_proj_gdn, w_proj_sb, w_out,
                           norm_x, norm_mem, w_xq, w_xkv, xq_norm, xk_norm, w_xo, norm_ffn, w_up, conv_ffn, w_down)))
    mom = dict(zip(names, (m_norm_mix, m_w_in, m_conv_gdn, m_a_log, m_dt_bias, m_gdn_out_norm, m_w_proj_gdn, m_w_proj_sb,
                           m_w_out, m_norm_x, m_norm_mem, m_w_xq, m_w_xkv, m_xq_norm, m_xk_norm, m_w_xo, m_norm_ffn, m_w_up,
                           m_conv_ffn, m_w_down)))
    vel = dict(zip(names, (v_norm_mix, v_w_in, v_conv_gdn, v_a_log, v_dt_bias, v_gdn_out_norm, v_w_proj_gdn, v_w_proj_sb,
                           v_w_out, v_norm_x, v_norm_mem, v_w_xq, v_w_xkv, v_xq_norm, v_xk_norm, v_w_xo, v_norm_ffn, v_w_up,
                           v_conv_ffn, v_w_down)))
    cfg = CFG
    d, bsz, s = cfg.d, cfg.b, cfg.s
    t = bsz * s
    gh, sbh = cfg.gh, cfg.sbh
    gw, sw = gh * HD, sbh * HD
    nchunk = s // cfg.gch
    mx, my, mc = _my_pos()
    me = 4 * mx + 2 * my + mc

    shard_shapes = {n: tuple(wts[n].shape[1:]) for n in _BIG}

    packed_w = _pack_big_shards({n: wts[n][0] for n in _BIG}, shard_shapes)
    gathered = _all_gather_big(_cast_rows(packed_w, BF16, "cast_weights"))
    full = _unpack_gathered(gathered, shard_shapes)

    conv_rows = {n: _rows_of(wts[n][0], d) for n in _SMALL_CONV}
    conv_cnt = {n: conv_rows[n].shape[0] for n in _SMALL_CONV}
    conv_blk = _pad_rows(jnp.concatenate([conv_rows[n] for n in _SMALL_CONV], axis=0), SUBLANE)
    conv_all = jnp.zeros((N_DEV,) + conv_blk.shape, F32)
    conv_all = lax.dynamic_update_slice(conv_all, conv_blk[None], (me, 0, 0))
    conv_all = _all_reduce_small(conv_all.reshape(-1, d), "gather_conv_taps").reshape((N_DEV,) + conv_blk.shape)

    def full_conv(n, r0):
        k, cols = wts[n].shape[1], wts[n].shape[2]
        part = conv_all[:, r0:r0 + conv_cnt[n], :].reshape(N_DEV, -1)[:, :k * cols].reshape(N_DEV, k, cols)
        return part.transpose(1, 0, 2).reshape(k, N_DEV * cols)

    cgdn = full_conv("conv_gdn", 0)
    cffn = full_conv("conv_ffn", conv_cnt["conv_gdn"])

    win = full["w_in"]
    o_ab = 3 * gw
    o_z = o_ab + 2 * gh
    o_sb = o_z + gw
    o_gate = o_sb + 3 * sw
    w_qkv = win[:o_ab]
    w_ab = jnp.concatenate([win[o_ab:o_z], jnp.zeros((LANE - 2 * gh, d), win.dtype)], axis=0)
    w_z = win[o_z:o_sb]
    w_sb = win[o_sb:o_gate]
    w_gate = win[o_gate:]

    alog_p = jnp.concatenate([a_log.reshape(1, -1), jnp.zeros((1, LANE - gh), F32)], axis=1)
    dtb_p = jnp.concatenate([dt_bias.reshape(1, -1), jnp.zeros((1, LANE - gh), F32)], axis=1)
    onorm = gdn_out_norm.reshape(1, HD)

    x2 = x.reshape(t, d)
    tgt2 = loss_target.reshape(t, d)
    mem2 = mem.reshape(bsz * cfg.mem, d)

    (xn,) = _rowwise(_rms, [x2], [norm_mix], [(d, BF16)], [], name="norm_mix_fwd")
    p_qkv = _mm(xn, w_qkv, tb=True, name="proj_qkv")
    p_ab = _mm(xn, w_ab, tb=True, name="proj_ab")
    p_z = _mm(xn, w_z, tb=True, name="proj_z")
    p_sb = _mm(xn, w_sb, tb=True, name="proj_sb")
    p_gate = _mm(xn, w_gate, tb=True, name="proj_gate")

    qkv_c = _gdn_conv_fwd(p_qkv.reshape(bsz, s, 3 * gw), cgdn)
    (gb,) = _rowwise(_gates_fn, [p_ab], [alog_p, dtb_p], [(LANE, F32)], [], name="gdn_gates_fwd")
    gbt = gb.reshape(bsz, s, LANE)[:, :, :2 * gh].transpose(0, 2, 1).reshape(bsz, 2 * gh, nchunk, 1, cfg.gch)
    o_a, states = _gdn_fwd(qkv_c, p_z.reshape(bsz, s, gw), gbt, onorm)
    o_b = _sb_fwd(p_sb.reshape(bsz, s, 3 * sw))

    pa = _mm(o_a.reshape(t, gw), full["w_proj_gdn"], name="proj_gdn_out")
    pb = _mm(o_b.reshape(t, sw), full["w_proj_sb"], name="proj_sb_out")

    def merge_fn(pa_, pb_, gate_):
        return _sigmoid(gate_[:, :d]) * pa_ + _sigmoid(gate_[:, d:]) * pb_

    (merged,) = _rowwise(merge_fn, [pa, pb, p_gate], [], [(d, BF16)], [], name="merge_fwd")
    h1 = _mm(merged, full["w_out"], add=x2, name="mixer_out")

    (hn_x,) = _rowwise(_rms, [h1], [norm_x], [(d, BF16)], [], name="norm_x_fwd")
    (mn,) = _rowwise(_rms, [mem2], [norm_mem], [(d, BF16)], [], name="norm_mem_fwd")
    q_raw = _mm(hn_x, full["w_xq"], name="xattn_q")
    kv = _mm(mn, full["w_xkv"], tb=True, name="xattn_kv")
    xo = _xattn_fwd(q_raw.reshape(bsz, s, d), kv.reshape(bsz, cfg.mem, 2 * d), xq_norm, xk_norm)
    h2 = _mm(xo.reshape(t, d), full["w_xo"], add=h1, name="xattn_out")

    (hn_f,) = _rowwise(_rms, [h2], [norm_ffn], [(d, BF16)], [], name="norm_ffn_fwd")
    up = _mm(hn_f, full["w_up"], tb=True, name="ffn_up")
    act = _ffn_conv_fwd(up.reshape(bsz, s, 2 * cfg.dff), cffn)
    y = _mm(act.reshape(t, cfg.dff), full["w_down"], add=h2, name="ffn_down")

    def loss_fn(y_, tg_):
        err = y_ - tg_
        part = 0.5 * jnp.sum(err * err) / d
        return err / d, err / d, jnp.full((1, LANE), part, F32)

    dy, dy_b, loss_part = _rowwise(loss_fn, [y, tgt2], [], [(d, F32), (d, BF16)], [(1, LANE)], name="loss")

    grads = {}
    dact = _mm(dy_b, full["w_down"], tb=True, name="d_act")
    grads["w_down"] = _mm(act.reshape(t, cfg.dff), dy_b, ta=True, name="dw_down")
    dup1, dup2, dcf1, dcf2 = _ffn_conv_bwd(up.reshape(bsz, s, 2 * cfg.dff), cffn, dact.reshape(bsz, s, cfg.dff))
    dup = jnp.concatenate([dup1, dup2], axis=2).reshape(t, 2 * cfg.dff)
    g_conv_ffn = jnp.concatenate([dcf1, dcf2], axis=1)
    dhn_f = _mm(dup, full["w_up"], name="d_hn_ffn")
    grads["w_up"] = _mm(dup, hn_f, ta=True, name="dw_up")

    def norm_bwd_fn(h_, res_, dn_, g_):
        _, f = jax.vjp(_rms, h_, g_)
        dh, dg = f(dn_)
        return res_ + dh, dg

    def norm_bwd_copy_fn(h_, res_, dn_, g_):
        dres, dg = norm_bwd_fn(h_, res_, dn_, g_)
        return dres, dres, dg

    dh2, dh2_b, g_norm_ffn = _rowwise(norm_bwd_copy_fn, [h2, dy, dhn_f], [norm_ffn], [(d, F32), (d, BF16)], [(1, d)],
                                      name="norm_ffn_bwd")

    dxo = _mm(dh2_b, full["w_xo"], tb=True, out_dtype=BF16, name="d_xo")
    grads["w_xo"] = _mm(xo.reshape(t, d), dh2_b, ta=True, name="dw_xo")
    dq_raw, dkv, g_xq_norm, g_xk_norm = _xattn_bwd(q_raw.reshape(bsz, s, d), kv.reshape(bsz, cfg.mem, 2 * d),
                                                   xq_norm, xk_norm, dxo.reshape(bsz, s, d))
    dq_raw2 = dq_raw.reshape(t, d)
    dkv2 = dkv.reshape(bsz * cfg.mem, 2 * d)
    dhn_x = _mm(dq_raw2, full["w_xq"], tb=True, name="d_hn_x")
    grads["w_xq"] = _mm(hn_x, dq_raw2, ta=True, name="dw_xq")
    dmn = _mm(dkv2, full["w_xkv"], name="d_mn")
    grads["w_xkv"] = _mm(dkv2, mn, ta=True, name="dw_xkv")

    def norm_w_bwd_fn(h_, dn_, g_):
        _, f = jax.vjp(lambda gg: _rms(h_, gg), g_)
        return f(dn_)[0]

    (g_norm_mem,) = _rowwise(norm_w_bwd_fn, [mem2, dmn], [norm_mem], [], [(1, d)], name="norm_mem_bwd")
    dh1, dh1_b, g_norm_x = _rowwise(norm_bwd_copy_fn, [h1, dh2, dhn_x], [norm_x], [(d, F32), (d, BF16)], [(1, d)],
                                    name="norm_x_bwd")

    dmerged = _mm(dh1_b, full["w_out"], tb=True, name="d_merged")
    grads["w_out"] = _mm(merged, dh1_b, ta=True, name="dw_out")

    def merge_bwd_fn(pa_, pb_, gate_, dm_):
        _, f = jax.vjp(merge_fn, pa_, pb_, gate_)
        return f(dm_)

    dpa, dpb, dgate = _rowwise(merge_bwd_fn, [pa, pb, p_gate, dmerged], [], [(d, BF16), (d, BF16), (2 * d, BF16)], [],
                               name="merge_bwd")
    do_a = _mm(dpa, full["w_proj_gdn"], tb=True, name="d_o_gdn")
    grads["w_proj_gdn"] = _mm(o_a.reshape(t, gw), dpa, ta=True, name="dw_proj_gdn")
    do_b = _mm(dpb, full["w_proj_sb"], tb=True, name="d_o_sb")
    grads["w_proj_sb"] = _mm(o_b.reshape(t, sw), dpb, ta=True, name="dw_proj_sb")

    dsq, dsk, dsv = _sb_bwd(p_sb.reshape(bsz, s, 3 * sw), do_b.reshape(bsz, s, sw))
    dp_sb = jnp.concatenate([dsq, dsk, dsv], axis=2).reshape(t, 3 * sw)

    dgq, dgk, dgv, dz, dg, dbeta, g_onorm = _gdn_bwd(qkv_c, p_z.reshape(bsz, s, gw), gbt, onorm, states,
                                                     do_a.reshape(bsz, s, gw))
    dgb = jnp.concatenate([dg, dbeta], axis=1).reshape(bsz, 2 * gh, s).transpose(0, 2, 1)
    dgb = jnp.concatenate([dgb, jnp.zeros((bsz, s, LANE - 2 * gh), F32)], axis=2).reshape(t, LANE)

    def gates_bwd_fn(ab_, dgb_, alog_, dtb_):
        _, f = jax.vjp(_gates_fn, ab_, alog_, dtb_)
        return f(dgb_)

    dp_ab, g_alog, g_dtb = _rowwise(gates_bwd_fn, [p_ab, dgb], [alog_p, dtb_p], [(LANE, BF16)], [(1, LANE), (1, LANE)],
                                    name="gdn_gates_bwd")
    dqkv_c = jnp.concatenate([dgq, dgk, dgv], axis=2)
    dp_qkv, g_conv_gdn = _gdn_conv_bwd(p_qkv.reshape(bsz, s, 3 * gw), cgdn, dqkv_c)
    dp_qkv = dp_qkv.reshape(t, 3 * gw)
    dp_z = dz.reshape(t, gw)

    dxn = _mm(dp_qkv, w_qkv, name="d_xn_qkv")
    dxn = _mm(dp_ab, w_ab, add=dxn, name="d_xn_ab")
    dxn = _mm(dp_z, w_z, add=dxn, name="d_xn_z")
    dxn = _mm(dp_sb, w_sb, add=dxn, name="d_xn_sb")
    dxn = _mm(dgate, w_gate, add=dxn, name="d_xn_gate")
    grads["w_in"] = jnp.concatenate([
        _mm(dp_qkv, xn, ta=True, name="dw_in_qkv"),
        _mm(dp_ab, xn, ta=True, name="dw_in_ab")[:2 * gh],
        _mm(dp_z, xn, ta=True, name="dw_in_z"),
        _mm(dp_sb, xn, ta=True, name="dw_in_sb"),
        _mm(dgate, xn, ta=True, name="dw_in_gate")], axis=0)
    grad_x, g_norm_mix = _rowwise(norm_bwd_fn, [x2, dh1, dxn], [norm_mix], [(d, F32)], [(1, d)], name="norm_mix_bwd")

    small_g = {"norm_mix": g_norm_mix, "norm_x": g_norm_x, "norm_mem": g_norm_mem, "norm_ffn": g_norm_ffn,
               "a_log": g_alog[:, :gh], "dt_bias": g_dtb[:, :gh], "gdn_out_norm": g_onorm,
               "xq_norm": g_xq_norm, "xk_norm": g_xk_norm}
    sm_rows = [_rows_of(small_g[n], d) for n in _SMALL_REP] + [_rows_of(loss_part, d)]
    sm_rows += [_rows_of(g_conv_gdn, d), _rows_of(g_conv_ffn, d)]
    sm_cnt = [r.shape[0] for r in sm_rows]
    sm_sum = _all_reduce_small(_pad_rows(jnp.concatenate(sm_rows, axis=0), SUBLANE), "all_reduce_small_grads")
    offs = [0]
    for cnt in sm_cnt:
        offs.append(offs[-1] + cnt)
    small_grad = {}
    for i, n in enumerate(_SMALL_REP):
        small_grad[n] = sm_sum[offs[i]:offs[i + 1]].reshape(-1)[:wts[n].size].reshape(wts[n].shape)
    loss = sm_sum[offs[len(_SMALL_REP)], 0]
    for i, n in enumerate(_SMALL_CONV):
        k, cols = wts[n].shape[1], wts[n].shape[2]
        o = offs[len(_SMALL_REP) + 1 + i]
        fullg = sm_sum[o:o + sm_cnt[len(_SMALL_REP) + 1 + i]].reshape(-1)[:k * cols * N_DEV].reshape(k, N_DEV * cols)
        small_grad[n] = lax.dynamic_slice(fullg, (0, me * cols), (k, cols)).reshape(wts[n].shape)

    small_names = _SMALL_REP + _SMALL_CONV

    def pack_small(src):
        return _pad_rows(jnp.concatenate([_rows_of(src[n], d) for n in small_names], axis=0), SUBLANE)

    sw_, sg_, sm_, sv_ = pack_small(wts), pack_small(small_grad), pack_small(mom), pack_small(vel)
    sd_, snm_, snv_ = _rowwise(_adamw_math, [sw_, sg_, sm_, sv_], [], [(d, F32)] * 3, [], name="adamw_small", tm=sw_.shape[0])

    def unpack_small(packed):
        out, r0 = {}, 0
        for n in small_names:
            cnt = _rows_of(wts[n], d).shape[0]
            out[n] = packed[r0:r0 + cnt].reshape(-1)[:wts[n].size].reshape(wts[n].shape)
            r0 += cnt
        return out

    small_delta, small_m, small_v = unpack_small(sd_), unpack_small(snm_), unpack_small(snv_)

    gpack = _pack_full_grads(grads, shard_shapes)
    recv1 = _exchange_sibling(gpack)
    s1 = _sum_sibling(gpack, recv1, jnp.reshape(mc, (1,)).astype(jnp.int32))
    recv2 = _exchange_chips(s1)
    idx = jnp.stack([me, 2 * mx + my]).astype(jnp.int32)
    pm = _pack_big_shards({n: mom[n][0] for n in _BIG}, shard_shapes)
    pv = _pack_big_shards({n: vel[n][0] for n in _BIG}, shard_shapes)
    bg, bd, bm, bv = _adamw_big(gpack, recv1, recv2, packed_w, pm, pv, idx)
    big_grad, big_delta, big_m, big_v = (_unpack_shard(a, shard_shapes) for a in (bg, bd, bm, bv))

    def pick(big, small, n):
        return big[n] if n in big else small[n]

    outs = [loss, grad_x.reshape(bsz, s, d)]
    outs += [pick(big_grad, small_grad, n) for n in names]
    outs += [pick(big_delta, small_delta, n) for n in names]
    outs += [pick(big_m, small_m, n) for n in names]
    outs += [pick(big_v, small_v, n) for n in names]
    return tuple(outs)
```

```python
import functools

import jax
import jax.numpy as jnp
from jax import lax
from jax.experimental import pallas as pl
from jax.experimental.pallas import tpu as pltpu

F32 = jnp.float32
BF16 = jnp.bfloat16
FDOT_PRECISION = lax.Precision.HIGH

LANE = 128
SUBLANE = 8
PACK_ROW_ALIGN = 16
VMEM_LIMIT = 56 * 2 ** 20
N_DEV = 8
MESH = pl.DeviceIdType.MESH

EPS = 1e-6
ADAM_LR = 0.001
ADAM_B1 = 0.9
ADAM_B2 = 0.999
ADAM_EPS = 1e-08
ADAM_WD = 0.01
ADAM_STEP = 10


class _Cfg:
    d = 1024
    b = 4
    s = 2048
    mem = 256
    gh = 8
    gch = 64
    sbh = 8
    xh = 4
    dff = 2816
    pack_tile = 256


CFG = _Cfg()
HD = 128
SB_BLK = 128
SB_HP = 4
GDN_HP = 8
MM_TILES = (1024, 1408, 704, 512, 256, 128)
MM_K_TILES = (1024, 1408, 704, 512, 256, 128)
MM_K_TILES_F32 = (512, 704, 256, 128)
CONV_CB = 256
XQ_TILE = 256


def _tile(n, prefs):
    for t in prefs:
        if n % t == 0:
            return t
    raise ValueError(f"no tile for {n}")


def _cp(sem, **kw):
    return pltpu.CompilerParams(dimension_semantics=sem, vmem_limit_bytes=VMEM_LIMIT, **kw)


def _dims(kind):
    return {"nn": (((1,), (0,)), ((), ())), "nt": (((1,), (1,)), ((), ())), "tn": (((0,), (0,)), ((), ()))}[kind]


def _raw_bdot(a, b, kind):
    return lax.dot_general(a.astype(BF16), b.astype(BF16), _dims(kind), preferred_element_type=F32)


def _raw_fdot(a, b, kind):
    return lax.dot_general(a.astype(F32), b.astype(F32), _dims(kind), precision=FDOT_PRECISION,
                           preferred_element_type=F32)


def _make_dot(raw):
    @functools.partial(jax.custom_vjp, nondiff_argnums=(2,))
    def dot(a, b, kind):
        return raw(a, b, kind)

    def fwd(a, b, kind):
        return raw(a, b, kind), (a, b)

    def bwd(kind, res, g):
        a, b = res
        if kind == "nn":
            return raw(g, b, "nt").astype(a.dtype), raw(a, g, "tn").astype(b.dtype)
        if kind == "nt":
            return raw(g, b, "nn").astype(a.dtype), raw(g, a, "tn").astype(b.dtype)
        return raw(b, g, "nt").astype(a.dtype), raw(a, g, "nn").astype(b.dtype)

    dot.defvjp(fwd, bwd)
    return dot


_bdot = _make_dot(_raw_bdot)
_fdot = _make_dot(_raw_fdot)


def _split_dot(x, m01):
    hi = x.astype(BF16)
    lo = (x - hi.astype(F32)).astype(BF16)
    return (lax.dot_general(hi, m01, _dims("nn"), preferred_element_type=F32)
            + lax.dot_general(lo, m01, _dims("nn"), preferred_element_type=F32))


_sigmoid = jax.nn.sigmoid


def _silu(x):
    return x * _sigmoid(x)


def _softplus(x):
    return jnp.maximum(x, 0.0) + jnp.log1p(jnp.exp(-jnp.abs(x)))


def _rms(x, g):
    return x * lax.rsqrt(jnp.mean(x * x, axis=-1, keepdims=True) + EPS) * g


def _iota2(shape, dim):
    return lax.broadcasted_iota(jnp.int32, shape, dim)


def _mm(a, b, *, ta=False, tb=False, add=None, out_dtype=F32, name):
    if ta:
        kd, m = a.shape
    else:
        m, kd = a.shape
    if tb:
        n, kb = b.shape
    else:
        kb, n = b.shape
    assert kd == kb, (a.shape, b.shape, ta, tb)
    tm = _tile(m, MM_TILES)
    tn = _tile(n, MM_TILES)
    wide = max(a.dtype.itemsize, b.dtype.itemsize) > 2
    tk = _tile(kd, MM_K_TILES_F32 if wide else MM_K_TILES)
    nk = kd // tk
    kind_dims = (((0 if ta else 1,), (1 if tb else 0,)), ((), ()))

    def body(*refs):
        a_ref, b_ref = refs[:2]
        add_ref = refs[2] if add is not None else None
        o_ref = refs[3 if add is not None else 2]
        part = lax.dot_general(a_ref[...].astype(BF16), b_ref[...].astype(BF16), kind_dims,
                               preferred_element_type=F32)

        def finish(r):
            if add is not None:
                r = r + add_ref[...].astype(F32)
            o_ref[...] = r.astype(o_ref.dtype)

        if nk == 1:
            finish(part)
            return
        acc = refs[-1]
        k = pl.program_id(2)

        @pl.when(k == 0)
        def _():
            acc[...] = part

        @pl.when((k > 0) & (k < nk - 1))
        def _():
            acc[...] += part

        @pl.when(k == nk - 1)
        def _():
            finish(acc[...] + part)

    a_spec = pl.BlockSpec((tk, tm), lambda i, j, k: (k, i)) if ta else pl.BlockSpec((tm, tk), lambda i, j, k: (i, k))
    b_spec = pl.BlockSpec((tn, tk), lambda i, j, k: (j, k)) if tb else pl.BlockSpec((tk, tn), lambda i, j, k: (k, j))
    in_specs = [a_spec, b_spec]
    args = [a, b]
    if add is not None:
        in_specs.append(pl.BlockSpec((tm, tn), lambda i, j, k: (i, j)))
        args.append(add)
    return pl.pallas_call(
        body, name=name, grid=(m // tm, n // tn, nk),
        in_specs=in_specs, out_specs=pl.BlockSpec((tm, tn), lambda i, j, k: (i, j)),
        out_shape=jax.ShapeDtypeStruct((m, n), out_dtype),
        scratch_shapes=[pltpu.VMEM((tm, tn), F32)] if nk > 1 else [],
        compiler_params=_cp(("parallel", "parallel", "arbitrary")),
    )(*args)


def _rowwise(fn, rows, pars, out_rows, out_accs, *, name, tm=None):
    t = rows[0].shape[0]
    if tm is None:
        tm = _tile(t, (256, 128, 64, 32, 16))
    assert t % tm == 0, (t, tm)
    n_r, n_p, n_or, n_oa = len(rows), len(pars), len(out_rows), len(out_accs)

    def body(*refs):
        r_in = refs[:n_r]
        p_in = refs[n_r:n_r + n_p]
        o_r = refs[n_r + n_p:n_r + n_p + n_or]
        o_a = refs[n_r + n_p + n_or:]
        outs = fn(*[r[...] for r in r_in], *[p[...] for p in p_in])
        if not isinstance(outs, (tuple, list)):
            outs = (outs,)
        assert len(outs) == n_or + n_oa, (name, len(outs))
        for ref, val in zip(o_r, outs[:n_or]):
            ref[...] = val.astype(ref.dtype)
        if n_oa:
            @pl.when(pl.program_id(0) == 0)
            def _():
                for ref in o_a:
                    ref[...] = jnp.zeros_like(ref)

            for ref, val in zip(o_a, outs[n_or:]):
                ref[...] += val.astype(F32)

    in_specs = [pl.BlockSpec((tm, r.shape[1]), lambda i: (i, 0)) for r in rows]
    in_specs += [pl.BlockSpec(p.shape, lambda i: (0, 0)) for p in pars]
    out_specs = [pl.BlockSpec((tm, c), lambda i: (i, 0)) for c, _ in out_rows]
    out_specs += [pl.BlockSpec(s, lambda i: (0, 0)) for s in out_accs]
    out_shape = [jax.ShapeDtypeStruct((t, c), dt) for c, dt in out_rows]
    out_shape += [jax.ShapeDtypeStruct(s, F32) for s in out_accs]
    return pl.pallas_call(
        body, name=name, grid=(t // tm,), in_specs=in_specs, out_specs=out_specs, out_shape=out_shape,
        compiler_params=_cp(("arbitrary",)),
    )(*rows, *pars)


def _shift_down(x, sh):
    rolled = pltpu.roll(x, sh, 0)
    return jnp.where(_iota2(x.shape, 0) >= sh, rolled, 0.0)


def _shift_up(x, sh):
    s = x.shape[0]
    rolled = pltpu.roll(x, s - sh, 0)
    return jnp.where(_iota2(x.shape, 0) < s - sh, rolled, 0.0)


def _conv(x, w):
    k = w.shape[0]
    y = x * w[k - 1:k, :]
    for i in range(k - 1):
        y = y + _shift_down(x, k - 1 - i) * w[i:i + 1, :]
    return y


def _conv_bwd(x, w, dy):
    k = w.shape[0]
    dx = dy * w[k - 1:k, :]
    dws = []
    for i in range(k - 1):
        dx = dx + _shift_up(dy, k - 1 - i) * w[i:i + 1, :]
        dws.append(jnp.sum(dy * _shift_down(x, k - 1 - i), axis=0, keepdims=True))
    dws.append(jnp.sum(dy * x, axis=0, keepdims=True))
    return dx, dws


def _gdn_post(y, j, nqb):
    a = _silu(y)
    sc = jnp.where(j < nqb, HD ** -0.5, 1.0).astype(F32)
    outs = []
    for h in range(y.shape[1] // HD):
        ah = a[:, h * HD:(h + 1) * HD]
        l2 = ah * lax.rsqrt(jnp.sum(ah * ah, axis=-1, keepdims=True) + EPS)
        outs.append(jnp.where(j < 2 * nqb, l2 * sc, ah))
    return jnp.concatenate(outs, axis=1) if len(outs) > 1 else outs[0]


def _gdn_conv_fwd(x, w):
    bsz, s, c3 = x.shape
    k = w.shape[0]
    nb = c3 // CONV_CB
    nqb = nb // 3

    def body(x_ref, w_ref, o_ref):
        j = pl.program_id(1)
        o_ref[0] = _gdn_post(_conv(x_ref[0], w_ref[...]), j, nqb)

    return pl.pallas_call(
        body, name="gdn_conv_fwd", grid=(bsz, nb),
        in_specs=[pl.BlockSpec((1, s, CONV_CB), lambda b, j: (b, 0, j)), pl.BlockSpec((k, CONV_CB), lambda b, j: (0, j))],
        out_specs=pl.BlockSpec((1, s, CONV_CB), lambda b, j: (b, 0, j)),
        out_shape=jax.ShapeDtypeStruct(x.shape, F32),
        compiler_params=_cp(("parallel", "parallel")),
    )(x, w)


def _gdn_conv_bwd(x, w, dout):
    bsz, s, c3 = x.shape
    k = w.shape[0]
    nb = c3 // CONV_CB
    nqb = nb // 3

    def body(x_ref, w_ref, d_ref, dx_ref, dw_ref):
        j = pl.program_id(0)
        b = pl.program_id(1)
        xv, wv = x_ref[0], w_ref[...]
        y = _conv(xv, wv)
        _, f = jax.vjp(lambda yy: _gdn_post(yy, j, nqb), y)
        (dy,) = f(d_ref[0])
        dx, dws = _conv_bwd(xv, wv, dy)
        dx_ref[0] = dx.astype(dx_ref.dtype)

        @pl.when(b == 0)
        def _():
            dw_ref[...] = jnp.zeros_like(dw_ref)

        for i in range(k):
            dw_ref[i:i + 1, :] += dws[i]

    return pl.pallas_call(
        body, name="gdn_conv_bwd", grid=(nb, bsz),
        in_specs=[pl.BlockSpec((1, s, CONV_CB), lambda j, b: (b, 0, j)), pl.BlockSpec((k, CONV_CB), lambda j, b: (0, j)),
                  pl.BlockSpec((1, s, CONV_CB), lambda j, b: (b, 0, j))],
        out_specs=[pl.BlockSpec((1, s, CONV_CB), lambda j, b: (b, 0, j)), pl.BlockSpec((k, CONV_CB), lambda j, b: (0, j))],
        out_shape=[jax.ShapeDtypeStruct(x.shape, BF16), jax.ShapeDtypeStruct(w.shape, F32)],
        compiler_params=_cp(("parallel", "arbitrary")),
    )(x, w, dout)


def _ffn_conv_fwd(up, w):
    bsz, s, c2 = up.shape
    k = w.shape[0]
    nb = (c2 // 2) // CONV_CB

    def body(x1_ref, x2_ref, w1_ref, w2_ref, o_ref):
        u1 = _conv(x1_ref[0], w1_ref[...])
        u2 = _conv(x2_ref[0], w2_ref[...])
        o_ref[0] = (_silu(u1) * u2).astype(o_ref.dtype)

    return pl.pallas_call(
        body, name="ffn_conv_fwd", grid=(bsz, nb),
        in_specs=[pl.BlockSpec((1, s, CONV_CB), lambda b, j: (b, 0, j)), pl.BlockSpec((1, s, CONV_CB), lambda b, j: (b, 0, j + nb)),
                  pl.BlockSpec((k, CONV_CB), lambda b, j: (0, j)), pl.BlockSpec((k, CONV_CB), lambda b, j: (0, j + nb))],
        out_specs=pl.BlockSpec((1, s, CONV_CB), lambda b, j: (b, 0, j)),
        out_shape=jax.ShapeDtypeStruct((bsz, s, c2 // 2), BF16),
        compiler_params=_cp(("parallel", "parallel")),
    )(up, up, w, w)


def _ffn_conv_bwd(up, w, dact):
    bsz, s, c2 = up.shape
    k = w.shape[0]
    half = c2 // 2
    nb = half // CONV_CB

    def body(x1_ref, x2_ref, w1_ref, w2_ref, d_ref, dx1_ref, dx2_ref, dw1_ref, dw2_ref):
        b = pl.program_id(1)
        x1, x2, w1, w2 = x1_ref[0], x2_ref[0], w1_ref[...], w2_ref[...]
        u1 = _conv(x1, w1)
        u2 = _conv(x2, w2)
        _, f = jax.vjp(lambda p, q: _silu(p) * q, u1, u2)
        du1, du2 = f(d_ref[0])
        dx1, dws1 = _conv_bwd(x1, w1, du1)
        dx2, dws2 = _conv_bwd(x2, w2, du2)
        dx1_ref[0] = dx1.astype(dx1_ref.dtype)
        dx2_ref[0] = dx2.astype(dx2_ref.dtype)

        @pl.when(b == 0)
        def _():
            dw1_ref[...] = jnp.zeros_like(dw1_ref)
            dw2_ref[...] = jnp.zeros_like(dw2_ref)

        for i in range(k):
            dw1_ref[i:i + 1, :] += dws1[i]
            dw2_ref[i:i + 1, :] += dws2[i]

    def blk(off):
        return pl.BlockSpec((1, s, CONV_CB), lambda j, b: (b, 0, j + off))

    def wblk(off):
        return pl.BlockSpec((k, CONV_CB), lambda j, b: (0, j + off))

    return pl.pallas_call(
        body, name="ffn_conv_bwd", grid=(nb, bsz),
        in_specs=[blk(0), blk(nb), wblk(0), wblk(nb), blk(0)],
        out_specs=[blk(0), blk(0), wblk(0), wblk(0)],
        out_shape=[jax.ShapeDtypeStruct((bsz, s, half), BF16), jax.ShapeDtypeStruct((bsz, s, half), BF16),
                   jax.ShapeDtypeStruct((k, half), F32), jax.ShapeDtypeStruct((k, half), F32)],
        compiler_params=_cp(("parallel", "arbitrary")),
    )(up, up, w, w, dact)


@jax.custom_vjp
def _inv_unit_lower(mats):
    c = mats[0].shape[0]
    eye = (_iota2((c, c), 0) == _iota2((c, c), 1)).astype(F32)
    ps = [-a for a in mats]
    ts = [eye + p for p in ps]
    n = 2
    while n < c:
        ps = [_raw_fdot(p, p, "nn") for p in ps]
        ts = [t + _raw_fdot(t, p, "nn") for t, p in zip(ts, ps)]
        n *= 2
    return ts


def _inv_fwd(mats):
    ts = _inv_unit_lower(mats)
    return ts, ts


def _inv_bwd(ts, gs):
    xs = [_raw_fdot(g, t, "nt") for g, t in zip(gs, ts)]
    return ([-_raw_fdot(t, x, "tn") for t, x in zip(ts, xs)],)


_inv_unit_lower.defvjp(_inv_fwd, _inv_bwd)


def _gdn_chunk(q, k, v, z, g_row, beta_row, state, onorm):
    nh = range(len(q))
    c = q[0].shape[0]
    ii, jj = _iota2((c, c), 0), _iota2((c, c), 1)
    incl, strict, eye = ii >= jj, ii > jj, ii == jj

    def to_col(row):
        return jnp.sum(jnp.where(eye, jnp.broadcast_to(row, (c, c)), 0.0), axis=1, keepdims=True)

    gc_col = [jnp.sum(jnp.where(incl, jnp.broadcast_to(g_row[h], (c, c)), 0.0), axis=1, keepdims=True) for h in nh]
    gc_row = [jnp.sum(jnp.where(eye, jnp.broadcast_to(gc_col[h], (c, c)), 0.0), axis=0, keepdims=True) for h in nh]
    beta_col = [to_col(beta_row[h]) for h in nh]
    gc_last = [jnp.sum(g_row[h], axis=1, keepdims=True) for h in nh]
    decay = [jnp.where(incl, jnp.exp(jnp.where(incl, gc_col[h] - gc_row[h], 0.0)), 0.0) for h in nh]
    kk = [_bdot(k[h], k[h], "nt") for h in nh]
    qk = [_bdot(q[h], k[h], "nt") * decay[h] for h in nh]
    tinv = _inv_unit_lower([jnp.where(strict, beta_col[h] * kk[h] * decay[h], 0.0) for h in nh])
    rhs = [jnp.concatenate([v[h] * beta_col[h], k[h] * (beta_col[h] * jnp.exp(gc_col[h]))], axis=1) for h in nh]
    uw = [_fdot(tinv[h], rhs[h], "nn") for h in nh]
    dv = v[0].shape[1]
    ws = [_bdot(uw[h][:, dv:], state[h], "nn") for h in nh]
    qs = [_bdot(q[h] * jnp.exp(gc_col[h]), state[h], "nn") for h in nh]
    v_new = [uw[h][:, :dv] - ws[h] for h in nh]
    o = [qs[h] + _bdot(qk[h], v_new[h], "nn") for h in nh]
    kv = [_bdot(k[h] * jnp.exp(gc_last[h] - gc_col[h]), v_new[h], "tn") for h in nh]
    new_state = [state[h] * jnp.exp(gc_last[h]) + kv[h] for h in nh]
    y = [_rms(o[h], onorm) * _silu(z[h]) for h in nh]
    return y, new_state


def _gdn_specs(s, c, reverse):
    n = s // c
    nn = (lambda i: n - 1 - i) if reverse else (lambda i: i)

    def qkv(off):
        return pl.BlockSpec((1, c, GDN_HP * HD), lambda b, h, i: (b, nn(i), h + off))

    def gate(off):
        return pl.BlockSpec((1, GDN_HP, 1, 1, c), lambda b, h, i: (b, h + off, nn(i), 0, 0))

    st = pl.BlockSpec((1, GDN_HP, 1, HD, HD), lambda b, h, i: (b, h, nn(i), 0, 0))
    onorm = pl.BlockSpec((1, HD), lambda b, h, i: (0, 0))
    return n, qkv, gate, st, onorm


def _gdn_fwd(qkv, z, gbt, onorm):
    bsz, s, _ = qkv.shape
    gh, c = CFG.gh, CFG.gch
    ng = gh // GDN_HP
    n, qs, gs, st, on = _gdn_specs(s, c, False)

    def body(q_ref, k_ref, v_ref, z_ref, g_ref, b_ref, on_ref, y_ref, st_ref, state):
        @pl.when(pl.program_id(2) == 0)
        def _():
            state[...] = jnp.zeros_like(state)

        nh = range(GDN_HP)
        hs = [slice(h * HD, (h + 1) * HD) for h in nh]
        s_in = [state[h] for h in nh]
        for h in nh:
            st_ref[0, h, 0] = s_in[h]
        y, s_out = _gdn_chunk([q_ref[0, :, hs[h]] for h in nh], [k_ref[0, :, hs[h]] for h in nh],
                              [v_ref[0, :, hs[h]] for h in nh], [z_ref[0, :, hs[h]] for h in nh],
                              [g_ref[0, h, 0] for h in nh], [b_ref[0, h, 0] for h in nh], s_in, on_ref[...])
        for h in nh:
            y_ref[0, :, hs[h]] = y[h].astype(y_ref.dtype)
            state[h] = s_out[h]

    return pl.pallas_call(
        body, name="gdn_fwd", grid=(bsz, ng, n),
        in_specs=[qs(0), qs(ng), qs(2 * ng), qs(0), gs(0), gs(ng), on],
        out_specs=[qs(0), st],
        out_shape=[jax.ShapeDtypeStruct((bsz, s, gh * HD), BF16), jax.ShapeDtypeStruct((bsz, gh, n, HD, HD), F32)],
        scratch_shapes=[pltpu.VMEM((GDN_HP, HD, HD), F32)],
        compiler_params=_cp(("parallel", "parallel", "arbitrary")),
    )(qkv, qkv, qkv, z, gbt, gbt, onorm)


def _gdn_bwd(qkv, z, gbt, onorm, states, dy):
    bsz, s, _ = qkv.shape
    gh, c = CFG.gh, CFG.gch
    ng = gh // GDN_HP
    n, qs, gs, st, on = _gdn_specs(s, c, True)

    def body(q_ref, k_ref, v_ref, z_ref, g_ref, b_ref, on_ref, st_ref, dy_ref,
             dq_ref, dk_ref, dv_ref, dz_ref, dg_ref, db_ref, don_ref, dstate):
        first = (pl.program_id(0) == 0) & (pl.program_id(1) == 0) & (pl.program_id(2) == 0)

        @pl.when(first)
        def _():
            don_ref[...] = jnp.zeros_like(don_ref)

        @pl.when(pl.program_id(2) == 0)
        def _():
            dstate[...] = jnp.zeros_like(dstate)

        nh = range(GDN_HP)
        hs = [slice(h * HD, (h + 1) * HD) for h in nh]
        _, f = jax.vjp(_gdn_chunk, [q_ref[0, :, hs[h]] for h in nh], [k_ref[0, :, hs[h]] for h in nh],
                       [v_ref[0, :, hs[h]] for h in nh], [z_ref[0, :, hs[h]] for h in nh],
                       [g_ref[0, h, 0] for h in nh], [b_ref[0, h, 0] for h in nh],
                       [st_ref[0, h, 0] for h in nh], on_ref[...])
        dq, dk, dv, dz, dg, db, ds, don = f(([dy_ref[0, :, hs[h]] for h in nh], [dstate[h] for h in nh]))
        for h in nh:
            dq_ref[0, :, hs[h]] = dq[h]
            dk_ref[0, :, hs[h]] = dk[h]
            dv_ref[0, :, hs[h]] = dv[h]
            dz_ref[0, :, hs[h]] = dz[h].astype(dz_ref.dtype)
            dg_ref[0, h, 0] = dg[h]
            db_ref[0, h, 0] = db[h]
            dstate[h] = ds[h]
        don_ref[...] += don

    act = jax.ShapeDtypeStruct((bsz, s, gh * HD), F32)
    gshape = jax.ShapeDtypeStruct((bsz, gh, n, 1, c), F32)
    return pl.pallas_call(
        body, name="gdn_bwd", grid=(bsz, ng, n),
        in_specs=[qs(0), qs(ng), qs(2 * ng), qs(0), gs(0), gs(ng), on, st, qs(0)],
        out_specs=[qs(0), qs(0), qs(0), qs(0), gs(0), gs(0), on],
        out_shape=[act, act, act, jax.ShapeDtypeStruct(act.shape, BF16), gshape, gshape, jax.ShapeDtypeStruct((1, HD), F32)],
        scratch_shapes=[pltpu.VMEM((GDN_HP, HD, HD), F32)],
        compiler_params=_cp(("arbitrary", "arbitrary", "arbitrary")),
    )(qkv, qkv, qkv, z, gbt, gbt, onorm, states, dy)


def _gates_fn(ab, alog, dtb):
    lane = _iota2(ab.shape, 1)
    g = -jnp.exp(alog) * _softplus(ab + dtb)
    beta = _sigmoid(ab)
    return jnp.where(lane < CFG.gh, g, jnp.where(lane < 2 * CFG.gh, beta, 0.0))


def _heads_cumsum(xs, tri):
    n = xs[0].shape[0]
    y = _split_dot(jnp.concatenate(xs, axis=0), tri)
    return [y[h * n:(h + 1) * n] for h in range(len(xs))]


def _blk_off(jblk):
    return jblk * SB_BLK if isinstance(jblk, int) else pl.multiple_of(jblk * SB_BLK, SB_BLK)


def _sb_span(qs, k_spans, mask, runs, tri_su):
    nh = range(len(qs))
    nb = k_spans[0].shape[0] // SB_BLK
    zs = [lax.dot_general(qs[h], k_spans[h], _dims("nt"), preferred_element_type=F32) for h in nh]
    l1p = [jnp.log(1.0 + jnp.exp(-jnp.abs(z))) for z in zs]
    lss = [jnp.minimum(zs[h], 0.0) - l1p[h] for h in nh]
    lfs = [lss[h] - zs[h] for h in nh]
    if mask is not None:
        lfs = [jnp.where(mask, lf, 0.0) for lf in lfs]
    units = [lfs[h][:, b * SB_BLK:(b + 1) * SB_BLK] for h in nh for b in range(nb)]
    cums = _heads_cumsum(units, tri_su)
    sfx, new_runs = [], []
    for h in nh:
        run, parts = runs[h], [None] * nb
        for b in reversed(range(nb)):
            parts[b] = cums[h * nb + b] + run
            run = run + jnp.sum(units[h * nb + b], axis=1, keepdims=True)
        sfx.append(jnp.concatenate(parts, axis=1) if nb > 1 else parts[0])
        new_runs.append(run)
    ws = [jnp.exp(lss[h] + sfx[h]) for h in nh]
    if mask is not None:
        ws = [jnp.where(mask, w, 0.0) for w in ws]
    return zs, lfs, ws, new_runs


def _sb_specs(s, w):
    def qb(off):
        return pl.BlockSpec((1, SB_BLK, w), lambda b, h, i: (b, i, h + off))

    def full(off):
        return pl.BlockSpec((1, s, w), lambda b, h, i: (b, 0, h + off))

    return qb, full


def _sb_fwd(qkv):
    bsz, s, _ = qkv.shape
    ng = CFG.sbh // SB_HP
    w = SB_HP * HD
    scale = HD ** -0.5
    qb, full = _sb_specs(s, w)

    def body(q_ref, k_ref, v_ref, o_ref):
        i = pl.program_id(2)
        r, c = _iota2((SB_BLK, SB_BLK), 0), _iota2((SB_BLK, SB_BLK), 1)
        tri_su = (r > c).astype(BF16)
        nh = range(SB_HP)
        hs = [slice(h * HD, (h + 1) * HD) for h in nh]
        qs = [(q_ref[0, :, hs[h]] * scale).astype(BF16) for h in nh]

        def span(off, nb, mask, carry):
            ks = [k_ref[0, pl.ds(off, nb * SB_BLK), hs[h]].astype(BF16) for h in nh]
            vs = [v_ref[0, pl.ds(off, nb * SB_BLK), hs[h]].astype(BF16) for h in nh]
            _, _, ws, runs = _sb_span(qs, ks, mask, [cr[1] for cr in carry], tri_su)
            pv = [lax.dot_general(ws[h].astype(BF16), vs[h], _dims("nn"), preferred_element_type=F32) for h in nh]
            return tuple((carry[h][0] + pv[h], runs[h]) for h in nh)

        carry = tuple((jnp.zeros((SB_BLK, HD), F32), jnp.zeros((SB_BLK, 1), F32)) for _ in nh)
        carry = span(_blk_off(i), 1, c < r, carry)
        rem = jnp.bitwise_and(i, 3)
        carry = lax.fori_loop(0, lax.shift_right_logical(i, 2),
                              lambda p, cr: span(_blk_off(i - 4 - 4 * p), 4, None, cr), carry)
        carry = lax.fori_loop(0, lax.shift_right_logical(rem, 1),
                              lambda _, cr: span(_blk_off(jnp.bitwise_and(rem, 1)), 2, None, cr), carry)
        carry = lax.fori_loop(0, jnp.bitwise_and(rem, 1), lambda _, cr: span(0, 1, None, cr), carry)
        for h in nh:
            o_ref[0, :, hs[h]] = carry[h][0].astype(o_ref.dtype)

    return pl.pallas_call(
        body, name="sb_fwd", grid=(bsz, ng, s // SB_BLK),
        in_specs=[qb(0), full(ng), full(2 * ng)], out_specs=qb(0),
        out_shape=jax.ShapeDtypeStruct((bsz, s, CFG.sbh * HD), BF16),
        compiler_params=_cp(("parallel", "parallel", "arbitrary")),
    )(qkv, qkv, qkv)


def _sb_bwd(qkv, do, s1):
    bsz, s, _ = qkv.shape
    ng = CFG.sbh // SB_HP
    w = SB_HP * HD
    nblk = s // SB_BLK
    scale = HD ** -0.5
    qb, full = _sb_specs(s, w)

    def body(q_ref, k_ref, v_ref, do_ref, s1_ref, dq_ref, dk_ref, dv_ref, r2_ref, dk_acc, dv_acc, dl_pan, z_pan,
             send_sems, recv_sems):
        i = pl.program_id(2)
        copies = _chip_copies(s1_ref, r2_ref, send_sems, recv_sems)

        @pl.when((pl.program_id(0) == 0) & (pl.program_id(1) == 0) & (i == 0))
        def _():
            for cp in copies:
                cp.start()

        @pl.when(i == 0)
        def _():
            dk_acc[...] = jnp.zeros_like(dk_acc)
            dv_acc[...] = jnp.zeros_like(dv_acc)

        r, c = _iota2((SB_BLK, SB_BLK), 0), _iota2((SB_BLK, SB_BLK), 1)
        tri_su = (r > c).astype(BF16)
        tri_pre = (r < c).astype(BF16)
        nh = range(SB_HP)
        hs = [slice(h * HD, (h + 1) * HD) for h in nh]
        qs = [(q_ref[0, :, hs[h]] * scale).astype(BF16) for h in nh]
        dob = [do_ref[0, :, hs[h]].astype(BF16) for h in nh]
        quads = lax.shift_right_logical(i, 2)
        rem = jnp.bitwise_and(i, 3)
        pair = lax.shift_right_logical(rem, 1)
        odd = jnp.bitwise_and(rem, 1)

        def span_a(jblk, nb, mask, runs):
            rows = pl.ds(_blk_off(jblk), nb * SB_BLK)
            ks = [k_ref[0, rows, hs[h]].astype(BF16) for h in nh]
            vs = [v_ref[0, rows, hs[h]].astype(BF16) for h in nh]
            dws = [lax.dot_general(dob[h], vs[h], _dims("nt"), preferred_element_type=F32) for h in nh]
            zs, _, ws, runs = _sb_span(qs, ks, mask, runs, tri_su)
            dvs = [lax.dot_general(ws[h].astype(BF16), dob[h], _dims("tn"), preferred_element_type=F32) for h in nh]
            for h in nh:
                dl = dws[h] * ws[h]
                for b in range(nb):
                    dl_pan[h, jblk + b] = dl[:, b * SB_BLK:(b + 1) * SB_BLK]
                    z_pan[h, jblk + b] = zs[h][:, b * SB_BLK:(b + 1) * SB_BLK]
                dv_acc[rows, hs[h]] += dvs[h]
            return tuple(runs)

        runs = tuple(jnp.zeros((SB_BLK, 1), F32) for _ in nh)
        runs = span_a(i, 1, c < r, runs)
        runs = lax.fori_loop(0, quads, lambda p, rn: span_a(i - 4 - 4 * p, 4, None, rn), runs)
        runs = lax.fori_loop(0, pair, lambda _, rn: span_a(odd, 2, None, rn), runs)
        lax.fori_loop(0, odd, lambda _, rn: span_a(0, 1, None, rn), runs)

        def span_b(jblk, nb, mask, carry):
            rows = pl.ds(_blk_off(jblk), nb * SB_BLK)
            ks = [k_ref[0, rows, hs[h]].astype(BF16) for h in nh]
            units = [dl_pan[h, jblk + b] for h in nh for b in range(nb)]
            sgs = [_sigmoid(z_pan[h, jblk + b]) for h in nh for b in range(nb)]
            cums = _heads_cumsum(units, tri_pre)
            dzs, pres = [], []
            for h in nh:
                pre, parts = carry[h][1], []
                for b in range(nb):
                    u, sg = units[h * nb + b], sgs[h * nb + b]
                    parts.append(u * (1.0 - sg) - sg * (cums[h * nb + b] + pre))
                    pre = pre + jnp.sum(u, axis=1, keepdims=True)
                dz = jnp.concatenate(parts, axis=1) if nb > 1 else parts[0]
                if mask is not None:
                    dz = jnp.where(mask, dz, 0.0)
                dzs.append(dz.astype(BF16))
                pres.append(pre)
            dqs = [lax.dot_general(dzs[h], ks[h], _dims("nn"), preferred_element_type=F32) for h in nh]
            dks = [lax.dot_general(dzs[h], qs[h], _dims("tn"), preferred_element_type=F32) for h in nh]
            for h in nh:
                dk_acc[rows, hs[h]] += dks[h]
            return tuple((carry[h][0] + dqs[h], pres[h]) for h in nh)

        carry = tuple((jnp.zeros((SB_BLK, HD), F32), jnp.zeros((SB_BLK, 1), F32)) for _ in nh)
        carry = lax.fori_loop(0, odd, lambda _, cr: span_b(0, 1, None, cr), carry)
        carry = lax.fori_loop(0, pair, lambda _, cr: span_b(odd, 2, None, cr), carry)
        carry = lax.fori_loop(0, quads, lambda p, cr: span_b(rem + 4 * p, 4, None, cr), carry)
        carry = span_b(i, 1, c < r, carry)
        for h in nh:
            dq_ref[0, :, hs[h]] = (carry[h][0] * scale).astype(dq_ref.dtype)

        @pl.when(i == nblk - 1)
        def _():
            dk_ref[0] = dk_acc[...].astype(dk_ref.dtype)
            dv_ref[0] = dv_acc[...].astype(dv_ref.dtype)

        @pl.when((pl.program_id(0) == bsz - 1) & (pl.program_id(1) == ng - 1) & (i == nblk - 1))
        def _():
            for cp in copies:
                cp.wait_recv()
            for cp in copies:
                cp.wait_send()

    out = jax.ShapeDtypeStruct((bsz, s, CFG.sbh * HD), BF16)
    hbm = pl.BlockSpec(memory_space=pl.ANY)
    return pl.pallas_call(
        body, name="sb_bwd", grid=(bsz, ng, nblk),
        in_specs=[qb(0), full(ng), full(2 * ng), qb(0), hbm],
        out_specs=[qb(0), full(0), full(0), hbm],
        out_shape=[out, out, out, jax.ShapeDtypeStruct((3,) + s1.shape[1:], s1.dtype)],
        scratch_shapes=[pltpu.VMEM((s, w), F32), pltpu.VMEM((s, w), F32),
                        pltpu.VMEM((SB_HP, nblk, SB_BLK, SB_BLK), F32), pltpu.VMEM((SB_HP, nblk, SB_BLK, SB_BLK), F32),
                        pltpu.SemaphoreType.DMA((3,)), pltpu.SemaphoreType.DMA((3,))],
        compiler_params=_cp(("arbitrary", "arbitrary", "arbitrary")),
    )(qkv, qkv, qkv, do, s1)


def _xattn_fn(q_raw, kv, qn, kn):
    d = q_raw.shape[1]
    dh = d // CFG.xh
    outs = []
    for h in range(CFG.xh):
        qh = _rms(q_raw[:, h * dh:(h + 1) * dh], qn)
        kh = _rms(kv[:, h * dh:(h + 1) * dh], kn)
        vh = kv[:, d + h * dh:d + (h + 1) * dh]
        sc = _bdot(qh, kh, "nt") * (dh ** -0.5)
        sc = sc - lax.stop_gradient(jnp.max(sc, axis=-1, keepdims=True))
        e = jnp.exp(sc)
        p = e / jnp.sum(e, axis=-1, keepdims=True)
        outs.append(_bdot(p, vh, "nn"))
    return jnp.concatenate(outs, axis=1)


def _xattn_fwd(q_raw, kv, qn, kn):
    bsz, s, d = q_raw.shape
    m = kv.shape[1]
    tq = _tile(s, (XQ_TILE, 128))

    def body(q_ref, kv_ref, qn_ref, kn_ref, o_ref):
        o_ref[0] = _xattn_fn(q_ref[0], kv_ref[0], qn_ref[...], kn_ref[...]).astype(o_ref.dtype)

    return pl.pallas_call(
        body, name="xattn_fwd", grid=(bsz, s // tq),
        in_specs=[pl.BlockSpec((1, tq, d), lambda b, i: (b, i, 0)), pl.BlockSpec((1, m, 2 * d), lambda b, i: (b, 0, 0)),
                  pl.BlockSpec(qn.shape, lambda b, i: (0, 0)), pl.BlockSpec(kn.shape, lambda b, i: (0, 0))],
        out_specs=pl.BlockSpec((1, tq, d), lambda b, i: (b, i, 0)),
        out_shape=jax.ShapeDtypeStruct((bsz, s, d), BF16),
        compiler_params=_cp(("parallel", "parallel")),
    )(q_raw, kv, qn, kn)


def _xattn_bwd(q_raw, kv, qn, kn, do):
    bsz, s, d = q_raw.shape
    m = kv.shape[1]
    tq = _tile(s, (XQ_TILE, 128))

    def body(q_ref, kv_ref, qn_ref, kn_ref, do_ref, dq_ref, dkv_ref, dqn_ref, dkn_ref):
        b, i = pl.program_id(0), pl.program_id(1)

        @pl.when((b == 0) & (i == 0))
        def _():
            dqn_ref[...] = jnp.zeros_like(dqn_ref)
            dkn_ref[...] = jnp.zeros_like(dkn_ref)

        @pl.when(i == 0)
        def _():
            dkv_ref[...] = jnp.zeros_like(dkv_ref)

        _, f = jax.vjp(_xattn_fn, q_ref[0], kv_ref[0], qn_ref[...], kn_ref[...])
        dq, dkv, dqn, dkn = f(do_ref[0].astype(F32))
        dq_ref[0] = dq.astype(dq_ref.dtype)
        dkv_ref[0] += dkv
        dqn_ref[...] += dqn
        dkn_ref[...] += dkn

    return pl.pallas_call(
        body, name="xattn_bwd", grid=(bsz, s // tq),
        in_specs=[pl.BlockSpec((1, tq, d), lambda b, i: (b, i, 0)), pl.BlockSpec((1, m, 2 * d), lambda b, i: (b, 0, 0)),
                  pl.BlockSpec(qn.shape, lambda b, i: (0, 0)), pl.BlockSpec(kn.shape, lambda b, i: (0, 0)),
                  pl.BlockSpec((1, tq, d), lambda b, i: (b, i, 0))],
        out_specs=[pl.BlockSpec((1, tq, d), lambda b, i: (b, i, 0)), pl.BlockSpec((1, m, 2 * d), lambda b, i: (b, 0, 0)),
                   pl.BlockSpec(qn.shape, lambda b, i: (0, 0)), pl.BlockSpec(kn.shape, lambda b, i: (0, 0))],
        out_shape=[jax.ShapeDtypeStruct((bsz, s, d), BF16), jax.ShapeDtypeStruct(kv.shape, F32),
                   jax.ShapeDtypeStruct(qn.shape, F32), jax.ShapeDtypeStruct(kn.shape, F32)],
        compiler_params=_cp(("arbitrary", "arbitrary")),
    )(q_raw, kv, qn, kn, do)


def _my_pos():
    return lax.axis_index("x"), lax.axis_index("y"), lax.axis_index("c")


def _all_gather_big(shard):
    r, d = shard.shape

    def body(x_ref, out_ref, send_sems, recv_sems, local_sem):
        x, y, c = _my_pos()
        me, sibling = (x, y, c), (x, y, 1 - c)
        chips = [(1 - x, y), (x, 1 - y), (1 - x, 1 - y)]

        def slot(px, py, pc):
            return out_ref.at[4 * px + 2 * py + pc]

        def copy(k, block, to, src=None):
            return pltpu.make_async_remote_copy(
                src_ref=slot(*block) if src is None else src, dst_ref=slot(*block),
                send_sem=send_sems.at[k], recv_sem=recv_sems.at[k], device_id=to, device_id_type=MESH)

        mine = pltpu.make_async_copy(x_ref, slot(*me), local_sem)
        mine.start()
        first = [copy(0, me, sibling, src=x_ref)]
        first += [copy(1 + j, me, (*chip, c), src=x_ref) for j, chip in enumerate(chips)]
        for cp in first:
            cp.start()
        passed = [copy(4 + j, (*chip, c), sibling) for j, chip in enumerate(chips)]
        for j, chip in enumerate(chips):
            copy(1 + j, (*chip, c), me).wait_recv()
            passed[j].start()
        copy(0, sibling, me).wait_recv()
        for j, chip in enumerate(chips):
            copy(4 + j, (*chip, 1 - c), me).wait_recv()
        for cp in first + passed:
            cp.wait_send()
        mine.wait()

    return pl.pallas_call(
        body, name="all_gather_weights",
        out_shape=jax.ShapeDtypeStruct((N_DEV, r, d), shard.dtype),
        in_specs=[pl.BlockSpec(memory_space=pl.ANY)], out_specs=pl.BlockSpec(memory_space=pl.ANY),
        scratch_shapes=[pltpu.SemaphoreType.DMA((7,)), pltpu.SemaphoreType.DMA((7,)), pltpu.SemaphoreType.DMA],
    )(shard)


def _exchange_sibling(g, name):
    _, r, d = g.shape

    def body(g_ref, out_ref, send_sems, recv_sems):
        x, y, c = _my_pos()
        copies = [pltpu.make_async_remote_copy(
            src_ref=g_ref.at[2 * k + (1 - c)], dst_ref=out_ref.at[k],
            send_sem=send_sems.at[k], recv_sem=recv_sems.at[k], device_id=(x, y, 1 - c), device_id_type=MESH)
            for k in range(4)]
        for cp in copies:
            cp.start()
        for cp in copies:
            cp.wait_recv()
        for cp in copies:
            cp.wait_send()

    return pl.pallas_call(
        body, name=name,
        out_shape=jax.ShapeDtypeStruct((4, r, d), g.dtype),
        in_specs=[pl.BlockSpec(memory_space=pl.ANY)], out_specs=pl.BlockSpec(memory_space=pl.ANY),
        scratch_shapes=[pltpu.SemaphoreType.DMA((4,)), pltpu.SemaphoreType.DMA((4,))],
    )(g)


def _chip_copies(s_ref, out_ref, send_sems, recv_sems):
    x, y, c = _my_pos()
    copies = []
    for rel in (1, 2, 3):
        px = jnp.bitwise_xor(x, rel >> 1)
        py = jnp.bitwise_xor(y, rel & 1)
        copies.append(pltpu.make_async_remote_copy(
            src_ref=s_ref.at[2 * px + py], dst_ref=out_ref.at[rel - 1],
            send_sem=send_sems.at[rel - 1], recv_sem=recv_sems.at[rel - 1],
            device_id=(px, py, c), device_id_type=MESH))
    return copies


def _exchange_chips(s1, name):
    _, r, d = s1.shape

    def body(s_ref, out_ref, send_sems, recv_sems):
        copies = _chip_copies(s_ref, out_ref, send_sems, recv_sems)
        for cp in copies:
            cp.start()
        for cp in copies:
            cp.wait_recv()
        for cp in copies:
            cp.wait_send()

    return pl.pallas_call(
        body, name=name,
        out_shape=jax.ShapeDtypeStruct((3, r, d), s1.dtype),
        in_specs=[pl.BlockSpec(memory_space=pl.ANY)], out_specs=pl.BlockSpec(memory_space=pl.ANY),
        scratch_shapes=[pltpu.SemaphoreType.DMA((3,)), pltpu.SemaphoreType.DMA((3,))],
    )(s1)


def _all_reduce_small(blk, name):
    rows, d = blk.shape

    def body(x_ref, out_ref, land, send_sems, recv_sems):
        x, y, c = _my_pos()
        me = 4 * x + 2 * y + c
        copies = []
        for rel in range(1, N_DEV):
            peer = (jnp.bitwise_xor(x, rel >> 2), jnp.bitwise_xor(y, (rel >> 1) & 1), jnp.bitwise_xor(c, rel & 1))
            copies.append(pltpu.make_async_remote_copy(
                src_ref=x_ref, dst_ref=land.at[rel - 1], send_sem=send_sems.at[rel - 1], recv_sem=recv_sems.at[rel - 1],
                device_id=peer, device_id_type=MESH))
        for cp in copies:
            cp.start()
        for cp in copies:
            cp.wait_recv()
        acc = jnp.zeros((rows, d), F32)
        for dev in range(N_DEV):
            rel = jnp.bitwise_xor(me, dev)
            got = land[jnp.maximum(rel - 1, 0)]
            acc = acc + jnp.where(rel == 0, x_ref[...], got)
        out_ref[...] = acc
        for cp in copies:
            cp.wait_send()

    return pl.pallas_call(
        body, name=name,
        out_shape=jax.ShapeDtypeStruct((rows, d), F32),
        in_specs=[pl.BlockSpec(memory_space=pltpu.VMEM)], out_specs=pl.BlockSpec(memory_space=pltpu.VMEM),
        scratch_shapes=[pltpu.VMEM((N_DEV - 1, rows, d), F32), pltpu.SemaphoreType.DMA((N_DEV - 1,)),
                        pltpu.SemaphoreType.DMA((N_DEV - 1,))],
    )(blk)


def _cast_rows(x, dtype, name):
    return _rowwise(lambda v: v, [x], [], [(x.shape[1], dtype)], [], name=name, tm=CFG.pack_tile)[0]


def _sum_sibling(g, recv1, c_idx, name):
    _, r, d = g.shape
    tm = CFG.pack_tile

    def body(c_ref, g_ref, r_ref, o_ref):
        o_ref[0] = (g_ref[0] + r_ref[0]).astype(o_ref.dtype)

    grid_spec = pltpu.PrefetchScalarGridSpec(
        num_scalar_prefetch=1, grid=(4, r // tm),
        in_specs=[pl.BlockSpec((1, tm, d), lambda k, i, c_ref: (2 * k + c_ref[0], i, 0)),
                  pl.BlockSpec((1, tm, d), lambda k, i, c_ref: (k, i, 0))],
        out_specs=pl.BlockSpec((1, tm, d), lambda k, i, c_ref: (k, i, 0)))
    return pl.pallas_call(
        body, name=name, grid_spec=grid_spec,
        out_shape=jax.ShapeDtypeStruct((4, r, d), BF16),
        compiler_params=_cp(("parallel", "parallel")),
    )(c_idx, g, recv1)


def _adamw_math(w, g, m, v):
    m2 = ADAM_B1 * m + (1.0 - ADAM_B1) * g
    v2 = ADAM_B2 * v + (1.0 - ADAM_B2) * (g * g)
    m_hat = m2 / (1.0 - ADAM_B1 ** ADAM_STEP)
    v_hat = v2 / (1.0 - ADAM_B2 ** ADAM_STEP)
    delta = -ADAM_LR * (m_hat / (jnp.sqrt(v_hat) + ADAM_EPS) + ADAM_WD * w)
    return delta, m2, v2


def _adamw_big(g, recv1, recv2, w, m, v, idx, row0, name):
    _, r, d = g.shape
    tm = CFG.pack_tile
    t0 = row0 // tm

    def body(idx_ref, g_ref, r1_ref, ra_ref, rb_ref, rc_ref, w_ref, m_ref, v_ref, og, od, om, ov):
        grad = (g_ref[0] + r1_ref[0]) + ra_ref[0].astype(F32) + rb_ref[0].astype(F32) + rc_ref[0].astype(F32)
        delta, m2, v2 = _adamw_math(w_ref[...], grad, m_ref[...], v_ref[...])
        og[...] = grad
        od[...] = delta
        om[...] = m2
        ov[...] = v2

    flat = pl.BlockSpec((tm, d), lambda i, idx_ref: (i, 0))
    shifted = pl.BlockSpec((tm, d), lambda i, idx_ref: (i + t0, 0))
    grid_spec = pltpu.PrefetchScalarGridSpec(
        num_scalar_prefetch=1, grid=(r // tm,),
        in_specs=[pl.BlockSpec((1, tm, d), lambda i, idx_ref: (idx_ref[0], i, 0)),
                  pl.BlockSpec((1, tm, d), lambda i, idx_ref: (idx_ref[1], i, 0)),
                  pl.BlockSpec((1, tm, d), lambda i, idx_ref: (0, i, 0)),
                  pl.BlockSpec((1, tm, d), lambda i, idx_ref: (1, i, 0)),
                  pl.BlockSpec((1, tm, d), lambda i, idx_ref: (2, i, 0)),
                  shifted, shifted, shifted],
        out_specs=[flat, flat, flat, flat])
    shp = jax.ShapeDtypeStruct((r, d), F32)
    return pl.pallas_call(
        body, name=name, grid_spec=grid_spec, out_shape=[shp, shp, shp, shp],
        compiler_params=_cp(("parallel",)),
    )(idx, g, recv1, recv2, recv2, recv2, w, m, v)


def _rows_of(v, d):
    flat = v.reshape(-1)
    rows = -(-flat.shape[0] // d)
    rows += (-rows) % SUBLANE
    return jnp.pad(flat, (0, rows * d - flat.shape[0])).reshape(rows, d)


def _pad_rows(a, mult):
    pad = (-a.shape[0]) % mult
    if pad:
        a = jnp.pad(a, ((0, pad),) + ((0, 0),) * (a.ndim - 1))
    return a


_BIG = ("w_in", "w_xkv", "w_up", "w_proj_gdn", "w_proj_sb", "w_out", "w_xq", "w_xo", "w_down")
_COL_SHARDED = ("w_in", "w_xkv", "w_up")
_SMALL_REP = ("norm_mix", "norm_x", "norm_mem", "norm_ffn", "a_log", "dt_bias", "gdn_out_norm", "xq_norm", "xk_norm")
_SMALL_CONV = ("conv_gdn", "conv_ffn")


def _part_rows(shapes):
    out = []
    for n in _BIG:
        rows, cols = shapes[n]
        cnt = cols if n in _COL_SHARDED else rows
        out.append((cnt, cnt + (-cnt) % (CFG.pack_tile if n == _BIG[0] else PACK_ROW_ALIGN)))
    return out


def _pack_big_shards(shards, shapes):
    parts = []
    for n, (_, padded) in zip(_BIG, _part_rows(shapes)):
        parts.append(_pad_rows(shards[n].T if n in _COL_SHARDED else shards[n], padded))
    return _pad_rows(jnp.concatenate(parts, axis=0), CFG.pack_tile)


def _unpack_gathered(gath, shapes):
    out, r0 = {}, 0
    for n, (cnt, padded) in zip(_BIG, _part_rows(shapes)):
        out[n] = gath[:, r0:r0 + cnt, :].reshape(N_DEV * cnt, gath.shape[2])
        r0 += padded
    return out


def _pack_full_grads(grads, shapes, names):
    d = CFG.d
    parts = []
    for n, (cnt, padded) in zip(_BIG, _part_rows(shapes)):
        if n not in names:
            continue
        g = grads[n].reshape(N_DEV, cnt, d)
        if padded > cnt:
            g = jnp.pad(g, ((0, 0), (0, padded - cnt), (0, 0)))
        parts.append(g)
    full = jnp.concatenate(parts, axis=1)
    pad = (-full.shape[1]) % CFG.pack_tile
    if pad:
        full = jnp.pad(full, ((0, 0), (0, pad), (0, 0)))
    return full


def _unpack_shard(packed, shapes, names):
    out, r0 = {}, 0
    for n, (cnt, padded) in zip(_BIG, _part_rows(shapes)):
        if n not in names:
            continue
        part = packed[r0:r0 + cnt]
        out[n] = (part.T if n in _COL_SHARDED else part).reshape((1,) + tuple(shapes[n]))
        r0 += padded
    return out


def kernel(x, mem, norm_mix, w_in, conv_gdn, a_log, dt_bias, gdn_out_norm, w_proj_gdn, w_proj_sb, w_out, norm_x, norm_mem, w_xq, w_xkv, xq_norm, xk_norm, w_xo, norm_ffn, w_up, conv_ffn, w_down, loss_target, m_norm_mix, m_w_in, m_conv_gdn, m_a_log, m_dt_bias, m_gdn_out_norm, m_w_proj_gdn, m_w_proj_sb, m_w_out, m_norm_x, m_norm_mem, m_w_xq, m_w_xkv, m_xq_norm, m_xk_norm, m_w_xo, m_norm_ffn, m_w_up, m_conv_ffn, m_w_down, v_norm_mix, v_w_in, v_conv_gdn, v_a_log, v_dt_bias, v_gdn_out_norm, v_w_proj_gdn, v_w_proj_sb, v_w_out, v_norm_x, v_norm_mem, v_w_xq, v_w_xkv, v_xq_norm, v_xk_norm, v_w_xo, v_norm_ffn, v_w_up, v_conv_ffn, v_w_down):
    names = ("norm_mix", "w_in", "conv_gdn", "a_log", "dt_bias", "gdn_out_norm", "w_proj_gdn", "w_proj_sb", "w_out",
             "norm_x", "norm_mem", "w_xq", "w_xkv", "xq_norm", "xk_norm", "w_xo", "norm_ffn", "w_up", "conv_ffn", "w_down")
    wts = dict(zip(names, (norm_mix, w_in, conv_gdn, a_log, dt_bias, gdn_out_norm, w_proj_gdn, w_proj_sb, w_out,
                           norm_x, norm_mem, w_xq, w_xkv, xq_norm, xk_norm, w_xo, norm_ffn, w_up, conv_ffn, w_down)))
    mom = dict(zip(names, (m_norm_mix, m_w_in, m_conv_gdn, m_a_log, m_dt_bias, m_gdn_out_norm, m_w_proj_gdn, m_w_proj_sb,
                           m_w_out, m_norm_x, m_norm_mem, m_w_xq, m_w_xkv, m_xq_norm, m_xk_norm, m_w_xo, m_norm_ffn, m_w_up,
                           m_conv_ffn, m_w_down)))
    vel = dict(zip(names, (v_norm_mix, v_w_in, v_conv_gdn, v_a_log, v_dt_bias, v_gdn_out_norm, v_w_proj_gdn, v_w_proj_sb,
                           v_w_out, v_norm_x, v_norm_mem, v_w_xq, v_w_xkv, v_xq_norm, v_xk_norm, v_w_xo, v_norm_ffn, v_w_up,
                           v_conv_ffn, v_w_down)))
    cfg = CFG
    d, bsz, s = cfg.d, cfg.b, cfg.s
    t = bsz * s
    gh, sbh = cfg.gh, cfg.sbh
    gw, sw = gh * HD, sbh * HD
    nchunk = s // cfg.gch
    mx, my, mc = _my_pos()
    me = 4 * mx + 2 * my + mc

    shard_shapes = {n: tuple(wts[n].shape[1:]) for n in _BIG}

    packed_w = _pack_big_shards({n: wts[n][0] for n in _BIG}, shard_shapes)
    gathered = _all_gather_big(_cast_rows(packed_w, BF16, "cast_weights"))
    full = _unpack_gathered(gathered, shard_shapes)

    conv_rows = {n: _rows_of(wts[n][0], d) for n in _SMALL_CONV}
    conv_cnt = {n: conv_rows[n].shape[0] for n in _SMALL_CONV}
    conv_blk = _pad_rows(jnp.concatenate([conv_rows[n] for n in _SMALL_CONV], axis=0), SUBLANE)
    conv_all = jnp.zeros((N_DEV,) + conv_blk.shape, F32)
    conv_all = lax.dynamic_update_slice(conv_all, conv_blk[None], (me, 0, 0))
    conv_all = _all_reduce_small(conv_all.reshape(-1, d), "gather_conv_taps").reshape((N_DEV,) + conv_blk.shape)

    def full_conv(n, r0):
        k, cols = wts[n].shape[1], wts[n].shape[2]
        part = conv_all[:, r0:r0 + conv_cnt[n], :].reshape(N_DEV, -1)[:, :k * cols].reshape(N_DEV, k, cols)
        return part.transpose(1, 0, 2).reshape(k, N_DEV * cols)

    cgdn = full_conv("conv_gdn", 0)
    cffn = full_conv("conv_ffn", conv_cnt["conv_gdn"])

    win = full["w_in"]
    o_ab = 3 * gw
    o_z = o_ab + 2 * gh
    o_sb = o_z + gw
    o_gate = o_sb + 3 * sw
    w_qkv = win[:o_ab]
    w_ab = jnp.concatenate([win[o_ab:o_z], jnp.zeros((LANE - 2 * gh, d), win.dtype)], axis=0)
    w_z = win[o_z:o_sb]
    w_sb = win[o_sb:o_gate]
    w_gate = win[o_gate:]

    alog_p = jnp.concatenate([a_log.reshape(1, -1), jnp.zeros((1, LANE - gh), F32)], axis=1)
    dtb_p = jnp.concatenate([dt_bias.reshape(1, -1), jnp.zeros((1, LANE - gh), F32)], axis=1)
    onorm = gdn_out_norm.reshape(1, HD)

    x2 = x.reshape(t, d)
    tgt2 = loss_target.reshape(t, d)
    mem2 = mem.reshape(bsz * cfg.mem, d)

    (xn,) = _rowwise(_rms, [x2], [norm_mix], [(d, BF16)], [], name="norm_mix_fwd")
    p_qkv = _mm(xn, w_qkv, tb=True, name="proj_qkv")
    p_ab = _mm(xn, w_ab, tb=True, name="proj_ab")
    p_z = _mm(xn, w_z, tb=True, name="proj_z")
    p_sb = _mm(xn, w_sb, tb=True, name="proj_sb")
    p_gate = _mm(xn, w_gate, tb=True, name="proj_gate")

    qkv_c = _gdn_conv_fwd(p_qkv.reshape(bsz, s, 3 * gw), cgdn)
    (gb,) = _rowwise(_gates_fn, [p_ab], [alog_p, dtb_p], [(LANE, F32)], [], name="gdn_gates_fwd")
    gbt = gb.reshape(bsz, s, LANE)[:, :, :2 * gh].transpose(0, 2, 1).reshape(bsz, 2 * gh, nchunk, 1, cfg.gch)
    o_a, states = _gdn_fwd(qkv_c, p_z.reshape(bsz, s, gw), gbt, onorm)
    o_b = _sb_fwd(p_sb.reshape(bsz, s, 3 * sw))

    pa = _mm(o_a.reshape(t, gw), full["w_proj_gdn"], name="proj_gdn_out")
    pb = _mm(o_b.reshape(t, sw), full["w_proj_sb"], name="proj_sb_out")

    def merge_fn(pa_, pb_, gate_):
        return _sigmoid(gate_[:, :d]) * pa_ + _sigmoid(gate_[:, d:]) * pb_

    (merged,) = _rowwise(merge_fn, [pa, pb, p_gate], [], [(d, BF16)], [], name="merge_fwd")
    h1 = _mm(merged, full["w_out"], add=x2, name="mixer_out")

    (hn_x,) = _rowwise(_rms, [h1], [norm_x], [(d, BF16)], [], name="norm_x_fwd")
    (mn,) = _rowwise(_rms, [mem2], [norm_mem], [(d, BF16)], [], name="norm_mem_fwd")
    q_raw = _mm(hn_x, full["w_xq"], name="xattn_q")
    kv = _mm(mn, full["w_xkv"], tb=True, name="xattn_kv")
    xo = _xattn_fwd(q_raw.reshape(bsz, s, d), kv.reshape(bsz, cfg.mem, 2 * d), xq_norm, xk_norm)
    h2 = _mm(xo.reshape(t, d), full["w_xo"], add=h1, name="xattn_out")

    (hn_f,) = _rowwise(_rms, [h2], [norm_ffn], [(d, BF16)], [], name="norm_ffn_fwd")
    up = _mm(hn_f, full["w_up"], tb=True, name="ffn_up")
    act = _ffn_conv_fwd(up.reshape(bsz, s, 2 * cfg.dff), cffn)
    y = _mm(act.reshape(t, cfg.dff), full["w_down"], add=h2, name="ffn_down")

    def loss_fn(y_, tg_):
        err = y_ - tg_
        part = 0.5 * jnp.sum(err * err) / d
        return err / d, err / d, jnp.full((1, LANE), part, F32)

    dy, dy_b, loss_part = _rowwise(loss_fn, [y, tgt2], [], [(d, F32), (d, BF16)], [(1, LANE)], name="loss")

    grads = {}
    dact = _mm(dy_b, full["w_down"], tb=True, name="d_act")
    grads["w_down"] = _mm(act.reshape(t, cfg.dff), dy_b, ta=True, name="dw_down")
    dup1, dup2, dcf1, dcf2 = _ffn_conv_bwd(up.reshape(bsz, s, 2 * cfg.dff), cffn, dact.reshape(bsz, s, cfg.dff))
    dup = jnp.concatenate([dup1, dup2], axis=2).reshape(t, 2 * cfg.dff)
    g_conv_ffn = jnp.concatenate([dcf1, dcf2], axis=1)
    dhn_f = _mm(dup, full["w_up"], name="d_hn_ffn")
    grads["w_up"] = _mm(dup, hn_f, ta=True, name="dw_up")

    def norm_bwd_fn(h_, res_, dn_, g_):
        _, f = jax.vjp(_rms, h_, g_)
        dh, dg = f(dn_)
        return res_ + dh, dg

    def norm_bwd_copy_fn(h_, res_, dn_, g_):
        dres, dg = norm_bwd_fn(h_, res_, dn_, g_)
        return dres, dres, dg

    dh2, dh2_b, g_norm_ffn = _rowwise(norm_bwd_copy_fn, [h2, dy, dhn_f], [norm_ffn], [(d, F32), (d, BF16)], [(1, d)],
                                      name="norm_ffn_bwd")

    dxo = _mm(dh2_b, full["w_xo"], tb=True, out_dtype=BF16, name="d_xo")
    grads["w_xo"] = _mm(xo.reshape(t, d), dh2_b, ta=True, name="dw_xo")
    dq_raw, dkv, g_xq_norm, g_xk_norm = _xattn_bwd(q_raw.reshape(bsz, s, d), kv.reshape(bsz, cfg.mem, 2 * d),
                                                   xq_norm, xk_norm, dxo.reshape(bsz, s, d))
    dq_raw2 = dq_raw.reshape(t, d)
    dkv2 = dkv.reshape(bsz * cfg.mem, 2 * d)
    dhn_x = _mm(dq_raw2, full["w_xq"], tb=True, name="d_hn_x")
    grads["w_xq"] = _mm(hn_x, dq_raw2, ta=True, name="dw_xq")
    dmn = _mm(dkv2, full["w_xkv"], name="d_mn")
    grads["w_xkv"] = _mm(dkv2, mn, ta=True, name="dw_xkv")

    def norm_w_bwd_fn(h_, dn_, g_):
        _, f = jax.vjp(lambda gg: _rms(h_, gg), g_)
        return f(dn_)[0]

    (g_norm_mem,) = _rowwise(norm_w_bwd_fn, [mem2, dmn], [norm_mem], [], [(1, d)], name="norm_mem_bwd")
    dh1, dh1_b, g_norm_x = _rowwise(norm_bwd_copy_fn, [h1, dh2, dhn_x], [norm_x], [(d, F32), (d, BF16)], [(1, d)],
                                    name="norm_x_bwd")

    dmerged = _mm(dh1_b, full["w_out"], tb=True, name="d_merged")
    grads["w_out"] = _mm(merged, dh1_b, ta=True, name="dw_out")

    def merge_bwd_fn(pa_, pb_, gate_, dm_):
        _, f = jax.vjp(merge_fn, pa_, pb_, gate_)
        return f(dm_)

    dpa, dpb, dgate = _rowwise(merge_bwd_fn, [pa, pb, p_gate, dmerged], [], [(d, BF16), (d, BF16), (2 * d, BF16)], [],
                               name="merge_bwd")
    do_a = _mm(dpa, full["w_proj_gdn"], tb=True, name="d_o_gdn")
    grads["w_proj_gdn"] = _mm(o_a.reshape(t, gw), dpa, ta=True, name="dw_proj_gdn")
    do_b = _mm(dpb, full["w_proj_sb"], tb=True, name="d_o_sb")
    grads["w_proj_sb"] = _mm(o_b.reshape(t, sw), dpb, ta=True, name="dw_proj_sb")

    c_idx = jnp.reshape(mc, (1,)).astype(jnp.int32)
    early = _BIG[1:]
    g_early = _pack_full_grads(grads, shard_shapes, early)
    r1_early = _exchange_sibling(g_early, "grads_to_sibling_early")
    s1_early = _sum_sibling(g_early, r1_early, c_idx, "sum_sibling_early")
    dsq, dsk, dsv, r2_early = _sb_bwd(p_sb.reshape(bsz, s, 3 * sw), do_b.reshape(bsz, s, sw), s1_early)
    dp_sb = jnp.concatenate([dsq, dsk, dsv], axis=2).reshape(t, 3 * sw)

    dgq, dgk, dgv, dz, dg, dbeta, g_onorm = _gdn_bwd(qkv_c, p_z.reshape(bsz, s, gw), gbt, onorm, states,
                                                     do_a.reshape(bsz, s, gw))
    dgb = jnp.concatenate([dg, dbeta], axis=1).reshape(bsz, 2 * gh, s).transpose(0, 2, 1)
    dgb = jnp.concatenate([dgb, jnp.zeros((bsz, s, LANE - 2 * gh), F32)], axis=2).reshape(t, LANE)

    def gates_bwd_fn(ab_, dgb_, alog_, dtb_):
        _, f = jax.vjp(_gates_fn, ab_, alog_, dtb_)
        return f(dgb_)

    dp_ab, g_alog, g_dtb = _rowwise(gates_bwd_fn, [p_ab, dgb], [alog_p, dtb_p], [(LANE, BF16)], [(1, LANE), (1, LANE)],
                                    name="gdn_gates_bwd")
    dqkv_c = jnp.concatenate([dgq, dgk, dgv], axis=2)
    dp_qkv, g_conv_gdn = _gdn_conv_bwd(p_qkv.reshape(bsz, s, 3 * gw), cgdn, dqkv_c)
    dp_qkv = dp_qkv.reshape(t, 3 * gw)
    dp_z = dz.reshape(t, gw)

    dxn = _mm(dp_qkv, w_qkv, name="d_xn_qkv")
    dxn = _mm(dp_ab, w_ab, add=dxn, name="d_xn_ab")
    dxn = _mm(dp_z, w_z, add=dxn, name="d_xn_z")
    dxn = _mm(dp_sb, w_sb, add=dxn, name="d_xn_sb")
    dxn = _mm(dgate, w_gate, add=dxn, name="d_xn_gate")
    grads["w_in"] = jnp.concatenate([
        _mm(dp_qkv, xn, ta=True, name="dw_in_qkv"),
        _mm(dp_ab, xn, ta=True, name="dw_in_ab")[:2 * gh],
        _mm(dp_z, xn, ta=True, name="dw_in_z"),
        _mm(dp_sb, xn, ta=True, name="dw_in_sb"),
        _mm(dgate, xn, ta=True, name="dw_in_gate")], axis=0)
    grad_x, g_norm_mix = _rowwise(norm_bwd_fn, [x2, dh1, dxn], [norm_mix], [(d, F32)], [(1, d)], name="norm_mix_bwd")

    small_g = {"norm_mix": g_norm_mix, "norm_x": g_norm_x, "norm_mem": g_norm_mem, "norm_ffn": g_norm_ffn,
               "a_log": g_alog[:, :gh], "dt_bias": g_dtb[:, :gh], "gdn_out_norm": g_onorm,
               "xq_norm": g_xq_norm, "xk_norm": g_xk_norm}
    sm_rows = [_rows_of(small_g[n], d) for n in _SMALL_REP] + [_rows_of(loss_part, d)]
    sm_rows += [_rows_of(g_conv_gdn, d), _rows_of(g_conv_ffn, d)]
    sm_cnt = [r.shape[0] for r in sm_rows]
    sm_sum = _all_reduce_small(_pad_rows(jnp.concatenate(sm_rows, axis=0), SUBLANE), "all_reduce_small_grads")
    offs = [0]
    for cnt in sm_cnt:
        offs.append(offs[-1] + cnt)
    small_grad = {}
    for i, n in enumerate(_SMALL_REP):
        small_grad[n] = sm_sum[offs[i]:offs[i + 1]].reshape(-1)[:wts[n].size].reshape(wts[n].shape)
    loss = sm_sum[offs[len(_SMALL_REP)], 0]
    for i, n in enumerate(_SMALL_CONV):
        k, cols = wts[n].shape[1], wts[n].shape[2]
        o = offs[len(_SMALL_REP) + 1 + i]
        fullg = sm_sum[o:o + sm_cnt[len(_SMALL_REP) + 1 + i]].reshape(-1)[:k * cols * N_DEV].reshape(k, N_DEV * cols)
        small_grad[n] = lax.dynamic_slice(fullg, (0, me * cols), (k, cols)).reshape(wts[n].shape)

    small_names = _SMALL_REP + _SMALL_CONV

    def pack_small(src):
        return _pad_rows(jnp.concatenate([_rows_of(src[n], d) for n in small_names], axis=0), SUBLANE)

    sw_, sg_, sm_, sv_ = pack_small(wts), pack_small(small_grad), pack_small(mom), pack_small(vel)
    sd_, snm_, snv_ = _rowwise(_adamw_math, [sw_, sg_, sm_, sv_], [], [(d, F32)] * 3, [], name="adamw_small", tm=sw_.shape[0])

    def unpack_small(packed):
        out, r0 = {}, 0
        for n in small_names:
            cnt = _rows_of(wts[n], d).shape[0]
            out[n] = packed[r0:r0 + cnt].reshape(-1)[:wts[n].size].reshape(wts[n].shape)
            r0 += cnt
        return out

    small_delta, small_m, small_v = unpack_small(sd_), unpack_small(snm_), unpack_small(snv_)

    g_late = _pack_full_grads(grads, shard_shapes, _BIG[:1])
    r1_late = _exchange_sibling(g_late, "grads_to_sibling_late")
    s1_late = _sum_sibling(g_late, r1_late, c_idx, "sum_sibling_late")
    r2_late = _exchange_chips(s1_late, "grads_to_chips_late")
    idx = jnp.stack([me, 2 * mx + my]).astype(jnp.int32)
    pm = _pack_big_shards({n: mom[n][0] for n in _BIG}, shard_shapes)
    pv = _pack_big_shards({n: vel[n][0] for n in _BIG}, shard_shapes)
    upd_late = _adamw_big(g_late, r1_late, r2_late, packed_w, pm, pv, idx, 0, "adamw_late")
    upd_early = _adamw_big(g_early, r1_early, r2_early, packed_w, pm, pv, idx, g_late.shape[1], "adamw_early")
    big_grad, big_delta, big_m, big_v = (
        {**_unpack_shard(a, shard_shapes, _BIG[:1]), **_unpack_shard(b, shard_shapes, early)}
        for a, b in zip(upd_late, upd_early))

    def pick(big, small, n):
        return big[n] if n in big else small[n]

    outs = [loss, grad_x.reshape(bsz, s, d)]
    outs += [pick(big_grad, small_grad, n) for n in names]
    outs += [pick(big_delta, small_delta, n) for n in names]
    outs += [pick(big_m, small_m, n) for n in names]
    outs += [pick(big_v, small_v, n) for n in names]
    return tuple(outs)
```

```python
import functools

import jax
import jax.numpy as jnp
from jax import lax
from jax.experimental import pallas as pl
from jax.experimental.pallas import tpu as pltpu

F32 = jnp.float32
BF16 = jnp.bfloat16
FDOT_PRECISION = lax.Precision.HIGH

LANE = 128
SUBLANE = 8
PACK_ROW_ALIGN = 16
VMEM_LIMIT = 56 * 2 ** 20
N_DEV = 8
MESH = pl.DeviceIdType.MESH

EPS = 1e-6
ADAM_LR = 0.001
ADAM_B1 = 0.9
ADAM_B2 = 0.999
ADAM_EPS = 1e-08
ADAM_WD = 0.01
ADAM_STEP = 10


class _Cfg:
    d = 1024
    b = 4
    s = 2048
    mem = 256
    gh = 8
    gch = 64
    sbh = 8
    xh = 4
    dff = 2816
    pack_tile = 256


CFG = _Cfg()
HD = 128
SB_BLK = 128
SB_HP = 4
GDN_HP = 8
MM_TILES = (1024, 1408, 704, 512, 256, 128)
MM_K_TILES = (1024, 1408, 704, 512, 256, 128)
MM_K_TILES_F32 = (512, 704, 256, 128)
CONV_CB = 256
XQ_TILE = 256


def _tile(n, prefs):
    for t in prefs:
        if n % t == 0:
            return t
    raise ValueError(f"no tile for {n}")


def _cp(sem, **kw):
    return pltpu.CompilerParams(dimension_semantics=sem, vmem_limit_bytes=VMEM_LIMIT, **kw)


def _dims(kind):
    return {"nn": (((1,), (0,)), ((), ())), "nt": (((1,), (1,)), ((), ())), "tn": (((0,), (0,)), ((), ()))}[kind]


def _raw_bdot(a, b, kind):
    return lax.dot_general(a.astype(BF16), b.astype(BF16), _dims(kind), preferred_element_type=F32)


def _raw_fdot(a, b, kind):
    return lax.dot_general(a.astype(F32), b.astype(F32), _dims(kind), precision=FDOT_PRECISION,
                           preferred_element_type=F32)


def _make_dot(raw):
    @functools.partial(jax.custom_vjp, nondiff_argnums=(2,))
    def dot(a, b, kind):
        return raw(a, b, kind)

    def fwd(a, b, kind):
        return raw(a, b, kind), (a, b)

    def bwd(kind, res, g):
        a, b = res
        if kind == "nn":
            return raw(g, b, "nt").astype(a.dtype), raw(a, g, "tn").astype(b.dtype)
        if kind == "nt":
            return raw(g, b, "nn").astype(a.dtype), raw(g, a, "tn").astype(b.dtype)
        return raw(b, g, "nt").astype(a.dtype), raw(a, g, "nn").astype(b.dtype)

    dot.defvjp(fwd, bwd)
    return dot


_bdot = _make_dot(_raw_bdot)
_fdot = _make_dot(_raw_fdot)


def _split_dot(x, m01):
    hi = x.astype(BF16)
    lo = (x - hi.astype(F32)).astype(BF16)
    return (lax.dot_general(hi, m01, _dims("nn"), preferred_element_type=F32)
            + lax.dot_general(lo, m01, _dims("nn"), preferred_element_type=F32))


_sigmoid = jax.nn.sigmoid


def _silu(x):
    return x * _sigmoid(x)


def _softplus(x):
    return jnp.maximum(x, 0.0) + jnp.log1p(jnp.exp(-jnp.abs(x)))


def _rms(x, g):
    return x * lax.rsqrt(jnp.mean(x * x, axis=-1, keepdims=True) + EPS) * g


def _iota2(shape, dim):
    return lax.broadcasted_iota(jnp.int32, shape, dim)


def _mm(a, b, *, ta=False, tb=False, add=None, out_dtype=F32, name):
    if ta:
        kd, m = a.shape
    else:
        m, kd = a.shape
    if tb:
        n, kb = b.shape
    else:
        kb, n = b.shape
    assert kd == kb, (a.shape, b.shape, ta, tb)
    tm = _tile(m, MM_TILES)
    tn = _tile(n, MM_TILES)
    wide = max(a.dtype.itemsize, b.dtype.itemsize) > 2
    tk = _tile(kd, MM_K_TILES_F32 if wide else MM_K_TILES)
    nk = kd // tk
    kind_dims = (((0 if ta else 1,), (1 if tb else 0,)), ((), ()))

    def body(*refs):
        a_ref, b_ref = refs[:2]
        add_ref = refs[2] if add is not None else None
        o_ref = refs[3 if add is not None else 2]
        part = lax.dot_general(a_ref[...].astype(BF16), b_ref[...].astype(BF16), kind_dims,
                               preferred_element_type=F32)

        def finish(r):
            if add is not None:
                r = r + add_ref[...].astype(F32)
            o_ref[...] = r.astype(o_ref.dtype)

        if nk == 1:
            finish(part)
            return
        acc = refs[-1]
        k = pl.program_id(2)

        @pl.when(k == 0)
        def _():
            acc[...] = part

        @pl.when((k > 0) & (k < nk - 1))
        def _():
            acc[...] += part

        @pl.when(k == nk - 1)
        def _():
            finish(acc[...] + part)

    a_spec = pl.BlockSpec((tk, tm), lambda i, j, k: (k, i)) if ta else pl.BlockSpec((tm, tk), lambda i, j, k: (i, k))
    b_spec = pl.BlockSpec((tn, tk), lambda i, j, k: (j, k)) if tb else pl.BlockSpec((tk, tn), lambda i, j, k: (k, j))
    in_specs = [a_spec, b_spec]
    args = [a, b]
    if add is not None:
        in_specs.append(pl.BlockSpec((tm, tn), lambda i, j, k: (i, j)))
        args.append(add)
    return pl.pallas_call(
        body, name=name, grid=(m // tm, n // tn, nk),
        in_specs=in_specs, out_specs=pl.BlockSpec((tm, tn), lambda i, j, k: (i, j)),
        out_shape=jax.ShapeDtypeStruct((m, n), out_dtype),
        scratch_shapes=[pltpu.VMEM((tm, tn), F32)] if nk > 1 else [],
        compiler_params=_cp(("parallel", "parallel", "arbitrary")),
    )(*args)


def _rowwise(fn, rows, pars, out_rows, out_accs, *, name, tm=None):
    t = rows[0].shape[0]
    if tm is None:
        tm = _tile(t, (256, 128, 64, 32, 16))
    assert t % tm == 0, (t, tm)
    n_r, n_p, n_or, n_oa = len(rows), len(pars), len(out_rows), len(out_accs)

    def body(*refs):
        r_in = refs[:n_r]
        p_in = refs[n_r:n_r + n_p]
        o_r = refs[n_r + n_p:n_r + n_p + n_or]
        o_a = refs[n_r + n_p + n_or:]
        outs = fn(*[r[...] for r in r_in], *[p[...] for p in p_in])
        if not isinstance(outs, (tuple, list)):
            outs = (outs,)
        assert len(outs) == n_or + n_oa, (name, len(outs))
        for ref, val in zip(o_r, outs[:n_or]):
            ref[...] = val.astype(ref.dtype)
        if n_oa:
            @pl.when(pl.program_id(0) == 0)
            def _():
                for ref in o_a:
                    ref[...] = jnp.zeros_like(ref)

            for ref, val in zip(o_a, outs[n_or:]):
                ref[...] += val.astype(F32)

    in_specs = [pl.BlockSpec((tm, r.shape[1]), lambda i: (i, 0)) for r in rows]
    in_specs += [pl.BlockSpec(p.shape, lambda i: (0, 0)) for p in pars]
    out_specs = [pl.BlockSpec((tm, c), lambda i: (i, 0)) for c, _ in out_rows]
    out_specs += [pl.BlockSpec(s, lambda i: (0, 0)) for s in out_accs]
    out_shape = [jax.ShapeDtypeStruct((t, c), dt) for c, dt in out_rows]
    out_shape += [jax.ShapeDtypeStruct(s, F32) for s in out_accs]
    return pl.pallas_call(
        body, name=name, grid=(t // tm,), in_specs=in_specs, out_specs=out_specs, out_shape=out_shape,
        compiler_params=_cp(("arbitrary",)),
    )(*rows, *pars)


def _shift_down(x, sh):
    rolled = pltpu.roll(x, sh, 0)
    return jnp.where(_iota2(x.shape, 0) >= sh, rolled, 0.0)


def _shift_up(x, sh):
    s = x.shape[0]
    rolled = pltpu.roll(x, s - sh, 0)
    return jnp.where(_iota2(x.shape, 0) < s - sh, rolled, 0.0)


def _conv(x, w):
    k = w.shape[0]
    y = x * w[k - 1:k, :]
    for i in range(k - 1):
        y = y + _shift_down(x, k - 1 - i) * w[i:i + 1, :]
    return y


def _conv_bwd(x, w, dy):
    k = w.shape[0]
    dx = dy * w[k - 1:k, :]
    dws = []
    for i in range(k - 1):
        dx = dx + _shift_up(dy, k - 1 - i) * w[i:i + 1, :]
        dws.append(jnp.sum(dy * _shift_down(x, k - 1 - i), axis=0, keepdims=True))
    dws.append(jnp.sum(dy * x, axis=0, keepdims=True))
    return dx, dws


def _gdn_post(y, j, nqb):
    a = _silu(y)
    sc = jnp.where(j < nqb, HD ** -0.5, 1.0).astype(F32)
    outs = []
    for h in range(y.shape[1] // HD):
        ah = a[:, h * HD:(h + 1) * HD]
        l2 = ah * lax.rsqrt(jnp.sum(ah * ah, axis=-1, keepdims=True) + EPS)
        outs.append(jnp.where(j < 2 * nqb, l2 * sc, ah))
    return jnp.concatenate(outs, axis=1) if len(outs) > 1 else outs[0]


def _gdn_conv_fwd(x, w):
    bsz, s, c3 = x.shape
    k = w.shape[0]
    nb = c3 // CONV_CB
    nqb = nb // 3

    def body(x_ref, w_ref, o_ref):
        j = pl.program_id(1)
        o_ref[0] = _gdn_post(_conv(x_ref[0], w_ref[...]), j, nqb)

    return pl.pallas_call(
        body, name="gdn_conv_fwd", grid=(bsz, nb),
        in_specs=[pl.BlockSpec((1, s, CONV_CB), lambda b, j: (b, 0, j)), pl.BlockSpec((k, CONV_CB), lambda b, j: (0, j))],
        out_specs=pl.BlockSpec((1, s, CONV_CB), lambda b, j: (b, 0, j)),
        out_shape=jax.ShapeDtypeStruct(x.shape, F32),
        compiler_params=_cp(("parallel", "parallel")),
    )(x, w)


def _gdn_conv_bwd(x, w, dout):
    bsz, s, c3 = x.shape
    k = w.shape[0]
    nb = c3 // CONV_CB
    nqb = nb // 3

    def body(x_ref, w_ref, d_ref, dx_ref, dw_ref):
        j = pl.program_id(0)
        b = pl.program_id(1)
        xv, wv = x_ref[0], w_ref[...]
        y = _conv(xv, wv)
        _, f = jax.vjp(lambda yy: _gdn_post(yy, j, nqb), y)
        (dy,) = f(d_ref[0])
        dx, dws = _conv_bwd(xv, wv, dy)
        dx_ref[0] = dx.astype(dx_ref.dtype)

        @pl.when(b == 0)
        def _():
            dw_ref[...] = jnp.zeros_like(dw_ref)

        for i in range(k):
            dw_ref[i:i + 1, :] += dws[i]

    return pl.pallas_call(
        body, name="gdn_conv_bwd", grid=(nb, bsz),
        in_specs=[pl.BlockSpec((1, s, CONV_CB), lambda j, b: (b, 0, j)), pl.BlockSpec((k, CONV_CB), lambda j, b: (0, j)),
                  pl.BlockSpec((1, s, CONV_CB), lambda j, b: (b, 0, j))],
        out_specs=[pl.BlockSpec((1, s, CONV_CB), lambda j, b: (b, 0, j)), pl.BlockSpec((k, CONV_CB), lambda j, b: (0, j))],
        out_shape=[jax.ShapeDtypeStruct(x.shape, BF16), jax.ShapeDtypeStruct(w.shape, F32)],
        compiler_params=_cp(("parallel", "arbitrary")),
    )(x, w, dout)


def _ffn_conv_fwd(up, w):
    bsz, s, c2 = up.shape
    k = w.shape[0]
    nb = (c2 // 2) // CONV_CB

    def body(x1_ref, x2_ref, w1_ref, w2_ref, o_ref):
        u1 = _conv(x1_ref[0], w1_ref[...])
        u2 = _conv(x2_ref[0], w2_ref[...])
        o_ref[0] = (_silu(u1) * u2).astype(o_ref.dtype)

    return pl.pallas_call(
        body, name="ffn_conv_fwd", grid=(bsz, nb),
        in_specs=[pl.BlockSpec((1, s, CONV_CB), lambda b, j: (b, 0, j)), pl.BlockSpec((1, s, CONV_CB), lambda b, j: (b, 0, j + nb)),
                  pl.BlockSpec((k, CONV_CB), lambda b, j: (0, j)), pl.BlockSpec((k, CONV_CB), lambda b, j: (0, j + nb))],
        out_specs=pl.BlockSpec((1, s, CONV_CB), lambda b, j: (b, 0, j)),
        out_shape=jax.ShapeDtypeStruct((bsz, s, c2 // 2), BF16),
        compiler_params=_cp(("parallel", "parallel")),
    )(up, up, w, w)


def _ffn_conv_bwd(up, w, dact):
    bsz, s, c2 = up.shape
    k = w.shape[0]
    half = c2 // 2
    nb = half // CONV_CB

    def body(x1_ref, x2_ref, w1_ref, w2_ref, d_ref, dx1_ref, dx2_ref, dw1_ref, dw2_ref):
        b = pl.program_id(1)
        x1, x2, w1, w2 = x1_ref[0], x2_ref[0], w1_ref[...], w2_ref[...]
        u1 = _conv(x1, w1)
        u2 = _conv(x2, w2)
        _, f = jax.vjp(lambda p, q: _silu(p) * q, u1, u2)
        du1, du2 = f(d_ref[0])
        dx1, dws1 = _conv_bwd(x1, w1, du1)
        dx2, dws2 = _conv_bwd(x2, w2, du2)
        dx1_ref[0] = dx1.astype(dx1_ref.dtype)
        dx2_ref[0] = dx2.astype(dx2_ref.dtype)

        @pl.when(b == 0)
        def _():
            dw1_ref[...] = jnp.zeros_like(dw1_ref)
            dw2_ref[...] = jnp.zeros_like(dw2_ref)

        for i in range(k):
            dw1_ref[i:i + 1, :] += dws1[i]
            dw2_ref[i:i + 1, :] += dws2[i]

    def blk(off):
        return pl.BlockSpec((1, s, CONV_CB), lambda j, b: (b, 0, j + off))

    def wblk(off):
        return pl.BlockSpec((k, CONV_CB), lambda j, b: (0, j + off))

    return pl.pallas_call(
        body, name="ffn_conv_bwd", grid=(nb, bsz),
        in_specs=[blk(0), blk(nb), wblk(0), wblk(nb), blk(0)],
        out_specs=[blk(0), blk(0), wblk(0), wblk(0)],
        out_shape=[jax.ShapeDtypeStruct((bsz, s, half), BF16), jax.ShapeDtypeStruct((bsz, s, half), BF16),
                   jax.ShapeDtypeStruct((k, half), F32), jax.ShapeDtypeStruct((k, half), F32)],
        compiler_params=_cp(("parallel", "arbitrary")),
    )(up, up, w, w, dact)


@jax.custom_vjp
def _inv_unit_lower(mats):
    c = mats[0].shape[0]
    eye = (_iota2((c, c), 0) == _iota2((c, c), 1)).astype(F32)
    ps = [-a for a in mats]
    ts = [eye + p for p in ps]
    n = 2
    while n < c:
        ps = [_raw_fdot(p, p, "nn") for p in ps]
        ts = [t + _raw_fdot(t, p, "nn") for t, p in zip(ts, ps)]
        n *= 2
    return ts


def _inv_fwd(mats):
    ts = _inv_unit_lower(mats)
    return ts, ts


def _inv_bwd(ts, gs):
    xs = [_raw_fdot(g, t, "nt") for g, t in zip(gs, ts)]
    return ([-_raw_fdot(t, x, "tn") for t, x in zip(ts, xs)],)


_inv_unit_lower.defvjp(_inv_fwd, _inv_bwd)


def _gdn_chunk(q, k, v, z, g_row, beta_row, state, onorm):
    nh = range(len(q))
    c = q[0].shape[0]
    ii, jj = _iota2((c, c), 0), _iota2((c, c), 1)
    incl, strict, eye = ii >= jj, ii > jj, ii == jj

    def to_col(row):
        return jnp.sum(jnp.where(eye, jnp.broadcast_to(row, (c, c)), 0.0), axis=1, keepdims=True)

    gc_col = [jnp.sum(jnp.where(incl, jnp.broadcast_to(g_row[h], (c, c)), 0.0), axis=1, keepdims=True) for h in nh]
    gc_row = [jnp.sum(jnp.where(eye, jnp.broadcast_to(gc_col[h], (c, c)), 0.0), axis=0, keepdims=True) for h in nh]
    beta_col = [to_col(beta_row[h]) for h in nh]
    gc_last = [jnp.sum(g_row[h], axis=1, keepdims=True) for h in nh]
    decay = [jnp.where(incl, jnp.exp(jnp.where(incl, gc_col[h] - gc_row[h], 0.0)), 0.0) for h in nh]
    kk = [_bdot(k[h], k[h], "nt") for h in nh]
    qk = [_bdot(q[h], k[h], "nt") * decay[h] for h in nh]
    tinv = _inv_unit_lower([jnp.where(strict, beta_col[h] * kk[h] * decay[h], 0.0) for h in nh])
    rhs = [jnp.concatenate([v[h] * beta_col[h], k[h] * (beta_col[h] * jnp.exp(gc_col[h]))], axis=1) for h in nh]
    uw = [_fdot(tinv[h], rhs[h], "nn") for h in nh]
    dv = v[0].shape[1]
    ws = [_bdot(uw[h][:, dv:], state[h], "nn") for h in nh]
    qs = [_bdot(q[h] * jnp.exp(gc_col[h]), state[h], "nn") for h in nh]
    v_new = [uw[h][:, :dv] - ws[h] for h in nh]
    o = [qs[h] + _bdot(qk[h], v_new[h], "nn") for h in nh]
    kv = [_bdot(k[h] * jnp.exp(gc_last[h] - gc_col[h]), v_new[h], "tn") for h in nh]
    new_state = [state[h] * jnp.exp(gc_last[h]) + kv[h] for h in nh]
    y = [_rms(o[h], onorm) * _silu(z[h]) for h in nh]
    return y, new_state


def _gdn_specs(s, c, reverse):
    n = s // c
    nn = (lambda i: n - 1 - i) if reverse else (lambda i: i)

    def qkv(off):
        return pl.BlockSpec((1, c, GDN_HP * HD), lambda b, h, i: (b, nn(i), h + off))

    def gate(off):
        return pl.BlockSpec((1, GDN_HP, 1, 1, c), lambda b, h, i: (b, h + off, nn(i), 0, 0))

    st = pl.BlockSpec((1, GDN_HP, 1, HD, HD), lambda b, h, i: (b, h, nn(i), 0, 0))
    onorm = pl.BlockSpec((1, HD), lambda b, h, i: (0, 0))
    return n, qkv, gate, st, onorm


def _gdn_fwd(qkv, z, gbt, onorm):
    bsz, s, _ = qkv.shape
    gh, c = CFG.gh, CFG.gch
    ng = gh // GDN_HP
    n, qs, gs, st, on = _gdn_specs(s, c, False)

    def body(q_ref, k_ref, v_ref, z_ref, g_ref, b_ref, on_ref, y_ref, st_ref, state):
        @pl.when(pl.program_id(2) == 0)
        def _():
            state[...] = jnp.zeros_like(state)

        nh = range(GDN_HP)
        hs = [slice(h * HD, (h + 1) * HD) for h in nh]
        s_in = [state[h] for h in nh]
        for h in nh:
            st_ref[0, h, 0] = s_in[h]
        y, s_out = _gdn_chunk([q_ref[0, :, hs[h]] for h in nh], [k_ref[0, :, hs[h]] for h in nh],
                              [v_ref[0, :, hs[h]] for h in nh], [z_ref[0, :, hs[h]] for h in nh],
                              [g_ref[0, h, 0] for h in nh], [b_ref[0, h, 0] for h in nh], s_in, on_ref[...])
        for h in nh:
            y_ref[0, :, hs[h]] = y[h].astype(y_ref.dtype)
            state[h] = s_out[h]

    return pl.pallas_call(
        body, name="gdn_fwd", grid=(bsz, ng, n),
        in_specs=[qs(0), qs(ng), qs(2 * ng), qs(0), gs(0), gs(ng), on],
        out_specs=[qs(0), st],
        out_shape=[jax.ShapeDtypeStruct((bsz, s, gh * HD), BF16), jax.ShapeDtypeStruct((bsz, gh, n, HD, HD), F32)],
        scratch_shapes=[pltpu.VMEM((GDN_HP, HD, HD), F32)],
        compiler_params=_cp(("parallel", "parallel", "arbitrary")),
    )(qkv, qkv, qkv, z, gbt, gbt, onorm)


def _gdn_bwd(qkv, z, gbt, onorm, states, dy):
    bsz, s, _ = qkv.shape
    gh, c = CFG.gh, CFG.gch
    ng = gh // GDN_HP
    n, qs, gs, st, on = _gdn_specs(s, c, True)

    def body(q_ref, k_ref, v_ref, z_ref, g_ref, b_ref, on_ref, st_ref, dy_ref,
             dq_ref, dk_ref, dv_ref, dz_ref, dg_ref, db_ref, don_ref, dstate):
        first = (pl.program_id(0) == 0) & (pl.program_id(1) == 0) & (pl.program_id(2) == 0)

        @pl.when(first)
        def _():
            don_ref[...] = jnp.zeros_like(don_ref)

        @pl.when(pl.program_id(2) == 0)
        def _():
            dstate[...] = jnp.zeros_like(dstate)

        nh = range(GDN_HP)
        hs = [slice(h * HD, (h + 1) * HD) for h in nh]
        _, f = jax.vjp(_gdn_chunk, [q_ref[0, :, hs[h]] for h in nh], [k_ref[0, :, hs[h]] for h in nh],
                       [v_ref[0, :, hs[h]] for h in nh], [z_ref[0, :, hs[h]] for h in nh],
                       [g_ref[0, h, 0] for h in nh], [b_ref[0, h, 0] for h in nh],
                       [st_ref[0, h, 0] for h in nh], on_ref[...])
        dq, dk, dv, dz, dg, db, ds, don = f(([dy_ref[0, :, hs[h]] for h in nh], [dstate[h] for h in nh]))
        for h in nh:
            dq_ref[0, :, hs[h]] = dq[h]
            dk_ref[0, :, hs[h]] = dk[h]
            dv_ref[0, :, hs[h]] = dv[h]
            dz_ref[0, :, hs[h]] = dz[h].astype(dz_ref.dtype)
            dg_ref[0, h, 0] = dg[h]
            db_ref[0, h, 0] = db[h]
            dstate[h] = ds[h]
        don_ref[...] += don

    act = jax.ShapeDtypeStruct((bsz, s, gh * HD), F32)
    gshape = jax.ShapeDtypeStruct((bsz, gh, n, 1, c), F32)
    return pl.pallas_call(
        body, name="gdn_bwd", grid=(bsz, ng, n),
        in_specs=[qs(0), qs(ng), qs(2 * ng), qs(0), gs(0), gs(ng), on, st, qs(0)],
        out_specs=[qs(0), qs(0), qs(0), qs(0), gs(0), gs(0), on],
        out_shape=[act, act, act, jax.ShapeDtypeStruct(act.shape, BF16), gshape, gshape, jax.ShapeDtypeStruct((1, HD), F32)],
        scratch_shapes=[pltpu.VMEM((GDN_HP, HD, HD), F32)],
        compiler_params=_cp(("arbitrary", "arbitrary", "arbitrary")),
    )(qkv, qkv, qkv, z, gbt, gbt, onorm, states, dy)


def _gates_fn(ab, alog, dtb):
    lane = _iota2(ab.shape, 1)
    g = -jnp.exp(alog) * _softplus(ab + dtb)
    beta = _sigmoid(ab)
    return jnp.where(lane < CFG.gh, g, jnp.where(lane < 2 * CFG.gh, beta, 0.0))


def _heads_cumsum(xs, tri):
    n = xs[0].shape[0]
    y = _split_dot(jnp.concatenate(xs, axis=0), tri)
    return [y[h * n:(h + 1) * n] for h in range(len(xs))]


def _blk_off(jblk):
    return jblk * SB_BLK if isinstance(jblk, int) else pl.multiple_of(jblk * SB_BLK, SB_BLK)


def _sb_span(qs, k_spans, mask, runs, tri_su):
    nh = range(len(qs))
    nb = k_spans[0].shape[0] // SB_BLK
    zs = [lax.dot_general(qs[h], k_spans[h], _dims("nt"), preferred_element_type=F32) for h in nh]
    l1p = [jnp.log(1.0 + jnp.exp(-jnp.abs(z))) for z in zs]
    lss = [jnp.minimum(zs[h], 0.0) - l1p[h] for h in nh]
    lfs = [lss[h] - zs[h] for h in nh]
    if mask is not None:
        lfs = [jnp.where(mask, lf, 0.0) for lf in lfs]
    units = [lfs[h][:, b * SB_BLK:(b + 1) * SB_BLK] for h in nh for b in range(nb)]
    cums = _heads_cumsum(units, tri_su)
    sfx, new_runs = [], []
    for h in nh:
        run, parts = runs[h], [None] * nb
        for b in reversed(range(nb)):
            parts[b] = cums[h * nb + b] + run
            run = run + jnp.sum(units[h * nb + b], axis=1, keepdims=True)
        sfx.append(jnp.concatenate(parts, axis=1) if nb > 1 else parts[0])
        new_runs.append(run)
    ws = [jnp.exp(lss[h] + sfx[h]) for h in nh]
    if mask is not None:
        ws = [jnp.where(mask, w, 0.0) for w in ws]
    return zs, lfs, ws, new_runs


def _sb_specs(s, w):
    def qb(off):
        return pl.BlockSpec((1, SB_BLK, w), lambda b, h, i: (b, i, h + off))

    def full(off):
        return pl.BlockSpec((1, s, w), lambda b, h, i: (b, 0, h + off))

    return qb, full


def _sb_fwd(qkv, shard):
    bsz, s, _ = qkv.shape
    ng = CFG.sbh // SB_HP
    w = SB_HP * HD
    scale = HD ** -0.5
    qb, full = _sb_specs(s, w)

    def body(q_ref, k_ref, v_ref, x_ref, o_ref, g_ref, send_sems, recv_sems, local_sem):
        i = pl.program_id(2)
        begin, relay, finish = _gather_stages(x_ref, g_ref, send_sems, recv_sems, local_sem)
        start_of_group = (pl.program_id(1) == 0) & (i == 0)
        pl.when((pl.program_id(0) == 0) & start_of_group)(begin)
        pl.when((pl.program_id(0) == bsz // 2) & start_of_group)(relay)

        r, c = _iota2((SB_BLK, SB_BLK), 0), _iota2((SB_BLK, SB_BLK), 1)
        tri_su = (r > c).astype(BF16)
        nh = range(SB_HP)
        hs = [slice(h * HD, (h + 1) * HD) for h in nh]
        qs = [(q_ref[0, :, hs[h]] * scale).astype(BF16) for h in nh]

        def span(off, nb, mask, carry):
            ks = [k_ref[0, pl.ds(off, nb * SB_BLK), hs[h]].astype(BF16) for h in nh]
            vs = [v_ref[0, pl.ds(off, nb * SB_BLK), hs[h]].astype(BF16) for h in nh]
            _, _, ws, runs = _sb_span(qs, ks, mask, [cr[1] for cr in carry], tri_su)
            pv = [lax.dot_general(ws[h].astype(BF16), vs[h], _dims("nn"), preferred_element_type=F32) for h in nh]
            return tuple((carry[h][0] + pv[h], runs[h]) for h in nh)

        carry = tuple((jnp.zeros((SB_BLK, HD), F32), jnp.zeros((SB_BLK, 1), F32)) for _ in nh)
        carry = span(_blk_off(i), 1, c < r, carry)
        rem = jnp.bitwise_and(i, 3)
        carry = lax.fori_loop(0, lax.shift_right_logical(i, 2),
                              lambda p, cr: span(_blk_off(i - 4 - 4 * p), 4, None, cr), carry)
        carry = lax.fori_loop(0, lax.shift_right_logical(rem, 1),
                              lambda _, cr: span(_blk_off(jnp.bitwise_and(rem, 1)), 2, None, cr), carry)
        carry = lax.fori_loop(0, jnp.bitwise_and(rem, 1), lambda _, cr: span(0, 1, None, cr), carry)
        for h in nh:
            o_ref[0, :, hs[h]] = carry[h][0].astype(o_ref.dtype)

        pl.when((pl.program_id(0) == bsz - 1) & (pl.program_id(1) == ng - 1) & (i == nblk - 1))(finish)

    nblk = s // SB_BLK
    hbm = pl.BlockSpec(memory_space=pl.ANY)
    return pl.pallas_call(
        body, name="sb_fwd", grid=(bsz, ng, nblk),
        in_specs=[qb(0), full(ng), full(2 * ng), hbm], out_specs=[qb(0), hbm],
        out_shape=[jax.ShapeDtypeStruct((bsz, s, CFG.sbh * HD), BF16),
                   jax.ShapeDtypeStruct((N_DEV,) + shard.shape, shard.dtype)],
        scratch_shapes=list(_GATHER_SEMS),
        compiler_params=_cp(("arbitrary", "arbitrary", "arbitrary")),
    )(qkv, qkv, qkv, shard)


def _sb_bwd(qkv, do, s1):
    bsz, s, _ = qkv.shape
    ng = CFG.sbh // SB_HP
    w = SB_HP * HD
    nblk = s // SB_BLK
    scale = HD ** -0.5
    qb, full = _sb_specs(s, w)

    def body(q_ref, k_ref, v_ref, do_ref, s1_ref, dq_ref, dk_ref, dv_ref, r2_ref, dk_acc, dv_acc, dl_pan, z_pan,
             send_sems, recv_sems):
        i = pl.program_id(2)
        copies = _chip_copies(s1_ref, r2_ref, send_sems, recv_sems)

        @pl.when((pl.program_id(0) == 0) & (pl.program_id(1) == 0) & (i == 0))
        def _():
            for cp in copies:
                cp.start()

        @pl.when(i == 0)
        def _():
            dk_acc[...] = jnp.zeros_like(dk_acc)
            dv_acc[...] = jnp.zeros_like(dv_acc)

        r, c = _iota2((SB_BLK, SB_BLK), 0), _iota2((SB_BLK, SB_BLK), 1)
        tri_su = (r > c).astype(BF16)
        tri_pre = (r < c).astype(BF16)
        nh = range(SB_HP)
        hs = [slice(h * HD, (h + 1) * HD) for h in nh]
        qs = [(q_ref[0, :, hs[h]] * scale).astype(BF16) for h in nh]
        dob = [do_ref[0, :, hs[h]].astype(BF16) for h in nh]
        quads = lax.shift_right_logical(i, 2)
        rem = jnp.bitwise_and(i, 3)
        pair = lax.shift_right_logical(rem, 1)
        odd = jnp.bitwise_and(rem, 1)

        def span_a(jblk, nb, mask, runs):
            rows = pl.ds(_blk_off(jblk), nb * SB_BLK)
            ks = [k_ref[0, rows, hs[h]].astype(BF16) for h in nh]
            vs = [v_ref[0, rows, hs[h]].astype(BF16) for h in nh]
            dws = [lax.dot_general(dob[h], vs[h], _dims("nt"), preferred_element_type=F32) for h in nh]
            zs, _, ws, runs = _sb_span(qs, ks, mask, runs, tri_su)
            dvs = [lax.dot_general(ws[h].astype(BF16), dob[h], _dims("tn"), preferred_element_type=F32) for h in nh]
            for h in nh:
                dl = dws[h] * ws[h]
                for b in range(nb):
                    dl_pan[h, jblk + b] = dl[:, b * SB_BLK:(b + 1) * SB_BLK]
                    z_pan[h, jblk + b] = zs[h][:, b * SB_BLK:(b + 1) * SB_BLK]
                dv_acc[rows, hs[h]] += dvs[h]
            return tuple(runs)

        runs = tuple(jnp.zeros((SB_BLK, 1), F32) for _ in nh)
        runs = span_a(i, 1, c < r, runs)
        runs = lax.fori_loop(0, quads, lambda p, rn: span_a(i - 4 - 4 * p, 4, None, rn), runs)
        runs = lax.fori_loop(0, pair, lambda _, rn: span_a(odd, 2, None, rn), runs)
        lax.fori_loop(0, odd, lambda _, rn: span_a(0, 1, None, rn), runs)

        def span_b(jblk, nb, mask, carry):
            rows = pl.ds(_blk_off(jblk), nb * SB_BLK)
            ks = [k_ref[0, rows, hs[h]].astype(BF16) for h in nh]
            units = [dl_pan[h, jblk + b] for h in nh for b in range(nb)]
            sgs = [_sigmoid(z_pan[h, jblk + b]) for h in nh for b in range(nb)]
            cums = _heads_cumsum(units, tri_pre)
            dzs, pres = [], []
            for h in nh:
                pre, parts = carry[h][1], []
                for b in range(nb):
                    u, sg = units[h * nb + b], sgs[h * nb + b]
                    parts.append(u * (1.0 - sg) - sg * (cums[h * nb + b] + pre))
                    pre = pre + jnp.sum(u, axis=1, keepdims=True)
                dz = jnp.concatenate(parts, axis=1) if nb > 1 else parts[0]
                if mask is not None:
                    dz = jnp.where(mask, dz, 0.0)
                dzs.append(dz.astype(BF16))
                pres.append(pre)
            dqs = [lax.dot_general(dzs[h], ks[h], _dims("nn"), preferred_element_type=F32) for h in nh]
            dks = [lax.dot_general(dzs[h], qs[h], _dims("tn"), preferred_element_type=F32) for h in nh]
            for h in nh:
                dk_acc[rows, hs[h]] += dks[h]
            return tuple((carry[h][0] + dqs[h], pres[h]) for h in nh)

        carry = tuple((jnp.zeros((SB_BLK, HD), F32), jnp.zeros((SB_BLK, 1), F32)) for _ in nh)
        carry = lax.fori_loop(0, odd, lambda _, cr: span_b(0, 1, None, cr), carry)
        carry = lax.fori_loop(0, pair, lambda _, cr: span_b(odd, 2, None, cr), carry)
        carry = lax.fori_loop(0, quads, lambda p, cr: span_b(rem + 4 * p, 4, None, cr), carry)
        carry = span_b(i, 1, c < r, carry)
        for h in nh:
            dq_ref[0, :, hs[h]] = (carry[h][0] * scale).astype(dq_ref.dtype)

        @pl.when(i == nblk - 1)
        def _():
            dk_ref[0] = dk_acc[...].astype(dk_ref.dtype)
            dv_ref[0] = dv_acc[...].astype(dv_ref.dtype)

        @pl.when((pl.program_id(0) == bsz - 1) & (pl.program_id(1) == ng - 1) & (i == nblk - 1))
        def _():
            for cp in copies:
                cp.wait_recv()
            for cp in copies:
                cp.wait_send()

    out = jax.ShapeDtypeStruct((bsz, s, CFG.sbh * HD), BF16)
    hbm = pl.BlockSpec(memory_space=pl.ANY)
    return pl.pallas_call(
        body, name="sb_bwd", grid=(bsz, ng, nblk),
        in_specs=[qb(0), full(ng), full(2 * ng), qb(0), hbm],
        out_specs=[qb(0), full(0), full(0), hbm],
        out_shape=[out, out, out, jax.ShapeDtypeStruct((3,) + s1.shape[1:], s1.dtype)],
        scratch_shapes=[pltpu.VMEM((s, w), F32), pltpu.VMEM((s, w), F32),
                        pltpu.VMEM((SB_HP, nblk, SB_BLK, SB_BLK), F32), pltpu.VMEM((SB_HP, nblk, SB_BLK, SB_BLK), F32),
                        pltpu.SemaphoreType.DMA((3,)), pltpu.SemaphoreType.DMA((3,))],
        compiler_params=_cp(("arbitrary", "arbitrary", "arbitrary")),
    )(qkv, qkv, qkv, do, s1)


def _xattn_fn(q_raw, kv, qn, kn):
    d = q_raw.shape[1]
    dh = d // CFG.xh
    outs = []
    for h in range(CFG.xh):
        qh = _rms(q_raw[:, h * dh:(h + 1) * dh], qn)
        kh = _rms(kv[:, h * dh:(h + 1) * dh], kn)
        vh = kv[:, d + h * dh:d + (h + 1) * dh]
        sc = _bdot(qh, kh, "nt") * (dh ** -0.5)
        sc = sc - lax.stop_gradient(jnp.max(sc, axis=-1, keepdims=True))
        e = jnp.exp(sc)
        p = e / jnp.sum(e, axis=-1, keepdims=True)
        outs.append(_bdot(p, vh, "nn"))
    return jnp.concatenate(outs, axis=1)


def _xattn_fwd(q_raw, kv, qn, kn):
    bsz, s, d = q_raw.shape
    m = kv.shape[1]
    tq = _tile(s, (XQ_TILE, 128))

    def body(q_ref, kv_ref, qn_ref, kn_ref, o_ref):
        o_ref[0] = _xattn_fn(q_ref[0], kv_ref[0], qn_ref[...], kn_ref[...]).astype(o_ref.dtype)

    return pl.pallas_call(
        body, name="xattn_fwd", grid=(bsz, s // tq),
        in_specs=[pl.BlockSpec((1, tq, d), lambda b, i: (b, i, 0)), pl.BlockSpec((1, m, 2 * d), lambda b, i: (b, 0, 0)),
                  pl.BlockSpec(qn.shape, lambda b, i: (0, 0)), pl.BlockSpec(kn.shape, lambda b, i: (0, 0))],
        out_specs=pl.BlockSpec((1, tq, d), lambda b, i: (b, i, 0)),
        out_shape=jax.ShapeDtypeStruct((bsz, s, d), BF16),
        compiler_params=_cp(("parallel", "parallel")),
    )(q_raw, kv, qn, kn)


def _xattn_bwd(q_raw, kv, qn, kn, do):
    bsz, s, d = q_raw.shape
    m = kv.shape[1]
    tq = _tile(s, (XQ_TILE, 128))

    def body(q_ref, kv_ref, qn_ref, kn_ref, do_ref, dq_ref, dkv_ref, dqn_ref, dkn_ref):
        b, i = pl.program_id(0), pl.program_id(1)

        @pl.when((b == 0) & (i == 0))
        def _():
            dqn_ref[...] = jnp.zeros_like(dqn_ref)
            dkn_ref[...] = jnp.zeros_like(dkn_ref)

        @pl.when(i == 0)
        def _():
            dkv_ref[...] = jnp.zeros_like(dkv_ref)

        _, f = jax.vjp(_xattn_fn, q_ref[0], kv_ref[0], qn_ref[...], kn_ref[...])
        dq, dkv, dqn, dkn = f(do_ref[0].astype(F32))
        dq_ref[0] = dq.astype(dq_ref.dtype)
        dkv_ref[0] += dkv
        dqn_ref[...] += dqn
        dkn_ref[...] += dkn

    return pl.pallas_call(
        body, name="xattn_bwd", grid=(bsz, s // tq),
        in_specs=[pl.BlockSpec((1, tq, d), lambda b, i: (b, i, 0)), pl.BlockSpec((1, m, 2 * d), lambda b, i: (b, 0, 0)),
                  pl.BlockSpec(qn.shape, lambda b, i: (0, 0)), pl.BlockSpec(kn.shape, lambda b, i: (0, 0)),
                  pl.BlockSpec((1, tq, d), lambda b, i: (b, i, 0))],
        out_specs=[pl.BlockSpec((1, tq, d), lambda b, i: (b, i, 0)), pl.BlockSpec((1, m, 2 * d), lambda b, i: (b, 0, 0)),
                   pl.BlockSpec(qn.shape, lambda b, i: (0, 0)), pl.BlockSpec(kn.shape, lambda b, i: (0, 0))],
        out_shape=[jax.ShapeDtypeStruct((bsz, s, d), BF16), jax.ShapeDtypeStruct(kv.shape, F32),
                   jax.ShapeDtypeStruct(qn.shape, F32), jax.ShapeDtypeStruct(kn.shape, F32)],
        compiler_params=_cp(("arbitrary", "arbitrary")),
    )(q_raw, kv, qn, kn, do)


def _my_pos():
    return lax.axis_index("x"), lax.axis_index("y"), lax.axis_index("c")


def _gather_stages(x_ref, out_ref, send_sems, recv_sems, local_sem):
    x, y, c = _my_pos()
    me, sibling = (x, y, c), (x, y, 1 - c)
    chips = [(1 - x, y), (x, 1 - y), (1 - x, 1 - y)]

    def slot(px, py, pc):
        return out_ref.at[4 * px + 2 * py + pc]

    def copy(k, block, to, src=None):
        return pltpu.make_async_remote_copy(
            src_ref=slot(*block) if src is None else src, dst_ref=slot(*block),
            send_sem=send_sems.at[k], recv_sem=recv_sems.at[k], device_id=to, device_id_type=MESH)

    mine = pltpu.make_async_copy(x_ref, slot(*me), local_sem)
    first = [copy(0, me, sibling, src=x_ref)]
    first += [copy(1 + j, me, (*chip, c), src=x_ref) for j, chip in enumerate(chips)]
    passed = [copy(4 + j, (*chip, c), sibling) for j, chip in enumerate(chips)]

    def begin():
        mine.start()
        for cp in first:
            cp.start()

    def relay():
        for j, chip in enumerate(chips):
            copy(1 + j, (*chip, c), me).wait_recv()
            passed[j].start()

    def finish():
        copy(0, sibling, me).wait_recv()
        for j, chip in enumerate(chips):
            copy(4 + j, (*chip, 1 - c), me).wait_recv()
        for cp in first + passed:
            cp.wait_send()
        mine.wait()

    return begin, relay, finish


_GATHER_SEMS = [pltpu.SemaphoreType.DMA((7,)), pltpu.SemaphoreType.DMA((7,)), pltpu.SemaphoreType.DMA]


def _all_gather_big(shard, name):
    r, d = shard.shape

    def body(x_ref, out_ref, send_sems, recv_sems, local_sem):
        begin, relay, finish = _gather_stages(x_ref, out_ref, send_sems, recv_sems, local_sem)
        begin()
        relay()
        finish()

    return pl.pallas_call(
        body, name=name,
        out_shape=jax.ShapeDtypeStruct((N_DEV, r, d), shard.dtype),
        in_specs=[pl.BlockSpec(memory_space=pl.ANY)], out_specs=pl.BlockSpec(memory_space=pl.ANY),
        scratch_shapes=list(_GATHER_SEMS),
    )(shard)


def _exchange_sibling(g, name):
    _, r, d = g.shape

    def body(g_ref, out_ref, send_sems, recv_sems):
        x, y, c = _my_pos()
        copies = [pltpu.make_async_remote_copy(
            src_ref=g_ref.at[2 * k + (1 - c)], dst_ref=out_ref.at[k],
            send_sem=send_sems.at[k], recv_sem=recv_sems.at[k], device_id=(x, y, 1 - c), device_id_type=MESH)
            for k in range(4)]
        for cp in copies:
            cp.start()
        for cp in copies:
            cp.wait_recv()
        for cp in copies:
            cp.wait_send()

    return pl.pallas_call(
        body, name=name,
        out_shape=jax.ShapeDtypeStruct((4, r, d), g.dtype),
        in_specs=[pl.BlockSpec(memory_space=pl.ANY)], out_specs=pl.BlockSpec(memory_space=pl.ANY),
        scratch_shapes=[pltpu.SemaphoreType.DMA((4,)), pltpu.SemaphoreType.DMA((4,))],
    )(g)


def _chip_copies(s_ref, out_ref, send_sems, recv_sems):
    x, y, c = _my_pos()
    copies = []
    for rel in (1, 2, 3):
        px = jnp.bitwise_xor(x, rel >> 1)
        py = jnp.bitwise_xor(y, rel & 1)
        copies.append(pltpu.make_async_remote_copy(
            src_ref=s_ref.at[2 * px + py], dst_ref=out_ref.at[rel - 1],
            send_sem=send_sems.at[rel - 1], recv_sem=recv_sems.at[rel - 1],
            device_id=(px, py, c), device_id_type=MESH))
    return copies


def _exchange_chips(s1, name):
    _, r, d = s1.shape

    def body(s_ref, out_ref, send_sems, recv_sems):
        copies = _chip_copies(s_ref, out_ref, send_sems, recv_sems)
        for cp in copies:
            cp.start()
        for cp in copies:
            cp.wait_recv()
        for cp in copies:
            cp.wait_send()

    return pl.pallas_call(
        body, name=name,
        out_shape=jax.ShapeDtypeStruct((3, r, d), s1.dtype),
        in_specs=[pl.BlockSpec(memory_space=pl.ANY)], out_specs=pl.BlockSpec(memory_space=pl.ANY),
        scratch_shapes=[pltpu.SemaphoreType.DMA((3,)), pltpu.SemaphoreType.DMA((3,))],
    )(s1)


def _all_reduce_small(blk, name):
    rows, d = blk.shape

    def body(x_ref, out_ref, land, send_sems, recv_sems):
        x, y, c = _my_pos()
        me = 4 * x + 2 * y + c
        copies = []
        for rel in range(1, N_DEV):
            peer = (jnp.bitwise_xor(x, rel >> 2), jnp.bitwise_xor(y, (rel >> 1) & 1), jnp.bitwise_xor(c, rel & 1))
            copies.append(pltpu.make_async_remote_copy(
                src_ref=x_ref, dst_ref=land.at[rel - 1], send_sem=send_sems.at[rel - 1], recv_sem=recv_sems.at[rel - 1],
                device_id=peer, device_id_type=MESH))
        for cp in copies:
            cp.start()
        for cp in copies:
            cp.wait_recv()
        acc = jnp.zeros((rows, d), F32)
        for dev in range(N_DEV):
            rel = jnp.bitwise_xor(me, dev)
            got = land[jnp.maximum(rel - 1, 0)]
            acc = acc + jnp.where(rel == 0, x_ref[...], got)
        out_ref[...] = acc
        for cp in copies:
            cp.wait_send()

    return pl.pallas_call(
        body, name=name,
        out_shape=jax.ShapeDtypeStruct((rows, d), F32),
        in_specs=[pl.BlockSpec(memory_space=pltpu.VMEM)], out_specs=pl.BlockSpec(memory_space=pltpu.VMEM),
        scratch_shapes=[pltpu.VMEM((N_DEV - 1, rows, d), F32), pltpu.SemaphoreType.DMA((N_DEV - 1,)),
                        pltpu.SemaphoreType.DMA((N_DEV - 1,))],
    )(blk)


def _cast_rows(x, dtype, name):
    return _rowwise(lambda v: v, [x], [], [(x.shape[1], dtype)], [], name=name, tm=CFG.pack_tile)[0]


def _sum_sibling(g, recv1, c_idx, name):
    _, r, d = g.shape
    tm = CFG.pack_tile

    def body(c_ref, g_ref, r_ref, o_ref):
        o_ref[0] = (g_ref[0] + r_ref[0]).astype(o_ref.dtype)

    grid_spec = pltpu.PrefetchScalarGridSpec(
        num_scalar_prefetch=1, grid=(4, r // tm),
        in_specs=[pl.BlockSpec((1, tm, d), lambda k, i, c_ref: (2 * k + c_ref[0], i, 0)),
                  pl.BlockSpec((1, tm, d), lambda k, i, c_ref: (k, i, 0))],
        out_specs=pl.BlockSpec((1, tm, d), lambda k, i, c_ref: (k, i, 0)))
    return pl.pallas_call(
        body, name=name, grid_spec=grid_spec,
        out_shape=jax.ShapeDtypeStruct((4, r, d), BF16),
        compiler_params=_cp(("parallel", "parallel")),
    )(c_idx, g, recv1)


def _adamw_math(w, g, m, v):
    m2 = ADAM_B1 * m + (1.0 - ADAM_B1) * g
    v2 = ADAM_B2 * v + (1.0 - ADAM_B2) * (g * g)
    m_hat = m2 / (1.0 - ADAM_B1 ** ADAM_STEP)
    v_hat = v2 / (1.0 - ADAM_B2 ** ADAM_STEP)
    delta = -ADAM_LR * (m_hat / (jnp.sqrt(v_hat) + ADAM_EPS) + ADAM_WD * w)
    return delta, m2, v2


def _adamw_big(g, recv1, recv2, w, m, v, idx, row0, name):
    _, r, d = g.shape
    tm = CFG.pack_tile
    t0 = row0 // tm

    def body(idx_ref, g_ref, r1_ref, ra_ref, rb_ref, rc_ref, w_ref, m_ref, v_ref, og, od, om, ov):
        grad = (g_ref[0] + r1_ref[0]) + ra_ref[0].astype(F32) + rb_ref[0].astype(F32) + rc_ref[0].astype(F32)
        delta, m2, v2 = _adamw_math(w_ref[...], grad, m_ref[...], v_ref[...])
        og[...] = grad
        od[...] = delta
        om[...] = m2
        ov[...] = v2

    flat = pl.BlockSpec((tm, d), lambda i, idx_ref: (i, 0))
    shifted = pl.BlockSpec((tm, d), lambda i, idx_ref: (i + t0, 0))
    grid_spec = pltpu.PrefetchScalarGridSpec(
        num_scalar_prefetch=1, grid=(r // tm,),
        in_specs=[pl.BlockSpec((1, tm, d), lambda i, idx_ref: (idx_ref[0], i, 0)),
                  pl.BlockSpec((1, tm, d), lambda i, idx_ref: (idx_ref[1], i, 0)),
                  pl.BlockSpec((1, tm, d), lambda i, idx_ref: (0, i, 0)),
                  pl.BlockSpec((1, tm, d), lambda i, idx_ref: (1, i, 0)),
                  pl.BlockSpec((1, tm, d), lambda i, idx_ref: (2, i, 0)),
                  shifted, shifted, shifted],
        out_specs=[flat, flat, flat, flat])
    shp = jax.ShapeDtypeStruct((r, d), F32)
    return pl.pallas_call(
        body, name=name, grid_spec=grid_spec, out_shape=[shp, shp, shp, shp],
        compiler_params=_cp(("parallel",)),
    )(idx, g, recv1, recv2, recv2, recv2, w, m, v)


def _rows_of(v, d):
    flat = v.reshape(-1)
    rows = -(-flat.shape[0] // d)
    rows += (-rows) % SUBLANE
    return jnp.pad(flat, (0, rows * d - flat.shape[0])).reshape(rows, d)


def _pad_rows(a, mult):
    pad = (-a.shape[0]) % mult
    if pad:
        a = jnp.pad(a, ((0, pad),) + ((0, 0),) * (a.ndim - 1))
    return a


_BIG = ("w_in", "w_xkv", "w_up", "w_proj_gdn", "w_proj_sb", "w_out", "w_xq", "w_xo", "w_down")
_COL_SHARDED = ("w_in", "w_xkv", "w_up")
_SMALL_REP = ("norm_mix", "norm_x", "norm_mem", "norm_ffn", "a_log", "dt_bias", "gdn_out_norm", "xq_norm", "xk_norm")
_SMALL_CONV = ("conv_gdn", "conv_ffn")


def _part_rows(shapes):
    out = []
    for n in _BIG:
        rows, cols = shapes[n]
        cnt = cols if n in _COL_SHARDED else rows
        out.append((cnt, cnt + (-cnt) % (CFG.pack_tile if n == _BIG[0] else PACK_ROW_ALIGN)))
    return out


def _pack_big_shards(shards, shapes):
    parts = []
    for n, (_, padded) in zip(_BIG, _part_rows(shapes)):
        parts.append(_pad_rows(shards[n].T if n in _COL_SHARDED else shards[n], padded))
    return _pad_rows(jnp.concatenate(parts, axis=0), CFG.pack_tile)


def _unpack_gathered(gath, shapes, names):
    out, r0 = {}, 0
    for n, (cnt, padded) in zip(_BIG, _part_rows(shapes)):
        if n not in names:
            continue
        out[n] = gath[:, r0:r0 + cnt, :].reshape(N_DEV * cnt, gath.shape[2])
        r0 += padded
    return out


def _pack_full_grads(grads, shapes, names):
    d = CFG.d
    parts = []
    for n, (cnt, padded) in zip(_BIG, _part_rows(shapes)):
        if n not in names:
            continue
        g = grads[n].reshape(N_DEV, cnt, d)
        if padded > cnt:
            g = jnp.pad(g, ((0, 0), (0, padded - cnt), (0, 0)))
        parts.append(g)
    full = jnp.concatenate(parts, axis=1)
    pad = (-full.shape[1]) % CFG.pack_tile
    if pad:
        full = jnp.pad(full, ((0, 0), (0, pad), (0, 0)))
    return full


def _unpack_shard(packed, shapes, names):
    out, r0 = {}, 0
    for n, (cnt, padded) in zip(_BIG, _part_rows(shapes)):
        if n not in names:
            continue
        part = packed[r0:r0 + cnt]
        out[n] = (part.T if n in _COL_SHARDED else part).reshape((1,) + tuple(shapes[n]))
        r0 += padded
    return out


def kernel(x, mem, norm_mix, w_in, conv_gdn, a_log, dt_bias, gdn_out_norm, w_proj_gdn, w_proj_sb, w_out, norm_x, norm_mem, w_xq, w_xkv, xq_norm, xk_norm, w_xo, norm_ffn, w_up, conv_ffn, w_down, loss_target, m_norm_mix, m_w_in, m_conv_gdn, m_a_log, m_dt_bias, m_gdn_out_norm, m_w_proj_gdn, m_w_proj_sb, m_w_out, m_norm_x, m_norm_mem, m_w_xq, m_w_xkv, m_xq_norm, m_xk_norm, m_w_xo, m_norm_ffn, m_w_up, m_conv_ffn, m_w_down, v_norm_mix, v_w_in, v_conv_gdn, v_a_log, v_dt_bias, v_gdn_out_norm, v_w_proj_gdn, v_w_proj_sb, v_w_out, v_norm_x, v_norm_mem, v_w_xq, v_w_xkv, v_xq_norm, v_xk_norm, v_w_xo, v_norm_ffn, v_w_up, v_conv_ffn, v_w_down):
    names = ("norm_mix", "w_in", "conv_gdn", "a_log", "dt_bias", "gdn_out_norm", "w_proj_gdn", "w_proj_sb", "w_out",
             "norm_x", "norm_mem", "w_xq", "w_xkv", "xq_norm", "xk_norm", "w_xo", "norm_ffn", "w_up", "conv_ffn", "w_down")
    wts = dict(zip(names, (norm_mix, w_in, conv_gdn, a_log, dt_bias, gdn_out_norm, w_proj_gdn, w_proj_sb, w_out,
                           norm_x, norm_mem, w_xq, w_xkv, xq_norm, xk_norm, w_xo, norm_ffn, w_up, conv_ffn, w_down)))
    mom = dict(zip(names, (m_norm_mix, m_w_in, m_conv_gdn, m_a_log, m_dt_bias, m_gdn_out_norm, m_w_proj_gdn, m_w_proj_sb,
                           m_w_out, m_norm_x, m_norm_mem, m_w_xq, m_w_xkv, m_xq_norm, m_xk_norm, m_w_xo, m_norm_ffn, m_w_up,
                           m_conv_ffn, m_w_down)))
    vel = dict(zip(names, (v_norm_mix, v_w_in, v_conv_gdn, v_a_log, v_dt_bias, v_gdn_out_norm, v_w_proj_gdn, v_w_proj_sb,
                           v_w_out, v_norm_x, v_norm_mem, v_w_xq, v_w_xkv, v_xq_norm, v_xk_norm, v_w_xo, v_norm_ffn, v_w_up,
                           v_conv_ffn, v_w_down)))
    cfg = CFG
    d, bsz, s = cfg.d, cfg.b, cfg.s
    t = bsz * s
    gh, sbh = cfg.gh, cfg.sbh
    gw, sw = gh * HD, sbh * HD
    nchunk = s // cfg.gch
    mx, my, mc = _my_pos()
    me = 4 * mx + 2 * my + mc

    shard_shapes = {n: tuple(wts[n].shape[1:]) for n in _BIG}

    packed_w = _pack_big_shards({n: wts[n][0] for n in _BIG}, shard_shapes)
    packed_wb = _cast_rows(packed_w, BF16, "cast_weights")
    rows_in = _part_rows(shard_shapes)[0][1]
    full = _unpack_gathered(_all_gather_big(packed_wb[:rows_in], "all_gather_w_in"), shard_shapes, _BIG[:1])

    conv_rows = {n: _rows_of(wts[n][0], d) for n in _SMALL_CONV}
    conv_cnt = {n: conv_rows[n].shape[0] for n in _SMALL_CONV}
    conv_blk = _pad_rows(jnp.concatenate([conv_rows[n] for n in _SMALL_CONV], axis=0), SUBLANE)
    conv_all = jnp.zeros((N_DEV,) + conv_blk.shape, F32)
    conv_all = lax.dynamic_update_slice(conv_all, conv_blk[None], (me, 0, 0))
    conv_all = _all_reduce_small(conv_all.reshape(-1, d), "gather_conv_taps").reshape((N_DEV,) + conv_blk.shape)

    def full_conv(n, r0):
        k, cols = wts[n].shape[1], wts[n].shape[2]
        part = conv_all[:, r0:r0 + conv_cnt[n], :].reshape(N_DEV, -1)[:, :k * cols].reshape(N_DEV, k, cols)
        return part.transpose(1, 0, 2).reshape(k, N_DEV * cols)

    cgdn = full_conv("conv_gdn", 0)
    cffn = full_conv("conv_ffn", conv_cnt["conv_gdn"])

    win = full["w_in"]
    o_ab = 3 * gw
    o_z = o_ab + 2 * gh
    o_sb = o_z + gw
    o_gate = o_sb + 3 * sw
    w_qkv = win[:o_ab]
    w_ab = jnp.concatenate([win[o_ab:o_z], jnp.zeros((LANE - 2 * gh, d), win.dtype)], axis=0)
    w_z = win[o_z:o_sb]
    w_sb = win[o_sb:o_gate]
    w_gate = win[o_gate:]

    alog_p = jnp.concatenate([a_log.reshape(1, -1), jnp.zeros((1, LANE - gh), F32)], axis=1)
    dtb_p = jnp.concatenate([dt_bias.reshape(1, -1), jnp.zeros((1, LANE - gh), F32)], axis=1)
    onorm = gdn_out_norm.reshape(1, HD)

    x2 = x.reshape(t, d)
    tgt2 = loss_target.reshape(t, d)
    mem2 = mem.reshape(bsz * cfg.mem, d)

    (xn,) = _rowwise(_rms, [x2], [norm_mix], [(d, BF16)], [], name="norm_mix_fwd")
    p_qkv = _mm(xn, w_qkv, tb=True, name="proj_qkv")
    p_ab = _mm(xn, w_ab, tb=True, name="proj_ab")
    p_z = _mm(xn, w_z, tb=True, name="proj_z")
    p_sb = _mm(xn, w_sb, tb=True, name="proj_sb")
    p_gate = _mm(xn, w_gate, tb=True, name="proj_gate")

    qkv_c = _gdn_conv_fwd(p_qkv.reshape(bsz, s, 3 * gw), cgdn)
    (gb,) = _rowwise(_gates_fn, [p_ab], [alog_p, dtb_p], [(LANE, F32)], [], name="gdn_gates_fwd")
    gbt = gb.reshape(bsz, s, LANE)[:, :, :2 * gh].transpose(0, 2, 1).reshape(bsz, 2 * gh, nchunk, 1, cfg.gch)
    o_a, states = _gdn_fwd(qkv_c, p_z.reshape(bsz, s, gw), gbt, onorm)
    o_b, gathered = _sb_fwd(p_sb.reshape(bsz, s, 3 * sw), packed_wb[rows_in:])
    full.update(_unpack_gathered(gathered, shard_shapes, _BIG[1:]))

    pa = _mm(o_a.reshape(t, gw), full["w_proj_gdn"], name="proj_gdn_out")
    pb = _mm(o_b.reshape(t, sw), full["w_proj_sb"], name="proj_sb_out")

    def merge_fn(pa_, pb_, gate_):
        return _sigmoid(gate_[:, :d]) * pa_ + _sigmoid(gate_[:, d:]) * pb_

    (merged,) = _rowwise(merge_fn, [pa, pb, p_gate], [], [(d, BF16)], [], name="merge_fwd")
    h1 = _mm(merged, full["w_out"], add=x2, name="mixer_out")

    (hn_x,) = _rowwise(_rms, [h1], [norm_x], [(d, BF16)], [], name="norm_x_fwd")
    (mn,) = _rowwise(_rms, [mem2], [norm_mem], [(d, BF16)], [], name="norm_mem_fwd")
    q_raw = _mm(hn_x, full["w_xq"], name="xattn_q")
    kv = _mm(mn, full["w_xkv"], tb=True, name="xattn_kv")
    xo = _xattn_fwd(q_raw.reshape(bsz, s, d), kv.reshape(bsz, cfg.mem, 2 * d), xq_norm, xk_norm)
    h2 = _mm(xo.reshape(t, d), full["w_xo"], add=h1, name="xattn_out")

    (hn_f,) = _rowwise(_rms, [h2], [norm_ffn], [(d, BF16)], [], name="norm_ffn_fwd")
    up = _mm(hn_f, full["w_up"], tb=True, name="ffn_up")
    act = _ffn_conv_fwd(up.reshape(bsz, s, 2 * cfg.dff), cffn)
    y = _mm(act.reshape(t, cfg.dff), full["w_down"], add=h2, name="ffn_down")

    def loss_fn(y_, tg_):
        err = y_ - tg_
        part = 0.5 * jnp.sum(err * err) / d
        return err / d, err / d, jnp.full((1, LANE), part, F32)

    dy, dy_b, loss_part = _rowwise(loss_fn, [y, tgt2], [], [(d, F32), (d, BF16)], [(1, LANE)], name="loss")

    grads = {}
    dact = _mm(dy_b, full["w_down"], tb=True, name="d_act")
    grads["w_down"] = _mm(act.reshape(t, cfg.dff), dy_b, ta=True, name="dw_down")
    dup1, dup2, dcf1, dcf2 = _ffn_conv_bwd(up.reshape(bsz, s, 2 * cfg.dff), cffn, dact.reshape(bsz, s, cfg.dff))
    dup = jnp.concatenate([dup1, dup2], axis=2).reshape(t, 2 * cfg.dff)
    g_conv_ffn = jnp.concatenate([dcf1, dcf2], axis=1)
    dhn_f = _mm(dup, full["w_up"], name="d_hn_ffn")
    grads["w_up"] = _mm(dup, hn_f, ta=True, name="dw_up")

    def norm_bwd_fn(h_, res_, dn_, g_):
        _, f = jax.vjp(_rms, h_, g_)
        dh, dg = f(dn_)
        return res_ + dh, dg

    def norm_bwd_copy_fn(h_, res_, dn_, g_):
        dres, dg = norm_bwd_fn(h_, res_, dn_, g_)
        return dres, dres, dg

    dh2, dh2_b, g_norm_ffn = _rowwise(norm_bwd_copy_fn, [h2, dy, dhn_f], [norm_ffn], [(d, F32), (d, BF16)], [(1, d)],
                                      name="norm_ffn_bwd")

    dxo = _mm(dh2_b, full["w_xo"], tb=True, out_dtype=BF16, name="d_xo")
    grads["w_xo"] = _mm(xo.reshape(t, d), dh2_b, ta=True, name="dw_xo")
    dq_raw, dkv, g_xq_norm, g_xk_norm = _xattn_bwd(q_raw.reshape(bsz, s, d), kv.reshape(bsz, cfg.mem, 2 * d),
                                                   xq_norm, xk_norm, dxo.reshape(bsz, s, d))
    dq_raw2 = dq_raw.reshape(t, d)
    dkv2 = dkv.reshape(bsz * cfg.mem, 2 * d)
    dhn_x = _mm(dq_raw2, full["w_xq"], tb=True, name="d_hn_x")
    grads["w_xq"] = _mm(hn_x, dq_raw2, ta=True, name="dw_xq")
    dmn = _mm(dkv2, full["w_xkv"], name="d_mn")
    grads["w_xkv"] = _mm(dkv2, mn, ta=True, name="dw_xkv")

    def norm_w_bwd_fn(h_, dn_, g_):
        _, f = jax.vjp(lambda gg: _rms(h_, gg), g_)
        return f(dn_)[0]

    (g_norm_mem,) = _rowwise(norm_w_bwd_fn, [mem2, dmn], [norm_mem], [], [(1, d)], name="norm_mem_bwd")
    dh1, dh1_b, g_norm_x = _rowwise(norm_bwd_copy_fn, [h1, dh2, dhn_x], [norm_x], [(d, F32), (d, BF16)], [(1, d)],
                                    name="norm_x_bwd")

    dmerged = _mm(dh1_b, full["w_out"], tb=True, name="d_merged")
    grads["w_out"] = _mm(merged, dh1_b, ta=True, name="dw_out")

    def merge_bwd_fn(pa_, pb_, gate_, dm_):
        _, f = jax.vjp(merge_fn, pa_, pb_, gate_)
        return f(dm_)

    dpa, dpb, dgate = _rowwise(merge_bwd_fn, [pa, pb, p_gate, dmerged], [], [(d, BF16), (d, BF16), (2 * d, BF16)], [],
                               name="merge_bwd")
    do_a = _mm(dpa, full["w_proj_gdn"], tb=True, name="d_o_gdn")
    grads["w_proj_gdn"] = _mm(o_a.reshape(t, gw), dpa, ta=True, name="dw_proj_gdn")
    do_b = _mm(dpb, full["w_proj_sb"], tb=True, name="d_o_sb")
    grads["w_proj_sb"] = _mm(o_b.reshape(t, sw), dpb, ta=True, name="dw_proj_sb")

    c_idx = jnp.reshape(mc, (1,)).astype(jnp.int32)
    early = _BIG[1:]
    g_early = _pack_full_grads(grads, shard_shapes, early)
    r1_early = _exchange_sibling(g_early, "grads_to_sibling_early")
    s1_early = _sum_sibling(g_early, r1_early, c_idx, "sum_sibling_early")
    dsq, dsk, dsv, r2_early = _sb_bwd(p_sb.reshape(bsz, s, 3 * sw), do_b.reshape(bsz, s, sw), s1_early)
    dp_sb = jnp.concatenate([dsq, dsk, dsv], axis=2).reshape(t, 3 * sw)

    dgq, dgk, dgv, dz, dg, dbeta, g_onorm = _gdn_bwd(qkv_c, p_z.reshape(bsz, s, gw), gbt, onorm, states,
                                                     do_a.reshape(bsz, s, gw))
    dgb = jnp.concatenate([dg, dbeta], axis=1).reshape(bsz, 2 * gh, s).transpose(0, 2, 1)
    dgb = jnp.concatenate([dgb, jnp.zeros((bsz, s, LANE - 2 * gh), F32)], axis=2).reshape(t, LANE)

    def gates_bwd_fn(ab_, dgb_, alog_, dtb_):
        _, f = jax.vjp(_gates_fn, ab_, alog_, dtb_)
        return f(dgb_)

    dp_ab, g_alog, g_dtb = _rowwise(gates_bwd_fn, [p_ab, dgb], [alog_p, dtb_p], [(LANE, BF16)], [(1, LANE), (1, LANE)],
                                    name="gdn_gates_bwd")
    dqkv_c = jnp.concatenate([dgq, dgk, dgv], axis=2)
    dp_qkv, g_conv_gdn = _gdn_conv_bwd(p_qkv.reshape(bsz, s, 3 * gw), cgdn, dqkv_c)
    dp_qkv = dp_qkv.reshape(t, 3 * gw)
    dp_z = dz.reshape(t, gw)

    dxn = _mm(dp_qkv, w_qkv, name="d_xn_qkv")
    dxn = _mm(dp_ab, w_ab, add=dxn, name="d_xn_ab")
    dxn = _mm(dp_z, w_z, add=dxn, name="d_xn_z")
    dxn = _mm(dp_sb, w_sb, add=dxn, name="d_xn_sb")
    dxn = _mm(dgate, w_gate, add=dxn, name="d_xn_gate")
    grads["w_in"] = jnp.concatenate([
        _mm(dp_qkv, xn, ta=True, name="dw_in_qkv"),
        _mm(dp_ab, xn, ta=True, name="dw_in_ab")[:2 * gh],
        _mm(dp_z, xn, ta=True, name="dw_in_z"),
        _mm(dp_sb, xn, ta=True, name="dw_in_sb"),
        _mm(dgate, xn, ta=True, name="dw_in_gate")], axis=0)
    grad_x, g_norm_mix = _rowwise(norm_bwd_fn, [x2, dh1, dxn], [norm_mix], [(d, F32)], [(1, d)], name="norm_mix_bwd")

    small_g = {"norm_mix": g_norm_mix, "norm_x": g_norm_x, "norm_mem": g_norm_mem, "norm_ffn": g_norm_ffn,
               "a_log": g_alog[:, :gh], "dt_bias": g_dtb[:, :gh], "gdn_out_norm": g_onorm,
               "xq_norm": g_xq_norm, "xk_norm": g_xk_norm}
    sm_rows = [_rows_of(small_g[n], d) for n in _SMALL_REP] + [_rows_of(loss_part, d)]
    sm_rows += [_rows_of(g_conv_gdn, d), _rows_of(g_conv_ffn, d)]
    sm_cnt = [r.shape[0] for r in sm_rows]
    sm_sum = _all_reduce_small(_pad_rows(jnp.concatenate(sm_rows, axis=0), SUBLANE), "all_reduce_small_grads")
    offs = [0]
    for cnt in sm_cnt:
        offs.append(offs[-1] + cnt)
    small_grad = {}
    for i, n in enumerate(_SMALL_REP):
        small_grad[n] = sm_sum[offs[i]:offs[i + 1]].reshape(-1)[:wts[n].size].reshape(wts[n].shape)
    loss = sm_sum[offs[len(_SMALL_REP)], 0]
    for i, n in enumerate(_SMALL_CONV):
        k, cols = wts[n].shape[1], wts[n].shape[2]
        o = offs[len(_SMALL_REP) + 1 + i]
        fullg = sm_sum[o:o + sm_cnt[len(_SMALL_REP) + 1 + i]].reshape(-1)[:k * cols * N_DEV].reshape(k, N_DEV * cols)
        small_grad[n] = lax.dynamic_slice(fullg, (0, me * cols), (k, cols)).reshape(wts[n].shape)

    small_names = _SMALL_REP + _SMALL_CONV

    def pack_small(src):
        return _pad_rows(jnp.concatenate([_rows_of(src[n], d) for n in small_names], axis=0), SUBLANE)

    sw_, sg_, sm_, sv_ = pack_small(wts), pack_small(small_grad), pack_small(mom), pack_small(vel)
    sd_, snm_, snv_ = _rowwise(_adamw_math, [sw_, sg_, sm_, sv_], [], [(d, F32)] * 3, [], name="adamw_small", tm=sw_.shape[0])

    def unpack_small(packed):
        out, r0 = {}, 0
        for n in small_names:
            cnt = _rows_of(wts[n], d).shape[0]
            out[n] = packed[r0:r0 + cnt].reshape(-1)[:wts[n].size].reshape(wts[n].shape)
            r0 += cnt
        return out

    small_delta, small_m, small_v = unpack_small(sd_), unpack_small(snm_), unpack_small(snv_)

    g_late = _pack_full_grads(grads, shard_shapes, _BIG[:1])
    r1_late = _exchange_sibling(g_late, "grads_to_sibling_late")
    s1_late = _sum_sibling(g_late, r1_late, c_idx, "sum_sibling_late")
    r2_late = _exchange_chips(s1_late, "grads_to_chips_late")
    idx = jnp.stack([me, 2 * mx + my]).astype(jnp.int32)
    pm = _pack_big_shards({n: mom[n][0] for n in _BIG}, shard_shapes)
    pv = _pack_big_shards({n: vel[n][0] for n in _BIG}, shard_shapes)
    upd_late = _adamw_big(g_late, r1_late, r2_late, packed_w, pm, pv, idx, 0, "adamw_late")
    upd_early = _adamw_big(g_early, r1_early, r2_early, packed_w, pm, pv, idx, g_late.shape[1], "adamw_early")
    big_grad, big_delta, big_m, big_v = (
        {**_unpack_shard(a, shard_shapes, _BIG[:1]), **_unpack_shard(b, shard_shapes, early)}
        for a, b in zip(upd_late, upd_early))

    def pick(big, small, n):
        return big[n] if n in big else small[n]

    outs = [loss, grad_x.reshape(bsz, s, d)]
    outs += [pick(big_grad, small_grad, n) for n in names]
    outs += [pick(big_delta, small_delta, n) for n in names]
    outs += [pick(big_m, small_m, n) for n in names]
    outs += [pick(big_v, small_v, n) for n in names]
    return tuple(outs)
```

```python
import functools

import jax
import jax.numpy as jnp
from jax import lax
from jax.experimental import pallas as pl
from jax.experimental.pallas import tpu as pltpu

F32 = jnp.float32
BF16 = jnp.bfloat16
FDOT_PRECISION = lax.Precision.HIGH

LANE = 128
SUBLANE = 8
PACK_ROW_ALIGN = 16
VMEM_LIMIT = 56 * 2 ** 20
N_DEV = 8
MESH = pl.DeviceIdType.MESH

EPS = 1e-6
ADAM_LR = 0.001
ADAM_B1 = 0.9
ADAM_B2 = 0.999
ADAM_EPS = 1e-08
ADAM_WD = 0.01
ADAM_STEP = 10


class _Cfg:
    d = 1024
    b = 4
    s = 2048
    mem = 256
    gh = 8
    gch = 64
    sbh = 8
    xh = 4
    dff = 2816
    pack_tile = 256


CFG = _Cfg()
HD = 128
SB_BLK = 128
SB_HP = 4
GDN_HP = 8
MM_TILES = (1024, 1408, 704, 512, 256, 128)
MM_K_TILES = (1024, 1408, 704, 512, 256, 128)
MM_K_TILES_F32 = (512, 704, 256, 128)
CONV_CB = 256
XQ_TILE = 256


def _tile(n, prefs):
    for t in prefs:
        if n % t == 0:
            return t
    raise ValueError(f"no tile for {n}")


def _cp(sem, **kw):
    return pltpu.CompilerParams(dimension_semantics=sem, vmem_limit_bytes=VMEM_LIMIT, **kw)


def _dims(kind):
    return {"nn": (((1,), (0,)), ((), ())), "nt": (((1,), (1,)), ((), ())), "tn": (((0,), (0,)), ((), ()))}[kind]


def _raw_bdot(a, b, kind):
    return lax.dot_general(a.astype(BF16), b.astype(BF16), _dims(kind), preferred_element_type=F32)


def _raw_fdot(a, b, kind):
    return lax.dot_general(a.astype(F32), b.astype(F32), _dims(kind), precision=FDOT_PRECISION,
                           preferred_element_type=F32)


def _make_dot(raw):
    @functools.partial(jax.custom_vjp, nondiff_argnums=(2,))
    def dot(a, b, kind):
        return raw(a, b, kind)

    def fwd(a, b, kind):
        return raw(a, b, kind), (a, b)

    def bwd(kind, res, g):
        a, b = res
        if kind == "nn":
            return raw(g, b, "nt").astype(a.dtype), raw(a, g, "tn").astype(b.dtype)
        if kind == "nt":
            return raw(g, b, "nn").astype(a.dtype), raw(g, a, "tn").astype(b.dtype)
        return raw(b, g, "nt").astype(a.dtype), raw(a, g, "nn").astype(b.dtype)

    dot.defvjp(fwd, bwd)
    return dot


_bdot = _make_dot(_raw_bdot)
_fdot = _make_dot(_raw_fdot)


def _split_dot(x, m01):
    hi = x.astype(BF16)
    lo = (x - hi.astype(F32)).astype(BF16)
    return (lax.dot_general(hi, m01, _dims("nn"), preferred_element_type=F32)
            + lax.dot_general(lo, m01, _dims("nn"), preferred_element_type=F32))


_sigmoid = jax.nn.sigmoid


def _silu(x):
    return x * _sigmoid(x)


def _softplus(x):
    return jnp.maximum(x, 0.0) + jnp.log1p(jnp.exp(-jnp.abs(x)))


def _rms(x, g):
    return x * lax.rsqrt(jnp.mean(x * x, axis=-1, keepdims=True) + EPS) * g


def _iota2(shape, dim):
    return lax.broadcasted_iota(jnp.int32, shape, dim)


def _mm(a, b, *, ta=False, tb=False, add=None, out_dtype=F32, name):
    if ta:
        kd, m = a.shape
    else:
        m, kd = a.shape
    if tb:
        n, kb = b.shape
    else:
        kb, n = b.shape
    assert kd == kb, (a.shape, b.shape, ta, tb)
    tm = _tile(m, MM_TILES)
    tn = _tile(n, MM_TILES)
    wide = max(a.dtype.itemsize, b.dtype.itemsize) > 2
    tk = _tile(kd, MM_K_TILES_F32 if wide else MM_K_TILES)
    nk = kd // tk
    kind_dims = (((0 if ta else 1,), (1 if tb else 0,)), ((), ()))

    def body(*refs):
        a_ref, b_ref = refs[:2]
        add_ref = refs[2] if add is not None else None
        o_ref = refs[3 if add is not None else 2]
        part = lax.dot_general(a_ref[...].astype(BF16), b_ref[...].astype(BF16), kind_dims,
                               preferred_element_type=F32)

        def finish(r):
            if add is not None:
                r = r + add_ref[...].astype(F32)
            o_ref[...] = r.astype(o_ref.dtype)

        if nk == 1:
            finish(part)
            return
        acc = refs[-1]
        k = pl.program_id(2)

        @pl.when(k == 0)
        def _():
            acc[...] = part

        @pl.when((k > 0) & (k < nk - 1))
        def _():
            acc[...] += part

        @pl.when(k == nk - 1)
        def _():
            finish(acc[...] + part)

    a_spec = pl.BlockSpec((tk, tm), lambda i, j, k: (k, i)) if ta else pl.BlockSpec((tm, tk), lambda i, j, k: (i, k))
    b_spec = pl.BlockSpec((tn, tk), lambda i, j, k: (j, k)) if tb else pl.BlockSpec((tk, tn), lambda i, j, k: (k, j))
    in_specs = [a_spec, b_spec]
    args = [a, b]
    if add is not None:
        in_specs.append(pl.BlockSpec((tm, tn), lambda i, j, k: (i, j)))
        args.append(add)
    return pl.pallas_call(
        body, name=name, grid=(m // tm, n // tn, nk),
        in_specs=in_specs, out_specs=pl.BlockSpec((tm, tn), lambda i, j, k: (i, j)),
        out_shape=jax.ShapeDtypeStruct((m, n), out_dtype),
        scratch_shapes=[pltpu.VMEM((tm, tn), F32)] if nk > 1 else [],
        compiler_params=_cp(("parallel", "parallel", "arbitrary")),
    )(*args)


def _rowwise(fn, rows, pars, out_rows, out_accs, *, name, tm=None):
    t = rows[0].shape[0]
    if tm is None:
        tm = _tile(t, (256, 128, 64, 32, 16))
    assert t % tm == 0, (t, tm)
    n_r, n_p, n_or, n_oa = len(rows), len(pars), len(out_rows), len(out_accs)

    def body(*refs):
        r_in = refs[:n_r]
        p_in = refs[n_r:n_r + n_p]
        o_r = refs[n_r + n_p:n_r + n_p + n_or]
        o_a = refs[n_r + n_p + n_or:]
        outs = fn(*[r[...] for r in r_in], *[p[...] for p in p_in])
        if not isinstance(outs, (tuple, list)):
            outs = (outs,)
        assert len(outs) == n_or + n_oa, (name, len(outs))
        for ref, val in zip(o_r, outs[:n_or]):
            ref[...] = val.astype(ref.dtype)
        if n_oa:
            @pl.when(pl.program_id(0) == 0)
            def _():
                for ref in o_a:
                    ref[...] = jnp.zeros_like(ref)

            for ref, val in zip(o_a, outs[n_or:]):
                ref[...] += val.astype(F32)

    in_specs = [pl.BlockSpec((tm, r.shape[1]), lambda i: (i, 0)) for r in rows]
    in_specs += [pl.BlockSpec(p.shape, lambda i: (0, 0)) for p in pars]
    out_specs = [pl.BlockSpec((tm, c), lambda i: (i, 0)) for c, _ in out_rows]
    out_specs += [pl.BlockSpec(s, lambda i: (0, 0)) for s in out_accs]
    out_shape = [jax.ShapeDtypeStruct((t, c), dt) for c, dt in out_rows]
    out_shape += [jax.ShapeDtypeStruct(s, F32) for s in out_accs]
    return pl.pallas_call(
        body, name=name, grid=(t // tm,), in_specs=in_specs, out_specs=out_specs, out_shape=out_shape,
        compiler_params=_cp(("arbitrary",)),
    )(*rows, *pars)


def _shift_down(x, sh):
    rolled = pltpu.roll(x, sh, 0)
    return jnp.where(_iota2(x.shape, 0) >= sh, rolled, 0.0)


def _shift_up(x, sh):
    s = x.shape[0]
    rolled = pltpu.roll(x, s - sh, 0)
    return jnp.where(_iota2(x.shape, 0) < s - sh, rolled, 0.0)


def _conv(x, w):
    k = w.shape[0]
    y = x * w[k - 1:k, :]
    for i in range(k - 1):
        y = y + _shift_down(x, k - 1 - i) * w[i:i + 1, :]
    return y


def _conv_bwd(x, w, dy):
    k = w.shape[0]
    dx = dy * w[k - 1:k, :]
    dws = []
    for i in range(k - 1):
        dx = dx + _shift_up(dy, k - 1 - i) * w[i:i + 1, :]
        dws.append(jnp.sum(dy * _shift_down(x, k - 1 - i), axis=0, keepdims=True))
    dws.append(jnp.sum(dy * x, axis=0, keepdims=True))
    return dx, dws


def _gdn_post(y, j, nqb):
    a = _silu(y)
    sc = jnp.where(j < nqb, HD ** -0.5, 1.0).astype(F32)
    outs = []
    for h in range(y.shape[1] // HD):
        ah = a[:, h * HD:(h + 1) * HD]
        l2 = ah * lax.rsqrt(jnp.sum(ah * ah, axis=-1, keepdims=True) + EPS)
        outs.append(jnp.where(j < 2 * nqb, l2 * sc, ah))
    return jnp.concatenate(outs, axis=1) if len(outs) > 1 else outs[0]


def _gdn_conv_fwd(x, w):
    bsz, s, c3 = x.shape
    k = w.shape[0]
    nb = c3 // CONV_CB
    nqb = nb // 3

    def body(x_ref, w_ref, o_ref):
        j = pl.program_id(1)
        o_ref[0] = _gdn_post(_conv(x_ref[0], w_ref[...]), j, nqb)

    return pl.pallas_call(
        body, name="gdn_conv_fwd", grid=(bsz, nb),
        in_specs=[pl.BlockSpec((1, s, CONV_CB), lambda b, j: (b, 0, j)), pl.BlockSpec((k, CONV_CB), lambda b, j: (0, j))],
        out_specs=pl.BlockSpec((1, s, CONV_CB), lambda b, j: (b, 0, j)),
        out_shape=jax.ShapeDtypeStruct(x.shape, F32),
        compiler_params=_cp(("parallel", "parallel")),
    )(x, w)


def _gdn_conv_bwd(x, w, dout):
    bsz, s, c3 = x.shape
    k = w.shape[0]
    nb = c3 // CONV_CB
    nqb = nb // 3

    def body(x_ref, w_ref, d_ref, dx_ref, dw_ref):
        j = pl.program_id(0)
        b = pl.program_id(1)
        xv, wv = x_ref[0], w_ref[...]
        y = _conv(xv, wv)
        _, f = jax.vjp(lambda yy: _gdn_post(yy, j, nqb), y)
        (dy,) = f(d_ref[0])
        dx, dws = _conv_bwd(xv, wv, dy)
        dx_ref[0] = dx.astype(dx_ref.dtype)

        @pl.when(b == 0)
        def _():
            dw_ref[...] = jnp.zeros_like(dw_ref)

        for i in range(k):
            dw_ref[i:i + 1, :] += dws[i]

    return pl.pallas_call(
        body, name="gdn_conv_bwd", grid=(nb, bsz),
        in_specs=[pl.BlockSpec((1, s, CONV_CB), lambda j, b: (b, 0, j)), pl.BlockSpec((k, CONV_CB), lambda j, b: (0, j)),
                  pl.BlockSpec((1, s, CONV_CB), lambda j, b: (b, 0, j))],
        out_specs=[pl.BlockSpec((1, s, CONV_CB), lambda j, b: (b, 0, j)), pl.BlockSpec((k, CONV_CB), lambda j, b: (0, j))],
        out_shape=[jax.ShapeDtypeStruct(x.shape, BF16), jax.ShapeDtypeStruct(w.shape, F32)],
        compiler_params=_cp(("parallel", "arbitrary")),
    )(x, w, dout)


def _ffn_conv_fwd(up, w):
    bsz, s, c2 = up.shape
    k = w.shape[0]
    nb = (c2 // 2) // CONV_CB

    def body(x1_ref, x2_ref, w1_ref, w2_ref, o_ref):
        u1 = _conv(x1_ref[0], w1_ref[...])
        u2 = _conv(x2_ref[0], w2_ref[...])
        o_ref[0] = (_silu(u1) * u2).astype(o_ref.dtype)

    return pl.pallas_call(
        body, name="ffn_conv_fwd", grid=(bsz, nb),
        in_specs=[pl.BlockSpec((1, s, CONV_CB), lambda b, j: (b, 0, j)), pl.BlockSpec((1, s, CONV_CB), lambda b, j: (b, 0, j + nb)),
                  pl.BlockSpec((k, CONV_CB), lambda b, j: (0, j)), pl.BlockSpec((k, CONV_CB), lambda b, j: (0, j + nb))],
        out_specs=pl.BlockSpec((1, s, CONV_CB), lambda b, j: (b, 0, j)),
        out_shape=jax.ShapeDtypeStruct((bsz, s, c2 // 2), BF16),
        compiler_params=_cp(("parallel", "parallel")),
    )(up, up, w, w)


def _ffn_conv_bwd(up, w, dact):
    bsz, s, c2 = up.shape
    k = w.shape[0]
    half = c2 // 2
    nb = half // CONV_CB

    def body(x1_ref, x2_ref, w1_ref, w2_ref, d_ref, dx1_ref, dx2_ref, dw1_ref, dw2_ref):
        b = pl.program_id(1)
        x1, x2, w1, w2 = x1_ref[0], x2_ref[0], w1_ref[...], w2_ref[...]
        u1 = _conv(x1, w1)
        u2 = _conv(x2, w2)
        _, f = jax.vjp(lambda p, q: _silu(p) * q, u1, u2)
        du1, du2 = f(d_ref[0])
        dx1, dws1 = _conv_bwd(x1, w1, du1)
        dx2, dws2 = _conv_bwd(x2, w2, du2)
        dx1_ref[0] = dx1.astype(dx1_ref.dtype)
        dx2_ref[0] = dx2.astype(dx2_ref.dtype)

        @pl.when(b == 0)
        def _():
            dw1_ref[...] = jnp.zeros_like(dw1_ref)
            dw2_ref[...] = jnp.zeros_like(dw2_ref)

        for i in range(k):
            dw1_ref[i:i + 1, :] += dws1[i]
            dw2_ref[i:i + 1, :] += dws2[i]

    def blk(off):
        return pl.BlockSpec((1, s, CONV_CB), lambda j, b: (b, 0, j + off))

    def wblk(off):
        return pl.BlockSpec((k, CONV_CB), lambda j, b: (0, j + off))

    return pl.pallas_call(
        body, name="ffn_conv_bwd", grid=(nb, bsz),
        in_specs=[blk(0), blk(nb), wblk(0), wblk(nb), blk(0)],
        out_specs=[blk(0), blk(0), wblk(0), wblk(0)],
        out_shape=[jax.ShapeDtypeStruct((bsz, s, half), BF16), jax.ShapeDtypeStruct((bsz, s, half), BF16),
                   jax.ShapeDtypeStruct((k, half), F32), jax.ShapeDtypeStruct((k, half), F32)],
        compiler_params=_cp(("parallel", "arbitrary")),
    )(up, up, w, w, dact)


@jax.custom_vjp
def _inv_unit_lower(mats):
    c = mats[0].shape[0]
    eye = (_iota2((c, c), 0) == _iota2((c, c), 1)).astype(F32)
    ps = [-a for a in mats]
    ts = [eye + p for p in ps]
    n = 2
    while n < c:
        ps = [_raw_fdot(p, p, "nn") for p in ps]
        ts = [t + _raw_fdot(t, p, "nn") for t, p in zip(ts, ps)]
        n *= 2
    return ts


def _inv_fwd(mats):
    ts = _inv_unit_lower(mats)
    return ts, ts


def _inv_bwd(ts, gs):
    xs = [_raw_fdot(g, t, "nt") for g, t in zip(gs, ts)]
    return ([-_raw_fdot(t, x, "tn") for t, x in zip(ts, xs)],)


_inv_unit_lower.defvjp(_inv_fwd, _inv_bwd)


def _gdn_chunk(q, k, v, z, g_row, beta_row, state, onorm):
    nh = range(len(q))
    c = q[0].shape[0]
    ii, jj = _iota2((c, c), 0), _iota2((c, c), 1)
    incl, strict, eye = ii >= jj, ii > jj, ii == jj

    def to_col(row):
        return jnp.sum(jnp.where(eye, jnp.broadcast_to(row, (c, c)), 0.0), axis=1, keepdims=True)

    gc_col = [jnp.sum(jnp.where(incl, jnp.broadcast_to(g_row[h], (c, c)), 0.0), axis=1, keepdims=True) for h in nh]
    gc_row = [jnp.sum(jnp.where(eye, jnp.broadcast_to(gc_col[h], (c, c)), 0.0), axis=0, keepdims=True) for h in nh]
    beta_col = [to_col(beta_row[h]) for h in nh]
    gc_last = [jnp.sum(g_row[h], axis=1, keepdims=True) for h in nh]
    decay = [jnp.where(incl, jnp.exp(jnp.where(incl, gc_col[h] - gc_row[h], 0.0)), 0.0) for h in nh]
    kk = [_bdot(k[h], k[h], "nt") for h in nh]
    qk = [_bdot(q[h], k[h], "nt") * decay[h] for h in nh]
    tinv = _inv_unit_lower([jnp.where(strict, beta_col[h] * kk[h] * decay[h], 0.0) for h in nh])
    rhs = [jnp.concatenate([v[h] * beta_col[h], k[h] * (beta_col[h] * jnp.exp(gc_col[h]))], axis=1) for h in nh]
    uw = [_fdot(tinv[h], rhs[h], "nn") for h in nh]
    dv = v[0].shape[1]
    ws = [_bdot(uw[h][:, dv:], state[h], "nn") for h in nh]
    qs = [_bdot(q[h] * jnp.exp(gc_col[h]), state[h], "nn") for h in nh]
    v_new = [uw[h][:, :dv] - ws[h] for h in nh]
    o = [qs[h] + _bdot(qk[h], v_new[h], "nn") for h in nh]
    kv = [_bdot(k[h] * jnp.exp(gc_last[h] - gc_col[h]), v_new[h], "tn") for h in nh]
    new_state = [state[h] * jnp.exp(gc_last[h]) + kv[h] for h in nh]
    y = [_rms(o[h], onorm) * _silu(z[h]) for h in nh]
    return y, new_state


def _gdn_specs(s, c, reverse):
    n = s // c
    nn = (lambda i: n - 1 - i) if reverse else (lambda i: i)

    def qkv(off):
        return pl.BlockSpec((1, c, GDN_HP * HD), lambda b, h, i: (b, nn(i), h + off))

    def gate(off):
        return pl.BlockSpec((1, GDN_HP, 1, 1, c), lambda b, h, i: (b, h + off, nn(i), 0, 0))

    st = pl.BlockSpec((1, GDN_HP, 1, HD, HD), lambda b, h, i: (b, h, nn(i), 0, 0))
    onorm = pl.BlockSpec((1, HD), lambda b, h, i: (0, 0))
    return n, qkv, gate, st, onorm


def _gdn_fwd(qkv, z, gbt, onorm):
    bsz, s, _ = qkv.shape
    gh, c = CFG.gh, CFG.gch
    ng = gh // GDN_HP
    n, qs, gs, st, on = _gdn_specs(s, c, False)

    def body(q_ref, k_ref, v_ref, z_ref, g_ref, b_ref, on_ref, y_ref, st_ref, state):
        @pl.when(pl.program_id(2) == 0)
        def _():
            state[...] = jnp.zeros_like(state)

        nh = range(GDN_HP)
        hs = [slice(h * HD, (h + 1) * HD) for h in nh]
        s_in = [state[h] for h in nh]
        for h in nh:
            st_ref[0, h, 0] = s_in[h]
        y, s_out = _gdn_chunk([q_ref[0, :, hs[h]] for h in nh], [k_ref[0, :, hs[h]] for h in nh],
                              [v_ref[0, :, hs[h]] for h in nh], [z_ref[0, :, hs[h]] for h in nh],
                              [g_ref[0, h, 0] for h in nh], [b_ref[0, h, 0] for h in nh], s_in, on_ref[...])
        for h in nh:
            y_ref[0, :, hs[h]] = y[h].astype(y_ref.dtype)
            state[h] = s_out[h]

    return pl.pallas_call(
        body, name="gdn_fwd", grid=(bsz, ng, n),
        in_specs=[qs(0), qs(ng), qs(2 * ng), qs(0), gs(0), gs(ng), on],
        out_specs=[qs(0), st],
        out_shape=[jax.ShapeDtypeStruct((bsz, s, gh * HD), BF16), jax.ShapeDtypeStruct((bsz, gh, n, HD, HD), F32)],
        scratch_shapes=[pltpu.VMEM((GDN_HP, HD, HD), F32)],
        compiler_params=_cp(("parallel", "parallel", "arbitrary")),
    )(qkv, qkv, qkv, z, gbt, gbt, onorm)


def _gdn_bwd(qkv, z, gbt, onorm, states, dy):
    bsz, s, _ = qkv.shape
    gh, c = CFG.gh, CFG.gch
    ng = gh // GDN_HP
    n, qs, gs, st, on = _gdn_specs(s, c, True)

    def body(q_ref, k_ref, v_ref, z_ref, g_ref, b_ref, on_ref, st_ref, dy_ref,
             dq_ref, dk_ref, dv_ref, dz_ref, dg_ref, db_ref, don_ref, dstate):
        first = (pl.program_id(0) == 0) & (pl.program_id(1) == 0) & (pl.program_id(2) == 0)

        @pl.when(first)
        def _():
            don_ref[...] = jnp.zeros_like(don_ref)

        @pl.when(pl.program_id(2) == 0)
        def _():
            dstate[...] = jnp.zeros_like(dstate)

        nh = range(GDN_HP)
        hs = [slice(h * HD, (h + 1) * HD) for h in nh]
        _, f = jax.vjp(_gdn_chunk, [q_ref[0, :, hs[h]] for h in nh], [k_ref[0, :, hs[h]] for h in nh],
                       [v_ref[0, :, hs[h]] for h in nh], [z_ref[0, :, hs[h]] for h in nh],
                       [g_ref[0, h, 0] for h in nh], [b_ref[0, h, 0] for h in nh],
                       [st_ref[0, h, 0] for h in nh], on_ref[...])
        dq, dk, dv, dz, dg, db, ds, don = f(([dy_ref[0, :, hs[h]] for h in nh], [dstate[h] for h in nh]))
        for h in nh:
            dq_ref[0, :, hs[h]] = dq[h]
            dk_ref[0, :, hs[h]] = dk[h]
            dv_ref[0, :, hs[h]] = dv[h]
            dz_ref[0, :, hs[h]] = dz[h].astype(dz_ref.dtype)
            dg_ref[0, h, 0] = dg[h]
            db_ref[0, h, 0] = db[h]
            dstate[h] = ds[h]
        don_ref[...] += don

    act = jax.ShapeDtypeStruct((bsz, s, gh * HD), F32)
    gshape = jax.ShapeDtypeStruct((bsz, gh, n, 1, c), F32)
    return pl.pallas_call(
        body, name="gdn_bwd", grid=(bsz, ng, n),
        in_specs=[qs(0), qs(ng), qs(2 * ng), qs(0), gs(0), gs(ng), on, st, qs(0)],
        out_specs=[qs(0), qs(0), qs(0), qs(0), gs(0), gs(0), on],
        out_shape=[act, act, act, jax.ShapeDtypeStruct(act.shape, BF16), gshape, gshape, jax.ShapeDtypeStruct((1, HD), F32)],
        scratch_shapes=[pltpu.VMEM((GDN_HP, HD, HD), F32)],
        compiler_params=_cp(("arbitrary", "arbitrary", "arbitrary")),
    )(qkv, qkv, qkv, z, gbt, gbt, onorm, states, dy)


def _gates_fn(ab, alog, dtb):
    lane = _iota2(ab.shape, 1)
    g = -jnp.exp(alog) * _softplus(ab + dtb)
    beta = _sigmoid(ab)
    return jnp.where(lane < CFG.gh, g, jnp.where(lane < 2 * CFG.gh, beta, 0.0))


def _heads_cumsum(xs, tri):
    n = xs[0].shape[0]
    y = _split_dot(jnp.concatenate(xs, axis=0), tri)
    return [y[h * n:(h + 1) * n] for h in range(len(xs))]


def _blk_off(jblk):
    return jblk * SB_BLK if isinstance(jblk, int) else pl.multiple_of(jblk * SB_BLK, SB_BLK)


def _sb_span(qs, k_spans, mask, runs, tri_su):
    nh = range(len(qs))
    nb = k_spans[0].shape[0] // SB_BLK
    zs = [lax.dot_general(qs[h], k_spans[h], _dims("nt"), preferred_element_type=F32) for h in nh]
    l1p = [jnp.log(1.0 + jnp.exp(-jnp.abs(z))) for z in zs]
    lss = [jnp.minimum(zs[h], 0.0) - l1p[h] for h in nh]
    lfs = [lss[h] - zs[h] for h in nh]
    if mask is not None:
        lfs = [jnp.where(mask, lf, 0.0) for lf in lfs]
    units = [lfs[h][:, b * SB_BLK:(b + 1) * SB_BLK] for h in nh for b in range(nb)]
    cums = _heads_cumsum(units, tri_su)
    sfx, new_runs = [], []
    for h in nh:
        run, parts = runs[h], [None] * nb
        for b in reversed(range(nb)):
            parts[b] = cums[h * nb + b] + run
            run = run + jnp.sum(units[h * nb + b], axis=1, keepdims=True)
        sfx.append(jnp.concatenate(parts, axis=1) if nb > 1 else parts[0])
        new_runs.append(run)
    ws = [jnp.exp(lss[h] + sfx[h]) for h in nh]
    if mask is not None:
        ws = [jnp.where(mask, w, 0.0) for w in ws]
    return zs, lfs, ws, new_runs


def _sb_specs(s, w):
    def qb(off):
        return pl.BlockSpec((1, SB_BLK, w), lambda b, h, i: (b, i, h + off))

    def full(off):
        return pl.BlockSpec((1, s, w), lambda b, h, i: (b, 0, h + off))

    return qb, full


def _sb_fwd(qkv, shard):
    bsz, s, _ = qkv.shape
    ng = CFG.sbh // SB_HP
    w = SB_HP * HD
    scale = HD ** -0.5
    qb, full = _sb_specs(s, w)

    def body(q_ref, k_ref, v_ref, x_ref, o_ref, g_ref, send_sems, recv_sems, local_sem):
        i = pl.program_id(2)
        begin, relay, finish = _gather_stages(x_ref, g_ref, send_sems, recv_sems, local_sem)
        start_of_group = (pl.program_id(1) == 0) & (i == 0)
        pl.when((pl.program_id(0) == 0) & start_of_group)(begin)
        pl.when((pl.program_id(0) == bsz // 2) & start_of_group)(relay)

        r, c = _iota2((SB_BLK, SB_BLK), 0), _iota2((SB_BLK, SB_BLK), 1)
        tri_su = (r > c).astype(BF16)
        nh = range(SB_HP)
        hs = [slice(h * HD, (h + 1) * HD) for h in nh]
        qs = [(q_ref[0, :, hs[h]] * scale).astype(BF16) for h in nh]

        def span(off, nb, mask, carry):
            ks = [k_ref[0, pl.ds(off, nb * SB_BLK), hs[h]].astype(BF16) for h in nh]
            vs = [v_ref[0, pl.ds(off, nb * SB_BLK), hs[h]].astype(BF16) for h in nh]
            _, _, ws, runs = _sb_span(qs, ks, mask, [cr[1] for cr in carry], tri_su)
            pv = [lax.dot_general(ws[h].astype(BF16), vs[h], _dims("nn"), preferred_element_type=F32) for h in nh]
            return tuple((carry[h][0] + pv[h], runs[h]) for h in nh)

        carry = tuple((jnp.zeros((SB_BLK, HD), F32), jnp.zeros((SB_BLK, 1), F32)) for _ in nh)
        carry = span(_blk_off(i), 1, c < r, carry)
        rem = jnp.bitwise_and(i, 3)
        carry = lax.fori_loop(0, lax.shift_right_logical(i, 2),
                              lambda p, cr: span(_blk_off(i - 4 - 4 * p), 4, None, cr), carry)
        carry = lax.fori_loop(0, lax.shift_right_logical(rem, 1),
                              lambda _, cr: span(_blk_off(jnp.bitwise_and(rem, 1)), 2, None, cr), carry)
        carry = lax.fori_loop(0, jnp.bitwise_and(rem, 1), lambda _, cr: span(0, 1, None, cr), carry)
        for h in nh:
            o_ref[0, :, hs[h]] = carry[h][0].astype(o_ref.dtype)

        pl.when((pl.program_id(0) == bsz - 1) & (pl.program_id(1) == ng - 1) & (i == nblk - 1))(finish)

    nblk = s // SB_BLK
    hbm = pl.BlockSpec(memory_space=pl.ANY)
    return pl.pallas_call(
        body, name="sb_fwd", grid=(bsz, ng, nblk),
        in_specs=[qb(0), full(ng), full(2 * ng), hbm], out_specs=[qb(0), hbm],
        out_shape=[jax.ShapeDtypeStruct((bsz, s, CFG.sbh * HD), BF16),
                   jax.ShapeDtypeStruct((N_DEV,) + shard.shape, shard.dtype)],
        scratch_shapes=list(_GATHER_SEMS),
        compiler_params=_cp(("arbitrary", "arbitrary", "arbitrary")),
    )(qkv, qkv, qkv, shard)


def _sb_bwd(qkv, do, s1):
    bsz, s, _ = qkv.shape
    ng = CFG.sbh // SB_HP
    w = SB_HP * HD
    nblk = s // SB_BLK
    scale = HD ** -0.5
    qb, full = _sb_specs(s, w)

    def body(q_ref, k_ref, v_ref, do_ref, s1_ref, dq_ref, dk_ref, dv_ref, r2_ref, dk_acc, dv_acc, dl_pan, z_pan,
             send_sems, recv_sems):
        i = pl.program_id(2)
        copies = _chip_copies(s1_ref, r2_ref, send_sems, recv_sems)

        @pl.when((pl.program_id(0) == 0) & (pl.program_id(1) == 0) & (i == 0))
        def _():
            for cp in copies:
                cp.start()

        @pl.when(i == 0)
        def _():
            dk_acc[...] = jnp.zeros_like(dk_acc)
            dv_acc[...] = jnp.zeros_like(dv_acc)

        r, c = _iota2((SB_BLK, SB_BLK), 0), _iota2((SB_BLK, SB_BLK), 1)
        tri_su = (r > c).astype(BF16)
        tri_pre = (r < c).astype(BF16)
        nh = range(SB_HP)
        hs = [slice(h * HD, (h + 1) * HD) for h in nh]
        qs = [(q_ref[0, :, hs[h]] * scale).astype(BF16) for h in nh]
        dob = [do_ref[0, :, hs[h]].astype(BF16) for h in nh]
        quads = lax.shift_right_logical(i, 2)
        rem = jnp.bitwise_and(i, 3)
        pair = lax.shift_right_logical(rem, 1)
        odd = jnp.bitwise_and(rem, 1)

        def span_a(jblk, nb, mask, runs):
            rows = pl.ds(_blk_off(jblk), nb * SB_BLK)
            ks = [k_ref[0, rows, hs[h]].astype(BF16) for h in nh]
            vs = [v_ref[0, rows, hs[h]].astype(BF16) for h in nh]
            dws = [lax.dot_general(dob[h], vs[h], _dims("nt"), preferred_element_type=F32) for h in nh]
            zs, _, ws, runs = _sb_span(qs, ks, mask, runs, tri_su)
            dvs = [lax.dot_general(ws[h].astype(BF16), dob[h], _dims("tn"), preferred_element_type=F32) for h in nh]
            for h in nh:
                dl = dws[h] * ws[h]
                for b in range(nb):
                    dl_pan[h, jblk + b] = dl[:, b * SB_BLK:(b + 1) * SB_BLK]
                    z_pan[h, jblk + b] = zs[h][:, b * SB_BLK:(b + 1) * SB_BLK]
                dv_acc[rows, hs[h]] += dvs[h]
            return tuple(runs)

        runs = tuple(jnp.zeros((SB_BLK, 1), F32) for _ in nh)
        runs = span_a(i, 1, c < r, runs)
        runs = lax.fori_loop(0, quads, lambda p, rn: span_a(i - 4 - 4 * p, 4, None, rn), runs)
        runs = lax.fori_loop(0, pair, lambda _, rn: span_a(odd, 2, None, rn), runs)
        lax.fori_loop(0, odd, lambda _, rn: span_a(0, 1, None, rn), runs)

        def span_b(jblk, nb, mask, carry):
            rows = pl.ds(_blk_off(jblk), nb * SB_BLK)
            ks = [k_ref[0, rows, hs[h]].astype(BF16) for h in nh]
            units = [dl_pan[h, jblk + b] for h in nh for b in range(nb)]
            sgs = [_sigmoid(z_pan[h, jblk + b]) for h in nh for b in range(nb)]
            cums = _heads_cumsum(units, tri_pre)
            dzs, pres = [], []
            for h in nh:
                pre, parts = carry[h][1], []
                for b in range(nb):
                    u, sg = units[h * nb + b], sgs[h * nb + b]
                    parts.append(u * (1.0 - sg) - sg * (cums[h * nb + b] + pre))
                    pre = pre + jnp.sum(u, axis=1, keepdims=True)
                dz = jnp.concatenate(parts, axis=1) if nb > 1 else parts[0]
                if mask is not None:
                    dz = jnp.where(mask, dz, 0.0)
                dzs.append(dz.astype(BF16))
                pres.append(pre)
            dqs = [lax.dot_general(dzs[h], ks[h], _dims("nn"), preferred_element_type=F32) for h in nh]
            dks = [lax.dot_general(dzs[h], qs[h], _dims("tn"), preferred_element_type=F32) for h in nh]
            for h in nh:
                dk_acc[rows, hs[h]] += dks[h]
            return tuple((carry[h][0] + dqs[h], pres[h]) for h in nh)

        carry = tuple((jnp.zeros((SB_BLK, HD), F32), jnp.zeros((SB_BLK, 1), F32)) for _ in nh)
        carry = lax.fori_loop(0, odd, lambda _, cr: span_b(0, 1, None, cr), carry)
        carry = lax.fori_loop(0, pair, lambda _, cr: span_b(odd, 2, None, cr), carry)
        carry = lax.fori_loop(0, quads, lambda p, cr: span_b(rem + 4 * p, 4, None, cr), carry)
        carry = span_b(i, 1, c < r, carry)
        for h in nh:
            dq_ref[0, :, hs[h]] = (carry[h][0] * scale).astype(dq_ref.dtype)

        @pl.when(i == nblk - 1)
        def _():
            dk_ref[0] = dk_acc[...].astype(dk_ref.dtype)
            dv_ref[0] = dv_acc[...].astype(dv_ref.dtype)

        @pl.when((pl.program_id(0) == bsz - 1) & (pl.program_id(1) == ng - 1) & (i == nblk - 1))
        def _():
            for cp in copies:
                cp.wait_recv()
            for cp in copies:
                cp.wait_send()

    out = jax.ShapeDtypeStruct((bsz, s, CFG.sbh * HD), BF16)
    hbm = pl.BlockSpec(memory_space=pl.ANY)
    return pl.pallas_call(
        body, name="sb_bwd", grid=(bsz, ng, nblk),
        in_specs=[qb(0), full(ng), full(2 * ng), qb(0), hbm],
        out_specs=[qb(0), full(0), full(0), hbm],
        out_shape=[out, out, out, jax.ShapeDtypeStruct((3,) + s1.shape[1:], s1.dtype)],
        scratch_shapes=[pltpu.VMEM((s, w), F32), pltpu.VMEM((s, w), F32),
                        pltpu.VMEM((SB_HP, nblk, SB_BLK, SB_BLK), F32), pltpu.VMEM((SB_HP, nblk, SB_BLK, SB_BLK), F32),
                        pltpu.SemaphoreType.DMA((3,)), pltpu.SemaphoreType.DMA((3,))],
        compiler_params=_cp(("arbitrary", "arbitrary", "arbitrary")),
    )(qkv, qkv, qkv, do, s1)


def _xattn_fn(q_raw, kv, qn, kn):
    d = q_raw.shape[1]
    dh = d // CFG.xh
    outs = []
    for h in range(CFG.xh):
        qh = _rms(q_raw[:, h * dh:(h + 1) * dh], qn)
        kh = _rms(kv[:, h * dh:(h + 1) * dh], kn)
        vh = kv[:, d + h * dh:d + (h + 1) * dh]
        sc = _bdot(qh, kh, "nt") * (dh ** -0.5)
        sc = sc - lax.stop_gradient(jnp.max(sc, axis=-1, keepdims=True))
        e = jnp.exp(sc)
        p = e / jnp.sum(e, axis=-1, keepdims=True)
        outs.append(_bdot(p, vh, "nn"))
    return jnp.concatenate(outs, axis=1)


def _xattn_fwd(q_raw, kv, qn, kn):
    bsz, s, d = q_raw.shape
    m = kv.shape[1]
    tq = _tile(s, (XQ_TILE, 128))

    def body(q_ref, kv_ref, qn_ref, kn_ref, o_ref):
        o_ref[0] = _xattn_fn(q_ref[0], kv_ref[0], qn_ref[...], kn_ref[...]).astype(o_ref.dtype)

    return pl.pallas_call(
        body, name="xattn_fwd", grid=(bsz, s // tq),
        in_specs=[pl.BlockSpec((1, tq, d), lambda b, i: (b, i, 0)), pl.BlockSpec((1, m, 2 * d), lambda b, i: (b, 0, 0)),
                  pl.BlockSpec(qn.shape, lambda b, i: (0, 0)), pl.BlockSpec(kn.shape, lambda b, i: (0, 0))],
        out_specs=pl.BlockSpec((1, tq, d), lambda b, i: (b, i, 0)),
        out_shape=jax.ShapeDtypeStruct((bsz, s, d), BF16),
        compiler_params=_cp(("parallel", "parallel")),
    )(q_raw, kv, qn, kn)


def _xattn_bwd(q_raw, kv, qn, kn, do):
    bsz, s, d = q_raw.shape
    m = kv.shape[1]
    tq = _tile(s, (XQ_TILE, 128))

    def body(q_ref, kv_ref, qn_ref, kn_ref, do_ref, dq_ref, dkv_ref, dqn_ref, dkn_ref):
        b, i = pl.program_id(0), pl.program_id(1)

        @pl.when((b == 0) & (i == 0))
        def _():
            dqn_ref[...] = jnp.zeros_like(dqn_ref)
            dkn_ref[...] = jnp.zeros_like(dkn_ref)

        @pl.when(i == 0)
        def _():
            dkv_ref[...] = jnp.zeros_like(dkv_ref)

        _, f = jax.vjp(_xattn_fn, q_ref[0], kv_ref[0], qn_ref[...], kn_ref[...])
        dq, dkv, dqn, dkn = f(do_ref[0].astype(F32))
        dq_ref[0] = dq.astype(dq_ref.dtype)
        dkv_ref[0] += dkv
        dqn_ref[...] += dqn
        dkn_ref[...] += dkn

    return pl.pallas_call(
        body, name="xattn_bwd", grid=(bsz, s // tq),
        in_specs=[pl.BlockSpec((1, tq, d), lambda b, i: (b, i, 0)), pl.BlockSpec((1, m, 2 * d), lambda b, i: (b, 0, 0)),
                  pl.BlockSpec(qn.shape, lambda b, i: (0, 0)), pl.BlockSpec(kn.shape, lambda b, i: (0, 0)),
                  pl.BlockSpec((1, tq, d), lambda b, i: (b, i, 0))],
        out_specs=[pl.BlockSpec((1, tq, d), lambda b, i: (b, i, 0)), pl.BlockSpec((1, m, 2 * d), lambda b, i: (b, 0, 0)),
                   pl.BlockSpec(qn.shape, lambda b, i: (0, 0)), pl.BlockSpec(kn.shape, lambda b, i: (0, 0))],
        out_shape=[jax.ShapeDtypeStruct((bsz, s, d), BF16), jax.ShapeDtypeStruct(kv.shape, F32),
                   jax.ShapeDtypeStruct(qn.shape, F32), jax.ShapeDtypeStruct(kn.shape, F32)],
        compiler_params=_cp(("arbitrary", "arbitrary")),
    )(q_raw, kv, qn, kn, do)


def _my_pos():
    return lax.axis_index("x"), lax.axis_index("y"), lax.axis_index("c")


def _gather_stages(x_ref, out_ref, send_sems, recv_sems, local_sem):
    x, y, c = _my_pos()
    me, sibling = (x, y, c), (x, y, 1 - c)
    chips = [(1 - x, y), (x, 1 - y), (1 - x, 1 - y)]

    def slot(px, py, pc):
        return out_ref.at[4 * px + 2 * py + pc]

    def copy(k, block, to, src=None):
        return pltpu.make_async_remote_copy(
            src_ref=slot(*block) if src is None else src, dst_ref=slot(*block),
            send_sem=send_sems.at[k], recv_sem=recv_sems.at[k], device_id=to, device_id_type=MESH)

    mine = pltpu.make_async_copy(x_ref, slot(*me), local_sem)
    first = [copy(0, me, sibling, src=x_ref)]
    first += [copy(1 + j, me, (*chip, c), src=x_ref) for j, chip in enumerate(chips)]
    passed = [copy(4 + j, (*chip, c), sibling) for j, chip in enumerate(chips)]

    def begin():
        mine.start()
        for cp in first:
            cp.start()

    def relay():
        for j, chip in enumerate(chips):
            copy(1 + j, (*chip, c), me).wait_recv()
            passed[j].start()

    def finish():
        copy(0, sibling, me).wait_recv()
        for j, chip in enumerate(chips):
            copy(4 + j, (*chip, 1 - c), me).wait_recv()
        for cp in first + passed:
            cp.wait_send()
        mine.wait()

    return begin, relay, finish


_GATHER_SEMS = [pltpu.SemaphoreType.DMA((7,)), pltpu.SemaphoreType.DMA((7,)), pltpu.SemaphoreType.DMA]


def _all_gather_big(shard, name):
    r, d = shard.shape

    def body(x_ref, out_ref, send_sems, recv_sems, local_sem):
        begin, relay, finish = _gather_stages(x_ref, out_ref, send_sems, recv_sems, local_sem)
        begin()
        relay()
        finish()

    return pl.pallas_call(
        body, name=name,
        out_shape=jax.ShapeDtypeStruct((N_DEV, r, d), shard.dtype),
        in_specs=[pl.BlockSpec(memory_space=pl.ANY)], out_specs=pl.BlockSpec(memory_space=pl.ANY),
        scratch_shapes=list(_GATHER_SEMS),
    )(shard)


def _exchange_sibling(g, name):
    _, r, d = g.shape

    def body(g_ref, out_ref, send_sems, recv_sems):
        x, y, c = _my_pos()
        copies = [pltpu.make_async_remote_copy(
            src_ref=g_ref.at[2 * k + (1 - c)], dst_ref=out_ref.at[k],
            send_sem=send_sems.at[k], recv_sem=recv_sems.at[k], device_id=(x, y, 1 - c), device_id_type=MESH)
            for k in range(4)]
        for cp in copies:
            cp.start()
        for cp in copies:
            cp.wait_recv()
        for cp in copies:
            cp.wait_send()

    return pl.pallas_call(
        body, name=name,
        out_shape=jax.ShapeDtypeStruct((4, r, d), g.dtype),
        in_specs=[pl.BlockSpec(memory_space=pl.ANY)], out_specs=pl.BlockSpec(memory_space=pl.ANY),
        scratch_shapes=[pltpu.SemaphoreType.DMA((4,)), pltpu.SemaphoreType.DMA((4,))],
    )(g)


def _chip_copies(s_ref, out_ref, send_sems, recv_sems):
    x, y, c = _my_pos()
    copies = []
    for rel in (1, 2, 3):
        px = jnp.bitwise_xor(x, rel >> 1)
        py = jnp.bitwise_xor(y, rel & 1)
        copies.append(pltpu.make_async_remote_copy(
            src_ref=s_ref.at[2 * px + py], dst_ref=out_ref.at[rel - 1],
            send_sem=send_sems.at[rel - 1], recv_sem=recv_sems.at[rel - 1],
            device_id=(px, py, c), device_id_type=MESH))
    return copies


def _all_reduce_small(blk, name):
    rows, d = blk.shape

    def body(x_ref, out_ref, land, send_sems, recv_sems):
        x, y, c = _my_pos()
        me = 4 * x + 2 * y + c
        copies = []
        for rel in range(1, N_DEV):
            peer = (jnp.bitwise_xor(x, rel >> 2), jnp.bitwise_xor(y, (rel >> 1) & 1), jnp.bitwise_xor(c, rel & 1))
            copies.append(pltpu.make_async_remote_copy(
                src_ref=x_ref, dst_ref=land.at[rel - 1], send_sem=send_sems.at[rel - 1], recv_sem=recv_sems.at[rel - 1],
                device_id=peer, device_id_type=MESH))
        for cp in copies:
            cp.start()
        for cp in copies:
            cp.wait_recv()
        acc = jnp.zeros((rows, d), F32)
        for dev in range(N_DEV):
            rel = jnp.bitwise_xor(me, dev)
            got = land[jnp.maximum(rel - 1, 0)]
            acc = acc + jnp.where(rel == 0, x_ref[...], got)
        out_ref[...] = acc
        for cp in copies:
            cp.wait_send()

    return pl.pallas_call(
        body, name=name,
        out_shape=jax.ShapeDtypeStruct((rows, d), F32),
        in_specs=[pl.BlockSpec(memory_space=pltpu.VMEM)], out_specs=pl.BlockSpec(memory_space=pltpu.VMEM),
        scratch_shapes=[pltpu.VMEM((N_DEV - 1, rows, d), F32), pltpu.SemaphoreType.DMA((N_DEV - 1,)),
                        pltpu.SemaphoreType.DMA((N_DEV - 1,))],
    )(blk)


def _dxn_fused(parts, s1):
    t, d = parts[0][0].shape[0], parts[0][1].shape[1]
    tm = _tile(t, MM_TILES)
    nm = t // tm
    segs, k0 = [], 0
    for a, _ in parts:
        tk = _tile(a.shape[1], (512, 256, 128))
        segs.append((k0, a.shape[1] // tk, tk))
        k0 += a.shape[1] // tk
    ktot = k0
    npart = len(parts)

    def body(*refs):
        ab = refs[:2 * npart]
        s1_ref, o_ref, r2_ref, acc, send_sems, recv_sems = refs[2 * npart:]
        i, k = pl.program_id(0), pl.program_id(1)
        copies = _chip_copies(s1_ref, r2_ref, send_sems, recv_sems)

        @pl.when((i == 0) & (k == 0))
        def _():
            for cp in copies:
                cp.start()

        @pl.when(k == 0)
        def _():
            acc[...] = jnp.zeros_like(acc)

        for p, (p0, nk, _) in enumerate(segs):
            @pl.when((k >= p0) & (k < p0 + nk))
            def _(p=p):
                acc[...] += lax.dot_general(ab[2 * p][...], ab[2 * p + 1][...], _dims("nn"), preferred_element_type=F32)

        @pl.when(k == ktot - 1)
        def _():
            o_ref[...] = acc[...]

        @pl.when((i == nm - 1) & (k == ktot - 1))
        def _():
            for cp in copies:
                cp.wait_recv()
            for cp in copies:
                cp.wait_send()

    in_specs, args = [], []
    for (a, b), (p0, nk, tk) in zip(parts, segs):
        def kk(k, p0=p0, nk=nk):
            return jnp.clip(k - p0, 0, nk - 1)
        in_specs.append(pl.BlockSpec((tm, tk), lambda i, k, kk=kk: (i, kk(k))))
        in_specs.append(pl.BlockSpec((tk, d), lambda i, k, kk=kk: (kk(k), 0)))
        args += [a, b]
    hbm = pl.BlockSpec(memory_space=pl.ANY)
    return pl.pallas_call(
        body, name="d_xn", grid=(nm, ktot),
        in_specs=in_specs + [hbm], out_specs=[pl.BlockSpec((tm, d), lambda i, k: (i, 0)), hbm],
        out_shape=[jax.ShapeDtypeStruct((t, d), F32), jax.ShapeDtypeStruct((3,) + s1.shape[1:], s1.dtype)],
        scratch_shapes=[pltpu.VMEM((tm, d), F32), pltpu.SemaphoreType.DMA((3,)), pltpu.SemaphoreType.DMA((3,))],
        compiler_params=_cp(("arbitrary", "arbitrary")),
    )(*args, s1)


def _cast_rows(x, dtype, name):
    return _rowwise(lambda v: v, [x], [], [(x.shape[1], dtype)], [], name=name, tm=CFG.pack_tile)[0]


def _sum_sibling(g, recv1, c_idx, name):
    _, r, d = g.shape
    tm = CFG.pack_tile

    def body(c_ref, g_ref, r_ref, o_ref):
        o_ref[0] = (g_ref[0] + r_ref[0]).astype(o_ref.dtype)

    grid_spec = pltpu.PrefetchScalarGridSpec(
        num_scalar_prefetch=1, grid=(4, r // tm),
        in_specs=[pl.BlockSpec((1, tm, d), lambda k, i, c_ref: (2 * k + c_ref[0], i, 0)),
                  pl.BlockSpec((1, tm, d), lambda k, i, c_ref: (k, i, 0))],
        out_specs=pl.BlockSpec((1, tm, d), lambda k, i, c_ref: (k, i, 0)))
    return pl.pallas_call(
        body, name=name, grid_spec=grid_spec,
        out_shape=jax.ShapeDtypeStruct((4, r, d), BF16),
        compiler_params=_cp(("parallel", "parallel")),
    )(c_idx, g, recv1)


def _adamw_math(w, g, m, v):
    m2 = ADAM_B1 * m + (1.0 - ADAM_B1) * g
    v2 = ADAM_B2 * v + (1.0 - ADAM_B2) * (g * g)
    m_hat = m2 / (1.0 - ADAM_B1 ** ADAM_STEP)
    v_hat = v2 / (1.0 - ADAM_B2 ** ADAM_STEP)
    delta = -ADAM_LR * (m_hat / (jnp.sqrt(v_hat) + ADAM_EPS) + ADAM_WD * w)
    return delta, m2, v2


def _adamw_big(g, recv1, recv2, w, m, v, idx, row0, name):
    _, r, d = g.shape
    tm = CFG.pack_tile
    t0 = row0 // tm

    def body(idx_ref, g_ref, r1_ref, ra_ref, rb_ref, rc_ref, w_ref, m_ref, v_ref, og, od, om, ov):
        grad = (g_ref[0] + r1_ref[0]) + ra_ref[0].astype(F32) + rb_ref[0].astype(F32) + rc_ref[0].astype(F32)
        delta, m2, v2 = _adamw_math(w_ref[...], grad, m_ref[...], v_ref[...])
        og[...] = grad
        od[...] = delta
        om[...] = m2
        ov[...] = v2

    flat = pl.BlockSpec((tm, d), lambda i, idx_ref: (i, 0))
    shifted = pl.BlockSpec((tm, d), lambda i, idx_ref: (i + t0, 0))
    grid_spec = pltpu.PrefetchScalarGridSpec(
        num_scalar_prefetch=1, grid=(r // tm,),
        in_specs=[pl.BlockSpec((1, tm, d), lambda i, idx_ref: (idx_ref[0], i, 0)),
                  pl.BlockSpec((1, tm, d), lambda i, idx_ref: (idx_ref[1], i, 0)),
                  pl.BlockSpec((1, tm, d), lambda i, idx_ref: (0, i, 0)),
                  pl.BlockSpec((1, tm, d), lambda i, idx_ref: (1, i, 0)),
                  pl.BlockSpec((1, tm, d), lambda i, idx_ref: (2, i, 0)),
                  shifted, shifted, shifted],
        out_specs=[flat, flat, flat, flat])
    shp = jax.ShapeDtypeStruct((r, d), F32)
    return pl.pallas_call(
        body, name=name, grid_spec=grid_spec, out_shape=[shp, shp, shp, shp],
        compiler_params=_cp(("parallel",)),
    )(idx, g, recv1, recv2, recv2, recv2, w, m, v)


def _rows_of(v, d):
    flat = v.reshape(-1)
    rows = -(-flat.shape[0] // d)
    rows += (-rows) % SUBLANE
    return jnp.pad(flat, (0, rows * d - flat.shape[0])).reshape(rows, d)


def _pad_rows(a, mult):
    pad = (-a.shape[0]) % mult
    if pad:
        a = jnp.pad(a, ((0, pad),) + ((0, 0),) * (a.ndim - 1))
    return a


_BIG = ("w_in", "w_xkv", "w_up", "w_proj_gdn", "w_proj_sb", "w_out", "w_xq", "w_xo", "w_down")
_COL_SHARDED = ("w_in", "w_xkv", "w_up")
_SMALL_REP = ("norm_mix", "norm_x", "norm_mem", "norm_ffn", "a_log", "dt_bias", "gdn_out_norm", "xq_norm", "xk_norm")
_SMALL_CONV = ("conv_gdn", "conv_ffn")


def _part_rows(shapes):
    out = []
    for n in _BIG:
        rows, cols = shapes[n]
        cnt = cols if n in _COL_SHARDED else rows
        out.append((cnt, cnt + (-cnt) % (CFG.pack_tile if n == _BIG[0] else PACK_ROW_ALIGN)))
    return out


def _pack_big_shards(shards, shapes):
    parts = []
    for n, (_, padded) in zip(_BIG, _part_rows(shapes)):
        parts.append(_pad_rows(shards[n].T if n in _COL_SHARDED else shards[n], padded))
    return _pad_rows(jnp.concatenate(parts, axis=0), CFG.pack_tile)


def _unpack_gathered(gath, shapes, names):
    out, r0 = {}, 0
    for n, (cnt, padded) in zip(_BIG, _part_rows(shapes)):
        if n not in names:
            continue
        out[n] = gath[:, r0:r0 + cnt, :].reshape(N_DEV * cnt, gath.shape[2])
        r0 += padded
    return out


def _pack_full_grads(grads, shapes, names):
    d = CFG.d
    parts = []
    for n, (cnt, padded) in zip(_BIG, _part_rows(shapes)):
        if n not in names:
            continue
        g = grads[n].reshape(N_DEV, cnt, d)
        if padded > cnt:
            g = jnp.pad(g, ((0, 0), (0, padded - cnt), (0, 0)))
        parts.append(g)
    full = jnp.concatenate(parts, axis=1)
    pad = (-full.shape[1]) % CFG.pack_tile
    if pad:
        full = jnp.pad(full, ((0, 0), (0, pad), (0, 0)))
    return full


def _unpack_shard(packed, shapes, names):
    out, r0 = {}, 0
    for n, (cnt, padded) in zip(_BIG, _part_rows(shapes)):
        if n not in names:
            continue
        part = packed[r0:r0 + cnt]
        out[n] = (part.T if n in _COL_SHARDED else part).reshape((1,) + tuple(shapes[n]))
        r0 += padded
    return out


def kernel(x, mem, norm_mix, w_in, conv_gdn, a_log, dt_bias, gdn_out_norm, w_proj_gdn, w_proj_sb, w_out, norm_x, norm_mem, w_xq, w_xkv, xq_norm, xk_norm, w_xo, norm_ffn, w_up, conv_ffn, w_down, loss_target, m_norm_mix, m_w_in, m_conv_gdn, m_a_log, m_dt_bias, m_gdn_out_norm, m_w_proj_gdn, m_w_proj_sb, m_w_out, m_norm_x, m_norm_mem, m_w_xq, m_w_xkv, m_xq_norm, m_xk_norm, m_w_xo, m_norm_ffn, m_w_up, m_conv_ffn, m_w_down, v_norm_mix, v_w_in, v_conv_gdn, v_a_log, v_dt_bias, v_gdn_out_norm, v_w_proj_gdn, v_w_proj_sb, v_w_out, v_norm_x, v_norm_mem, v_w_xq, v_w_xkv, v_xq_norm, v_xk_norm, v_w_xo, v_norm_ffn, v_w_up, v_conv_ffn, v_w_down):
    names = ("norm_mix", "w_in", "conv_gdn", "a_log", "dt_bias", "gdn_out_norm", "w_proj_gdn", "w_proj_sb", "w_out",
             "norm_x", "norm_mem", "w_xq", "w_xkv", "xq_norm", "xk_norm", "w_xo", "norm_ffn", "w_up", "conv_ffn", "w_down")
    wts = dict(zip(names, (norm_mix, w_in, conv_gdn, a_log, dt_bias, gdn_out_norm, w_proj_gdn, w_proj_sb, w_out,
                           norm_x, norm_mem, w_xq, w_xkv, xq_norm, xk_norm, w_xo, norm_ffn, w_up, conv_ffn, w_down)))
    mom = dict(zip(names, (m_norm_mix, m_w_in, m_conv_gdn, m_a_log, m_dt_bias, m_gdn_out_norm, m_w_proj_gdn, m_w_proj_sb,
                           m_w_out, m_norm_x, m_norm_mem, m_w_xq, m_w_xkv, m_xq_norm, m_xk_norm, m_w_xo, m_norm_ffn, m_w_up,
                           m_conv_ffn, m_w_down)))
    vel = dict(zip(names, (v_norm_mix, v_w_in, v_conv_gdn, v_a_log, v_dt_bias, v_gdn_out_norm, v_w_proj_gdn, v_w_proj_sb,
                           v_w_out, v_norm_x, v_norm_mem, v_w_xq, v_w_xkv, v_xq_norm, v_xk_norm, v_w_xo, v_norm_ffn, v_w_up,
                           v_conv_ffn, v_w_down)))
    cfg = CFG
    d, bsz, s = cfg.d, cfg.b, cfg.s
    t = bsz * s
    gh, sbh = cfg.gh, cfg.sbh
    gw, sw = gh * HD, sbh * HD
    nchunk = s // cfg.gch
    mx, my, mc = _my_pos()
    me = 4 * mx + 2 * my + mc

    shard_shapes = {n: tuple(wts[n].shape[1:]) for n in _BIG}

    packed_w = _pack_big_shards({n: wts[n][0] for n in _BIG}, shard_shapes)
    packed_wb = _cast_rows(packed_w, BF16, "cast_weights")
    rows_in = _part_rows(shard_shapes)[0][1]
    full = _unpack_gathered(_all_gather_big(packed_wb[:rows_in], "all_gather_w_in"), shard_shapes, _BIG[:1])

    conv_rows = {n: _rows_of(wts[n][0], d) for n in _SMALL_CONV}
    conv_cnt = {n: conv_rows[n].shape[0] for n in _SMALL_CONV}
    conv_blk = _pad_rows(jnp.concatenate([conv_rows[n] for n in _SMALL_CONV], axis=0), SUBLANE)
    conv_all = jnp.zeros((N_DEV,) + conv_blk.shape, F32)
    conv_all = lax.dynamic_update_slice(conv_all, conv_blk[None], (me, 0, 0))
    conv_all = _all_reduce_small(conv_all.reshape(-1, d), "gather_conv_taps").reshape((N_DEV,) + conv_blk.shape)

    def full_conv(n, r0):
        k, cols = wts[n].shape[1], wts[n].shape[2]
        part = conv_all[:, r0:r0 + conv_cnt[n], :].reshape(N_DEV, -1)[:, :k * cols].reshape(N_DEV, k, cols)
        return part.transpose(1, 0, 2).reshape(k, N_DEV * cols)

    cgdn = full_conv("conv_gdn", 0)
    cffn = full_conv("conv_ffn", conv_cnt["conv_gdn"])

    win = full["w_in"]
    o_ab = 3 * gw
    o_z = o_ab + 2 * gh
    o_sb = o_z + gw
    o_gate = o_sb + 3 * sw
    w_qkv = win[:o_ab]
    w_ab = jnp.concatenate([win[o_ab:o_z], jnp.zeros((LANE - 2 * gh, d), win.dtype)], axis=0)
    w_z = win[o_z:o_sb]
    w_sb = win[o_sb:o_gate]
    w_gate = win[o_gate:]

    alog_p = jnp.concatenate([a_log.reshape(1, -1), jnp.zeros((1, LANE - gh), F32)], axis=1)
    dtb_p = jnp.concatenate([dt_bias.reshape(1, -1), jnp.zeros((1, LANE - gh), F32)], axis=1)
    onorm = gdn_out_norm.reshape(1, HD)

    x2 = x.reshape(t, d)
    tgt2 = loss_target.reshape(t, d)
    mem2 = mem.reshape(bsz * cfg.mem, d)

    (xn,) = _rowwise(_rms, [x2], [norm_mix], [(d, BF16)], [], name="norm_mix_fwd")
    p_qkv = _mm(xn, w_qkv, tb=True, name="proj_qkv")
    p_ab = _mm(xn, w_ab, tb=True, name="proj_ab")
    p_z = _mm(xn, w_z, tb=True, name="proj_z")
    p_sb = _mm(xn, w_sb, tb=True, name="proj_sb")
    p_gate = _mm(xn, w_gate, tb=True, name="proj_gate")

    qkv_c = _gdn_conv_fwd(p_qkv.reshape(bsz, s, 3 * gw), cgdn)
    (gb,) = _rowwise(_gates_fn, [p_ab], [alog_p, dtb_p], [(LANE, F32)], [], name="gdn_gates_fwd")
    gbt = gb.reshape(bsz, s, LANE)[:, :, :2 * gh].transpose(0, 2, 1).reshape(bsz, 2 * gh, nchunk, 1, cfg.gch)
    o_a, states = _gdn_fwd(qkv_c, p_z.reshape(bsz, s, gw), gbt, onorm)
    o_b, gathered = _sb_fwd(p_sb.reshape(bsz, s, 3 * sw), packed_wb[rows_in:])
    full.update(_unpack_gathered(gathered, shard_shapes, _BIG[1:]))

    pa = _mm(o_a.reshape(t, gw), full["w_proj_gdn"], name="proj_gdn_out")
    pb = _mm(o_b.reshape(t, sw), full["w_proj_sb"], name="proj_sb_out")

    def merge_fn(pa_, pb_, gate_):
        return _sigmoid(gate_[:, :d]) * pa_ + _sigmoid(gate_[:, d:]) * pb_

    (merged,) = _rowwise(merge_fn, [pa, pb, p_gate], [], [(d, BF16)], [], name="merge_fwd")
    h1 = _mm(merged, full["w_out"], add=x2, name="mixer_out")

    (hn_x,) = _rowwise(_rms, [h1], [norm_x], [(d, BF16)], [], name="norm_x_fwd")
    (mn,) = _rowwise(_rms, [mem2], [norm_mem], [(d, BF16)], [], name="norm_mem_fwd")
    q_raw = _mm(hn_x, full["w_xq"], name="xattn_q")
    kv = _mm(mn, full["w_xkv"], tb=True, name="xattn_kv")
    xo = _xattn_fwd(q_raw.reshape(bsz, s, d), kv.reshape(bsz, cfg.mem, 2 * d), xq_norm, xk_norm)
    h2 = _mm(xo.reshape(t, d), full["w_xo"], add=h1, name="xattn_out")

    (hn_f,) = _rowwise(_rms, [h2], [norm_ffn], [(d, BF16)], [], name="norm_ffn_fwd")
    up = _mm(hn_f, full["w_up"], tb=True, name="ffn_up")
    act = _ffn_conv_fwd(up.reshape(bsz, s, 2 * cfg.dff), cffn)
    y = _mm(act.reshape(t, cfg.dff), full["w_down"], add=h2, name="ffn_down")

    def loss_fn(y_, tg_):
        err = y_ - tg_
        part = 0.5 * jnp.sum(err * err) / d
        return err / d, err / d, jnp.full((1, LANE), part, F32)

    dy, dy_b, loss_part = _rowwise(loss_fn, [y, tgt2], [], [(d, F32), (d, BF16)], [(1, LANE)], name="loss")

    grads = {}
    dact = _mm(dy_b, full["w_down"], tb=True, name="d_act")
    grads["w_down"] = _mm(act.reshape(t, cfg.dff), dy_b, ta=True, name="dw_down")
    dup1, dup2, dcf1, dcf2 = _ffn_conv_bwd(up.reshape(bsz, s, 2 * cfg.dff), cffn, dact.reshape(bsz, s, cfg.dff))
    dup = jnp.concatenate([dup1, dup2], axis=2).reshape(t, 2 * cfg.dff)
    g_conv_ffn = jnp.concatenate([dcf1, dcf2], axis=1)
    dhn_f = _mm(dup, full["w_up"], name="d_hn_ffn")
    grads["w_up"] = _mm(dup, hn_f, ta=True, name="dw_up")

    def norm_bwd_fn(h_, res_, dn_, g_):
        _, f = jax.vjp(_rms, h_, g_)
        dh, dg = f(dn_)
        return res_ + dh, dg

    def norm_bwd_copy_fn(h_, res_, dn_, g_):
        dres, dg = norm_bwd_fn(h_, res_, dn_, g_)
        return dres, dres, dg

    dh2, dh2_b, g_norm_ffn = _rowwise(norm_bwd_copy_fn, [h2, dy, dhn_f], [norm_ffn], [(d, F32), (d, BF16)], [(1, d)],
                                      name="norm_ffn_bwd")

    dxo = _mm(dh2_b, full["w_xo"], tb=True, out_dtype=BF16, name="d_xo")
    grads["w_xo"] = _mm(xo.reshape(t, d), dh2_b, ta=True, name="dw_xo")
    dq_raw, dkv, g_xq_norm, g_xk_norm = _xattn_bwd(q_raw.reshape(bsz, s, d), kv.reshape(bsz, cfg.mem, 2 * d),
                                                   xq_norm, xk_norm, dxo.reshape(bsz, s, d))
    dq_raw2 = dq_raw.reshape(t, d)
    dkv2 = dkv.reshape(bsz * cfg.mem, 2 * d)
    dhn_x = _mm(dq_raw2, full["w_xq"], tb=True, name="d_hn_x")
    grads["w_xq"] = _mm(hn_x, dq_raw2, ta=True, name="dw_xq")
    dmn = _mm(dkv2, full["w_xkv"], name="d_mn")
    grads["w_xkv"] = _mm(dkv2, mn, ta=True, name="dw_xkv")

    def norm_w_bwd_fn(h_, dn_, g_):
        _, f = jax.vjp(lambda gg: _rms(h_, gg), g_)
        return f(dn_)[0]

    (g_norm_mem,) = _rowwise(norm_w_bwd_fn, [mem2, dmn], [norm_mem], [], [(1, d)], name="norm_mem_bwd")
    dh1, dh1_b, g_norm_x = _rowwise(norm_bwd_copy_fn, [h1, dh2, dhn_x], [norm_x], [(d, F32), (d, BF16)], [(1, d)],
                                    name="norm_x_bwd")

    dmerged = _mm(dh1_b, full["w_out"], tb=True, name="d_merged")
    grads["w_out"] = _mm(merged, dh1_b, ta=True, name="dw_out")

    def merge_bwd_fn(pa_, pb_, gate_, dm_):
        _, f = jax.vjp(merge_fn, pa_, pb_, gate_)
        return f(dm_)

    dpa, dpb, dgate = _rowwise(merge_bwd_fn, [pa, pb, p_gate, dmerged], [], [(d, BF16), (d, BF16), (2 * d, BF16)], [],
                               name="merge_bwd")
    do_a = _mm(dpa, full["w_proj_gdn"], tb=True, name="d_o_gdn")
    grads["w_proj_gdn"] = _mm(o_a.reshape(t, gw), dpa, ta=True, name="dw_proj_gdn")
    do_b = _mm(dpb, full["w_proj_sb"], tb=True, name="d_o_sb")
    grads["w_proj_sb"] = _mm(o_b.reshape(t, sw), dpb, ta=True, name="dw_proj_sb")

    c_idx = jnp.reshape(mc, (1,)).astype(jnp.int32)
    early = _BIG[1:]
    g_early = _pack_full_grads(grads, shard_shapes, early)
    r1_early = _exchange_sibling(g_early, "grads_to_sibling_early")
    s1_early = _sum_sibling(g_early, r1_early, c_idx, "sum_sibling_early")
    dsq, dsk, dsv, r2_early = _sb_bwd(p_sb.reshape(bsz, s, 3 * sw), do_b.reshape(bsz, s, sw), s1_early)
    dp_sb = jnp.concatenate([dsq, dsk, dsv], axis=2).reshape(t, 3 * sw)

    dgq, dgk, dgv, dz, dg, dbeta, g_onorm = _gdn_bwd(qkv_c, p_z.reshape(bsz, s, gw), gbt, onorm, states,
                                                     do_a.reshape(bsz, s, gw))
    dgb = jnp.concatenate([dg, dbeta], axis=1).reshape(bsz, 2 * gh, s).transpose(0, 2, 1)
    dgb = jnp.concatenate([dgb, jnp.zeros((bsz, s, LANE - 2 * gh), F32)], axis=2).reshape(t, LANE)

    def gates_bwd_fn(ab_, dgb_, alog_, dtb_):
        _, f = jax.vjp(_gates_fn, ab_, alog_, dtb_)
        return f(dgb_)

    dp_ab, g_alog, g_dtb = _rowwise(gates_bwd_fn, [p_ab, dgb], [alog_p, dtb_p], [(LANE, BF16)], [(1, LANE), (1, LANE)],
                                    name="gdn_gates_bwd")
    dqkv_c = jnp.concatenate([dgq, dgk, dgv], axis=2)
    dp_qkv, g_conv_gdn = _gdn_conv_bwd(p_qkv.reshape(bsz, s, 3 * gw), cgdn, dqkv_c)
    dp_qkv = dp_qkv.reshape(t, 3 * gw)
    dp_z = dz.reshape(t, gw)

    grads["w_in"] = jnp.concatenate([
        _mm(dp_qkv, xn, ta=True, name="dw_in_qkv"),
        _mm(dp_ab, xn, ta=True, name="dw_in_ab")[:2 * gh],
        _mm(dp_z, xn, ta=True, name="dw_in_z"),
        _mm(dp_sb, xn, ta=True, name="dw_in_sb"),
        _mm(dgate, xn, ta=True, name="dw_in_gate")], axis=0)
    g_late = _pack_full_grads(grads, shard_shapes, _BIG[:1])
    r1_late = _exchange_sibling(g_late, "grads_to_sibling_late")
    s1_late = _sum_sibling(g_late, r1_late, c_idx, "sum_sibling_late")
    dxn, r2_late = _dxn_fused([(dp_qkv, w_qkv), (dp_ab, w_ab), (dp_z, w_z), (dp_sb, w_sb), (dgate, w_gate)], s1_late)
    grad_x, g_norm_mix = _rowwise(norm_bwd_fn, [x2, dh1, dxn], [norm_mix], [(d, F32)], [(1, d)], name="norm_mix_bwd")

    small_g = {"norm_mix": g_norm_mix, "norm_x": g_norm_x, "norm_mem": g_norm_mem, "norm_ffn": g_norm_ffn,
               "a_log": g_alog[:, :gh], "dt_bias": g_dtb[:, :gh], "gdn_out_norm": g_onorm,
               "xq_norm": g_xq_norm, "xk_norm": g_xk_norm}
    sm_rows = [_rows_of(small_g[n], d) for n in _SMALL_REP] + [_rows_of(loss_part, d)]
    sm_rows += [_rows_of(g_conv_gdn, d), _rows_of(g_conv_ffn, d)]
    sm_cnt = [r.shape[0] for r in sm_rows]
    sm_sum = _all_reduce_small(_pad_rows(jnp.concatenate(sm_rows, axis=0), SUBLANE), "all_reduce_small_grads")
    offs = [0]
    for cnt in sm_cnt:
        offs.append(offs[-1] + cnt)
    small_grad = {}
    for i, n in enumerate(_SMALL_REP):
        small_grad[n] = sm_sum[offs[i]:offs[i + 1]].reshape(-1)[:wts[n].size].reshape(wts[n].shape)
    loss = sm_sum[offs[len(_SMALL_REP)], 0]
    for i, n in enumerate(_SMALL_CONV):
        k, cols = wts[n].shape[1], wts[n].shape[2]
        o = offs[len(_SMALL_REP) + 1 + i]
        fullg = sm_sum[o:o + sm_cnt[len(_SMALL_REP) + 1 + i]].reshape(-1)[:k * cols * N_DEV].reshape(k, N_DEV * cols)
        small_grad[n] = lax.dynamic_slice(fullg, (0, me * cols), (k, cols)).reshape(wts[n].shape)

    small_names = _SMALL_REP + _SMALL_CONV

    def pack_small(src):
        return _pad_rows(jnp.concatenate([_rows_of(src[n], d) for n in small_names], axis=0), SUBLANE)

    sw_, sg_, sm_, sv_ = pack_small(wts), pack_small(small_grad), pack_small(mom), pack_small(vel)
    sd_, snm_, snv_ = _rowwise(_adamw_math, [sw_, sg_, sm_, sv_], [], [(d, F32)] * 3, [], name="adamw_small", tm=sw_.shape[0])

    def unpack_small(packed):
        out, r0 = {}, 0
        for n in small_names:
            cnt = _rows_of(wts[n], d).shape[0]
            out[n] = packed[r0:r0 + cnt].reshape(-1)[:wts[n].size].reshape(wts[n].shape)
            r0 += cnt
        return out

    small_delta, small_m, small_v = unpack_small(sd_), unpack_small(snm_), unpack_small(snv_)

    idx = jnp.stack([me, 2 * mx + my]).astype(jnp.int32)
    pm = _pack_big_shards({n: mom[n][0] for n in _BIG}, shard_shapes)
    pv = _pack_big_shards({n: vel[n][0] for n in _BIG}, shard_shapes)
    upd_late = _adamw_big(g_late, r1_late, r2_late, packed_w, pm, pv, idx, 0, "adamw_late")
    upd_early = _adamw_big(g_early, r1_early, r2_early, packed_w, pm, pv, idx, g_late.shape[1], "adamw_early")
    big_grad, big_delta, big_m, big_v = (
        {**_unpack_shard(a, shard_shapes, _BIG[:1]), **_unpack_shard(b, shard_shapes, early)}
        for a, b in zip(upd_late, upd_early))

    def pick(big, small, n):
        return big[n] if n in big else small[n]

    outs = [loss, grad_x.reshape(bsz, s, d)]
    outs += [pick(big_grad, small_grad, n) for n in names]
    outs += [pick(big_delta, small_delta, n) for n in names]
    outs += [pick(big_m, small_m, n) for n in names]
    outs += [pick(big_v, small_v, n) for n in names]
    return tuple(outs)
```

```python
import functools

import jax
import jax.numpy as jnp
from jax import lax
from jax.experimental import pallas as pl
from jax.experimental.pallas import tpu as pltpu

F32 = jnp.float32
BF16 = jnp.bfloat16
FDOT_PRECISION = lax.Precision.HIGH

LANE = 128
SUBLANE = 8
PACK_ROW_ALIGN = 16
VMEM_LIMIT = 56 * 2 ** 20
N_DEV = 8
MESH = pl.DeviceIdType.MESH

EPS = 1e-6
ADAM_LR = 0.001
ADAM_B1 = 0.9
ADAM_B2 = 0.999
ADAM_EPS = 1e-08
ADAM_WD = 0.01
ADAM_STEP = 10


class _Cfg:
    d = 1024
    b = 4
    s = 2048
    mem = 256
    gh = 8
    gch = 64
    sbh = 8
    xh = 4
    dff = 2816
    pack_tile = 256


CFG = _Cfg()
HD = 128
SB_BLK = 128
SB_HP = 4
GDN_HP = 8
MM_TILES = (1024, 1408, 704, 512, 256, 128)
MM_K_TILES = (1024, 1408, 704, 512, 256, 128)
MM_K_TILES_F32 = (512, 704, 256, 128)
CONV_CB = 256
XQ_TILE = 256


def _tile(n, prefs):
    for t in prefs:
        if n % t == 0:
            return t
    raise ValueError(f"no tile for {n}")


def _cp(sem, **kw):
    return pltpu.CompilerParams(dimension_semantics=sem, vmem_limit_bytes=VMEM_LIMIT, **kw)


def _dims(kind):
    return {"nn": (((1,), (0,)), ((), ())), "nt": (((1,), (1,)), ((), ())), "tn": (((0,), (0,)), ((), ()))}[kind]


def _raw_bdot(a, b, kind):
    return lax.dot_general(a.astype(BF16), b.astype(BF16), _dims(kind), preferred_element_type=F32)


def _raw_fdot(a, b, kind):
    return lax.dot_general(a.astype(F32), b.astype(F32), _dims(kind), precision=FDOT_PRECISION,
                           preferred_element_type=F32)


def _make_dot(raw):
    @functools.partial(jax.custom_vjp, nondiff_argnums=(2,))
    def dot(a, b, kind):
        return raw(a, b, kind)

    def fwd(a, b, kind):
        return raw(a, b, kind), (a, b)

    def bwd(kind, res, g):
        a, b = res
        if kind == "nn":
            return raw(g, b, "nt").astype(a.dtype), raw(a, g, "tn").astype(b.dtype)
        if kind == "nt":
            return raw(g, b, "nn").astype(a.dtype), raw(g, a, "tn").astype(b.dtype)
        return raw(b, g, "nt").astype(a.dtype), raw(a, g, "nn").astype(b.dtype)

    dot.defvjp(fwd, bwd)
    return dot


_bdot = _make_dot(_raw_bdot)
_fdot = _make_dot(_raw_fdot)


def _split_dot(x, m01):
    hi = x.astype(BF16)
    lo = (x - hi.astype(F32)).astype(BF16)
    return (lax.dot_general(hi, m01, _dims("nn"), preferred_element_type=F32)
            + lax.dot_general(lo, m01, _dims("nn"), preferred_element_type=F32))


_sigmoid = jax.nn.sigmoid


def _silu(x):
    return x * _sigmoid(x)


def _softplus(x):
    return jnp.maximum(x, 0.0) + jnp.log1p(jnp.exp(-jnp.abs(x)))


def _rms(x, g):
    return x * lax.rsqrt(jnp.mean(x * x, axis=-1, keepdims=True) + EPS) * g


def _iota2(shape, dim):
    return lax.broadcasted_iota(jnp.int32, shape, dim)


def _mm(a, b, *, ta=False, tb=False, add=None, out_dtype=F32, name, epilogue=None):
    if ta:
        kd, m = a.shape
    else:
        m, kd = a.shape
    if tb:
        n, kb = b.shape
    else:
        kb, n = b.shape
    assert kd == kb, (a.shape, b.shape, ta, tb)
    fn, e_rows, e_pars, e_dtypes, n_acc, full_rows = epilogue or (None, [], [], [out_dtype], 0, False)
    tm = _tile(m, MM_TILES)
    tn = n if full_rows else _tile(n, MM_TILES)
    wide = max(a.dtype.itemsize, b.dtype.itemsize) > 2
    tk = _tile(kd, MM_K_TILES_F32 if wide else MM_K_TILES)
    nk = kd // tk
    kind_dims = (((0 if ta else 1,), (1 if tb else 0,)), ((), ()))
    n_in = 2 + (add is not None) + len(e_rows) + len(e_pars)
    n_out = len(e_dtypes)

    def body(*refs):
        a_ref, b_ref = refs[:2]
        add_ref = refs[2] if add is not None else None
        extra = refs[2 + (add is not None):n_in]
        o_refs = refs[n_in:n_in + n_out]
        acc_refs = refs[n_in + n_out:n_in + n_out + n_acc]
        part = lax.dot_general(a_ref[...].astype(BF16), b_ref[...].astype(BF16), kind_dims,
                               preferred_element_type=F32)

        def finish(r):
            if add is not None:
                r = r + add_ref[...].astype(F32)
            outs = (r,) if fn is None else fn(r, *[e[...] for e in extra])
            for o_ref, val in zip(o_refs, outs[:n_out]):
                o_ref[...] = val.astype(o_ref.dtype)
            if n_acc:
                first = (pl.program_id(0) == 0) & (pl.program_id(1) == 0)

                @pl.when(first)
                def _():
                    for ref in acc_refs:
                        ref[...] = jnp.zeros_like(ref)

                for ref, val in zip(acc_refs, outs[n_out:]):
                    ref[...] += val

        if nk == 1:
            finish(part)
            return
        acc = refs[-1]
        k = pl.program_id(2)

        @pl.when(k == 0)
        def _():
            acc[...] = part

        @pl.when((k > 0) & (k < nk - 1))
        def _():
            acc[...] += part

        @pl.when(k == nk - 1)
        def _():
            finish(acc[...] + part)

    a_spec = pl.BlockSpec((tk, tm), lambda i, j, k: (k, i)) if ta else pl.BlockSpec((tm, tk), lambda i, j, k: (i, k))
    b_spec = pl.BlockSpec((tn, tk), lambda i, j, k: (j, k)) if tb else pl.BlockSpec((tk, tn), lambda i, j, k: (k, j))
    tile = pl.BlockSpec((tm, tn), lambda i, j, k: (i, j))
    in_specs = [a_spec, b_spec] + [tile] * ((add is not None) + len(e_rows))
    in_specs += [pl.BlockSpec((1, tn), lambda i, j, k: (0, j)) for _ in e_pars]
    args = [a, b] + ([add] if add is not None else []) + list(e_rows) + list(e_pars)
    out_specs = [tile] * n_out + [pl.BlockSpec((1, LANE), lambda i, j, k: (0, 0))] * n_acc
    out_shape = [jax.ShapeDtypeStruct((m, n), dt) for dt in e_dtypes] + [jax.ShapeDtypeStruct((1, LANE), F32)] * n_acc
    res = pl.pallas_call(
        body, name=name, grid=(m // tm, n // tn, nk),
        in_specs=in_specs, out_specs=out_specs, out_shape=out_shape,
        scratch_shapes=[pltpu.VMEM((tm, tn), F32)] if nk > 1 else [],
        compiler_params=_cp(("arbitrary", "arbitrary", "arbitrary") if n_acc else ("parallel", "parallel", "arbitrary")),
    )(*args)
    return res[0] if epilogue is None else res


def _rowwise(fn, rows, pars, out_rows, out_accs, *, name, tm=None):
    t = rows[0].shape[0]
    if tm is None:
        tm = _tile(t, (256, 128, 64, 32, 16))
    assert t % tm == 0, (t, tm)
    n_r, n_p, n_or, n_oa = len(rows), len(pars), len(out_rows), len(out_accs)

    def body(*refs):
        r_in = refs[:n_r]
        p_in = refs[n_r:n_r + n_p]
        o_r = refs[n_r + n_p:n_r + n_p + n_or]
        o_a = refs[n_r + n_p + n_or:]
        outs = fn(*[r[...] for r in r_in], *[p[...] for p in p_in])
        if not isinstance(outs, (tuple, list)):
            outs = (outs,)
        assert len(outs) == n_or + n_oa, (name, len(outs))
        for ref, val in zip(o_r, outs[:n_or]):
            ref[...] = val.astype(ref.dtype)
        if n_oa:
            @pl.when(pl.program_id(0) == 0)
            def _():
                for ref in o_a:
                    ref[...] = jnp.zeros_like(ref)

            for ref, val in zip(o_a, outs[n_or:]):
                ref[...] += val.astype(F32)

    in_specs = [pl.BlockSpec((tm, r.shape[1]), lambda i: (i, 0)) for r in rows]
    in_specs += [pl.BlockSpec(p.shape, lambda i: (0, 0)) for p in pars]
    out_specs = [pl.BlockSpec((tm, c), lambda i: (i, 0)) for c, _ in out_rows]
    out_specs += [pl.BlockSpec(s, lambda i: (0, 0)) for s in out_accs]
    out_shape = [jax.ShapeDtypeStruct((t, c), dt) for c, dt in out_rows]
    out_shape += [jax.ShapeDtypeStruct(s, F32) for s in out_accs]
    return pl.pallas_call(
        body, name=name, grid=(t // tm,), in_specs=in_specs, out_specs=out_specs, out_shape=out_shape,
        compiler_params=_cp(("arbitrary",)),
    )(*rows, *pars)


def _shift_down(x, sh):
    rolled = pltpu.roll(x, sh, 0)
    top = rolled[:SUBLANE]
    top = jnp.where(_iota2(top.shape, 0) >= sh, top, 0.0)
    return jnp.concatenate([top, rolled[SUBLANE:]], axis=0)


def _shift_up(x, sh):
    s = x.shape[0]
    rolled = pltpu.roll(x, s - sh, 0)
    bottom = rolled[s - SUBLANE:]
    bottom = jnp.where(_iota2(bottom.shape, 0) < SUBLANE - sh, bottom, 0.0)
    return jnp.concatenate([rolled[:s - SUBLANE], bottom], axis=0)


def _conv(x, w):
    k = w.shape[0]
    y = x * w[k - 1:k, :]
    for i in range(k - 1):
        y = y + _shift_down(x, k - 1 - i) * w[i:i + 1, :]
    return y


def _conv_bwd(x, w, dy):
    k = w.shape[0]
    dx = dy * w[k - 1:k, :]
    dws = []
    for i in range(k - 1):
        dx = dx + _shift_up(dy, k - 1 - i) * w[i:i + 1, :]
        dws.append(jnp.sum(dy * _shift_down(x, k - 1 - i), axis=0, keepdims=True))
    dws.append(jnp.sum(dy * x, axis=0, keepdims=True))
    return dx, dws


def _gdn_post(y, j, nqb):
    a = _silu(y)
    sc = jnp.where(j < nqb, HD ** -0.5, 1.0).astype(F32)
    outs = []
    for h in range(y.shape[1] // HD):
        ah = a[:, h * HD:(h + 1) * HD]
        l2 = ah * lax.rsqrt(jnp.sum(ah * ah, axis=-1, keepdims=True) + EPS)
        outs.append(jnp.where(j < 2 * nqb, l2 * sc, ah))
    return jnp.concatenate(outs, axis=1) if len(outs) > 1 else outs[0]


def _gdn_conv_fwd(x, w):
    bsz, s, c3 = x.shape
    k = w.shape[0]
    nb = c3 // CONV_CB
    nqb = nb // 3

    def body(x_ref, w_ref, o_ref):
        j = pl.program_id(1)
        o_ref[0] = _gdn_post(_conv(x_ref[0], w_ref[...]), j, nqb)

    return pl.pallas_call(
        body, name="gdn_conv_fwd", grid=(bsz, nb),
        in_specs=[pl.BlockSpec((1, s, CONV_CB), lambda b, j: (b, 0, j)), pl.BlockSpec((k, CONV_CB), lambda b, j: (0, j))],
        out_specs=pl.BlockSpec((1, s, CONV_CB), lambda b, j: (b, 0, j)),
        out_shape=jax.ShapeDtypeStruct(x.shape, F32),
        compiler_params=_cp(("parallel", "parallel")),
    )(x, w)


def _gdn_conv_bwd(x, w, dout):
    bsz, s, c3 = x.shape
    k = w.shape[0]
    nb = c3 // CONV_CB
    nqb = nb // 3

    def body(x_ref, w_ref, d_ref, dx_ref, dw_ref):
        j = pl.program_id(0)
        b = pl.program_id(1)
        xv, wv = x_ref[0], w_ref[...]
        y = _conv(xv, wv)
        _, f = jax.vjp(lambda yy: _gdn_post(yy, j, nqb), y)
        (dy,) = f(d_ref[0])
        dx, dws = _conv_bwd(xv, wv, dy)
        dx_ref[0] = dx.astype(dx_ref.dtype)

        @pl.when(b == 0)
        def _():
            dw_ref[...] = jnp.zeros_like(dw_ref)

        for i in range(k):
            dw_ref[i:i + 1, :] += dws[i]

    return pl.pallas_call(
        body, name="gdn_conv_bwd", grid=(nb, bsz),
        in_specs=[pl.BlockSpec((1, s, CONV_CB), lambda j, b: (b, 0, j)), pl.BlockSpec((k, CONV_CB), lambda j, b: (0, j)),
                  pl.BlockSpec((1, s, CONV_CB), lambda j, b: (b, 0, j))],
        out_specs=[pl.BlockSpec((1, s, CONV_CB), lambda j, b: (b, 0, j)), pl.BlockSpec((k, CONV_CB), lambda j, b: (0, j))],
        out_shape=[jax.ShapeDtypeStruct(x.shape, BF16), jax.ShapeDtypeStruct(w.shape, F32)],
        compiler_params=_cp(("parallel", "arbitrary")),
    )(x, w, dout)


def _ffn_conv_fwd(up, w):
    bsz, s, c2 = up.shape
    k = w.shape[0]
    nb = (c2 // 2) // CONV_CB

    def body(x1_ref, x2_ref, w1_ref, w2_ref, o_ref):
        u1 = _conv(x1_ref[0], w1_ref[...])
        u2 = _conv(x2_ref[0], w2_ref[...])
        o_ref[0] = (_silu(u1) * u2).astype(o_ref.dtype)

    return pl.pallas_call(
        body, name="ffn_conv_fwd", grid=(bsz, nb),
        in_specs=[pl.BlockSpec((1, s, CONV_CB), lambda b, j: (b, 0, j)), pl.BlockSpec((1, s, CONV_CB), lambda b, j: (b, 0, j + nb)),
                  pl.BlockSpec((k, CONV_CB), lambda b, j: (0, j)), pl.BlockSpec((k, CONV_CB), lambda b, j: (0, j + nb))],
        out_specs=pl.BlockSpec((1, s, CONV_CB), lambda b, j: (b, 0, j)),
        out_shape=jax.ShapeDtypeStruct((bsz, s, c2 // 2), BF16),
        compiler_params=_cp(("parallel", "parallel")),
    )(up, up, w, w)


def _ffn_conv_bwd(up, w, dact):
    bsz, s, c2 = up.shape
    k = w.shape[0]
    half = c2 // 2
    nb = half // CONV_CB

    def body(x1_ref, x2_ref, w1_ref, w2_ref, d_ref, dx1_ref, dx2_ref, dw1_ref, dw2_ref):
        b = pl.program_id(1)
        x1, x2, w1, w2 = x1_ref[0], x2_ref[0], w1_ref[...], w2_ref[...]
        u1 = _conv(x1, w1)
        u2 = _conv(x2, w2)
        _, f = jax.vjp(lambda p, q: _silu(p) * q, u1, u2)
        du1, du2 = f(d_ref[0])
        dx1, dws1 = _conv_bwd(x1, w1, du1)
        dx2, dws2 = _conv_bwd(x2, w2, du2)
        dx1_ref[0] = dx1.astype(dx1_ref.dtype)
        dx2_ref[0] = dx2.astype(dx2_ref.dtype)

        @pl.when(b == 0)
        def _():
            dw1_ref[...] = jnp.zeros_like(dw1_ref)
            dw2_ref[...] = jnp.zeros_like(dw2_ref)

        for i in range(k):
            dw1_ref[i:i + 1, :] += dws1[i]
            dw2_ref[i:i + 1, :] += dws2[i]

    def blk(off):
        return pl.BlockSpec((1, s, CONV_CB), lambda j, b: (b, 0, j + off))

    def wblk(off):
        return pl.BlockSpec((k, CONV_CB), lambda j, b: (0, j + off))

    return pl.pallas_call(
        body, name="ffn_conv_bwd", grid=(nb, bsz),
        in_specs=[blk(0), blk(nb), wblk(0), wblk(nb), blk(0)],
        out_specs=[blk(0), blk(0), wblk(0), wblk(0)],
        out_shape=[jax.ShapeDtypeStruct((bsz, s, half), BF16), jax.ShapeDtypeStruct((bsz, s, half), BF16),
                   jax.ShapeDtypeStruct((k, half), F32), jax.ShapeDtypeStruct((k, half), F32)],
        compiler_params=_cp(("parallel", "arbitrary")),
    )(up, up, w, w, dact)


@jax.custom_vjp
def _inv_unit_lower(mats):
    c = mats[0].shape[0]
    eye = (_iota2((c, c), 0) == _iota2((c, c), 1)).astype(F32)
    ps = [-a for a in mats]
    ts = [eye + p for p in ps]
    n = 2
    while n < c:
        ps = [_raw_fdot(p, p, "nn") for p in ps]
        ts = [t + _raw_fdot(t, p, "nn") for t, p in zip(ts, ps)]
        n *= 2
    return ts


def _inv_fwd(mats):
    ts = _inv_unit_lower(mats)
    return ts, ts


def _inv_bwd(ts, gs):
    xs = [_raw_fdot(g, t, "nt") for g, t in zip(gs, ts)]
    return ([-_raw_fdot(t, x, "tn") for t, x in zip(ts, xs)],)


_inv_unit_lower.defvjp(_inv_fwd, _inv_bwd)


def _gdn_chunk(q, k, v, z, g_row, beta_row, state, onorm):
    nh = range(len(q))
    c = q[0].shape[0]
    ii, jj = _iota2((c, c), 0), _iota2((c, c), 1)
    incl, strict, eye = ii >= jj, ii > jj, ii == jj

    def to_col(row):
        return jnp.sum(jnp.where(eye, jnp.broadcast_to(row, (c, c)), 0.0), axis=1, keepdims=True)

    gc_col = [jnp.sum(jnp.where(incl, jnp.broadcast_to(g_row[h], (c, c)), 0.0), axis=1, keepdims=True) for h in nh]
    gc_row = [jnp.sum(jnp.where(eye, jnp.broadcast_to(gc_col[h], (c, c)), 0.0), axis=0, keepdims=True) for h in nh]
    beta_col = [to_col(beta_row[h]) for h in nh]
    gc_last = [jnp.sum(g_row[h], axis=1, keepdims=True) for h in nh]
    decay = [jnp.where(incl, jnp.exp(jnp.where(incl, gc_col[h] - gc_row[h], 0.0)), 0.0) for h in nh]
    kk = [_bdot(k[h], k[h], "nt") for h in nh]
    qk = [_bdot(q[h], k[h], "nt") * decay[h] for h in nh]
    tinv = _inv_unit_lower([jnp.where(strict, beta_col[h] * kk[h] * decay[h], 0.0) for h in nh])
    rhs = [jnp.concatenate([v[h] * beta_col[h], k[h] * (beta_col[h] * jnp.exp(gc_col[h]))], axis=1) for h in nh]
    uw = [_fdot(tinv[h], rhs[h], "nn") for h in nh]
    dv = v[0].shape[1]
    ws = [_bdot(uw[h][:, dv:], state[h], "nn") for h in nh]
    qs = [_bdot(q[h] * jnp.exp(gc_col[h]), state[h], "nn") for h in nh]
    v_new = [uw[h][:, :dv] - ws[h] for h in nh]
    o = [qs[h] + _bdot(qk[h], v_new[h], "nn") for h in nh]
    kv = [_bdot(k[h] * jnp.exp(gc_last[h] - gc_col[h]), v_new[h], "tn") for h in nh]
    new_state = [state[h] * jnp.exp(gc_last[h]) + kv[h] for h in nh]
    y = [_rms(o[h], onorm) * _silu(z[h]) for h in nh]
    return y, new_state


def _gdn_specs(s, c, reverse):
    n = s // c
    nn = (lambda i: n - 1 - i) if reverse else (lambda i: i)

    def qkv(off):
        return pl.BlockSpec((1, c, GDN_HP * HD), lambda b, h, i: (b, nn(i), h + off))

    def gate(off):
        return pl.BlockSpec((1, GDN_HP, 1, 1, c), lambda b, h, i: (b, h + off, nn(i), 0, 0))

    st = pl.BlockSpec((1, GDN_HP, 1, HD, HD), lambda b, h, i: (b, h, nn(i), 0, 0))
    onorm = pl.BlockSpec((1, HD), lambda b, h, i: (0, 0))
    return n, qkv, gate, st, onorm


def _gdn_fwd(qkv, z, gbt, onorm):
    bsz, s, _ = qkv.shape
    gh, c = CFG.gh, CFG.gch
    ng = gh // GDN_HP
    n, qs, gs, st, on = _gdn_specs(s, c, False)

    def body(q_ref, k_ref, v_ref, z_ref, g_ref, b_ref, on_ref, y_ref, st_ref, state):
        @pl.when(pl.program_id(2) == 0)
        def _():
            state[...] = jnp.zeros_like(state)

        nh = range(GDN_HP)
        hs = [slice(h * HD, (h + 1) * HD) for h in nh]
        s_in = [state[h] for h in nh]
        for h in nh:
            st_ref[0, h, 0] = s_in[h]
        y, s_out = _gdn_chunk([q_ref[0, :, hs[h]] for h in nh], [k_ref[0, :, hs[h]] for h in nh],
                              [v_ref[0, :, hs[h]] for h in nh], [z_ref[0, :, hs[h]] for h in nh],
                              [g_ref[0, h, 0] for h in nh], [b_ref[0, h, 0] for h in nh], s_in, on_ref[...])
        for h in nh:
            y_ref[0, :, hs[h]] = y[h].astype(y_ref.dtype)
            state[h] = s_out[h]

    return pl.pallas_call(
        body, name="gdn_fwd", grid=(bsz, ng, n),
        in_specs=[qs(0), qs(ng), qs(2 * ng), qs(0), gs(0), gs(ng), on],
        out_specs=[qs(0), st],
        out_shape=[jax.ShapeDtypeStruct((bsz, s, gh * HD), BF16), jax.ShapeDtypeStruct((bsz, gh, n, HD, HD), F32)],
        scratch_shapes=[pltpu.VMEM((GDN_HP, HD, HD), F32)],
        compiler_params=_cp(("parallel", "parallel", "arbitrary")),
    )(qkv, qkv, qkv, z, gbt, gbt, onorm)


def _gdn_bwd(qkv, z, gbt, onorm, states, dy):
    bsz, s, _ = qkv.shape
    gh, c = CFG.gh, CFG.gch
    ng = gh // GDN_HP
    n, qs, gs, st, on = _gdn_specs(s, c, True)

    def body(q_ref, k_ref, v_ref, z_ref, g_ref, b_ref, on_ref, st_ref, dy_ref,
             dq_ref, dk_ref, dv_ref, dz_ref, dg_ref, db_ref, don_ref, dstate):
        first = (pl.program_id(0) == 0) & (pl.program_id(1) == 0) & (pl.program_id(2) == 0)

        @pl.when(first)
        def _():
            don_ref[...] = jnp.zeros_like(don_ref)

        @pl.when(pl.program_id(2) == 0)
        def _():
            dstate[...] = jnp.zeros_like(dstate)

        nh = range(GDN_HP)
        hs = [slice(h * HD, (h + 1) * HD) for h in nh]
        _, f = jax.vjp(_gdn_chunk, [q_ref[0, :, hs[h]] for h in nh], [k_ref[0, :, hs[h]] for h in nh],
                       [v_ref[0, :, hs[h]] for h in nh], [z_ref[0, :, hs[h]] for h in nh],
                       [g_ref[0, h, 0] for h in nh], [b_ref[0, h, 0] for h in nh],
                       [st_ref[0, h, 0] for h in nh], on_ref[...])
        dq, dk, dv, dz, dg, db, ds, don = f(([dy_ref[0, :, hs[h]] for h in nh], [dstate[h] for h in nh]))
        for h in nh:
            dq_ref[0, :, hs[h]] = dq[h]
            dk_ref[0, :, hs[h]] = dk[h]
            dv_ref[0, :, hs[h]] = dv[h]
            dz_ref[0, :, hs[h]] = dz[h].astype(dz_ref.dtype)
            dg_ref[0, h, 0] = dg[h]
            db_ref[0, h, 0] = db[h]
            dstate[h] = ds[h]
        don_ref[...] += don

    act = jax.ShapeDtypeStruct((bsz, s, gh * HD), F32)
    gshape = jax.ShapeDtypeStruct((bsz, gh, n, 1, c), F32)
    return pl.pallas_call(
        body, name="gdn_bwd", grid=(bsz, ng, n),
        in_specs=[qs(0), qs(ng), qs(2 * ng), qs(0), gs(0), gs(ng), on, st, qs(0)],
        out_specs=[qs(0), qs(0), qs(0), qs(0), gs(0), gs(0), on],
        out_shape=[act, act, act, jax.ShapeDtypeStruct(act.shape, BF16), gshape, gshape, jax.ShapeDtypeStruct((1, HD), F32)],
        scratch_shapes=[pltpu.VMEM((GDN_HP, HD, HD), F32)],
        compiler_params=_cp(("arbitrary", "arbitrary", "arbitrary")),
    )(qkv, qkv, qkv, z, gbt, gbt, onorm, states, dy)


def _gates_fn(ab, alog, dtb):
    lane = _iota2(ab.shape, 1)
    g = -jnp.exp(alog) * _softplus(ab + dtb)
    beta = _sigmoid(ab)
    return jnp.where(lane < CFG.gh, g, jnp.where(lane < 2 * CFG.gh, beta, 0.0))


def _heads_cumsum(xs, tri):
    n = xs[0].shape[0]
    y = _split_dot(jnp.concatenate(xs, axis=0), tri)
    return [y[h * n:(h + 1) * n] for h in range(len(xs))]


def _blk_off(jblk):
    return jblk * SB_BLK if isinstance(jblk, int) else pl.multiple_of(jblk * SB_BLK, SB_BLK)


def _sb_span(qs, k_spans, mask, runs, tri_su):
    nh = range(len(qs))
    nb = k_spans[0].shape[0] // SB_BLK
    zs = [lax.dot_general(qs[h], k_spans[h], _dims("nt"), preferred_element_type=F32) for h in nh]
    l1p = [jnp.log(1.0 + jnp.exp(-jnp.abs(z))) for z in zs]
    lss = [jnp.minimum(zs[h], 0.0) - l1p[h] for h in nh]
    lfs = [lss[h] - zs[h] for h in nh]
    if mask is not None:
        lfs = [jnp.where(mask, lf, 0.0) for lf in lfs]
    units = [lfs[h][:, b * SB_BLK:(b + 1) * SB_BLK] for h in nh for b in range(nb)]
    cums = _heads_cumsum(units, tri_su)
    sfx, new_runs = [], []
    for h in nh:
        run, parts = runs[h], [None] * nb
        for b in reversed(range(nb)):
            parts[b] = cums[h * nb + b] + run
            run = run + jnp.sum(units[h * nb + b], axis=1, keepdims=True)
        sfx.append(jnp.concatenate(parts, axis=1) if nb > 1 else parts[0])
        new_runs.append(run)
    ws = [jnp.exp(lss[h] + sfx[h]) for h in nh]
    if mask is not None:
        ws = [jnp.where(mask, w, 0.0) for w in ws]
    return zs, lfs, ws, new_runs


def _sb_specs(s, w):
    def qb(off):
        return pl.BlockSpec((1, SB_BLK, w), lambda b, h, i: (b, i, h + off))

    def full(off):
        return pl.BlockSpec((1, s, w), lambda b, h, i: (b, 0, h + off))

    return qb, full


def _sb_fwd(qkv, shard):
    bsz, s, _ = qkv.shape
    ng = CFG.sbh // SB_HP
    w = SB_HP * HD
    scale = HD ** -0.5
    qb, full = _sb_specs(s, w)

    def body(q_ref, k_ref, v_ref, x_ref, o_ref, g_ref, send_sems, recv_sems, local_sem):
        i = pl.program_id(2)
        begin, relay, finish = _gather_stages(x_ref, g_ref, send_sems, recv_sems, local_sem)
        start_of_group = (pl.program_id(1) == 0) & (i == 0)
        pl.when((pl.program_id(0) == 0) & start_of_group)(begin)
        pl.when((pl.program_id(0) == bsz // 2) & start_of_group)(relay)

        r, c = _iota2((SB_BLK, SB_BLK), 0), _iota2((SB_BLK, SB_BLK), 1)
        tri_su = (r > c).astype(BF16)
        nh = range(SB_HP)
        hs = [slice(h * HD, (h + 1) * HD) for h in nh]
        qs = [(q_ref[0, :, hs[h]] * scale).astype(BF16) for h in nh]

        def span(off, nb, mask, carry):
            ks = [k_ref[0, pl.ds(off, nb * SB_BLK), hs[h]].astype(BF16) for h in nh]
            vs = [v_ref[0, pl.ds(off, nb * SB_BLK), hs[h]].astype(BF16) for h in nh]
            _, _, ws, runs = _sb_span(qs, ks, mask, [cr[1] for cr in carry], tri_su)
            pv = [lax.dot_general(ws[h].astype(BF16), vs[h], _dims("nn"), preferred_element_type=F32) for h in nh]
            return tuple((carry[h][0] + pv[h], runs[h]) for h in nh)

        carry = tuple((jnp.zeros((SB_BLK, HD), F32), jnp.zeros((SB_BLK, 1), F32)) for _ in nh)
        carry = span(_blk_off(i), 1, c < r, carry)
        rem = jnp.bitwise_and(i, 3)
        carry = lax.fori_loop(0, lax.shift_right_logical(i, 2),
                              lambda p, cr: span(_blk_off(i - 4 - 4 * p), 4, None, cr), carry)
        carry = lax.fori_loop(0, lax.shift_right_logical(rem, 1),
                              lambda _, cr: span(_blk_off(jnp.bitwise_and(rem, 1)), 2, None, cr), carry)
        carry = lax.fori_loop(0, jnp.bitwise_and(rem, 1), lambda _, cr: span(0, 1, None, cr), carry)
        for h in nh:
            o_ref[0, :, hs[h]] = carry[h][0].astype(o_ref.dtype)

        pl.when((pl.program_id(0) == bsz - 1) & (pl.program_id(1) == ng - 1) & (i == nblk - 1))(finish)

    nblk = s // SB_BLK
    hbm = pl.BlockSpec(memory_space=pl.ANY)
    return pl.pallas_call(
        body, name="sb_fwd", grid=(bsz, ng, nblk),
        in_specs=[qb(0), full(ng), full(2 * ng), hbm], out_specs=[qb(0), hbm],
        out_shape=[jax.ShapeDtypeStruct((bsz, s, CFG.sbh * HD), BF16),
                   jax.ShapeDtypeStruct((N_DEV,) + shard.shape, shard.dtype)],
        scratch_shapes=list(_GATHER_SEMS),
        compiler_params=_cp(("arbitrary", "arbitrary", "arbitrary")),
    )(qkv, qkv, qkv, shard)


def _sb_bwd(qkv, do, s1):
    bsz, s, _ = qkv.shape
    ng = CFG.sbh // SB_HP
    w = SB_HP * HD
    nblk = s // SB_BLK
    scale = HD ** -0.5
    qb, full = _sb_specs(s, w)

    def body(q_ref, k_ref, v_ref, do_ref, s1_ref, dq_ref, dk_ref, dv_ref, r2_ref, dk_acc, dv_acc, dl_pan, z_pan,
             send_sems, recv_sems):
        i = pl.program_id(2)
        copies = _chip_copies(s1_ref, r2_ref, send_sems, recv_sems)

        @pl.when((pl.program_id(0) == 0) & (pl.program_id(1) == 0) & (i == 0))
        def _():
            for cp in copies:
                cp.start()

        @pl.when(i == 0)
        def _():
            dk_acc[...] = jnp.zeros_like(dk_acc)
            dv_acc[...] = jnp.zeros_like(dv_acc)

        r, c = _iota2((SB_BLK, SB_BLK), 0), _iota2((SB_BLK, SB_BLK), 1)
        tri_su = (r > c).astype(BF16)
        tri_pre = (r < c).astype(BF16)
        nh = range(SB_HP)
        hs = [slice(h * HD, (h + 1) * HD) for h in nh]
        qs = [(q_ref[0, :, hs[h]] * scale).astype(BF16) for h in nh]
        dob = [do_ref[0, :, hs[h]].astype(BF16) for h in nh]
        quads = lax.shift_right_logical(i, 2)
        rem = jnp.bitwise_and(i, 3)
        pair = lax.shift_right_logical(rem, 1)
        odd = jnp.bitwise_and(rem, 1)

        def span_a(jblk, nb, mask, runs):
            rows = pl.ds(_blk_off(jblk), nb * SB_BLK)
            ks = [k_ref[0, rows, hs[h]].astype(BF16) for h in nh]
            vs = [v_ref[0, rows, hs[h]].astype(BF16) for h in nh]
            dws = [lax.dot_general(dob[h], vs[h], _dims("nt"), preferred_element_type=F32) for h in nh]
            zs, _, ws, runs = _sb_span(qs, ks, mask, runs, tri_su)
            dvs = [lax.dot_general(ws[h].astype(BF16), dob[h], _dims("tn"), preferred_element_type=F32) for h in nh]
            for h in nh:
                dl = dws[h] * ws[h]
                for b in range(nb):
                    dl_pan[h, jblk + b] = dl[:, b * SB_BLK:(b + 1) * SB_BLK]
                    z_pan[h, jblk + b] = zs[h][:, b * SB_BLK:(b + 1) * SB_BLK]
                dv_acc[rows, hs[h]] += dvs[h]
            return tuple(runs)

        runs = tuple(jnp.zeros((SB_BLK, 1), F32) for _ in nh)
        runs = span_a(i, 1, c < r, runs)
        runs = lax.fori_loop(0, quads, lambda p, rn: span_a(i - 4 - 4 * p, 4, None, rn), runs)
        runs = lax.fori_loop(0, pair, lambda _, rn: span_a(odd, 2, None, rn), runs)
        lax.fori_loop(0, odd, lambda _, rn: span_a(0, 1, None, rn), runs)

        def span_b(jblk, nb, mask, carry):
            rows = pl.ds(_blk_off(jblk), nb * SB_BLK)
            ks = [k_ref[0, rows, hs[h]].astype(BF16) for h in nh]
            units = [dl_pan[h, jblk + b] for h in nh for b in range(nb)]
            sgs = [_sigmoid(z_pan[h, jblk + b]) for h in nh for b in range(nb)]
            cums = _heads_cumsum(units, tri_pre)
            dzs, pres = [], []
            for h in nh:
                pre, parts = carry[h][1], []
                for b in range(nb):
                    u, sg = units[h * nb + b], sgs[h * nb + b]
                    parts.append(u * (1.0 - sg) - sg * (cums[h * nb + b] + pre))
                    pre = pre + jnp.sum(u, axis=1, keepdims=True)
                dz = jnp.concatenate(parts, axis=1) if nb > 1 else parts[0]
                if mask is not None:
                    dz = jnp.where(mask, dz, 0.0)
                dzs.append(dz.astype(BF16))
                pres.append(pre)
            dqs = [lax.dot_general(dzs[h], ks[h], _dims("nn"), preferred_element_type=F32) for h in nh]
            dks = [lax.dot_general(dzs[h], qs[h], _dims("tn"), preferred_element_type=F32) for h in nh]
            for h in nh:
                dk_acc[rows, hs[h]] += dks[h]
            return tuple((carry[h][0] + dqs[h], pres[h]) for h in nh)

        carry = tuple((jnp.zeros((SB_BLK, HD), F32), jnp.zeros((SB_BLK, 1), F32)) for _ in nh)
        carry = lax.fori_loop(0, odd, lambda _, cr: span_b(0, 1, None, cr), carry)
        carry = lax.fori_loop(0, pair, lambda _, cr: span_b(odd, 2, None, cr), carry)
        carry = lax.fori_loop(0, quads, lambda p, cr: span_b(rem + 4 * p, 4, None, cr), carry)
        carry = span_b(i, 1, c < r, carry)
        for h in nh:
            dq_ref[0, :, hs[h]] = (carry[h][0] * scale).astype(dq_ref.dtype)

        @pl.when(i == nblk - 1)
        def _():
            dk_ref[0] = dk_acc[...].astype(dk_ref.dtype)
            dv_ref[0] = dv_acc[...].astype(dv_ref.dtype)

        @pl.when((pl.program_id(0) == bsz - 1) & (pl.program_id(1) == ng - 1) & (i == nblk - 1))
        def _():
            for cp in copies:
                cp.wait_recv()
            for cp in copies:
                cp.wait_send()

    out = jax.ShapeDtypeStruct((bsz, s, CFG.sbh * HD), BF16)
    hbm = pl.BlockSpec(memory_space=pl.ANY)
    return pl.pallas_call(
        body, name="sb_bwd", grid=(bsz, ng, nblk),
        in_specs=[qb(0), full(ng), full(2 * ng), qb(0), hbm],
        out_specs=[qb(0), full(0), full(0), hbm],
        out_shape=[out, out, out, jax.ShapeDtypeStruct((3,) + s1.shape[1:], s1.dtype)],
        scratch_shapes=[pltpu.VMEM((s, w), F32), pltpu.VMEM((s, w), F32),
                        pltpu.VMEM((SB_HP, nblk, SB_BLK, SB_BLK), F32), pltpu.VMEM((SB_HP, nblk, SB_BLK, SB_BLK), F32),
                        pltpu.SemaphoreType.DMA((3,)), pltpu.SemaphoreType.DMA((3,))],
        compiler_params=_cp(("arbitrary", "arbitrary", "arbitrary")),
    )(qkv, qkv, qkv, do, s1)


def _xattn_fn(q_raw, kv, qn, kn):
    d = q_raw.shape[1]
    dh = d // CFG.xh
    outs = []
    for h in range(CFG.xh):
        qh = _rms(q_raw[:, h * dh:(h + 1) * dh], qn)
        kh = _rms(kv[:, h * dh:(h + 1) * dh], kn)
        vh = kv[:, d + h * dh:d + (h + 1) * dh]
        sc = _bdot(qh, kh, "nt") * (dh ** -0.5)
        sc = sc - lax.stop_gradient(jnp.max(sc, axis=-1, keepdims=True))
        e = jnp.exp(sc)
        p = e / jnp.sum(e, axis=-1, keepdims=True)
        outs.append(_bdot(p, vh, "nn"))
    return jnp.concatenate(outs, axis=1)


def _xattn_fwd(q_raw, kv, qn, kn):
    bsz, s, d = q_raw.shape
    m = kv.shape[1]
    tq = _tile(s, (XQ_TILE, 128))

    def body(q_ref, kv_ref, qn_ref, kn_ref, o_ref):
        o_ref[0] = _xattn_fn(q_ref[0], kv_ref[0], qn_ref[...], kn_ref[...]).astype(o_ref.dtype)

    return pl.pallas_call(
        body, name="xattn_fwd", grid=(bsz, s // tq),
        in_specs=[pl.BlockSpec((1, tq, d), lambda b, i: (b, i, 0)), pl.BlockSpec((1, m, 2 * d), lambda b, i: (b, 0, 0)),
                  pl.BlockSpec(qn.shape, lambda b, i: (0, 0)), pl.BlockSpec(kn.shape, lambda b, i: (0, 0))],
        out_specs=pl.BlockSpec((1, tq, d), lambda b, i: (b, i, 0)),
        out_shape=jax.ShapeDtypeStruct((bsz, s, d), BF16),
        compiler_params=_cp(("parallel", "parallel")),
    )(q_raw, kv, qn, kn)


def _xattn_bwd(q_raw, kv, qn, kn, do):
    bsz, s, d = q_raw.shape
    m = kv.shape[1]
    tq = _tile(s, (XQ_TILE, 128))

    def body(q_ref, kv_ref, qn_ref, kn_ref, do_ref, dq_ref, dkv_ref, dqn_ref, dkn_ref):
        b, i = pl.program_id(0), pl.program_id(1)

        @pl.when((b == 0) & (i == 0))
        def _():
            dqn_ref[...] = jnp.zeros_like(dqn_ref)
            dkn_ref[...] = jnp.zeros_like(dkn_ref)

        @pl.when(i == 0)
        def _():
            dkv_ref[...] = jnp.zeros_like(dkv_ref)

        _, f = jax.vjp(_xattn_fn, q_ref[0], kv_ref[0], qn_ref[...], kn_ref[...])
        dq, dkv, dqn, dkn = f(do_ref[0].astype(F32))
        dq_ref[0] = dq.astype(dq_ref.dtype)
        dkv_ref[0] += dkv
        dqn_ref[...] += dqn
        dkn_ref[...] += dkn

    return pl.pallas_call(
        body, name="xattn_bwd", grid=(bsz, s // tq),
        in_specs=[pl.BlockSpec((1, tq, d), lambda b, i: (b, i, 0)), pl.BlockSpec((1, m, 2 * d), lambda b, i: (b, 0, 0)),
                  pl.BlockSpec(qn.shape, lambda b, i: (0, 0)), pl.BlockSpec(kn.shape, lambda b, i: (0, 0)),
                  pl.BlockSpec((1, tq, d), lambda b, i: (b, i, 0))],
        out_specs=[pl.BlockSpec((1, tq, d), lambda b, i: (b, i, 0)), pl.BlockSpec((1, m, 2 * d), lambda b, i: (b, 0, 0)),
                   pl.BlockSpec(qn.shape, lambda b, i: (0, 0)), pl.BlockSpec(kn.shape, lambda b, i: (0, 0))],
        out_shape=[jax.ShapeDtypeStruct((bsz, s, d), BF16), jax.ShapeDtypeStruct(kv.shape, F32),
                   jax.ShapeDtypeStruct(qn.shape, F32), jax.ShapeDtypeStruct(kn.shape, F32)],
        compiler_params=_cp(("arbitrary", "arbitrary")),
    )(q_raw, kv, qn, kn, do)


def _my_pos():
    return lax.axis_index("x"), lax.axis_index("y"), lax.axis_index("c")


def _gather_stages(x_ref, out_ref, send_sems, recv_sems, local_sem):
    x, y, c = _my_pos()
    me, sibling = (x, y, c), (x, y, 1 - c)
    chips = [(1 - x, y), (x, 1 - y), (1 - x, 1 - y)]

    def slot(px, py, pc):
        return out_ref.at[4 * px + 2 * py + pc]

    def copy(k, block, to, src=None):
        return pltpu.make_async_remote_copy(
            src_ref=slot(*block) if src is None else src, dst_ref=slot(*block),
            send_sem=send_sems.at[k], recv_sem=recv_sems.at[k], device_id=to, device_id_type=MESH)

    mine = pltpu.make_async_copy(x_ref, slot(*me), local_sem)
    first = [copy(0, me, sibling, src=x_ref)]
    first += [copy(1 + j, me, (*chip, c), src=x_ref) for j, chip in enumerate(chips)]
    passed = [copy(4 + j, (*chip, c), sibling) for j, chip in enumerate(chips)]

    def begin():
        mine.start()
        for cp in first:
            cp.start()

    def relay():
        for j, chip in enumerate(chips):
            copy(1 + j, (*chip, c), me).wait_recv()
            passed[j].start()

    def finish():
        copy(0, sibling, me).wait_recv()
        for j, chip in enumerate(chips):
            copy(4 + j, (*chip, 1 - c), me).wait_recv()
        for cp in first + passed:
            cp.wait_send()
        mine.wait()

    return begin, relay, finish


_GATHER_SEMS = [pltpu.SemaphoreType.DMA((7,)), pltpu.SemaphoreType.DMA((7,)), pltpu.SemaphoreType.DMA]


def _all_gather_big(shard, name):
    r, d = shard.shape

    def body(x_ref, out_ref, send_sems, recv_sems, local_sem):
        begin, relay, finish = _gather_stages(x_ref, out_ref, send_sems, recv_sems, local_sem)
        begin()
        relay()
        finish()

    return pl.pallas_call(
        body, name=name,
        out_shape=jax.ShapeDtypeStruct((N_DEV, r, d), shard.dtype),
        in_specs=[pl.BlockSpec(memory_space=pl.ANY)], out_specs=pl.BlockSpec(memory_space=pl.ANY),
        scratch_shapes=list(_GATHER_SEMS),
    )(shard)


def _exchange_sibling(g, name):
    _, r, d = g.shape

    def body(g_ref, out_ref, send_sems, recv_sems):
        x, y, c = _my_pos()
        copies = [pltpu.make_async_remote_copy(
            src_ref=g_ref.at[2 * k + (1 - c)], dst_ref=out_ref.at[k],
            send_sem=send_sems.at[k], recv_sem=recv_sems.at[k], device_id=(x, y, 1 - c), device_id_type=MESH)
            for k in range(4)]
        for cp in copies:
            cp.start()
        for cp in copies:
            cp.wait_recv()
        for cp in copies:
            cp.wait_send()

    return pl.pallas_call(
        body, name=name,
        out_shape=jax.ShapeDtypeStruct((4, r, d), g.dtype),
        in_specs=[pl.BlockSpec(memory_space=pl.ANY)], out_specs=pl.BlockSpec(memory_space=pl.ANY),
        scratch_shapes=[pltpu.SemaphoreType.DMA((4,)), pltpu.SemaphoreType.DMA((4,))],
    )(g)


def _chip_copies(s_ref, out_ref, send_sems, recv_sems):
    x, y, c = _my_pos()
    copies = []
    for rel in (1, 2, 3):
        px = jnp.bitwise_xor(x, rel >> 1)
        py = jnp.bitwise_xor(y, rel & 1)
        copies.append(pltpu.make_async_remote_copy(
            src_ref=s_ref.at[2 * px + py], dst_ref=out_ref.at[rel - 1],
            send_sem=send_sems.at[rel - 1], recv_sem=recv_sems.at[rel - 1],
            device_id=(px, py, c), device_id_type=MESH))
    return copies


def _all_reduce_small(blk, name):
    rows, d = blk.shape

    def body(x_ref, out_ref, land, send_sems, recv_sems):
        x, y, c = _my_pos()
        me = 4 * x + 2 * y + c
        copies = []
        for rel in range(1, N_DEV):
            peer = (jnp.bitwise_xor(x, rel >> 2), jnp.bitwise_xor(y, (rel >> 1) & 1), jnp.bitwise_xor(c, rel & 1))
            copies.append(pltpu.make_async_remote_copy(
                src_ref=x_ref, dst_ref=land.at[rel - 1], send_sem=send_sems.at[rel - 1], recv_sem=recv_sems.at[rel - 1],
                device_id=peer, device_id_type=MESH))
        for cp in copies:
            cp.start()
        for cp in copies:
            cp.wait_recv()
        acc = jnp.zeros((rows, d), F32)
        for dev in range(N_DEV):
            rel = jnp.bitwise_xor(me, dev)
            got = land[jnp.maximum(rel - 1, 0)]
            acc = acc + jnp.where(rel == 0, x_ref[...], got)
        out_ref[...] = acc
        for cp in copies:
            cp.wait_send()

    return pl.pallas_call(
        body, name=name,
        out_shape=jax.ShapeDtypeStruct((rows, d), F32),
        in_specs=[pl.BlockSpec(memory_space=pltpu.VMEM)], out_specs=pl.BlockSpec(memory_space=pltpu.VMEM),
        scratch_shapes=[pltpu.VMEM((N_DEV - 1, rows, d), F32), pltpu.SemaphoreType.DMA((N_DEV - 1,)),
                        pltpu.SemaphoreType.DMA((N_DEV - 1,))],
    )(blk)


def _dxn_fused(parts, s1):
    t, d = parts[0][0].shape[0], parts[0][1].shape[1]
    tm = _tile(t, MM_TILES)
    nm = t // tm
    segs, k0 = [], 0
    for a, _ in parts:
        tk = _tile(a.shape[1], (512, 256, 128))
        segs.append((k0, a.shape[1] // tk, tk))
        k0 += a.shape[1] // tk
    ktot = k0
    npart = len(parts)

    def body(*refs):
        ab = refs[:2 * npart]
        s1_ref, o_ref, r2_ref, acc, send_sems, recv_sems = refs[2 * npart:]
        i, k = pl.program_id(0), pl.program_id(1)
        copies = _chip_copies(s1_ref, r2_ref, send_sems, recv_sems)

        @pl.when((i == 0) & (k == 0))
        def _():
            for cp in copies:
                cp.start()

        @pl.when(k == 0)
        def _():
            acc[...] = jnp.zeros_like(acc)

        for p, (p0, nk, _) in enumerate(segs):
            @pl.when((k >= p0) & (k < p0 + nk))
            def _(p=p):
                acc[...] += lax.dot_general(ab[2 * p][...], ab[2 * p + 1][...], _dims("nn"), preferred_element_type=F32)

        @pl.when(k == ktot - 1)
        def _():
            o_ref[...] = acc[...]

        @pl.when((i == nm - 1) & (k == ktot - 1))
        def _():
            for cp in copies:
                cp.wait_recv()
            for cp in copies:
                cp.wait_send()

    in_specs, args = [], []
    for (a, b), (p0, nk, tk) in zip(parts, segs):
        def kk(k, p0=p0, nk=nk):
            return jnp.clip(k - p0, 0, nk - 1)
        in_specs.append(pl.BlockSpec((tm, tk), lambda i, k, kk=kk: (i, kk(k))))
        in_specs.append(pl.BlockSpec((tk, d), lambda i, k, kk=kk: (kk(k), 0)))
        args += [a, b]
    hbm = pl.BlockSpec(memory_space=pl.ANY)
    return pl.pallas_call(
        body, name="d_xn", grid=(nm, ktot),
        in_specs=in_specs + [hbm], out_specs=[pl.BlockSpec((tm, d), lambda i, k: (i, 0)), hbm],
        out_shape=[jax.ShapeDtypeStruct((t, d), F32), jax.ShapeDtypeStruct((3,) + s1.shape[1:], s1.dtype)],
        scratch_shapes=[pltpu.VMEM((tm, d), F32), pltpu.SemaphoreType.DMA((3,)), pltpu.SemaphoreType.DMA((3,))],
        compiler_params=_cp(("arbitrary", "arbitrary")),
    )(*args, s1)


def _cast_rows(x, dtype, name):
    return _rowwise(lambda v: v, [x], [], [(x.shape[1], dtype)], [], name=name, tm=CFG.pack_tile)[0]


def _sum_sibling(g, recv1, c_idx, name):
    _, r, d = g.shape
    tm = CFG.pack_tile

    def body(c_ref, g_ref, r_ref, o_ref):
        o_ref[0] = (g_ref[0] + r_ref[0]).astype(o_ref.dtype)

    grid_spec = pltpu.PrefetchScalarGridSpec(
        num_scalar_prefetch=1, grid=(4, r // tm),
        in_specs=[pl.BlockSpec((1, tm, d), lambda k, i, c_ref: (2 * k + c_ref[0], i, 0)),
                  pl.BlockSpec((1, tm, d), lambda k, i, c_ref: (k, i, 0))],
        out_specs=pl.BlockSpec((1, tm, d), lambda k, i, c_ref: (k, i, 0)))
    return pl.pallas_call(
        body, name=name, grid_spec=grid_spec,
        out_shape=jax.ShapeDtypeStruct((4, r, d), BF16),
        compiler_params=_cp(("parallel", "parallel")),
    )(c_idx, g, recv1)


def _adamw_math(w, g, m, v):
    m2 = ADAM_B1 * m + (1.0 - ADAM_B1) * g
    v2 = ADAM_B2 * v + (1.0 - ADAM_B2) * (g * g)
    m_hat = m2 / (1.0 - ADAM_B1 ** ADAM_STEP)
    v_hat = v2 / (1.0 - ADAM_B2 ** ADAM_STEP)
    delta = -ADAM_LR * (m_hat / (jnp.sqrt(v_hat) + ADAM_EPS) + ADAM_WD * w)
    return delta, m2, v2


def _adamw_big(g, recv1, recv2, w, m, v, idx, row0, name):
    _, r, d = g.shape
    tm = CFG.pack_tile
    t0 = row0 // tm

    def body(idx_ref, g_ref, r1_ref, ra_ref, rb_ref, rc_ref, w_ref, m_ref, v_ref, og, od, om, ov):
        grad = (g_ref[0] + r1_ref[0]) + ra_ref[0].astype(F32) + rb_ref[0].astype(F32) + rc_ref[0].astype(F32)
        delta, m2, v2 = _adamw_math(w_ref[...], grad, m_ref[...], v_ref[...])
        og[...] = grad
        od[...] = delta
        om[...] = m2
        ov[...] = v2

    flat = pl.BlockSpec((tm, d), lambda i, idx_ref: (i, 0))
    shifted = pl.BlockSpec((tm, d), lambda i, idx_ref: (i + t0, 0))
    grid_spec = pltpu.PrefetchScalarGridSpec(
        num_scalar_prefetch=1, grid=(r // tm,),
        in_specs=[pl.BlockSpec((1, tm, d), lambda i, idx_ref: (idx_ref[0], i, 0)),
                  pl.BlockSpec((1, tm, d), lambda i, idx_ref: (idx_ref[1], i, 0)),
                  pl.BlockSpec((1, tm, d), lambda i, idx_ref: (0, i, 0)),
                  pl.BlockSpec((1, tm, d), lambda i, idx_ref: (1, i, 0)),
                  pl.BlockSpec((1, tm, d), lambda i, idx_ref: (2, i, 0)),
                  shifted, shifted, shifted],
        out_specs=[flat, flat, flat, flat])
    shp = jax.ShapeDtypeStruct((r, d), F32)
    return pl.pallas_call(
        body, name=name, grid_spec=grid_spec, out_shape=[shp, shp, shp, shp],
        compiler_params=_cp(("parallel",)),
    )(idx, g, recv1, recv2, recv2, recv2, w, m, v)


def _rows_of(v, d):
    flat = v.reshape(-1)
    rows = -(-flat.shape[0] // d)
    rows += (-rows) % SUBLANE
    return jnp.pad(flat, (0, rows * d - flat.shape[0])).reshape(rows, d)


def _pad_rows(a, mult):
    pad = (-a.shape[0]) % mult
    if pad:
        a = jnp.pad(a, ((0, pad),) + ((0, 0),) * (a.ndim - 1))
    return a


_BIG = ("w_in", "w_xkv", "w_up", "w_proj_gdn", "w_proj_sb", "w_out", "w_xq", "w_xo", "w_down")
_COL_SHARDED = ("w_in", "w_xkv", "w_up")
_SMALL_REP = ("norm_mix", "norm_x", "norm_mem", "norm_ffn", "a_log", "dt_bias", "gdn_out_norm", "xq_norm", "xk_norm")
_SMALL_CONV = ("conv_gdn", "conv_ffn")


def _part_rows(shapes):
    out = []
    for n in _BIG:
        rows, cols = shapes[n]
        cnt = cols if n in _COL_SHARDED else rows
        out.append((cnt, cnt + (-cnt) % (CFG.pack_tile if n == _BIG[0] else PACK_ROW_ALIGN)))
    return out


def _pack_big_shards(shards, shapes):
    parts = []
    for n, (_, padded) in zip(_BIG, _part_rows(shapes)):
        parts.append(_pad_rows(shards[n].T if n in _COL_SHARDED else shards[n], padded))
    return _pad_rows(jnp.concatenate(parts, axis=0), CFG.pack_tile)


def _unpack_gathered(gath, shapes, names):
    out, r0 = {}, 0
    for n, (cnt, padded) in zip(_BIG, _part_rows(shapes)):
        if n not in names:
            continue
        out[n] = gath[:, r0:r0 + cnt, :].reshape(N_DEV * cnt, gath.shape[2])
        r0 += padded
    return out


def _pack_full_grads(grads, shapes, names):
    d = CFG.d
    parts = []
    for n, (cnt, padded) in zip(_BIG, _part_rows(shapes)):
        if n not in names:
            continue
        g = grads[n].reshape(N_DEV, cnt, d)
        if padded > cnt:
            g = jnp.pad(g, ((0, 0), (0, padded - cnt), (0, 0)))
        parts.append(g)
    full = jnp.concatenate(parts, axis=1)
    pad = (-full.shape[1]) % CFG.pack_tile
    if pad:
        full = jnp.pad(full, ((0, 0), (0, pad), (0, 0)))
    return full


def _unpack_shard(packed, shapes, names):
    out, r0 = {}, 0
    for n, (cnt, padded) in zip(_BIG, _part_rows(shapes)):
        if n not in names:
            continue
        part = packed[r0:r0 + cnt]
        out[n] = (part.T if n in _COL_SHARDED else part).reshape((1,) + tuple(shapes[n]))
        r0 += padded
    return out


def kernel(x, mem, norm_mix, w_in, conv_gdn, a_log, dt_bias, gdn_out_norm, w_proj_gdn, w_proj_sb, w_out, norm_x, norm_mem, w_xq, w_xkv, xq_norm, xk_norm, w_xo, norm_ffn, w_up, conv_ffn, w_down, loss_target, m_norm_mix, m_w_in, m_conv_gdn, m_a_log, m_dt_bias, m_gdn_out_norm, m_w_proj_gdn, m_w_proj_sb, m_w_out, m_norm_x, m_norm_mem, m_w_xq, m_w_xkv, m_xq_norm, m_xk_norm, m_w_xo, m_norm_ffn, m_w_up, m_conv_ffn, m_w_down, v_norm_mix, v_w_in, v_conv_gdn, v_a_log, v_dt_bias, v_gdn_out_norm, v_w_proj_gdn, v_w_proj_sb, v_w_out, v_norm_x, v_norm_mem, v_w_xq, v_w_xkv, v_xq_norm, v_xk_norm, v_w_xo, v_norm_ffn, v_w_up, v_conv_ffn, v_w_down):
    names = ("norm_mix", "w_in", "conv_gdn", "a_log", "dt_bias", "gdn_out_norm", "w_proj_gdn", "w_proj_sb", "w_out",
             "norm_x", "norm_mem", "w_xq", "w_xkv", "xq_norm", "xk_norm", "w_xo", "norm_ffn", "w_up", "conv_ffn", "w_down")
    wts = dict(zip(names, (norm_mix, w_in, conv_gdn, a_log, dt_bias, gdn_out_norm, w_proj_gdn, w_proj_sb, w_out,
                           norm_x, norm_mem, w_xq, w_xkv, xq_norm, xk_norm, w_xo, norm_ffn, w_up, conv_ffn, w_down)))
    mom = dict(zip(names, (m_norm_mix, m_w_in, m_conv_gdn, m_a_log, m_dt_bias, m_gdn_out_norm, m_w_proj_gdn, m_w_proj_sb,
                           m_w_out, m_norm_x, m_norm_mem, m_w_xq, m_w_xkv, m_xq_norm, m_xk_norm, m_w_xo, m_norm_ffn, m_w_up,
                           m_conv_ffn, m_w_down)))
    vel = dict(zip(names, (v_norm_mix, v_w_in, v_conv_gdn, v_a_log, v_dt_bias, v_gdn_out_norm, v_w_proj_gdn, v_w_proj_sb,
                           v_w_out, v_norm_x, v_norm_mem, v_w_xq, v_w_xkv, v_xq_norm, v_xk_norm, v_w_xo, v_norm_ffn, v_w_up,
                           v_conv_ffn, v_w_down)))
    cfg = CFG
    d, bsz, s = cfg.d, cfg.b, cfg.s
    t = bsz * s
    gh, sbh = cfg.gh, cfg.sbh
    gw, sw = gh * HD, sbh * HD
    nchunk = s // cfg.gch
    mx, my, mc = _my_pos()
    me = 4 * mx + 2 * my + mc

    shard_shapes = {n: tuple(wts[n].shape[1:]) for n in _BIG}

    packed_w = _pack_big_shards({n: wts[n][0] for n in _BIG}, shard_shapes)
    packed_wb = _cast_rows(packed_w, BF16, "cast_weights")
    rows_in = _part_rows(shard_shapes)[0][1]
    full = _unpack_gathered(_all_gather_big(packed_wb[:rows_in], "all_gather_w_in"), shard_shapes, _BIG[:1])

    conv_rows = {n: _rows_of(wts[n][0], d) for n in _SMALL_CONV}
    conv_cnt = {n: conv_rows[n].shape[0] for n in _SMALL_CONV}
    conv_blk = _pad_rows(jnp.concatenate([conv_rows[n] for n in _SMALL_CONV], axis=0), SUBLANE)
    conv_all = jnp.zeros((N_DEV,) + conv_blk.shape, F32)
    conv_all = lax.dynamic_update_slice(conv_all, conv_blk[None], (me, 0, 0))
    conv_all = _all_reduce_small(conv_all.reshape(-1, d), "gather_conv_taps").reshape((N_DEV,) + conv_blk.shape)

    def full_conv(n, r0):
        k, cols = wts[n].shape[1], wts[n].shape[2]
        part = conv_all[:, r0:r0 + conv_cnt[n], :].reshape(N_DEV, -1)[:, :k * cols].reshape(N_DEV, k, cols)
        return part.transpose(1, 0, 2).reshape(k, N_DEV * cols)

    cgdn = full_conv("conv_gdn", 0)
    cffn = full_conv("conv_ffn", conv_cnt["conv_gdn"])

    win = full["w_in"]
    o_ab = 3 * gw
    o_z = o_ab + 2 * gh
    o_sb = o_z + gw
    o_gate = o_sb + 3 * sw
    w_qkv = win[:o_ab]
    w_ab = jnp.concatenate([win[o_ab:o_z], jnp.zeros((LANE - 2 * gh, d), win.dtype)], axis=0)
    w_z = win[o_z:o_sb]
    w_sb = win[o_sb:o_gate]
    w_gate = win[o_gate:]

    alog_p = jnp.concatenate([a_log.reshape(1, -1), jnp.zeros((1, LANE - gh), F32)], axis=1)
    dtb_p = jnp.concatenate([dt_bias.reshape(1, -1), jnp.zeros((1, LANE - gh), F32)], axis=1)
    onorm = gdn_out_norm.reshape(1, HD)

    x2 = x.reshape(t, d)
    tgt2 = loss_target.reshape(t, d)
    mem2 = mem.reshape(bsz * cfg.mem, d)

    (xn,) = _rowwise(_rms, [x2], [norm_mix], [(d, BF16)], [], name="norm_mix_fwd")
    p_qkv = _mm(xn, w_qkv, tb=True, name="proj_qkv")
    p_ab = _mm(xn, w_ab, tb=True, name="proj_ab")
    p_z = _mm(xn, w_z, tb=True, name="proj_z")
    p_sb = _mm(xn, w_sb, tb=True, name="proj_sb")
    p_gate = _mm(xn, w_gate, tb=True, name="proj_gate")

    qkv_c = _gdn_conv_fwd(p_qkv.reshape(bsz, s, 3 * gw), cgdn)
    (gb,) = _rowwise(_gates_fn, [p_ab], [alog_p, dtb_p], [(LANE, F32)], [], name="gdn_gates_fwd")
    gbt = gb.reshape(bsz, s, LANE)[:, :, :2 * gh].transpose(0, 2, 1).reshape(bsz, 2 * gh, nchunk, 1, cfg.gch)
    o_a, states = _gdn_fwd(qkv_c, p_z.reshape(bsz, s, gw), gbt, onorm)
    o_b, gathered = _sb_fwd(p_sb.reshape(bsz, s, 3 * sw), packed_wb[rows_in:])
    full.update(_unpack_gathered(gathered, shard_shapes, _BIG[1:]))

    pa = _mm(o_a.reshape(t, gw), full["w_proj_gdn"], name="proj_gdn_out")
    pb = _mm(o_b.reshape(t, sw), full["w_proj_sb"], name="proj_sb_out")

    def merge_fn(pa_, pb_, gate_):
        return _sigmoid(gate_[:, :d]) * pa_ + _sigmoid(gate_[:, d:]) * pb_

    (merged,) = _rowwise(merge_fn, [pa, pb, p_gate], [], [(d, BF16)], [], name="merge_fwd")
    def with_norm(r, g):
        return r, _rms(r, g)

    h1, hn_x = _mm(merged, full["w_out"], add=x2, name="mixer_out", epilogue=(with_norm, [], [norm_x], [F32, BF16], 0, True))
    (mn,) = _rowwise(_rms, [mem2], [norm_mem], [(d, BF16)], [], name="norm_mem_fwd")
    q_raw = _mm(hn_x, full["w_xq"], name="xattn_q")
    kv = _mm(mn, full["w_xkv"], tb=True, name="xattn_kv")
    xo = _xattn_fwd(q_raw.reshape(bsz, s, d), kv.reshape(bsz, cfg.mem, 2 * d), xq_norm, xk_norm)
    h2, hn_f = _mm(xo.reshape(t, d), full["w_xo"], add=h1, name="xattn_out",
                   epilogue=(with_norm, [], [norm_ffn], [F32, BF16], 0, True))
    up = _mm(hn_f, full["w_up"], tb=True, name="ffn_up")
    act = _ffn_conv_fwd(up.reshape(bsz, s, 2 * cfg.dff), cffn)
    def loss_fn(y_, tg_):
        err = y_ - tg_
        part = 0.5 * jnp.sum(err * err) / d
        return err / d, err / d, jnp.full((1, LANE), part, F32)

    dy, dy_b, loss_part = _mm(act.reshape(t, cfg.dff), full["w_down"], add=h2, name="ffn_down_loss",
                              epilogue=(loss_fn, [tgt2], [], [F32, BF16], 1, False))

    grads = {}
    dact = _mm(dy_b, full["w_down"], tb=True, name="d_act")
    grads["w_down"] = _mm(act.reshape(t, cfg.dff), dy_b, ta=True, name="dw_down")
    dup1, dup2, dcf1, dcf2 = _ffn_conv_bwd(up.reshape(bsz, s, 2 * cfg.dff), cffn, dact.reshape(bsz, s, cfg.dff))
    dup = jnp.concatenate([dup1, dup2], axis=2).reshape(t, 2 * cfg.dff)
    g_conv_ffn = jnp.concatenate([dcf1, dcf2], axis=1)
    dhn_f = _mm(dup, full["w_up"], name="d_hn_ffn")
    grads["w_up"] = _mm(dup, hn_f, ta=True, name="dw_up")

    def norm_bwd_fn(h_, res_, dn_, g_):
        _, f = jax.vjp(_rms, h_, g_)
        dh, dg = f(dn_)
        return res_ + dh, dg

    def norm_bwd_copy_fn(h_, res_, dn_, g_):
        dres, dg = norm_bwd_fn(h_, res_, dn_, g_)
        return dres, dres, dg

    dh2, dh2_b, g_norm_ffn = _rowwise(norm_bwd_copy_fn, [h2, dy, dhn_f], [norm_ffn], [(d, F32), (d, BF16)], [(1, d)],
                                      name="norm_ffn_bwd")

    dxo = _mm(dh2_b, full["w_xo"], tb=True, out_dtype=BF16, name="d_xo")
    grads["w_xo"] = _mm(xo.reshape(t, d), dh2_b, ta=True, name="dw_xo")
    dq_raw, dkv, g_xq_norm, g_xk_norm = _xattn_bwd(q_raw.reshape(bsz, s, d), kv.reshape(bsz, cfg.mem, 2 * d),
                                                   xq_norm, xk_norm, dxo.reshape(bsz, s, d))
    dq_raw2 = dq_raw.reshape(t, d)
    dkv2 = dkv.reshape(bsz * cfg.mem, 2 * d)
    dhn_x = _mm(dq_raw2, full["w_xq"], tb=True, name="d_hn_x")
    grads["w_xq"] = _mm(hn_x, dq_raw2, ta=True, name="dw_xq")
    dmn = _mm(dkv2, full["w_xkv"], name="d_mn")
    grads["w_xkv"] = _mm(dkv2, mn, ta=True, name="dw_xkv")

    def norm_w_bwd_fn(h_, dn_, g_):
        _, f = jax.vjp(lambda gg: _rms(h_, gg), g_)
        return f(dn_)[0]

    (g_norm_mem,) = _rowwise(norm_w_bwd_fn, [mem2, dmn], [norm_mem], [], [(1, d)], name="norm_mem_bwd")
    dh1, dh1_b, g_norm_x = _rowwise(norm_bwd_copy_fn, [h1, dh2, dhn_x], [norm_x], [(d, F32), (d, BF16)], [(1, d)],
                                    name="norm_x_bwd")

    dmerged = _mm(dh1_b, full["w_out"], tb=True, name="d_merged")
    grads["w_out"] = _mm(merged, dh1_b, ta=True, name="dw_out")

    def merge_bwd_fn(pa_, pb_, gate_, dm_):
        _, f = jax.vjp(merge_fn, pa_, pb_, gate_)
        return f(dm_)

    dpa, dpb, dgate = _rowwise(merge_bwd_fn, [pa, pb, p_gate, dmerged], [], [(d, BF16), (d, BF16), (2 * d, BF16)], [],
                               name="merge_bwd")
    do_a = _mm(dpa, full["w_proj_gdn"], tb=True, name="d_o_gdn")
    grads["w_proj_gdn"] = _mm(o_a.reshape(t, gw), dpa, ta=True, name="dw_proj_gdn")
    do_b = _mm(dpb, full["w_proj_sb"], tb=True, name="d_o_sb")
    grads["w_proj_sb"] = _mm(o_b.reshape(t, sw), dpb, ta=True, name="dw_proj_sb")

    c_idx = jnp.reshape(mc, (1,)).astype(jnp.int32)
    early = _BIG[1:]
    g_early = _pack_full_grads(grads, shard_shapes, early)
    r1_early = _exchange_sibling(g_early, "grads_to_sibling_early")
    s1_early = _sum_sibling(g_early, r1_early, c_idx, "sum_sibling_early")
    dsq, dsk, dsv, r2_early = _sb_bwd(p_sb.reshape(bsz, s, 3 * sw), do_b.reshape(bsz, s, sw), s1_early)
    dp_sb = jnp.concatenate([dsq, dsk, dsv], axis=2).reshape(t, 3 * sw)

    dgq, dgk, dgv, dz, dg, dbeta, g_onorm = _gdn_bwd(qkv_c, p_z.reshape(bsz, s, gw), gbt, onorm, states,
                                                     do_a.reshape(bsz, s, gw))
    dgb = jnp.concatenate([dg, dbeta], axis=1).reshape(bsz, 2 * gh, s).transpose(0, 2, 1)
    dgb = jnp.concatenate([dgb, jnp.zeros((bsz, s, LANE - 2 * gh), F32)], axis=2).reshape(t, LANE)

    def gates_bwd_fn(ab_, dgb_, alog_, dtb_):
        _, f = jax.vjp(_gates_fn, ab_, alog_, dtb_)
        return f(dgb_)

    dp_ab, g_alog, g_dtb = _rowwise(gates_bwd_fn, [p_ab, dgb], [alog_p, dtb_p], [(LANE, BF16)], [(1, LANE), (1, LANE)],
                                    name="gdn_gates_bwd")
    dqkv_c = jnp.concatenate([dgq, dgk, dgv], axis=2)
    dp_qkv, g_conv_gdn = _gdn_conv_bwd(p_qkv.reshape(bsz, s, 3 * gw), cgdn, dqkv_c)
    dp_qkv = dp_qkv.reshape(t, 3 * gw)
    dp_z = dz.reshape(t, gw)

    grads["w_in"] = jnp.concatenate([
        _mm(dp_qkv, xn, ta=True, name="dw_in_qkv"),
        _mm(dp_ab, xn, ta=True, name="dw_in_ab")[:2 * gh],
        _mm(dp_z, xn, ta=True, name="dw_in_z"),
        _mm(dp_sb, xn, ta=True, name="dw_in_sb"),
        _mm(dgate, xn, ta=True, name="dw_in_gate")], axis=0)
    g_late = _pack_full_grads(grads, shard_shapes, _BIG[:1])
    r1_late = _exchange_sibling(g_late, "grads_to_sibling_late")
    s1_late = _sum_sibling(g_late, r1_late, c_idx, "sum_sibling_late")
    dxn, r2_late = _dxn_fused([(dp_qkv, w_qkv), (dp_ab, w_ab), (dp_z, w_z), (dp_sb, w_sb), (dgate, w_gate)], s1_late)
    grad_x, g_norm_mix = _rowwise(norm_bwd_fn, [x2, dh1, dxn], [norm_mix], [(d, F32)], [(1, d)], name="norm_mix_bwd")

    small_g = {"norm_mix": g_norm_mix, "norm_x": g_norm_x, "norm_mem": g_norm_mem, "norm_ffn": g_norm_ffn,
               "a_log": g_alog[:, :gh], "dt_bias": g_dtb[:, :gh], "gdn_out_norm": g_onorm,
               "xq_norm": g_xq_norm, "xk_norm": g_xk_norm}
    sm_rows = [_rows_of(small_g[n], d) for n in _SMALL_REP] + [_rows_of(loss_part, d)]
    sm_rows += [_rows_of(g_conv_gdn, d), _rows_of(g_conv_ffn, d)]
    sm_cnt = [r.shape[0] for r in sm_rows]
    sm_sum = _all_reduce_small(_pad_rows(jnp.concatenate(sm_rows, axis=0), SUBLANE), "all_reduce_small_grads")
    offs = [0]
    for cnt in sm_cnt:
        offs.append(offs[-1] + cnt)
    small_grad = {}
    for i, n in enumerate(_SMALL_REP):
        small_grad[n] = sm_sum[offs[i]:offs[i + 1]].reshape(-1)[:wts[n].size].reshape(wts[n].shape)
    loss = sm_sum[offs[len(_SMALL_REP)], 0]
    for i, n in enumerate(_SMALL_CONV):
        k, cols = wts[n].shape[1], wts[n].shape[2]
        o = offs[len(_SMALL_REP) + 1 + i]
        fullg = sm_sum[o:o + sm_cnt[len(_SMALL_REP) + 1 + i]].reshape(-1)[:k * cols * N_DEV].reshape(k, N_DEV * cols)
        small_grad[n] = lax.dynamic_slice(fullg, (0, me * cols), (k, cols)).reshape(wts[n].shape)

    small_names = _SMALL_REP + _SMALL_CONV

    def pack_small(src):
        return _pad_rows(jnp.concatenate([_rows_of(src[n], d) for n in small_names], axis=0), SUBLANE)

    sw_, sg_, sm_, sv_ = pack_small(wts), pack_small(small_grad), pack_small(mom), pack_small(vel)
    sd_, snm_, snv_ = _rowwise(_adamw_math, [sw_, sg_, sm_, sv_], [], [(d, F32)] * 3, [], name="adamw_small", tm=sw_.shape[0])

    def unpack_small(packed):
        out, r0 = {}, 0
        for n in small_names:
            cnt = _rows_of(wts[n], d).shape[0]
            out[n] = packed[r0:r0 + cnt].reshape(-1)[:wts[n].size].reshape(wts[n].shape)
            r0 += cnt
        return out

    small_delta, small_m, small_v = unpack_small(sd_), unpack_small(snm_), unpack_small(snv_)

    idx = jnp.stack([me, 2 * mx + my]).astype(jnp.int32)
    pm = _pack_big_shards({n: mom[n][0] for n in _BIG}, shard_shapes)
    pv = _pack_big_shards({n: vel[n][0] for n in _BIG}, shard_shapes)
    upd_late = _adamw_big(g_late, r1_late, r2_late, packed_w, pm, pv, idx, 0, "adamw_late")
    upd_early = _adamw_big(g_early, r1_early, r2_early, packed_w, pm, pv, idx, g_late.shape[1], "adamw_early")
    big_grad, big_delta, big_m, big_v = (
        {**_unpack_shard(a, shard_shapes, _BIG[:1]), **_unpack_shard(b, shard_shapes, early)}
        for a, b in zip(upd_late, upd_early))

    def pick(big, small, n):
        return big[n] if n in big else small[n]

    outs = [loss, grad_x.reshape(bsz, s, d)]
    outs += [pick(big_grad, small_grad, n) for n in names]
    outs += [pick(big_delta, small_delta, n) for n in names]
    outs += [pick(big_m, small_m, n) for n in names]
    outs += [pick(big_v, small_v, n) for n in names]
    return tuple(outs)
```

```python
import functools

import jax
import jax.numpy as jnp
from jax import lax
from jax.experimental import pallas as pl
from jax.experimental.pallas import tpu as pltpu

F32 = jnp.float32
BF16 = jnp.bfloat16
FDOT_PRECISION = lax.Precision.HIGH

LANE = 128
SUBLANE = 8
PACK_ROW_ALIGN = 16
VMEM_LIMIT = 56 * 2 ** 20
N_DEV = 8
MESH = pl.DeviceIdType.MESH

EPS = 1e-6
ADAM_LR = 0.001
ADAM_B1 = 0.9
ADAM_B2 = 0.999
ADAM_EPS = 1e-08
ADAM_WD = 0.01
ADAM_STEP = 10


class _Cfg:
    d = 1024
    b = 4
    s = 2048
    mem = 256
    gh = 8
    gch = 64
    sbh = 8
    xh = 4
    dff = 2816
    pack_tile = 256


CFG = _Cfg()
HD = 128
SB_BLK = 128
SB_HP = 4
GDN_HP = 8
MM_TILES = (1024, 1408, 704, 512, 256, 128)
MM_K_TILES = (1024, 1408, 704, 512, 256, 128)
MM_K_TILES_F32 = (512, 704, 256, 128)
CONV_CB = 256
XQ_TILE = 256


def _tile(n, prefs):
    for t in prefs:
        if n % t == 0:
            return t
    raise ValueError(f"no tile for {n}")


def _cp(sem, **kw):
    return pltpu.CompilerParams(dimension_semantics=sem, vmem_limit_bytes=VMEM_LIMIT, **kw)


def _dims(kind):
    return {"nn": (((1,), (0,)), ((), ())), "nt": (((1,), (1,)), ((), ())), "tn": (((0,), (0,)), ((), ()))}[kind]


def _raw_bdot(a, b, kind):
    return lax.dot_general(a.astype(BF16), b.astype(BF16), _dims(kind), preferred_element_type=F32)


def _raw_fdot(a, b, kind):
    return lax.dot_general(a.astype(F32), b.astype(F32), _dims(kind), precision=FDOT_PRECISION,
                           preferred_element_type=F32)


def _make_dot(raw):
    @functools.partial(jax.custom_vjp, nondiff_argnums=(2,))
    def dot(a, b, kind):
        return raw(a, b, kind)

    def fwd(a, b, kind):
        return raw(a, b, kind), (a, b)

    def bwd(kind, res, g):
        a, b = res
        if kind == "nn":
            return raw(g, b, "nt").astype(a.dtype), raw(a, g, "tn").astype(b.dtype)
        if kind == "nt":
            return raw(g, b, "nn").astype(a.dtype), raw(g, a, "tn").astype(b.dtype)
        return raw(b, g, "nt").astype(a.dtype), raw(a, g, "nn").astype(b.dtype)

    dot.defvjp(fwd, bwd)
    return dot


_bdot = _make_dot(_raw_bdot)
_fdot = _make_dot(_raw_fdot)


def _split_dot(x, m01):
    hi = x.astype(BF16)
    lo = (x - hi.astype(F32)).astype(BF16)
    return (lax.dot_general(hi, m01, _dims("nn"), preferred_element_type=F32)
            + lax.dot_general(lo, m01, _dims("nn"), preferred_element_type=F32))


_sigmoid = jax.nn.sigmoid


def _silu(x):
    return x * _sigmoid(x)


def _softplus(x):
    return jnp.maximum(x, 0.0) + jnp.log1p(jnp.exp(-jnp.abs(x)))


def _rms(x, g):
    return x * lax.rsqrt(jnp.mean(x * x, axis=-1, keepdims=True) + EPS) * g


def _iota2(shape, dim):
    return lax.broadcasted_iota(jnp.int32, shape, dim)


def _mm(a, b, *, ta=False, tb=False, add=None, out_dtype=F32, name, epilogue=None):
    if ta:
        kd, m = a.shape
    else:
        m, kd = a.shape
    if tb:
        n, kb = b.shape
    else:
        kb, n = b.shape
    assert kd == kb, (a.shape, b.shape, ta, tb)
    fn, e_rows, e_pars, e_dtypes, n_acc, full_rows = epilogue or (None, [], [], [out_dtype], 0, False)
    tm = _tile(m, MM_TILES)
    tn = n if full_rows else _tile(n, MM_TILES)
    wide = max(a.dtype.itemsize, b.dtype.itemsize) > 2
    tk = _tile(kd, MM_K_TILES_F32 if wide else MM_K_TILES)
    nk = kd // tk
    kind_dims = (((0 if ta else 1,), (1 if tb else 0,)), ((), ()))
    n_in = 2 + (add is not None) + len(e_rows) + len(e_pars)
    n_out = len(e_dtypes)

    def body(*refs):
        a_ref, b_ref = refs[:2]
        add_ref = refs[2] if add is not None else None
        extra = refs[2 + (add is not None):n_in]
        o_refs = refs[n_in:n_in + n_out]
        acc_refs = refs[n_in + n_out:n_in + n_out + n_acc]
        part = lax.dot_general(a_ref[...].astype(BF16), b_ref[...].astype(BF16), kind_dims,
                               preferred_element_type=F32)

        def finish(r):
            if add is not None:
                r = r + add_ref[...].astype(F32)
            outs = (r,) if fn is None else fn(r, *[e[...] for e in extra])
            for o_ref, val in zip(o_refs, outs[:n_out]):
                o_ref[...] = val.astype(o_ref.dtype)
            if n_acc:
                first = (pl.program_id(0) == 0) & (pl.program_id(1) == 0)

                @pl.when(first)
                def _():
                    for ref in acc_refs:
                        ref[...] = jnp.zeros_like(ref)

                for ref, val in zip(acc_refs, outs[n_out:]):
                    ref[...] += val

        if nk == 1:
            finish(part)
            return
        acc = refs[-1]
        k = pl.program_id(2)

        @pl.when(k == 0)
        def _():
            acc[...] = part

        @pl.when((k > 0) & (k < nk - 1))
        def _():
            acc[...] += part

        @pl.when(k == nk - 1)
        def _():
            finish(acc[...] + part)

    a_spec = pl.BlockSpec((tk, tm), lambda i, j, k: (k, i)) if ta else pl.BlockSpec((tm, tk), lambda i, j, k: (i, k))
    b_spec = pl.BlockSpec((tn, tk), lambda i, j, k: (j, k)) if tb else pl.BlockSpec((tk, tn), lambda i, j, k: (k, j))
    tile = pl.BlockSpec((tm, tn), lambda i, j, k: (i, j))
    in_specs = [a_spec, b_spec] + [tile] * ((add is not None) + len(e_rows))
    in_specs += [pl.BlockSpec((1, tn), lambda i, j, k: (0, j)) for _ in e_pars]
    args = [a, b] + ([add] if add is not None else []) + list(e_rows) + list(e_pars)
    out_specs = [tile] * n_out + [pl.BlockSpec((1, LANE), lambda i, j, k: (0, 0))] * n_acc
    out_shape = [jax.ShapeDtypeStruct((m, n), dt) for dt in e_dtypes] + [jax.ShapeDtypeStruct((1, LANE), F32)] * n_acc
    res = pl.pallas_call(
        body, name=name, grid=(m // tm, n // tn, nk),
        in_specs=in_specs, out_specs=out_specs, out_shape=out_shape,
        scratch_shapes=[pltpu.VMEM((tm, tn), F32)] if nk > 1 else [],
        compiler_params=_cp(("arbitrary", "arbitrary", "arbitrary") if n_acc else ("parallel", "parallel", "arbitrary")),
    )(*args)
    return res[0] if epilogue is None else res


def _rowwise(fn, rows, pars, out_rows, out_accs, *, name, tm=None):
    t = rows[0].shape[0]
    if tm is None:
        tm = _tile(t, (256, 128, 64, 32, 16))
    assert t % tm == 0, (t, tm)
    n_r, n_p, n_or, n_oa = len(rows), len(pars), len(out_rows), len(out_accs)

    def body(*refs):
        r_in = refs[:n_r]
        p_in = refs[n_r:n_r + n_p]
        o_r = refs[n_r + n_p:n_r + n_p + n_or]
        o_a = refs[n_r + n_p + n_or:]
        outs = fn(*[r[...] for r in r_in], *[p[...] for p in p_in])
        if not isinstance(outs, (tuple, list)):
            outs = (outs,)
        assert len(outs) == n_or + n_oa, (name, len(outs))
        for ref, val in zip(o_r, outs[:n_or]):
            ref[...] = val.astype(ref.dtype)
        if n_oa:
            @pl.when(pl.program_id(0) == 0)
            def _():
                for ref in o_a:
                    ref[...] = jnp.zeros_like(ref)

            for ref, val in zip(o_a, outs[n_or:]):
                ref[...] += val.astype(F32)

    in_specs = [pl.BlockSpec((tm, r.shape[1]), lambda i: (i, 0)) for r in rows]
    in_specs += [pl.BlockSpec(p.shape, lambda i: (0, 0)) for p in pars]
    out_specs = [pl.BlockSpec((tm, c), lambda i: (i, 0)) for c, _ in out_rows]
    out_specs += [pl.BlockSpec(s, lambda i: (0, 0)) for s in out_accs]
    out_shape = [jax.ShapeDtypeStruct((t, c), dt) for c, dt in out_rows]
    out_shape += [jax.ShapeDtypeStruct(s, F32) for s in out_accs]
    return pl.pallas_call(
        body, name=name, grid=(t // tm,), in_specs=in_specs, out_specs=out_specs, out_shape=out_shape,
        compiler_params=_cp(("arbitrary",)),
    )(*rows, *pars)


def _shift_down(x, sh):
    rolled = pltpu.roll(x, sh, 0)
    top = rolled[:SUBLANE]
    top = jnp.where(_iota2(top.shape, 0) >= sh, top, 0.0)
    return jnp.concatenate([top, rolled[SUBLANE:]], axis=0)


def _shift_up(x, sh):
    s = x.shape[0]
    rolled = pltpu.roll(x, s - sh, 0)
    bottom = rolled[s - SUBLANE:]
    bottom = jnp.where(_iota2(bottom.shape, 0) < SUBLANE - sh, bottom, 0.0)
    return jnp.concatenate([rolled[:s - SUBLANE], bottom], axis=0)


def _conv(x, w):
    k = w.shape[0]
    y = x * w[k - 1:k, :]
    for i in range(k - 1):
        y = y + _shift_down(x, k - 1 - i) * w[i:i + 1, :]
    return y


def _conv_bwd(x, w, dy):
    k = w.shape[0]
    dx = dy * w[k - 1:k, :]
    dws = []
    for i in range(k - 1):
        dx = dx + _shift_up(dy, k - 1 - i) * w[i:i + 1, :]
        dws.append(jnp.sum(dy * _shift_down(x, k - 1 - i), axis=0, keepdims=True))
    dws.append(jnp.sum(dy * x, axis=0, keepdims=True))
    return dx, dws


def _gdn_post(y, j, nqb):
    a = _silu(y)
    sc = jnp.where(j < nqb, HD ** -0.5, 1.0).astype(F32)
    outs = []
    for h in range(y.shape[1] // HD):
        ah = a[:, h * HD:(h + 1) * HD]
        l2 = ah * lax.rsqrt(jnp.sum(ah * ah, axis=-1, keepdims=True) + EPS)
        outs.append(jnp.where(j < 2 * nqb, l2 * sc, ah))
    return jnp.concatenate(outs, axis=1) if len(outs) > 1 else outs[0]


def _gdn_conv_fwd(x, w):
    bsz, s, c3 = x.shape
    k = w.shape[0]
    nb = c3 // CONV_CB
    nqb = nb // 3

    def body(x_ref, w_ref, o_ref):
        j = pl.program_id(1)
        o_ref[0] = _gdn_post(_conv(x_ref[0], w_ref[...]), j, nqb)

    return pl.pallas_call(
        body, name="gdn_conv_fwd", grid=(bsz, nb),
        in_specs=[pl.BlockSpec((1, s, CONV_CB), lambda b, j: (b, 0, j)), pl.BlockSpec((k, CONV_CB), lambda b, j: (0, j))],
        out_specs=pl.BlockSpec((1, s, CONV_CB), lambda b, j: (b, 0, j)),
        out_shape=jax.ShapeDtypeStruct(x.shape, F32),
        compiler_params=_cp(("parallel", "parallel")),
    )(x, w)


def _gdn_conv_bwd(x, w, dout):
    bsz, s, c3 = x.shape
    k = w.shape[0]
    nb = c3 // CONV_CB
    nqb = nb // 3

    def body(x_ref, w_ref, d_ref, dx_ref, dw_ref):
        j = pl.program_id(0)
        b = pl.program_id(1)
        xv, wv = x_ref[0], w_ref[...]
        y = _conv(xv, wv)
        _, f = jax.vjp(lambda yy: _gdn_post(yy, j, nqb), y)
        (dy,) = f(d_ref[0])
        dx, dws = _conv_bwd(xv, wv, dy)
        dx_ref[0] = dx.astype(dx_ref.dtype)

        @pl.when(b == 0)
        def _():
            dw_ref[...] = jnp.zeros_like(dw_ref)

        for i in range(k):
            dw_ref[i:i + 1, :] += dws[i]

    return pl.pallas_call(
        body, name="gdn_conv_bwd", grid=(nb, bsz),
        in_specs=[pl.BlockSpec((1, s, CONV_CB), lambda j, b: (b, 0, j)), pl.BlockSpec((k, CONV_CB), lambda j, b: (0, j)),
                  pl.BlockSpec((1, s, CONV_CB), lambda j, b: (b, 0, j))],
        out_specs=[pl.BlockSpec((1, s, CONV_CB), lambda j, b: (b, 0, j)), pl.BlockSpec((k, CONV_CB), lambda j, b: (0, j))],
        out_shape=[jax.ShapeDtypeStruct(x.shape, BF16), jax.ShapeDtypeStruct(w.shape, F32)],
        compiler_params=_cp(("parallel", "arbitrary")),
    )(x, w, dout)


def _ffn_conv_fwd(up, w):
    bsz, s, c2 = up.shape
    k = w.shape[0]
    nb = (c2 // 2) // CONV_CB

    def body(x1_ref, x2_ref, w1_ref, w2_ref, o_ref):
        u1 = _conv(x1_ref[0], w1_ref[...])
        u2 = _conv(x2_ref[0], w2_ref[...])
        o_ref[0] = (_silu(u1) * u2).astype(o_ref.dtype)

    return pl.pallas_call(
        body, name="ffn_conv_fwd", grid=(bsz, nb),
        in_specs=[pl.BlockSpec((1, s, CONV_CB), lambda b, j: (b, 0, j)), pl.BlockSpec((1, s, CONV_CB), lambda b, j: (b, 0, j + nb)),
                  pl.BlockSpec((k, CONV_CB), lambda b, j: (0, j)), pl.BlockSpec((k, CONV_CB), lambda b, j: (0, j + nb))],
        out_specs=pl.BlockSpec((1, s, CONV_CB), lambda b, j: (b, 0, j)),
        out_shape=jax.ShapeDtypeStruct((bsz, s, c2 // 2), BF16),
        compiler_params=_cp(("parallel", "parallel")),
    )(up, up, w, w)


def _ffn_conv_bwd(up, w, dact):
    bsz, s, c2 = up.shape
    k = w.shape[0]
    half = c2 // 2
    nb = half // CONV_CB

    def body(x1_ref, x2_ref, w1_ref, w2_ref, d_ref, dx1_ref, dx2_ref, dw1_ref, dw2_ref):
        b = pl.program_id(1)
        x1, x2, w1, w2 = x1_ref[0], x2_ref[0], w1_ref[...], w2_ref[...]
        u1 = _conv(x1, w1)
        u2 = _conv(x2, w2)
        _, f = jax.vjp(lambda p, q: _silu(p) * q, u1, u2)
        du1, du2 = f(d_ref[0])
        dx1, dws1 = _conv_bwd(x1, w1, du1)
        dx2, dws2 = _conv_bwd(x2, w2, du2)
        dx1_ref[0] = dx1.astype(dx1_ref.dtype)
        dx2_ref[0] = dx2.astype(dx2_ref.dtype)

        @pl.when(b == 0)
        def _():
            dw1_ref[...] = jnp.zeros_like(dw1_ref)
            dw2_ref[...] = jnp.zeros_like(dw2_ref)

        for i in range(k):
            dw1_ref[i:i + 1, :] += dws1[i]
            dw2_ref[i:i + 1, :] += dws2[i]

    def blk(off):
        return pl.BlockSpec((1, s, CONV_CB), lambda j, b: (b, 0, j + off))

    def wblk(off):
        return pl.BlockSpec((k, CONV_CB), lambda j, b: (0, j + off))

    return pl.pallas_call(
        body, name="ffn_conv_bwd", grid=(nb, bsz),
        in_specs=[blk(0), blk(nb), wblk(0), wblk(nb), blk(0)],
        out_specs=[blk(0), blk(0), wblk(0), wblk(0)],
        out_shape=[jax.ShapeDtypeStruct((bsz, s, half), BF16), jax.ShapeDtypeStruct((bsz, s, half), BF16),
                   jax.ShapeDtypeStruct((k, half), F32), jax.ShapeDtypeStruct((k, half), F32)],
        compiler_params=_cp(("parallel", "arbitrary")),
    )(up, up, w, w, dact)


@jax.custom_vjp
def _inv_unit_lower(mats):
    c = mats[0].shape[0]
    eye = (_iota2((c, c), 0) == _iota2((c, c), 1)).astype(F32)
    ps = [-a for a in mats]
    ts = [eye + p for p in ps]
    n = 2
    while n < c:
        ps = [_raw_fdot(p, p, "nn") for p in ps]
        ts = [t + _raw_fdot(t, p, "nn") for t, p in zip(ts, ps)]
        n *= 2
    return ts


def _inv_fwd(mats):
    ts = _inv_unit_lower(mats)
    return ts, ts


def _inv_bwd(ts, gs):
    xs = [_raw_fdot(g, t, "nt") for g, t in zip(gs, ts)]
    return ([-_raw_fdot(t, x, "tn") for t, x in zip(ts, xs)],)


_inv_unit_lower.defvjp(_inv_fwd, _inv_bwd)


def _gdn_chunk(q, k, v, z, g_row, beta_row, state, onorm):
    nh = range(len(q))
    c = q[0].shape[0]
    ii, jj = _iota2((c, c), 0), _iota2((c, c), 1)
    incl, strict, eye = ii >= jj, ii > jj, ii == jj

    def to_col(row):
        return jnp.sum(jnp.where(eye, jnp.broadcast_to(row, (c, c)), 0.0), axis=1, keepdims=True)

    gc_col = [jnp.sum(jnp.where(incl, jnp.broadcast_to(g_row[h], (c, c)), 0.0), axis=1, keepdims=True) for h in nh]
    gc_row = [jnp.sum(jnp.where(eye, jnp.broadcast_to(gc_col[h], (c, c)), 0.0), axis=0, keepdims=True) for h in nh]
    beta_col = [to_col(beta_row[h]) for h in nh]
    gc_last = [jnp.sum(g_row[h], axis=1, keepdims=True) for h in nh]
    decay = [jnp.where(incl, jnp.exp(jnp.where(incl, gc_col[h] - gc_row[h], 0.0)), 0.0) for h in nh]
    kk = [_bdot(k[h], k[h], "nt") for h in nh]
    qk = [_bdot(q[h], k[h], "nt") * decay[h] for h in nh]
    tinv = _inv_unit_lower([jnp.where(strict, beta_col[h] * kk[h] * decay[h], 0.0) for h in nh])
    rhs = [jnp.concatenate([v[h] * beta_col[h], k[h] * (beta_col[h] * jnp.exp(gc_col[h]))], axis=1) for h in nh]
    uw = [_fdot(tinv[h], rhs[h], "nn") for h in nh]
    dv = v[0].shape[1]
    ws = [_bdot(uw[h][:, dv:], state[h], "nn") for h in nh]
    qs = [_bdot(q[h] * jnp.exp(gc_col[h]), state[h], "nn") for h in nh]
    v_new = [uw[h][:, :dv] - ws[h] for h in nh]
    o = [qs[h] + _bdot(qk[h], v_new[h], "nn") for h in nh]
    kv = [_bdot(k[h] * jnp.exp(gc_last[h] - gc_col[h]), v_new[h], "tn") for h in nh]
    new_state = [state[h] * jnp.exp(gc_last[h]) + kv[h] for h in nh]
    y = [_rms(o[h], onorm) * _silu(z[h]) for h in nh]
    return y, new_state


def _gdn_specs(s, c, reverse):
    n = s // c
    nn = (lambda i: n - 1 - i) if reverse else (lambda i: i)

    def qkv(off):
        return pl.BlockSpec((1, c, GDN_HP * HD), lambda b, h, i: (b, nn(i), h + off))

    def gate(off):
        return pl.BlockSpec((1, GDN_HP, 1, 1, c), lambda b, h, i: (b, h + off, nn(i), 0, 0))

    st = pl.BlockSpec((1, GDN_HP, 1, HD, HD), lambda b, h, i: (b, h, nn(i), 0, 0))
    onorm = pl.BlockSpec((1, HD), lambda b, h, i: (0, 0))
    return n, qkv, gate, st, onorm


def _gdn_fwd(qkv, z, gbt, onorm):
    bsz, s, _ = qkv.shape
    gh, c = CFG.gh, CFG.gch
    ng = gh // GDN_HP
    n, qs, gs, st, on = _gdn_specs(s, c, False)

    def body(q_ref, k_ref, v_ref, z_ref, g_ref, b_ref, on_ref, y_ref, st_ref, state):
        @pl.when(pl.program_id(2) == 0)
        def _():
            state[...] = jnp.zeros_like(state)

        nh = range(GDN_HP)
        hs = [slice(h * HD, (h + 1) * HD) for h in nh]
        s_in = [state[h] for h in nh]
        for h in nh:
            st_ref[0, h, 0] = s_in[h]
        y, s_out = _gdn_chunk([q_ref[0, :, hs[h]] for h in nh], [k_ref[0, :, hs[h]] for h in nh],
                              [v_ref[0, :, hs[h]] for h in nh], [z_ref[0, :, hs[h]] for h in nh],
                              [g_ref[0, h, 0] for h in nh], [b_ref[0, h, 0] for h in nh], s_in, on_ref[...])
        for h in nh:
            y_ref[0, :, hs[h]] = y[h].astype(y_ref.dtype)
            state[h] = s_out[h]

    return pl.pallas_call(
        body, name="gdn_fwd", grid=(bsz, ng, n),
        in_specs=[qs(0), qs(ng), qs(2 * ng), qs(0), gs(0), gs(ng), on],
        out_specs=[qs(0), st],
        out_shape=[jax.ShapeDtypeStruct((bsz, s, gh * HD), BF16), jax.ShapeDtypeStruct((bsz, gh, n, HD, HD), F32)],
        scratch_shapes=[pltpu.VMEM((GDN_HP, HD, HD), F32)],
        compiler_params=_cp(("parallel", "parallel", "arbitrary")),
    )(qkv, qkv, qkv, z, gbt, gbt, onorm)


def _gdn_bwd(qkv, z, gbt, onorm, states, dy):
    bsz, s, _ = qkv.shape
    gh, c = CFG.gh, CFG.gch
    ng = gh // GDN_HP
    assert ng == 1, "d(q | k | v) is written as one block of all heads"
    w = gh * HD
    n, qs, gs, st, on = _gdn_specs(s, c, True)
    dqkv_spec = pl.BlockSpec((1, c, 3 * w), lambda b, h, i: (b, n - 1 - i, 0))

    def body(q_ref, k_ref, v_ref, z_ref, g_ref, b_ref, on_ref, st_ref, dy_ref,
             dqkv_ref, dz_ref, dg_ref, db_ref, don_ref, dstate):
        first = (pl.program_id(0) == 0) & (pl.program_id(1) == 0) & (pl.program_id(2) == 0)

        @pl.when(first)
        def _():
            don_ref[...] = jnp.zeros_like(don_ref)

        @pl.when(pl.program_id(2) == 0)
        def _():
            dstate[...] = jnp.zeros_like(dstate)

        nh = range(GDN_HP)
        hs = [slice(h * HD, (h + 1) * HD) for h in nh]
        _, f = jax.vjp(_gdn_chunk, [q_ref[0, :, hs[h]] for h in nh], [k_ref[0, :, hs[h]] for h in nh],
                       [v_ref[0, :, hs[h]] for h in nh], [z_ref[0, :, hs[h]] for h in nh],
                       [g_ref[0, h, 0] for h in nh], [b_ref[0, h, 0] for h in nh],
                       [st_ref[0, h, 0] for h in nh], on_ref[...])
        dq, dk, dv, dz, dg, db, ds, don = f(([dy_ref[0, :, hs[h]] for h in nh], [dstate[h] for h in nh]))
        for h in nh:
            dqkv_ref[0, :, h * HD:(h + 1) * HD] = dq[h]
            dqkv_ref[0, :, w + h * HD:w + (h + 1) * HD] = dk[h]
            dqkv_ref[0, :, 2 * w + h * HD:2 * w + (h + 1) * HD] = dv[h]
            dz_ref[0, :, hs[h]] = dz[h].astype(dz_ref.dtype)
            dg_ref[0, h, 0] = dg[h]
            db_ref[0, h, 0] = db[h]
            dstate[h] = ds[h]
        don_ref[...] += don

    act = jax.ShapeDtypeStruct((bsz, s, gh * HD), F32)
    gshape = jax.ShapeDtypeStruct((bsz, gh, n, 1, c), F32)
    return pl.pallas_call(
        body, name="gdn_bwd", grid=(bsz, ng, n),
        in_specs=[qs(0), qs(ng), qs(2 * ng), qs(0), gs(0), gs(ng), on, st, qs(0)],
        out_specs=[dqkv_spec, qs(0), gs(0), gs(0), on],
        out_shape=[jax.ShapeDtypeStruct((bsz, s, 3 * w), F32), jax.ShapeDtypeStruct(act.shape, BF16), gshape, gshape,
                   jax.ShapeDtypeStruct((1, HD), F32)],
        scratch_shapes=[pltpu.VMEM((GDN_HP, HD, HD), F32)],
        compiler_params=_cp(("arbitrary", "arbitrary", "arbitrary")),
    )(qkv, qkv, qkv, z, gbt, gbt, onorm, states, dy)


def _gates_fn(ab, alog, dtb):
    lane = _iota2(ab.shape, 1)
    g = -jnp.exp(alog) * _softplus(ab + dtb)
    beta = _sigmoid(ab)
    return jnp.where(lane < CFG.gh, g, jnp.where(lane < 2 * CFG.gh, beta, 0.0))


def _heads_cumsum(xs, tri):
    n = xs[0].shape[0]
    y = _split_dot(jnp.concatenate(xs, axis=0), tri)
    return [y[h * n:(h + 1) * n] for h in range(len(xs))]


def _blk_off(jblk):
    return jblk * SB_BLK if isinstance(jblk, int) else pl.multiple_of(jblk * SB_BLK, SB_BLK)


def _sb_span(qs, k_spans, mask, runs, tri_su):
    nh = range(len(qs))
    nb = k_spans[0].shape[0] // SB_BLK
    zs = [lax.dot_general(qs[h], k_spans[h], _dims("nt"), preferred_element_type=F32) for h in nh]
    l1p = [jnp.log(1.0 + jnp.exp(-jnp.abs(z))) for z in zs]
    lss = [jnp.minimum(zs[h], 0.0) - l1p[h] for h in nh]
    lfs = [lss[h] - zs[h] for h in nh]
    if mask is not None:
        lfs = [jnp.where(mask, lf, 0.0) for lf in lfs]
    units = [lfs[h][:, b * SB_BLK:(b + 1) * SB_BLK] for h in nh for b in range(nb)]
    cums = _heads_cumsum(units, tri_su)
    sfx, new_runs = [], []
    for h in nh:
        run, parts = runs[h], [None] * nb
        for b in reversed(range(nb)):
            parts[b] = cums[h * nb + b] + run
            run = run + jnp.sum(units[h * nb + b], axis=1, keepdims=True)
        sfx.append(jnp.concatenate(parts, axis=1) if nb > 1 else parts[0])
        new_runs.append(run)
    ws = [jnp.exp(lss[h] + sfx[h]) for h in nh]
    if mask is not None:
        ws = [jnp.where(mask, w, 0.0) for w in ws]
    return zs, lfs, ws, new_runs


def _sb_specs(s, w):
    def qb(off):
        return pl.BlockSpec((1, SB_BLK, w), lambda b, h, i: (b, i, h + off))

    def full(off):
        return pl.BlockSpec((1, s, w), lambda b, h, i: (b, 0, h + off))

    return qb, full


def _sb_fwd(qkv, shard):
    bsz, s, _ = qkv.shape
    ng = CFG.sbh // SB_HP
    w = SB_HP * HD
    scale = HD ** -0.5
    qb, full = _sb_specs(s, w)

    def body(q_ref, k_ref, v_ref, x_ref, o_ref, g_ref, send_sems, recv_sems, local_sem):
        i = pl.program_id(2)
        begin, relay, finish = _gather_stages(x_ref, g_ref, send_sems, recv_sems, local_sem)
        start_of_group = (pl.program_id(1) == 0) & (i == 0)
        pl.when((pl.program_id(0) == 0) & start_of_group)(begin)
        pl.when((pl.program_id(0) == bsz // 2) & start_of_group)(relay)

        r, c = _iota2((SB_BLK, SB_BLK), 0), _iota2((SB_BLK, SB_BLK), 1)
        tri_su = (r > c).astype(BF16)
        nh = range(SB_HP)
        hs = [slice(h * HD, (h + 1) * HD) for h in nh]
        qs = [(q_ref[0, :, hs[h]] * scale).astype(BF16) for h in nh]

        def span(off, nb, mask, carry):
            ks = [k_ref[0, pl.ds(off, nb * SB_BLK), hs[h]].astype(BF16) for h in nh]
            vs = [v_ref[0, pl.ds(off, nb * SB_BLK), hs[h]].astype(BF16) for h in nh]
            _, _, ws, runs = _sb_span(qs, ks, mask, [cr[1] for cr in carry], tri_su)
            pv = [lax.dot_general(ws[h].astype(BF16), vs[h], _dims("nn"), preferred_element_type=F32) for h in nh]
            return tuple((carry[h][0] + pv[h], runs[h]) for h in nh)

        carry = tuple((jnp.zeros((SB_BLK, HD), F32), jnp.zeros((SB_BLK, 1), F32)) for _ in nh)
        carry = span(_blk_off(i), 1, c < r, carry)
        rem = jnp.bitwise_and(i, 3)
        carry = lax.fori_loop(0, lax.shift_right_logical(i, 2),
                              lambda p, cr: span(_blk_off(i - 4 - 4 * p), 4, None, cr), carry)
        carry = lax.fori_loop(0, lax.shift_right_logical(rem, 1),
                              lambda _, cr: span(_blk_off(jnp.bitwise_and(rem, 1)), 2, None, cr), carry)
        carry = lax.fori_loop(0, jnp.bitwise_and(rem, 1), lambda _, cr: span(0, 1, None, cr), carry)
        for h in nh:
            o_ref[0, :, hs[h]] = carry[h][0].astype(o_ref.dtype)

        pl.when((pl.program_id(0) == bsz - 1) & (pl.program_id(1) == ng - 1) & (i == nblk - 1))(finish)

    nblk = s // SB_BLK
    hbm = pl.BlockSpec(memory_space=pl.ANY)
    return pl.pallas_call(
        body, name="sb_fwd", grid=(bsz, ng, nblk),
        in_specs=[qb(0), full(ng), full(2 * ng), hbm], out_specs=[qb(0), hbm],
        out_shape=[jax.ShapeDtypeStruct((bsz, s, CFG.sbh * HD), BF16),
                   jax.ShapeDtypeStruct((N_DEV,) + shard.shape, shard.dtype)],
        scratch_shapes=list(_GATHER_SEMS),
        compiler_params=_cp(("arbitrary", "arbitrary", "arbitrary")),
    )(qkv, qkv, qkv, shard)


def _sb_bwd(qkv, do, s1):
    bsz, s, _ = qkv.shape
    ng = CFG.sbh // SB_HP
    w = SB_HP * HD
    nblk = s // SB_BLK
    scale = HD ** -0.5
    qb, full = _sb_specs(s, w)

    def body(q_ref, k_ref, v_ref, do_ref, s1_ref, dq_ref, dk_ref, dv_ref, r2_ref, dk_acc, dv_acc, dl_pan, z_pan,
             send_sems, recv_sems):
        i = pl.program_id(2)
        copies = _chip_copies(s1_ref, r2_ref, send_sems, recv_sems)

        @pl.when((pl.program_id(0) == 0) & (pl.program_id(1) == 0) & (i == 0))
        def _():
            for cp in copies:
                cp.start()

        @pl.when(i == 0)
        def _():
            dk_acc[...] = jnp.zeros_like(dk_acc)
            dv_acc[...] = jnp.zeros_like(dv_acc)

        r, c = _iota2((SB_BLK, SB_BLK), 0), _iota2((SB_BLK, SB_BLK), 1)
        tri_su = (r > c).astype(BF16)
        tri_pre = (r < c).astype(BF16)
        nh = range(SB_HP)
        hs = [slice(h * HD, (h + 1) * HD) for h in nh]
        qs = [(q_ref[0, :, hs[h]] * scale).astype(BF16) for h in nh]
        dob = [do_ref[0, :, hs[h]].astype(BF16) for h in nh]
        quads = lax.shift_right_logical(i, 2)
        rem = jnp.bitwise_and(i, 3)
        pair = lax.shift_right_logical(rem, 1)
        odd = jnp.bitwise_and(rem, 1)

        def span_a(jblk, nb, mask, runs):
            rows = pl.ds(_blk_off(jblk), nb * SB_BLK)
            ks = [k_ref[0, rows, hs[h]].astype(BF16) for h in nh]
            vs = [v_ref[0, rows, hs[h]].astype(BF16) for h in nh]
            dws = [lax.dot_general(dob[h], vs[h], _dims("nt"), preferred_element_type=F32) for h in nh]
            zs, _, ws, runs = _sb_span(qs, ks, mask, runs, tri_su)
            dvs = [lax.dot_general(ws[h].astype(BF16), dob[h], _dims("tn"), preferred_element_type=F32) for h in nh]
            for h in nh:
                dl = dws[h] * ws[h]
                for b in range(nb):
                    dl_pan[h, jblk + b] = dl[:, b * SB_BLK:(b + 1) * SB_BLK]
                    z_pan[h, jblk + b] = zs[h][:, b * SB_BLK:(b + 1) * SB_BLK]
                dv_acc[rows, hs[h]] += dvs[h]
            return tuple(runs)

        runs = tuple(jnp.zeros((SB_BLK, 1), F32) for _ in nh)
        runs = span_a(i, 1, c < r, runs)
        runs = lax.fori_loop(0, quads, lambda p, rn: span_a(i - 4 - 4 * p, 4, None, rn), runs)
        runs = lax.fori_loop(0, pair, lambda _, rn: span_a(odd, 2, None, rn), runs)
        lax.fori_loop(0, odd, lambda _, rn: span_a(0, 1, None, rn), runs)

        def span_b(jblk, nb, mask, carry):
            rows = pl.ds(_blk_off(jblk), nb * SB_BLK)
            ks = [k_ref[0, rows, hs[h]].astype(BF16) for h in nh]
            units = [dl_pan[h, jblk + b] for h in nh for b in range(nb)]
            sgs = [_sigmoid(z_pan[h, jblk + b]) for h in nh for b in range(nb)]
            cums = _heads_cumsum(units, tri_pre)
            dzs, pres = [], []
            for h in nh:
                pre, parts = carry[h][1], []
                for b in range(nb):
                    u, sg = units[h * nb + b], sgs[h * nb + b]
                    parts.append(u * (1.0 - sg) - sg * (cums[h * nb + b] + pre))
                    pre = pre + jnp.sum(u, axis=1, keepdims=True)
                dz = jnp.concatenate(parts, axis=1) if nb > 1 else parts[0]
                if mask is not None:
                    dz = jnp.where(mask, dz, 0.0)
                dzs.append(dz.astype(BF16))
                pres.append(pre)
            dqs = [lax.dot_general(dzs[h], ks[h], _dims("nn"), preferred_element_type=F32) for h in nh]
            dks = [lax.dot_general(dzs[h], qs[h], _dims("tn"), preferred_element_type=F32) for h in nh]
            for h in nh:
                dk_acc[rows, hs[h]] += dks[h]
            return tuple((carry[h][0] + dqs[h], pres[h]) for h in nh)

        carry = tuple((jnp.zeros((SB_BLK, HD), F32), jnp.zeros((SB_BLK, 1), F32)) for _ in nh)
        carry = lax.fori_loop(0, odd, lambda _, cr: span_b(0, 1, None, cr), carry)
        carry = lax.fori_loop(0, pair, lambda _, cr: span_b(odd, 2, None, cr), carry)
        carry = lax.fori_loop(0, quads, lambda p, cr: span_b(rem + 4 * p, 4, None, cr), carry)
        carry = span_b(i, 1, c < r, carry)
        for h in nh:
            dq_ref[0, :, hs[h]] = (carry[h][0] * scale).astype(dq_ref.dtype)

        @pl.when(i == nblk - 1)
        def _():
            dk_ref[0] = dk_acc[...].astype(dk_ref.dtype)
            dv_ref[0] = dv_acc[...].astype(dv_ref.dtype)

        @pl.when((pl.program_id(0) == bsz - 1) & (pl.program_id(1) == ng - 1) & (i == nblk - 1))
        def _():
            for cp in copies:
                cp.wait_recv()
            for cp in copies:
                cp.wait_send()

    out = jax.ShapeDtypeStruct((bsz, s, CFG.sbh * HD), BF16)
    hbm = pl.BlockSpec(memory_space=pl.ANY)
    return pl.pallas_call(
        body, name="sb_bwd", grid=(bsz, ng, nblk),
        in_specs=[qb(0), full(ng), full(2 * ng), qb(0), hbm],
        out_specs=[qb(0), full(0), full(0), hbm],
        out_shape=[out, out, out, jax.ShapeDtypeStruct((3,) + s1.shape[1:], s1.dtype)],
        scratch_shapes=[pltpu.VMEM((s, w), F32), pltpu.VMEM((s, w), F32),
                        pltpu.VMEM((SB_HP, nblk, SB_BLK, SB_BLK), F32), pltpu.VMEM((SB_HP, nblk, SB_BLK, SB_BLK), F32),
                        pltpu.SemaphoreType.DMA((3,)), pltpu.SemaphoreType.DMA((3,))],
        compiler_params=_cp(("arbitrary", "arbitrary", "arbitrary")),
    )(qkv, qkv, qkv, do, s1)


def _xattn_fn(q_raw, kv, qn, kn):
    d = q_raw.shape[1]
    dh = d // CFG.xh
    outs = []
    for h in range(CFG.xh):
        qh = _rms(q_raw[:, h * dh:(h + 1) * dh], qn)
        kh = _rms(kv[:, h * dh:(h + 1) * dh], kn)
        vh = kv[:, d + h * dh:d + (h + 1) * dh]
        sc = _bdot(qh, kh, "nt") * (dh ** -0.5)
        sc = sc - lax.stop_gradient(jnp.max(sc, axis=-1, keepdims=True))
        e = jnp.exp(sc)
        p = e / jnp.sum(e, axis=-1, keepdims=True)
        outs.append(_bdot(p, vh, "nn"))
    return jnp.concatenate(outs, axis=1)


def _xattn_fwd(q_raw, kv, qn, kn):
    bsz, s, d = q_raw.shape
    m = kv.shape[1]
    tq = _tile(s, (XQ_TILE, 128))

    def body(q_ref, kv_ref, qn_ref, kn_ref, o_ref):
        o_ref[0] = _xattn_fn(q_ref[0], kv_ref[0], qn_ref[...], kn_ref[...]).astype(o_ref.dtype)

    return pl.pallas_call(
        body, name="xattn_fwd", grid=(bsz, s // tq),
        in_specs=[pl.BlockSpec((1, tq, d), lambda b, i: (b, i, 0)), pl.BlockSpec((1, m, 2 * d), lambda b, i: (b, 0, 0)),
                  pl.BlockSpec(qn.shape, lambda b, i: (0, 0)), pl.BlockSpec(kn.shape, lambda b, i: (0, 0))],
        out_specs=pl.BlockSpec((1, tq, d), lambda b, i: (b, i, 0)),
        out_shape=jax.ShapeDtypeStruct((bsz, s, d), BF16),
        compiler_params=_cp(("parallel", "parallel")),
    )(q_raw, kv, qn, kn)


def _xattn_bwd(q_raw, kv, qn, kn, do):
    bsz, s, d = q_raw.shape
    m = kv.shape[1]
    tq = _tile(s, (XQ_TILE, 128))

    def body(q_ref, kv_ref, qn_ref, kn_ref, do_ref, dq_ref, dkv_ref, dqn_ref, dkn_ref):
        b, i = pl.program_id(0), pl.program_id(1)

        @pl.when((b == 0) & (i == 0))
        def _():
            dqn_ref[...] = jnp.zeros_like(dqn_ref)
            dkn_ref[...] = jnp.zeros_like(dkn_ref)

        @pl.when(i == 0)
        def _():
            dkv_ref[...] = jnp.zeros_like(dkv_ref)

        _, f = jax.vjp(_xattn_fn, q_ref[0], kv_ref[0], qn_ref[...], kn_ref[...])
        dq, dkv, dqn, dkn = f(do_ref[0].astype(F32))
        dq_ref[0] = dq.astype(dq_ref.dtype)
        dkv_ref[0] += dkv
        dqn_ref[...] += dqn
        dkn_ref[...] += dkn

    return pl.pallas_call(
        body, name="xattn_bwd", grid=(bsz, s // tq),
        in_specs=[pl.BlockSpec((1, tq, d), lambda b, i: (b, i, 0)), pl.BlockSpec((1, m, 2 * d), lambda b, i: (b, 0, 0)),
                  pl.BlockSpec(qn.shape, lambda b, i: (0, 0)), pl.BlockSpec(kn.shape, lambda b, i: (0, 0)),
                  pl.BlockSpec((1, tq, d), lambda b, i: (b, i, 0))],
        out_specs=[pl.BlockSpec((1, tq, d), lambda b, i: (b, i, 0)), pl.BlockSpec((1, m, 2 * d), lambda b, i: (b, 0, 0)),
                   pl.BlockSpec(qn.shape, lambda b, i: (0, 0)), pl.BlockSpec(kn.shape, lambda b, i: (0, 0))],
        out_shape=[jax.ShapeDtypeStruct((bsz, s, d), BF16), jax.ShapeDtypeStruct(kv.shape, F32),
                   jax.ShapeDtypeStruct(qn.shape, F32), jax.ShapeDtypeStruct(kn.shape, F32)],
        compiler_params=_cp(("arbitrary", "arbitrary")),
    )(q_raw, kv, qn, kn, do)


def _my_pos():
    return lax.axis_index("x"), lax.axis_index("y"), lax.axis_index("c")


def _gather_stages(x_ref, out_ref, send_sems, recv_sems, local_sem):
    x, y, c = _my_pos()
    me, sibling = (x, y, c), (x, y, 1 - c)
    chips = [(1 - x, y), (x, 1 - y), (1 - x, 1 - y)]

    def slot(px, py, pc):
        return out_ref.at[4 * px + 2 * py + pc]

    def copy(k, block, to, src=None):
        return pltpu.make_async_remote_copy(
            src_ref=slot(*block) if src is None else src, dst_ref=slot(*block),
            send_sem=send_sems.at[k], recv_sem=recv_sems.at[k], device_id=to, device_id_type=MESH)

    mine = pltpu.make_async_copy(x_ref, slot(*me), local_sem)
    first = [copy(0, me, sibling, src=x_ref)]
    first += [copy(1 + j, me, (*chip, c), src=x_ref) for j, chip in enumerate(chips)]
    passed = [copy(4 + j, (*chip, c), sibling) for j, chip in enumerate(chips)]

    def begin():
        mine.start()
        for cp in first:
            cp.start()

    def relay():
        for j, chip in enumerate(chips):
            copy(1 + j, (*chip, c), me).wait_recv()
            passed[j].start()

    def finish():
        copy(0, sibling, me).wait_recv()
        for j, chip in enumerate(chips):
            copy(4 + j, (*chip, 1 - c), me).wait_recv()
        for cp in first + passed:
            cp.wait_send()
        mine.wait()

    return begin, relay, finish


_GATHER_SEMS = [pltpu.SemaphoreType.DMA((7,)), pltpu.SemaphoreType.DMA((7,)), pltpu.SemaphoreType.DMA]


def _all_gather_big(shard, name):
    r, d = shard.shape

    def body(x_ref, out_ref, send_sems, recv_sems, local_sem):
        begin, relay, finish = _gather_stages(x_ref, out_ref, send_sems, recv_sems, local_sem)
        begin()
        relay()
        finish()

    return pl.pallas_call(
        body, name=name,
        out_shape=jax.ShapeDtypeStruct((N_DEV, r, d), shard.dtype),
        in_specs=[pl.BlockSpec(memory_space=pl.ANY)], out_specs=pl.BlockSpec(memory_space=pl.ANY),
        scratch_shapes=list(_GATHER_SEMS),
    )(shard)


def _exchange_sibling(g, name):
    _, r, d = g.shape

    def body(g_ref, out_ref, send_sems, recv_sems):
        x, y, c = _my_pos()
        copies = [pltpu.make_async_remote_copy(
            src_ref=g_ref.at[2 * k + (1 - c)], dst_ref=out_ref.at[k],
            send_sem=send_sems.at[k], recv_sem=recv_sems.at[k], device_id=(x, y, 1 - c), device_id_type=MESH)
            for k in range(4)]
        for cp in copies:
            cp.start()
        for cp in copies:
            cp.wait_recv()
        for cp in copies:
            cp.wait_send()

    return pl.pallas_call(
        body, name=name,
        out_shape=jax.ShapeDtypeStruct((4, r, d), g.dtype),
        in_specs=[pl.BlockSpec(memory_space=pl.ANY)], out_specs=pl.BlockSpec(memory_space=pl.ANY),
        scratch_shapes=[pltpu.SemaphoreType.DMA((4,)), pltpu.SemaphoreType.DMA((4,))],
    )(g)


def _chip_copies(s_ref, out_ref, send_sems, recv_sems):
    x, y, c = _my_pos()
    copies = []
    for rel in (1, 2, 3):
        px = jnp.bitwise_xor(x, rel >> 1)
        py = jnp.bitwise_xor(y, rel & 1)
        copies.append(pltpu.make_async_remote_copy(
            src_ref=s_ref.at[2 * px + py], dst_ref=out_ref.at[rel - 1],
            send_sem=send_sems.at[rel - 1], recv_sem=recv_sems.at[rel - 1],
            device_id=(px, py, c), device_id_type=MESH))
    return copies


def _all_reduce_small(blk, name):
    rows, d = blk.shape

    def body(x_ref, out_ref, land, send_sems, recv_sems):
        x, y, c = _my_pos()
        me = 4 * x + 2 * y + c
        copies = []
        for rel in range(1, N_DEV):
            peer = (jnp.bitwise_xor(x, rel >> 2), jnp.bitwise_xor(y, (rel >> 1) & 1), jnp.bitwise_xor(c, rel & 1))
            copies.append(pltpu.make_async_remote_copy(
                src_ref=x_ref, dst_ref=land.at[rel - 1], send_sem=send_sems.at[rel - 1], recv_sem=recv_sems.at[rel - 1],
                device_id=peer, device_id_type=MESH))
        for cp in copies:
            cp.start()
        for cp in copies:
            cp.wait_recv()
        acc = jnp.zeros((rows, d), F32)
        for dev in range(N_DEV):
            rel = jnp.bitwise_xor(me, dev)
            got = land[jnp.maximum(rel - 1, 0)]
            acc = acc + jnp.where(rel == 0, x_ref[...], got)
        out_ref[...] = acc
        for cp in copies:
            cp.wait_send()

    return pl.pallas_call(
        body, name=name,
        out_shape=jax.ShapeDtypeStruct((rows, d), F32),
        in_specs=[pl.BlockSpec(memory_space=pltpu.VMEM)], out_specs=pl.BlockSpec(memory_space=pltpu.VMEM),
        scratch_shapes=[pltpu.VMEM((N_DEV - 1, rows, d), F32), pltpu.SemaphoreType.DMA((N_DEV - 1,)),
                        pltpu.SemaphoreType.DMA((N_DEV - 1,))],
    )(blk)


def _dxn_fused(parts, s1):
    t, d = parts[0][0].shape[0], parts[0][1].shape[1]
    tm = _tile(t, MM_TILES)
    nm = t // tm
    segs, k0 = [], 0
    for a, _ in parts:
        tk = _tile(a.shape[1], (512, 256, 128))
        segs.append((k0, a.shape[1] // tk, tk))
        k0 += a.shape[1] // tk
    ktot = k0
    npart = len(parts)

    def body(*refs):
        ab = refs[:2 * npart]
        s1_ref, o_ref, r2_ref, acc, send_sems, recv_sems = refs[2 * npart:]
        i, k = pl.program_id(0), pl.program_id(1)
        copies = _chip_copies(s1_ref, r2_ref, send_sems, recv_sems)

        @pl.when((i == 0) & (k == 0))
        def _():
            for cp in copies:
                cp.start()

        @pl.when(k == 0)
        def _():
            acc[...] = jnp.zeros_like(acc)

        for p, (p0, nk, _) in enumerate(segs):
            @pl.when((k >= p0) & (k < p0 + nk))
            def _(p=p):
                acc[...] += lax.dot_general(ab[2 * p][...], ab[2 * p + 1][...], _dims("nn"), preferred_element_type=F32)

        @pl.when(k == ktot - 1)
        def _():
            o_ref[...] = acc[...]

        @pl.when((i == nm - 1) & (k == ktot - 1))
        def _():
            for cp in copies:
                cp.wait_recv()
            for cp in copies:
                cp.wait_send()

    in_specs, args = [], []
    for (a, b), (p0, nk, tk) in zip(parts, segs):
        def kk(k, p0=p0, nk=nk):
            return jnp.clip(k - p0, 0, nk - 1)
        in_specs.append(pl.BlockSpec((tm, tk), lambda i, k, kk=kk: (i, kk(k))))
        in_specs.append(pl.BlockSpec((tk, d), lambda i, k, kk=kk: (kk(k), 0)))
        args += [a, b]
    hbm = pl.BlockSpec(memory_space=pl.ANY)
    return pl.pallas_call(
        body, name="d_xn", grid=(nm, ktot),
        in_specs=in_specs + [hbm], out_specs=[pl.BlockSpec((tm, d), lambda i, k: (i, 0)), hbm],
        out_shape=[jax.ShapeDtypeStruct((t, d), F32), jax.ShapeDtypeStruct((3,) + s1.shape[1:], s1.dtype)],
        scratch_shapes=[pltpu.VMEM((tm, d), F32), pltpu.SemaphoreType.DMA((3,)), pltpu.SemaphoreType.DMA((3,))],
        compiler_params=_cp(("arbitrary", "arbitrary")),
    )(*args, s1)


def _cast_rows(x, dtype, name):
    return _rowwise(lambda v: v, [x], [], [(x.shape[1], dtype)], [], name=name, tm=CFG.pack_tile)[0]


def _sum_sibling(g, recv1, c_idx, name):
    _, r, d = g.shape
    tm = CFG.pack_tile

    def body(c_ref, g_ref, r_ref, o_ref):
        o_ref[0] = (g_ref[0] + r_ref[0]).astype(o_ref.dtype)

    grid_spec = pltpu.PrefetchScalarGridSpec(
        num_scalar_prefetch=1, grid=(4, r // tm),
        in_specs=[pl.BlockSpec((1, tm, d), lambda k, i, c_ref: (2 * k + c_ref[0], i, 0)),
                  pl.BlockSpec((1, tm, d), lambda k, i, c_ref: (k, i, 0))],
        out_specs=pl.BlockSpec((1, tm, d), lambda k, i, c_ref: (k, i, 0)))
    return pl.pallas_call(
        body, name=name, grid_spec=grid_spec,
        out_shape=jax.ShapeDtypeStruct((4, r, d), BF16),
        compiler_params=_cp(("parallel", "parallel")),
    )(c_idx, g, recv1)


def _adamw_math(w, g, m, v):
    m2 = ADAM_B1 * m + (1.0 - ADAM_B1) * g
    v2 = ADAM_B2 * v + (1.0 - ADAM_B2) * (g * g)
    m_hat = m2 / (1.0 - ADAM_B1 ** ADAM_STEP)
    v_hat = v2 / (1.0 - ADAM_B2 ** ADAM_STEP)
    delta = -ADAM_LR * (m_hat / (jnp.sqrt(v_hat) + ADAM_EPS) + ADAM_WD * w)
    return delta, m2, v2


def _adamw_big(g, recv1, recv2, w, m, v, idx, row0, name):
    _, r, d = g.shape
    tm = CFG.pack_tile
    t0 = row0 // tm

    def body(idx_ref, g_ref, r1_ref, ra_ref, rb_ref, rc_ref, w_ref, m_ref, v_ref, og, od, om, ov):
        grad = (g_ref[0] + r1_ref[0]) + ra_ref[0].astype(F32) + rb_ref[0].astype(F32) + rc_ref[0].astype(F32)
        delta, m2, v2 = _adamw_math(w_ref[...], grad, m_ref[...], v_ref[...])
        og[...] = grad
        od[...] = delta
        om[...] = m2
        ov[...] = v2

    flat = pl.BlockSpec((tm, d), lambda i, idx_ref: (i, 0))
    shifted = pl.BlockSpec((tm, d), lambda i, idx_ref: (i + t0, 0))
    grid_spec = pltpu.PrefetchScalarGridSpec(
        num_scalar_prefetch=1, grid=(r // tm,),
        in_specs=[pl.BlockSpec((1, tm, d), lambda i, idx_ref: (idx_ref[0], i, 0)),
                  pl.BlockSpec((1, tm, d), lambda i, idx_ref: (idx_ref[1], i, 0)),
                  pl.BlockSpec((1, tm, d), lambda i, idx_ref: (0, i, 0)),
                  pl.BlockSpec((1, tm, d), lambda i, idx_ref: (1, i, 0)),
                  pl.BlockSpec((1, tm, d), lambda i, idx_ref: (2, i, 0)),
                  shifted, shifted, shifted],
        out_specs=[flat, flat, flat, flat])
    shp = jax.ShapeDtypeStruct((r, d), F32)
    return pl.pallas_call(
        body, name=name, grid_spec=grid_spec, out_shape=[shp, shp, shp, shp],
        compiler_params=_cp(("parallel",)),
    )(idx, g, recv1, recv2, recv2, recv2, w, m, v)


def _rows_of(v, d):
    flat = v.reshape(-1)
    rows = -(-flat.shape[0] // d)
    rows += (-rows) % SUBLANE
    return jnp.pad(flat, (0, rows * d - flat.shape[0])).reshape(rows, d)


def _pad_rows(a, mult):
    pad = (-a.shape[0]) % mult
    if pad:
        a = jnp.pad(a, ((0, pad),) + ((0, 0),) * (a.ndim - 1))
    return a


_BIG = ("w_in", "w_xkv", "w_up", "w_proj_gdn", "w_proj_sb", "w_out", "w_xq", "w_xo", "w_down")
_COL_SHARDED = ("w_in", "w_xkv", "w_up")
_SMALL_REP = ("norm_mix", "norm_x", "norm_mem", "norm_ffn", "a_log", "dt_bias", "gdn_out_norm", "xq_norm", "xk_norm")
_SMALL_CONV = ("conv_gdn", "conv_ffn")


def _part_rows(shapes):
    out = []
    for n in _BIG:
        rows, cols = shapes[n]
        cnt = cols if n in _COL_SHARDED else rows
        out.append((cnt, cnt + (-cnt) % (CFG.pack_tile if n == _BIG[0] else PACK_ROW_ALIGN)))
    return out


def _pack_big_shards(shards, shapes):
    parts = []
    for n, (_, padded) in zip(_BIG, _part_rows(shapes)):
        parts.append(_pad_rows(shards[n].T if n in _COL_SHARDED else shards[n], padded))
    return _pad_rows(jnp.concatenate(parts, axis=0), CFG.pack_tile)


def _unpack_gathered(gath, shapes, names):
    out, r0 = {}, 0
    for n, (cnt, padded) in zip(_BIG, _part_rows(shapes)):
        if n not in names:
            continue
        out[n] = gath[:, r0:r0 + cnt, :].reshape(N_DEV * cnt, gath.shape[2])
        r0 += padded
    return out


def _pack_full_grads(grads, shapes, names):
    d = CFG.d
    parts = []
    for n, (cnt, padded) in zip(_BIG, _part_rows(shapes)):
        if n not in names:
            continue
        g = grads[n].reshape(N_DEV, cnt, d)
        if padded > cnt:
            g = jnp.pad(g, ((0, 0), (0, padded - cnt), (0, 0)))
        parts.append(g)
    full = jnp.concatenate(parts, axis=1)
    pad = (-full.shape[1]) % CFG.pack_tile
    if pad:
        full = jnp.pad(full, ((0, 0), (0, pad), (0, 0)))
    return full


def _unpack_shard(packed, shapes, names):
    out, r0 = {}, 0
    for n, (cnt, padded) in zip(_BIG, _part_rows(shapes)):
        if n not in names:
            continue
        part = packed[r0:r0 + cnt]
        out[n] = (part.T if n in _COL_SHARDED else part).reshape((1,) + tuple(shapes[n]))
        r0 += padded
    return out


def kernel(x, mem, norm_mix, w_in, conv_gdn, a_log, dt_bias, gdn_out_norm, w_proj_gdn, w_proj_sb, w_out, norm_x, norm_mem, w_xq, w_xkv, xq_norm, xk_norm, w_xo, norm_ffn, w_up, conv_ffn, w_down, loss_target, m_norm_mix, m_w_in, m_conv_gdn, m_a_log, m_dt_bias, m_gdn_out_norm, m_w_proj_gdn, m_w_proj_sb, m_w_out, m_norm_x, m_norm_mem, m_w_xq, m_w_xkv, m_xq_norm, m_xk_norm, m_w_xo, m_norm_ffn, m_w_up, m_conv_ffn, m_w_down, v_norm_mix, v_w_in, v_conv_gdn, v_a_log, v_dt_bias, v_gdn_out_norm, v_w_proj_gdn, v_w_proj_sb, v_w_out, v_norm_x, v_norm_mem, v_w_xq, v_w_xkv, v_xq_norm, v_xk_norm, v_w_xo, v_norm_ffn, v_w_up, v_conv_ffn, v_w_down):
    names = ("norm_mix", "w_in", "conv_gdn", "a_log", "dt_bias", "gdn_out_norm", "w_proj_gdn", "w_proj_sb", "w_out",
             "norm_x", "norm_mem", "w_xq", "w_xkv", "xq_norm", "xk_norm", "w_xo", "norm_ffn", "w_up", "conv_ffn", "w_down")
    wts = dict(zip(names, (norm_mix, w_in, conv_gdn, a_log, dt_bias, gdn_out_norm, w_proj_gdn, w_proj_sb, w_out,
                           norm_x, norm_mem, w_xq, w_xkv, xq_norm, xk_norm, w_xo, norm_ffn, w_up, conv_ffn, w_down)))
    mom = dict(zip(names, (m_norm_mix, m_w_in, m_conv_gdn, m_a_log, m_dt_bias, m_gdn_out_norm, m_w_proj_gdn, m_w_proj_sb,
                           m_w_out, m_norm_x, m_norm_mem, m_w_xq, m_w_xkv, m_xq_norm, m_xk_norm, m_w_xo, m_norm_ffn, m_w_up,
                           m_conv_ffn, m_w_down)))
    vel = dict(zip(names, (v_norm_mix, v_w_in, v_conv_gdn, v_a_log, v_dt_bias, v_gdn_out_norm, v_w_proj_gdn, v_w_proj_sb,
                           v_w_out, v_norm_x, v_norm_mem, v_w_xq, v_w_xkv, v_xq_norm, v_xk_norm, v_w_xo, v_norm_ffn, v_w_up,
                           v_conv_ffn, v_w_down)))
    cfg = CFG
    d, bsz, s = cfg.d, cfg.b, cfg.s
    t = bsz * s
    gh, sbh = cfg.gh, cfg.sbh
    gw, sw = gh * HD, sbh * HD
    nchunk = s // cfg.gch
    mx, my, mc = _my_pos()
    me = 4 * mx + 2 * my + mc

    shard_shapes = {n: tuple(wts[n].shape[1:]) for n in _BIG}

    packed_w = _pack_big_shards({n: wts[n][0] for n in _BIG}, shard_shapes)
    packed_wb = _cast_rows(packed_w, BF16, "cast_weights")
    rows_in = _part_rows(shard_shapes)[0][1]
    full = _unpack_gathered(_all_gather_big(packed_wb[:rows_in], "all_gather_w_in"), shard_shapes, _BIG[:1])

    conv_rows = {n: _rows_of(wts[n][0], d) for n in _SMALL_CONV}
    conv_cnt = {n: conv_rows[n].shape[0] for n in _SMALL_CONV}
    conv_blk = _pad_rows(jnp.concatenate([conv_rows[n] for n in _SMALL_CONV], axis=0), SUBLANE)
    conv_all = jnp.zeros((N_DEV,) + conv_blk.shape, F32)
    conv_all = lax.dynamic_update_slice(conv_all, conv_blk[None], (me, 0, 0))
    conv_all = _all_reduce_small(conv_all.reshape(-1, d), "gather_conv_taps").reshape((N_DEV,) + conv_blk.shape)

    def full_conv(n, r0):
        k, cols = wts[n].shape[1], wts[n].shape[2]
        part = conv_all[:, r0:r0 + conv_cnt[n], :].reshape(N_DEV, -1)[:, :k * cols].reshape(N_DEV, k, cols)
        return part.transpose(1, 0, 2).reshape(k, N_DEV * cols)

    cgdn = full_conv("conv_gdn", 0)
    cffn = full_conv("conv_ffn", conv_cnt["conv_gdn"])

    win = full["w_in"]
    o_ab = 3 * gw
    o_z = o_ab + 2 * gh
    o_sb = o_z + gw
    o_gate = o_sb + 3 * sw
    w_qkv = win[:o_ab]
    w_ab = jnp.concatenate([win[o_ab:o_z], jnp.zeros((LANE - 2 * gh, d), win.dtype)], axis=0)
    w_z = win[o_z:o_sb]
    w_sb = win[o_sb:o_gate]
    w_gate = win[o_gate:]

    alog_p = jnp.concatenate([a_log.reshape(1, -1), jnp.zeros((1, LANE - gh), F32)], axis=1)
    dtb_p = jnp.concatenate([dt_bias.reshape(1, -1), jnp.zeros((1, LANE - gh), F32)], axis=1)
    onorm = gdn_out_norm.reshape(1, HD)

    x2 = x.reshape(t, d)
    tgt2 = loss_target.reshape(t, d)
    mem2 = mem.reshape(bsz * cfg.mem, d)

    (xn,) = _rowwise(_rms, [x2], [norm_mix], [(d, BF16)], [], name="norm_mix_fwd")
    p_qkv = _mm(xn, w_qkv, tb=True, name="proj_qkv")
    p_ab = _mm(xn, w_ab, tb=True, name="proj_ab")
    p_z = _mm(xn, w_z, tb=True, name="proj_z")
    p_sb = _mm(xn, w_sb, tb=True, name="proj_sb")
    p_gate = _mm(xn, w_gate, tb=True, name="proj_gate")

    qkv_c = _gdn_conv_fwd(p_qkv.reshape(bsz, s, 3 * gw), cgdn)
    (gb,) = _rowwise(_gates_fn, [p_ab], [alog_p, dtb_p], [(LANE, F32)], [], name="gdn_gates_fwd")
    gbt = gb.reshape(bsz, s, LANE)[:, :, :2 * gh].transpose(0, 2, 1).reshape(bsz, 2 * gh, nchunk, 1, cfg.gch)
    o_a, states = _gdn_fwd(qkv_c, p_z.reshape(bsz, s, gw), gbt, onorm)
    o_b, gathered = _sb_fwd(p_sb.reshape(bsz, s, 3 * sw), packed_wb[rows_in:])
    full.update(_unpack_gathered(gathered, shard_shapes, _BIG[1:]))

    pa = _mm(o_a.reshape(t, gw), full["w_proj_gdn"], name="proj_gdn_out")
    pb = _mm(o_b.reshape(t, sw), full["w_proj_sb"], name="proj_sb_out")

    def merge_fn(pa_, pb_, gate_):
        return _sigmoid(gate_[:, :d]) * pa_ + _sigmoid(gate_[:, d:]) * pb_

    (merged,) = _rowwise(merge_fn, [pa, pb, p_gate], [], [(d, BF16)], [], name="merge_fwd")
    def with_norm(r, g):
        return r, _rms(r, g)

    h1, hn_x = _mm(merged, full["w_out"], add=x2, name="mixer_out", epilogue=(with_norm, [], [norm_x], [F32, BF16], 0, True))
    (mn,) = _rowwise(_rms, [mem2], [norm_mem], [(d, BF16)], [], name="norm_mem_fwd")
    q_raw = _mm(hn_x, full["w_xq"], name="xattn_q")
    kv = _mm(mn, full["w_xkv"], tb=True, name="xattn_kv")
    xo = _xattn_fwd(q_raw.reshape(bsz, s, d), kv.reshape(bsz, cfg.mem, 2 * d), xq_norm, xk_norm)
    h2, hn_f = _mm(xo.reshape(t, d), full["w_xo"], add=h1, name="xattn_out",
                   epilogue=(with_norm, [], [norm_ffn], [F32, BF16], 0, True))
    up = _mm(hn_f, full["w_up"], tb=True, name="ffn_up")
    act = _ffn_conv_fwd(up.reshape(bsz, s, 2 * cfg.dff), cffn)
    def loss_fn(y_, tg_):
        err = y_ - tg_
        part = 0.5 * jnp.sum(err * err) / d
        return err / d, err / d, jnp.full((1, LANE), part, F32)

    dy, dy_b, loss_part = _mm(act.reshape(t, cfg.dff), full["w_down"], add=h2, name="ffn_down_loss",
                              epilogue=(loss_fn, [tgt2], [], [F32, BF16], 1, False))

    grads = {}
    dact = _mm(dy_b, full["w_down"], tb=True, name="d_act")
    grads["w_down"] = _mm(act.reshape(t, cfg.dff), dy_b, ta=True, name="dw_down")
    dup1, dup2, dcf1, dcf2 = _ffn_conv_bwd(up.reshape(bsz, s, 2 * cfg.dff), cffn, dact.reshape(bsz, s, cfg.dff))
    dup = jnp.concatenate([dup1, dup2], axis=2).reshape(t, 2 * cfg.dff)
    g_conv_ffn = jnp.concatenate([dcf1, dcf2], axis=1)
    dhn_f = _mm(dup, full["w_up"], name="d_hn_ffn")
    grads["w_up"] = _mm(dup, hn_f, ta=True, name="dw_up")

    def norm_bwd_fn(h_, res_, dn_, g_):
        _, f = jax.vjp(_rms, h_, g_)
        dh, dg = f(dn_)
        return res_ + dh, dg

    def norm_bwd_copy_fn(h_, res_, dn_, g_):
        dres, dg = norm_bwd_fn(h_, res_, dn_, g_)
        return dres, dres, dg

    dh2, dh2_b, g_norm_ffn = _rowwise(norm_bwd_copy_fn, [h2, dy, dhn_f], [norm_ffn], [(d, F32), (d, BF16)], [(1, d)],
                                      name="norm_ffn_bwd")

    dxo = _mm(dh2_b, full["w_xo"], tb=True, out_dtype=BF16, name="d_xo")
    grads["w_xo"] = _mm(xo.reshape(t, d), dh2_b, ta=True, name="dw_xo")
    dq_raw, dkv, g_xq_norm, g_xk_norm = _xattn_bwd(q_raw.reshape(bsz, s, d), kv.reshape(bsz, cfg.mem, 2 * d),
                                                   xq_norm, xk_norm, dxo.reshape(bsz, s, d))
    dq_raw2 = dq_raw.reshape(t, d)
    dkv2 = dkv.reshape(bsz * cfg.mem, 2 * d)
    dhn_x = _mm(dq_raw2, full["w_xq"], tb=True, name="d_hn_x")
    grads["w_xq"] = _mm(hn_x, dq_raw2, ta=True, name="dw_xq")
    dmn = _mm(dkv2, full["w_xkv"], name="d_mn")
    grads["w_xkv"] = _mm(dkv2, mn, ta=True, name="dw_xkv")

    def norm_w_bwd_fn(h_, dn_, g_):
        _, f = jax.vjp(lambda gg: _rms(h_, gg), g_)
        return f(dn_)[0]

    (g_norm_mem,) = _rowwise(norm_w_bwd_fn, [mem2, dmn], [norm_mem], [], [(1, d)], name="norm_mem_bwd")
    dh1, dh1_b, g_norm_x = _rowwise(norm_bwd_copy_fn, [h1, dh2, dhn_x], [norm_x], [(d, F32), (d, BF16)], [(1, d)],
                                    name="norm_x_bwd")

    dmerged = _mm(dh1_b, full["w_out"], tb=True, name="d_merged")
    grads["w_out"] = _mm(merged, dh1_b, ta=True, name="dw_out")

    def merge_bwd_fn(pa_, pb_, gate_, dm_):
        _, f = jax.vjp(merge_fn, pa_, pb_, gate_)
        return f(dm_)

    dpa, dpb, dgate = _rowwise(merge_bwd_fn, [pa, pb, p_gate, dmerged], [], [(d, BF16), (d, BF16), (2 * d, BF16)], [],
                               name="merge_bwd")
    do_a = _mm(dpa, full["w_proj_gdn"], tb=True, name="d_o_gdn")
    grads["w_proj_gdn"] = _mm(o_a.reshape(t, gw), dpa, ta=True, name="dw_proj_gdn")
    do_b = _mm(dpb, full["w_proj_sb"], tb=True, name="d_o_sb")
    grads["w_proj_sb"] = _mm(o_b.reshape(t, sw), dpb, ta=True, name="dw_proj_sb")

    c_idx = jnp.reshape(mc, (1,)).astype(jnp.int32)
    early = _BIG[1:]
    g_early = _pack_full_grads(grads, shard_shapes, early)
    r1_early = _exchange_sibling(g_early, "grads_to_sibling_early")
    s1_early = _sum_sibling(g_early, r1_early, c_idx, "sum_sibling_early")
    dsq, dsk, dsv, r2_early = _sb_bwd(p_sb.reshape(bsz, s, 3 * sw), do_b.reshape(bsz, s, sw), s1_early)
    dp_sb = [a.reshape(t, sw) for a in (dsq, dsk, dsv)]
    w_sb3 = [w_sb[i * sw:(i + 1) * sw] for i in range(3)]

    dqkv_c, dz, dg, dbeta, g_onorm = _gdn_bwd(qkv_c, p_z.reshape(bsz, s, gw), gbt, onorm, states,
                                              do_a.reshape(bsz, s, gw))
    dgb = jnp.concatenate([dg, dbeta], axis=1).reshape(bsz, 2 * gh, s).transpose(0, 2, 1)
    dgb = jnp.concatenate([dgb, jnp.zeros((bsz, s, LANE - 2 * gh), F32)], axis=2).reshape(t, LANE)

    def gates_bwd_fn(ab_, dgb_, alog_, dtb_):
        _, f = jax.vjp(_gates_fn, ab_, alog_, dtb_)
        return f(dgb_)

    dp_ab, g_alog, g_dtb = _rowwise(gates_bwd_fn, [p_ab, dgb], [alog_p, dtb_p], [(LANE, BF16)], [(1, LANE), (1, LANE)],
                                    name="gdn_gates_bwd")
    dp_qkv, g_conv_gdn = _gdn_conv_bwd(p_qkv.reshape(bsz, s, 3 * gw), cgdn, dqkv_c)
    dp_qkv = dp_qkv.reshape(t, 3 * gw)
    dp_z = dz.reshape(t, gw)

    grads["w_in"] = jnp.concatenate([
        _mm(dp_qkv, xn, ta=True, name="dw_in_qkv"),
        _mm(dp_ab, xn, ta=True, name="dw_in_ab")[:2 * gh],
        _mm(dp_z, xn, ta=True, name="dw_in_z"),
        _mm(dp_sb[0], xn, ta=True, name="dw_in_sbq"),
        _mm(dp_sb[1], xn, ta=True, name="dw_in_sbk"),
        _mm(dp_sb[2], xn, ta=True, name="dw_in_sbv"),
        _mm(dgate, xn, ta=True, name="dw_in_gate")], axis=0)
    g_late = _pack_full_grads(grads, shard_shapes, _BIG[:1])
    r1_late = _exchange_sibling(g_late, "grads_to_sibling_late")
    s1_late = _sum_sibling(g_late, r1_late, c_idx, "sum_sibling_late")
    dxn, r2_late = _dxn_fused([(dp_qkv, w_qkv), (dp_ab, w_ab), (dp_z, w_z), *zip(dp_sb, w_sb3), (dgate, w_gate)], s1_late)
    grad_x, g_norm_mix = _rowwise(norm_bwd_fn, [x2, dh1, dxn], [norm_mix], [(d, F32)], [(1, d)], name="norm_mix_bwd")

    small_g = {"norm_mix": g_norm_mix, "norm_x": g_norm_x, "norm_mem": g_norm_mem, "norm_ffn": g_norm_ffn,
               "a_log": g_alog[:, :gh], "dt_bias": g_dtb[:, :gh], "gdn_out_norm": g_onorm,
               "xq_norm": g_xq_norm, "xk_norm": g_xk_norm}
    sm_rows = [_rows_of(small_g[n], d) for n in _SMALL_REP] + [_rows_of(loss_part, d)]
    sm_rows += [_rows_of(g_conv_gdn, d), _rows_of(g_conv_ffn, d)]
    sm_cnt = [r.shape[0] for r in sm_rows]
    sm_sum = _all_reduce_small(_pad_rows(jnp.concatenate(sm_rows, axis=0), SUBLANE), "all_reduce_small_grads")
    offs = [0]
    for cnt in sm_cnt:
        offs.append(offs[-1] + cnt)
    small_grad = {}
    for i, n in enumerate(_SMALL_REP):
        small_grad[n] = sm_sum[offs[i]:offs[i + 1]].reshape(-1)[:wts[n].size].reshape(wts[n].shape)
    loss = sm_sum[offs[len(_SMALL_REP)], 0]
    for i, n in enumerate(_SMALL_CONV):
        k, cols = wts[n].shape[1], wts[n].shape[2]
        o = offs[len(_SMALL_REP) + 1 + i]
        fullg = sm_sum[o:o + sm_cnt[len(_SMALL_REP) + 1 + i]].reshape(-1)[:k * cols * N_DEV].reshape(k, N_DEV * cols)
        small_grad[n] = lax.dynamic_slice(fullg, (0, me * cols), (k, cols)).reshape(wts[n].shape)

    small_names = _SMALL_REP + _SMALL_CONV

    def pack_small(src):
        return _pad_rows(jnp.concatenate([_rows_of(src[n], d) for n in small_names], axis=0), SUBLANE)

    sw_, sg_, sm_, sv_ = pack_small(wts), pack_small(small_grad), pack_small(mom), pack_small(vel)
    sd_, snm_, snv_ = _rowwise(_adamw_math, [sw_, sg_, sm_, sv_], [], [(d, F32)] * 3, [], name="adamw_small", tm=sw_.shape[0])

    def unpack_small(packed):
        out, r0 = {}, 0
        for n in small_names:
            cnt = _rows_of(wts[n], d).shape[0]
            out[n] = packed[r0:r0 + cnt].reshape(-1)[:wts[n].size].reshape(wts[n].shape)
            r0 += cnt
        return out

    small_delta, small_m, small_v = unpack_small(sd_), unpack_small(snm_), unpack_small(snv_)

    idx = jnp.stack([me, 2 * mx + my]).astype(jnp.int32)
    pm = _pack_big_shards({n: mom[n][0] for n in _BIG}, shard_shapes)
    pv = _pack_big_shards({n: vel[n][0] for n in _BIG}, shard_shapes)
    upd_late = _adamw_big(g_late, r1_late, r2_late, packed_w, pm, pv, idx, 0, "adamw_late")
    upd_early = _adamw_big(g_early, r1_early, r2_early, packed_w, pm, pv, idx, g_late.shape[1], "adamw_early")
    big_grad, big_delta, big_m, big_v = (
        {**_unpack_shard(a, shard_shapes, _BIG[:1]), **_unpack_shard(b, shard_shapes, early)}
        for a, b in zip(upd_late, upd_early))

    def pick(big, small, n):
        return big[n] if n in big else small[n]

    outs = [loss, grad_x.reshape(bsz, s, d)]
    outs += [pick(big_grad, small_grad, n) for n in names]
    outs += [pick(big_delta, small_delta, n) for n in names]
    outs += [pick(big_m, small_m, n) for n in names]
    outs += [pick(big_v, small_v, n) for n in names]
    return tuple(outs)
```

```python
import functools

import jax
import jax.numpy as jnp
from jax import lax
from jax.experimental import pallas as pl
from jax.experimental.pallas import tpu as pltpu

F32 = jnp.float32
BF16 = jnp.bfloat16
FDOT_PRECISION = lax.Precision.HIGH

LANE = 128
SUBLANE = 8
PACK_ROW_ALIGN = 16
VMEM_LIMIT = 56 * 2 ** 20
N_DEV = 8
MESH = pl.DeviceIdType.MESH

EPS = 1e-6
ADAM_LR = 0.001
ADAM_B1 = 0.9
ADAM_B2 = 0.999
ADAM_EPS = 1e-08
ADAM_WD = 0.01
ADAM_STEP = 10


class _Cfg:
    d = 1024
    b = 4
    s = 2048
    mem = 256
    gh = 8
    gch = 64
    sbh = 8
    xh = 4
    dff = 2816
    pack_tile = 256


CFG = _Cfg()
HD = 128
SB_BLK = 128
SB_HP = 4
GDN_HP = 8
MM_TILES = (1024, 1408, 704, 512, 256, 128)
MM_K_TILES = (1024, 1408, 704, 512, 256, 128)
MM_K_TILES_F32 = (512, 704, 256, 128)
CONV_CB = 256
XQ_TILE = 256


def _tile(n, prefs):
    for t in prefs:
        if n % t == 0:
            return t
    raise ValueError(f"no tile for {n}")


def _cp(sem, **kw):
    return pltpu.CompilerParams(dimension_semantics=sem, vmem_limit_bytes=VMEM_LIMIT, **kw)


def _dims(kind):
    return {"nn": (((1,), (0,)), ((), ())), "nt": (((1,), (1,)), ((), ())), "tn": (((0,), (0,)), ((), ()))}[kind]


def _raw_bdot(a, b, kind):
    return lax.dot_general(a.astype(BF16), b.astype(BF16), _dims(kind), preferred_element_type=F32)


def _raw_fdot(a, b, kind):
    return lax.dot_general(a.astype(F32), b.astype(F32), _dims(kind), precision=FDOT_PRECISION,
                           preferred_element_type=F32)


def _make_dot(raw):
    @functools.partial(jax.custom_vjp, nondiff_argnums=(2,))
    def dot(a, b, kind):
        return raw(a, b, kind)

    def fwd(a, b, kind):
        return raw(a, b, kind), (a, b)

    def bwd(kind, res, g):
        a, b = res
        if kind == "nn":
            return raw(g, b, "nt").astype(a.dtype), raw(a, g, "tn").astype(b.dtype)
        if kind == "nt":
            return raw(g, b, "nn").astype(a.dtype), raw(g, a, "tn").astype(b.dtype)
        return raw(b, g, "nt").astype(a.dtype), raw(a, g, "nn").astype(b.dtype)

    dot.defvjp(fwd, bwd)
    return dot


_bdot = _make_dot(_raw_bdot)
_fdot = _make_dot(_raw_fdot)


def _split_dot(x, m01):
    hi = x.astype(BF16)
    lo = (x - hi.astype(F32)).astype(BF16)
    return (lax.dot_general(hi, m01, _dims("nn"), preferred_element_type=F32)
            + lax.dot_general(lo, m01, _dims("nn"), preferred_element_type=F32))


_sigmoid = jax.nn.sigmoid


def _silu(x):
    return x * _sigmoid(x)


def _softplus(x):
    return jnp.maximum(x, 0.0) + jnp.log1p(jnp.exp(-jnp.abs(x)))


def _rms(x, g):
    return x * lax.rsqrt(jnp.mean(x * x, axis=-1, keepdims=True) + EPS) * g


def _iota2(shape, dim):
    return lax.broadcasted_iota(jnp.int32, shape, dim)


def _mm(a, b, *, ta=False, tb=False, add=None, out_dtype=F32, name, epilogue=None):
    if ta:
        kd, m = a.shape
    else:
        m, kd = a.shape
    if tb:
        n, kb = b.shape
    else:
        kb, n = b.shape
    assert kd == kb, (a.shape, b.shape, ta, tb)
    fn, e_rows, e_pars, e_dtypes, n_acc, full_rows = epilogue or (None, [], [], [out_dtype], 0, False)
    tm = _tile(m, MM_TILES)
    tn = n if full_rows else _tile(n, MM_TILES)
    wide = max(a.dtype.itemsize, b.dtype.itemsize) > 2
    tk = _tile(kd, MM_K_TILES_F32 if wide else MM_K_TILES)
    nk = kd // tk
    kind_dims = (((0 if ta else 1,), (1 if tb else 0,)), ((), ()))
    n_in = 2 + (add is not None) + len(e_rows) + len(e_pars)
    n_out = len(e_dtypes)

    def body(*refs):
        a_ref, b_ref = refs[:2]
        add_ref = refs[2] if add is not None else None
        extra = refs[2 + (add is not None):n_in]
        o_refs = refs[n_in:n_in + n_out]
        acc_refs = refs[n_in + n_out:n_in + n_out + n_acc]
        part = lax.dot_general(a_ref[...].astype(BF16), b_ref[...].astype(BF16), kind_dims,
                               preferred_element_type=F32)

        def finish(r):
            if add is not None:
                r = r + add_ref[...].astype(F32)
            outs = (r,) if fn is None else fn(r, *[e[...] for e in extra])
            for o_ref, val in zip(o_refs, outs[:n_out]):
                o_ref[...] = val.astype(o_ref.dtype)
            if n_acc:
                first = (pl.program_id(0) == 0) & (pl.program_id(1) == 0)

                @pl.when(first)
                def _():
                    for ref in acc_refs:
                        ref[...] = jnp.zeros_like(ref)

                for ref, val in zip(acc_refs, outs[n_out:]):
                    ref[...] += val

        if nk == 1:
            finish(part)
            return
        acc = refs[-1]
        k = pl.program_id(2)

        @pl.when(k == 0)
        def _():
            acc[...] = part

        @pl.when((k > 0) & (k < nk - 1))
        def _():
            acc[...] += part

        @pl.when(k == nk - 1)
        def _():
            finish(acc[...] + part)

    a_spec = pl.BlockSpec((tk, tm), lambda i, j, k: (k, i)) if ta else pl.BlockSpec((tm, tk), lambda i, j, k: (i, k))
    b_spec = pl.BlockSpec((tn, tk), lambda i, j, k: (j, k)) if tb else pl.BlockSpec((tk, tn), lambda i, j, k: (k, j))
    tile = pl.BlockSpec((tm, tn), lambda i, j, k: (i, j))
    in_specs = [a_spec, b_spec] + [tile] * ((add is not None) + len(e_rows))
    in_specs += [pl.BlockSpec((1, tn), lambda i, j, k: (0, j)) for _ in e_pars]
    args = [a, b] + ([add] if add is not None else []) + list(e_rows) + list(e_pars)
    out_specs = [tile] * n_out + [pl.BlockSpec((1, LANE), lambda i, j, k: (0, 0))] * n_acc
    out_shape = [jax.ShapeDtypeStruct((m, n), dt) for dt in e_dtypes] + [jax.ShapeDtypeStruct((1, LANE), F32)] * n_acc
    res = pl.pallas_call(
        body, name=name, grid=(m // tm, n // tn, nk),
        in_specs=in_specs, out_specs=out_specs, out_shape=out_shape,
        scratch_shapes=[pltpu.VMEM((tm, tn), F32)] if nk > 1 else [],
        compiler_params=_cp(("arbitrary", "arbitrary", "arbitrary") if n_acc else ("parallel", "parallel", "arbitrary")),
    )(*args)
    return res[0] if epilogue is None else res


def _rowwise(fn, rows, pars, out_rows, out_accs, *, name, tm=None):
    t = rows[0].shape[0]
    if tm is None:
        tm = _tile(t, (256, 128, 64, 32, 16))
    assert t % tm == 0, (t, tm)
    n_r, n_p, n_or, n_oa = len(rows), len(pars), len(out_rows), len(out_accs)

    def body(*refs):
        r_in = refs[:n_r]
        p_in = refs[n_r:n_r + n_p]
        o_r = refs[n_r + n_p:n_r + n_p + n_or]
        o_a = refs[n_r + n_p + n_or:]
        outs = fn(*[r[...] for r in r_in], *[p[...] for p in p_in])
        if not isinstance(outs, (tuple, list)):
            outs = (outs,)
        assert len(outs) == n_or + n_oa, (name, len(outs))
        for ref, val in zip(o_r, outs[:n_or]):
            ref[...] = val.astype(ref.dtype)
        if n_oa:
            @pl.when(pl.program_id(0) == 0)
            def _():
                for ref in o_a:
                    ref[...] = jnp.zeros_like(ref)

            for ref, val in zip(o_a, outs[n_or:]):
                ref[...] += val.astype(F32)

    in_specs = [pl.BlockSpec((tm, r.shape[1]), lambda i: (i, 0)) for r in rows]
    in_specs += [pl.BlockSpec(p.shape, lambda i: (0, 0)) for p in pars]
    out_specs = [pl.BlockSpec((tm, c), lambda i: (i, 0)) for c, _ in out_rows]
    out_specs += [pl.BlockSpec(s, lambda i: (0, 0)) for s in out_accs]
    out_shape = [jax.ShapeDtypeStruct((t, c), dt) for c, dt in out_rows]
    out_shape += [jax.ShapeDtypeStruct(s, F32) for s in out_accs]
    return pl.pallas_call(
        body, name=name, grid=(t // tm,), in_specs=in_specs, out_specs=out_specs, out_shape=out_shape,
        compiler_params=_cp(("arbitrary",)),
    )(*rows, *pars)


def _shift_down(x, sh):
    rolled = pltpu.roll(x, sh, 0)
    top = rolled[:SUBLANE]
    top = jnp.where(_iota2(top.shape, 0) >= sh, top, 0.0)
    return jnp.concatenate([top, rolled[SUBLANE:]], axis=0)


def _shift_up(x, sh):
    s = x.shape[0]
    rolled = pltpu.roll(x, s - sh, 0)
    bottom = rolled[s - SUBLANE:]
    bottom = jnp.where(_iota2(bottom.shape, 0) < SUBLANE - sh, bottom, 0.0)
    return jnp.concatenate([rolled[:s - SUBLANE], bottom], axis=0)


def _conv(x, w):
    k = w.shape[0]
    y = x * w[k - 1:k, :]
    for i in range(k - 1):
        y = y + _shift_down(x, k - 1 - i) * w[i:i + 1, :]
    return y


def _conv_bwd(x, w, dy):
    k = w.shape[0]
    dx = dy * w[k - 1:k, :]
    dws = []
    for i in range(k - 1):
        dx = dx + _shift_up(dy, k - 1 - i) * w[i:i + 1, :]
        dws.append(jnp.sum(dy * _shift_down(x, k - 1 - i), axis=0, keepdims=True))
    dws.append(jnp.sum(dy * x, axis=0, keepdims=True))
    return dx, dws


def _gdn_post(y, j, nqb):
    a = _silu(y)
    sc = jnp.where(j < nqb, HD ** -0.5, 1.0).astype(F32)
    outs = []
    for h in range(y.shape[1] // HD):
        ah = a[:, h * HD:(h + 1) * HD]
        l2 = ah * lax.rsqrt(jnp.sum(ah * ah, axis=-1, keepdims=True) + EPS)
        outs.append(jnp.where(j < 2 * nqb, l2 * sc, ah))
    return jnp.concatenate(outs, axis=1) if len(outs) > 1 else outs[0]


def _gdn_conv_fwd(x, w):
    bsz, s, c3 = x.shape
    k = w.shape[0]
    nb = c3 // CONV_CB
    nqb = nb // 3

    def body(x_ref, w_ref, o_ref):
        j = pl.program_id(1)
        o_ref[0] = _gdn_post(_conv(x_ref[0], w_ref[...]), j, nqb)

    return pl.pallas_call(
        body, name="gdn_conv_fwd", grid=(bsz, nb),
        in_specs=[pl.BlockSpec((1, s, CONV_CB), lambda b, j: (b, 0, j)), pl.BlockSpec((k, CONV_CB), lambda b, j: (0, j))],
        out_specs=pl.BlockSpec((1, s, CONV_CB), lambda b, j: (b, 0, j)),
        out_shape=jax.ShapeDtypeStruct(x.shape, F32),
        compiler_params=_cp(("parallel", "parallel")),
    )(x, w)


def _gdn_conv_bwd(x, w, dout):
    bsz, s, c3 = x.shape
    k = w.shape[0]
    nb = c3 // CONV_CB
    nqb = nb // 3

    def body(x_ref, w_ref, d_ref, dx_ref, dw_ref):
        j = pl.program_id(0)
        b = pl.program_id(1)
        xv, wv = x_ref[0], w_ref[...]
        y = _conv(xv, wv)
        _, f = jax.vjp(lambda yy: _gdn_post(yy, j, nqb), y)
        (dy,) = f(d_ref[0])
        dx, dws = _conv_bwd(xv, wv, dy)
        dx_ref[0] = dx.astype(dx_ref.dtype)

        @pl.when(b == 0)
        def _():
            dw_ref[...] = jnp.zeros_like(dw_ref)

        for i in range(k):
            dw_ref[i:i + 1, :] += dws[i]

    return pl.pallas_call(
        body, name="gdn_conv_bwd", grid=(nb, bsz),
        in_specs=[pl.BlockSpec((1, s, CONV_CB), lambda j, b: (b, 0, j)), pl.BlockSpec((k, CONV_CB), lambda j, b: (0, j)),
                  pl.BlockSpec((1, s, CONV_CB), lambda j, b: (b, 0, j))],
        out_specs=[pl.BlockSpec((1, s, CONV_CB), lambda j, b: (b, 0, j)), pl.BlockSpec((k, CONV_CB), lambda j, b: (0, j))],
        out_shape=[jax.ShapeDtypeStruct(x.shape, BF16), jax.ShapeDtypeStruct(w.shape, F32)],
        compiler_params=_cp(("parallel", "arbitrary")),
    )(x, w, dout)


def _ffn_conv_fwd(up, w):
    bsz, s, c2 = up.shape
    k = w.shape[0]
    nb = (c2 // 2) // CONV_CB

    def body(x1_ref, x2_ref, w1_ref, w2_ref, o_ref):
        u1 = _conv(x1_ref[0], w1_ref[...])
        u2 = _conv(x2_ref[0], w2_ref[...])
        o_ref[0] = (_silu(u1) * u2).astype(o_ref.dtype)

    return pl.pallas_call(
        body, name="ffn_conv_fwd", grid=(bsz, nb),
        in_specs=[pl.BlockSpec((1, s, CONV_CB), lambda b, j: (b, 0, j)), pl.BlockSpec((1, s, CONV_CB), lambda b, j: (b, 0, j + nb)),
                  pl.BlockSpec((k, CONV_CB), lambda b, j: (0, j)), pl.BlockSpec((k, CONV_CB), lambda b, j: (0, j + nb))],
        out_specs=pl.BlockSpec((1, s, CONV_CB), lambda b, j: (b, 0, j)),
        out_shape=jax.ShapeDtypeStruct((bsz, s, c2 // 2), BF16),
        compiler_params=_cp(("parallel", "parallel")),
    )(up, up, w, w)


def _ffn_conv_bwd(up, w, dact):
    bsz, s, c2 = up.shape
    k = w.shape[0]
    half = c2 // 2
    nb = half // CONV_CB

    def body(x1_ref, x2_ref, w1_ref, w2_ref, d_ref, dx1_ref, dx2_ref, dw1_ref, dw2_ref):
        b = pl.program_id(1)
        x1, x2, w1, w2 = x1_ref[0], x2_ref[0], w1_ref[...], w2_ref[...]
        u1 = _conv(x1, w1)
        u2 = _conv(x2, w2)
        _, f = jax.vjp(lambda p, q: _silu(p) * q, u1, u2)
        du1, du2 = f(d_ref[0])
        dx1, dws1 = _conv_bwd(x1, w1, du1)
        dx2, dws2 = _conv_bwd(x2, w2, du2)
        dx1_ref[0] = dx1.astype(dx1_ref.dtype)
        dx2_ref[0] = dx2.astype(dx2_ref.dtype)

        @pl.when(b == 0)
        def _():
            dw1_ref[...] = jnp.zeros_like(dw1_ref)
            dw2_ref[...] = jnp.zeros_like(dw2_ref)

        for i in range(k):
            dw1_ref[i:i + 1, :] += dws1[i]
            dw2_ref[i:i + 1, :] += dws2[i]

    def blk(off):
        return pl.BlockSpec((1, s, CONV_CB), lambda j, b: (b, 0, j + off))

    def wblk(off):
        return pl.BlockSpec((k, CONV_CB), lambda j, b: (0, j + off))

    return pl.pallas_call(
        body, name="ffn_conv_bwd", grid=(nb, bsz),
        in_specs=[blk(0), blk(nb), wblk(0), wblk(nb), blk(0)],
        out_specs=[blk(0), blk(0), wblk(0), wblk(0)],
        out_shape=[jax.ShapeDtypeStruct((bsz, s, half), BF16), jax.ShapeDtypeStruct((bsz, s, half), BF16),
                   jax.ShapeDtypeStruct((k, half), F32), jax.ShapeDtypeStruct((k, half), F32)],
        compiler_params=_cp(("parallel", "arbitrary")),
    )(up, up, w, w, dact)


@jax.custom_vjp
def _inv_unit_lower(mats):
    c = mats[0].shape[0]
    eye = (_iota2((c, c), 0) == _iota2((c, c), 1)).astype(F32)
    ps = [-a for a in mats]
    ts = [eye + p for p in ps]
    n = 2
    while n < c:
        ps = [_raw_fdot(p, p, "nn") for p in ps]
        ts = [t + _raw_fdot(t, p, "nn") for t, p in zip(ts, ps)]
        n *= 2
    return ts


def _inv_fwd(mats):
    ts = _inv_unit_lower(mats)
    return ts, ts


def _inv_bwd(ts, gs):
    xs = [_raw_fdot(g, t, "nt") for g, t in zip(gs, ts)]
    return ([-_raw_fdot(t, x, "tn") for t, x in zip(ts, xs)],)


_inv_unit_lower.defvjp(_inv_fwd, _inv_bwd)


@jax.custom_vjp
def _inv_known(mats, ts):
    return ts


def _inv_known_fwd(mats, ts):
    return ts, ts


def _inv_known_bwd(ts, gs):
    return _inv_bwd(ts, gs)[0], [jnp.zeros_like(t) for t in ts]


_inv_known.defvjp(_inv_known_fwd, _inv_known_bwd)


def _gdn_chunk(q, k, v, z, g_row, beta_row, state, onorm, tinv_known=None):
    nh = range(len(q))
    c = q[0].shape[0]
    ii, jj = _iota2((c, c), 0), _iota2((c, c), 1)
    incl, strict, eye = ii >= jj, ii > jj, ii == jj

    def to_col(row):
        return jnp.sum(jnp.where(eye, jnp.broadcast_to(row, (c, c)), 0.0), axis=1, keepdims=True)

    gc_col = [jnp.sum(jnp.where(incl, jnp.broadcast_to(g_row[h], (c, c)), 0.0), axis=1, keepdims=True) for h in nh]
    gc_row = [jnp.sum(jnp.where(eye, jnp.broadcast_to(gc_col[h], (c, c)), 0.0), axis=0, keepdims=True) for h in nh]
    beta_col = [to_col(beta_row[h]) for h in nh]
    gc_last = [jnp.sum(g_row[h], axis=1, keepdims=True) for h in nh]
    decay = [jnp.where(incl, jnp.exp(jnp.where(incl, gc_col[h] - gc_row[h], 0.0)), 0.0) for h in nh]
    kk = [_bdot(k[h], k[h], "nt") for h in nh]
    qk = [_bdot(q[h], k[h], "nt") * decay[h] for h in nh]
    mats = [jnp.where(strict, beta_col[h] * kk[h] * decay[h], 0.0) for h in nh]
    tinv = _inv_unit_lower(mats) if tinv_known is None else _inv_known(mats, tinv_known)
    rhs =[jnp.concatenate([v[h] * beta_col[h], k[h] * (beta_col[h] * jnp.exp(gc_col[h]))], axis=1) for h in nh]
    uw = [_fdot(tinv[h], rhs[h], "nn") for h in nh]
    dv = v[0].shape[1]
    ws = [_bdot(uw[h][:, dv:], state[h], "nn") for h in nh]
    qs = [_bdot(q[h] * jnp.exp(gc_col[h]), state[h], "nn") for h in nh]
    v_new = [uw[h][:, :dv] - ws[h] for h in nh]
    o = [qs[h] + _bdot(qk[h], v_new[h], "nn") for h in nh]
    kv = [_bdot(k[h] * jnp.exp(gc_last[h] - gc_col[h]), v_new[h], "tn") for h in nh]
    new_state = [state[h] * jnp.exp(gc_last[h]) + kv[h] for h in nh]
    y = [_rms(o[h], onorm) * _silu(z[h]) for h in nh]
    return y, new_state, tinv


def _gdn_specs(s, c, reverse):
    n = s // c
    nn = (lambda i: n - 1 - i) if reverse else (lambda i: i)

    def qkv(off):
        return pl.BlockSpec((1, c, GDN_HP * HD), lambda b, h, i: (b, nn(i), h + off))

    def gate(off):
        return pl.BlockSpec((1, GDN_HP, 1, 1, c), lambda b, h, i: (b, h + off, nn(i), 0, 0))

    st = pl.BlockSpec((1, GDN_HP, 1, HD, HD), lambda b, h, i: (b, h, nn(i), 0, 0))
    ti = pl.BlockSpec((1, GDN_HP, 1, c, c), lambda b, h, i: (b, h, nn(i), 0, 0))
    onorm = pl.BlockSpec((1, HD), lambda b, h, i: (0, 0))
    return n, qkv, gate, (st, ti), onorm


def _gdn_fwd(qkv, z, gbt, onorm):
    bsz, s, _ = qkv.shape
    gh, c = CFG.gh, CFG.gch
    ng = gh // GDN_HP
    n, qs, gs, (st, ti), on = _gdn_specs(s, c, False)

    def body(q_ref, k_ref, v_ref, z_ref, g_ref, b_ref, on_ref, y_ref, st_ref, ti_ref, state):
        @pl.when(pl.program_id(2) == 0)
        def _():
            state[...] = jnp.zeros_like(state)

        nh = range(GDN_HP)
        hs = [slice(h * HD, (h + 1) * HD) for h in nh]
        s_in = [state[h] for h in nh]
        for h in nh:
            st_ref[0, h, 0] = s_in[h]
        y, s_out, tinv = _gdn_chunk([q_ref[0, :, hs[h]] for h in nh], [k_ref[0, :, hs[h]] for h in nh],
                                    [v_ref[0, :, hs[h]] for h in nh], [z_ref[0, :, hs[h]] for h in nh],
                                    [g_ref[0, h, 0] for h in nh], [b_ref[0, h, 0] for h in nh], s_in, on_ref[...])
        for h in nh:
            y_ref[0, :, hs[h]] = y[h].astype(y_ref.dtype)
            ti_ref[0, h, 0] = tinv[h]
            state[h] = s_out[h]

    return pl.pallas_call(
        body, name="gdn_fwd", grid=(bsz, ng, n),
        in_specs=[qs(0), qs(ng), qs(2 * ng), qs(0), gs(0), gs(ng), on],
        out_specs=[qs(0), st, ti],
        out_shape=[jax.ShapeDtypeStruct((bsz, s, gh * HD), BF16), jax.ShapeDtypeStruct((bsz, gh, n, HD, HD), F32),
                   jax.ShapeDtypeStruct((bsz, gh, n, c, c), F32)],
        scratch_shapes=[pltpu.VMEM((GDN_HP, HD, HD), F32)],
        compiler_params=_cp(("parallel", "parallel", "arbitrary")),
    )(qkv, qkv, qkv, z, gbt, gbt, onorm)


def _gdn_bwd(qkv, z, gbt, onorm, states, tinvs, dy):
    bsz, s, _ = qkv.shape
    gh, c = CFG.gh, CFG.gch
    ng = gh // GDN_HP
    assert ng == 1, "d(q | k | v) is written as one block of all heads"
    w = gh * HD
    n, qs, gs, (st, ti), on = _gdn_specs(s, c, True)
    dqkv_spec = pl.BlockSpec((1, c, 3 * w), lambda b, h, i: (b, n - 1 - i, 0))

    def body(q_ref, k_ref, v_ref, z_ref, g_ref, b_ref, on_ref, st_ref, ti_ref, dy_ref,
             dqkv_ref, dz_ref, dg_ref, db_ref, don_ref, dstate):
        first = (pl.program_id(0) == 0) & (pl.program_id(1) == 0) & (pl.program_id(2) == 0)

        @pl.when(first)
        def _():
            don_ref[...] = jnp.zeros_like(don_ref)

        @pl.when(pl.program_id(2) == 0)
        def _():
            dstate[...] = jnp.zeros_like(dstate)

        nh = range(GDN_HP)
        hs = [slice(h * HD, (h + 1) * HD) for h in nh]
        tinv = [ti_ref[0, h, 0] for h in nh]

        def chunk(*args):
            return _gdn_chunk(*args, tinv_known=tinv)[:2]

        _, f = jax.vjp(chunk, [q_ref[0, :, hs[h]] for h in nh], [k_ref[0, :, hs[h]] for h in nh],
                       [v_ref[0, :, hs[h]] for h in nh], [z_ref[0, :, hs[h]] for h in nh],
                       [g_ref[0, h, 0] for h in nh], [b_ref[0, h, 0] for h in nh],
                       [st_ref[0, h, 0] for h in nh], on_ref[...])
        dq, dk, dv, dz, dg, db, ds, don = f(([dy_ref[0, :, hs[h]] for h in nh], [dstate[h] for h in nh]))
        for h in nh:
            dqkv_ref[0, :, h * HD:(h + 1) * HD] = dq[h]
            dqkv_ref[0, :, w + h * HD:w + (h + 1) * HD] = dk[h]
            dqkv_ref[0, :, 2 * w + h * HD:2 * w + (h + 1) * HD] = dv[h]
            dz_ref[0, :, hs[h]] = dz[h].astype(dz_ref.dtype)
            dg_ref[0, h, 0] = dg[h]
            db_ref[0, h, 0] = db[h]
            dstate[h] = ds[h]
        don_ref[...] += don

    act = jax.ShapeDtypeStruct((bsz, s, gh * HD), F32)
    gshape = jax.ShapeDtypeStruct((bsz, gh, n, 1, c), F32)
    return pl.pallas_call(
        body, name="gdn_bwd", grid=(bsz, ng, n),
        in_specs=[qs(0), qs(ng), qs(2 * ng), qs(0), gs(0), gs(ng), on, st, ti, qs(0)],
        out_specs=[dqkv_spec, qs(0), gs(0), gs(0), on],
        out_shape=[jax.ShapeDtypeStruct((bsz, s, 3 * w), F32), jax.ShapeDtypeStruct(act.shape, BF16), gshape, gshape,
                   jax.ShapeDtypeStruct((1, HD), F32)],
        scratch_shapes=[pltpu.VMEM((GDN_HP, HD, HD), F32)],
        compiler_params=_cp(("arbitrary", "arbitrary", "arbitrary")),
    )(qkv, qkv, qkv, z, gbt, gbt, onorm, states, tinvs, dy)


def _gates_fn(ab, alog, dtb):
    lane = _iota2(ab.shape, 1)
    g = -jnp.exp(alog) * _softplus(ab + dtb)
    beta = _sigmoid(ab)
    return jnp.where(lane < CFG.gh, g, jnp.where(lane < 2 * CFG.gh, beta, 0.0))


def _heads_cumsum(xs, tri):
    n = xs[0].shape[0]
    y = _split_dot(jnp.concatenate(xs, axis=0), tri)
    return [y[h * n:(h + 1) * n] for h in range(len(xs))]


def _blk_off(jblk):
    return jblk * SB_BLK if isinstance(jblk, int) else pl.multiple_of(jblk * SB_BLK, SB_BLK)


def _sb_span(qs, k_spans, mask, runs, tri_su):
    nh = range(len(qs))
    nb = k_spans[0].shape[0] // SB_BLK
    zs = [lax.dot_general(qs[h], k_spans[h], _dims("nt"), preferred_element_type=F32) for h in nh]
    l1p = [jnp.log(1.0 + jnp.exp(-jnp.abs(z))) for z in zs]
    lss = [jnp.minimum(zs[h], 0.0) - l1p[h] for h in nh]
    lfs = [lss[h] - zs[h] for h in nh]
    if mask is not None:
        lfs = [jnp.where(mask, lf, 0.0) for lf in lfs]
    units = [lfs[h][:, b * SB_BLK:(b + 1) * SB_BLK] for h in nh for b in range(nb)]
    cums = _heads_cumsum(units, tri_su)
    sfx, new_runs = [], []
    for h in nh:
        run, parts = runs[h], [None] * nb
        for b in reversed(range(nb)):
            parts[b] = cums[h * nb + b] + run
            run = run + jnp.sum(units[h * nb + b], axis=1, keepdims=True)
        sfx.append(jnp.concatenate(parts, axis=1) if nb > 1 else parts[0])
        new_runs.append(run)
    ws = [jnp.exp(lss[h] + sfx[h]) for h in nh]
    if mask is not None:
        ws = [jnp.where(mask, w, 0.0) for w in ws]
    return zs, lfs, ws, new_runs


def _sb_specs(s, w):
    def qb(off):
        return pl.BlockSpec((1, SB_BLK, w), lambda b, h, i: (b, i, h + off))

    def full(off):
        return pl.BlockSpec((1, s, w), lambda b, h, i: (b, 0, h + off))

    return qb, full


def _sb_fwd(qkv, shard):
    bsz, s, _ = qkv.shape
    ng = CFG.sbh // SB_HP
    w = SB_HP * HD
    scale = HD ** -0.5
    qb, full = _sb_specs(s, w)

    def body(q_ref, k_ref, v_ref, x_ref, o_ref, g_ref, send_sems, recv_sems, local_sem):
        i = pl.program_id(2)
        begin, relay, finish = _gather_stages(x_ref, g_ref, send_sems, recv_sems, local_sem)
        start_of_group = (pl.program_id(1) == 0) & (i == 0)
        pl.when((pl.program_id(0) == 0) & start_of_group)(begin)
        pl.when((pl.program_id(0) == bsz // 2) & start_of_group)(relay)

        r, c = _iota2((SB_BLK, SB_BLK), 0), _iota2((SB_BLK, SB_BLK), 1)
        tri_su = (r > c).astype(BF16)
        nh = range(SB_HP)
        hs = [slice(h * HD, (h + 1) * HD) for h in nh]
        qs = [(q_ref[0, :, hs[h]] * scale).astype(BF16) for h in nh]

        def span(off, nb, mask, carry):
            ks = [k_ref[0, pl.ds(off, nb * SB_BLK), hs[h]].astype(BF16) for h in nh]
            vs = [v_ref[0, pl.ds(off, nb * SB_BLK), hs[h]].astype(BF16) for h in nh]
            _, _, ws, runs = _sb_span(qs, ks, mask, [cr[1] for cr in carry], tri_su)
            pv = [lax.dot_general(ws[h].astype(BF16), vs[h], _dims("nn"), preferred_element_type=F32) for h in nh]
            return tuple((carry[h][0] + pv[h], runs[h]) for h in nh)

        carry = tuple((jnp.zeros((SB_BLK, HD), F32), jnp.zeros((SB_BLK, 1), F32)) for _ in nh)
        carry = span(_blk_off(i), 1, c < r, carry)
        rem = jnp.bitwise_and(i, 3)
        carry = lax.fori_loop(0, lax.shift_right_logical(i, 2),
                              lambda p, cr: span(_blk_off(i - 4 - 4 * p), 4, None, cr), carry)
        carry = lax.fori_loop(0, lax.shift_right_logical(rem, 1),
                              lambda _, cr: span(_blk_off(jnp.bitwise_and(rem, 1)), 2, None, cr), carry)
        carry = lax.fori_loop(0, jnp.bitwise_and(rem, 1), lambda _, cr: span(0, 1, None, cr), carry)
        for h in nh:
            o_ref[0, :, hs[h]] = carry[h][0].astype(o_ref.dtype)

        pl.when((pl.program_id(0) == bsz - 1) & (pl.program_id(1) == ng - 1) & (i == nblk - 1))(finish)

    nblk = s // SB_BLK
    hbm = pl.BlockSpec(memory_space=pl.ANY)
    return pl.pallas_call(
        body, name="sb_fwd", grid=(bsz, ng, nblk),
        in_specs=[qb(0), full(ng), full(2 * ng), hbm], out_specs=[qb(0), hbm],
        out_shape=[jax.ShapeDtypeStruct((bsz, s, CFG.sbh * HD), BF16),
                   jax.ShapeDtypeStruct((N_DEV,) + shard.shape, shard.dtype)],
        scratch_shapes=list(_GATHER_SEMS),
        compiler_params=_cp(("arbitrary", "arbitrary", "arbitrary")),
    )(qkv, qkv, qkv, shard)


def _sb_bwd(qkv, do, s1):
    bsz, s, _ = qkv.shape
    ng = CFG.sbh // SB_HP
    w = SB_HP * HD
    nblk = s // SB_BLK
    scale = HD ** -0.5
    qb, full = _sb_specs(s, w)

    def body(q_ref, k_ref, v_ref, do_ref, s1_ref, dq_ref, dk_ref, dv_ref, r2_ref, dk_acc, dv_acc, dl_pan, z_pan,
             send_sems, recv_sems):
        i = pl.program_id(2)
        copies = _chip_copies(s1_ref, r2_ref, send_sems, recv_sems)

        @pl.when((pl.program_id(0) == 0) & (pl.program_id(1) == 0) & (i == 0))
        def _():
            for cp in copies:
                cp.start()

        @pl.when(i == 0)
        def _():
            dk_acc[...] = jnp.zeros_like(dk_acc)
            dv_acc[...] = jnp.zeros_like(dv_acc)

        r, c = _iota2((SB_BLK, SB_BLK), 0), _iota2((SB_BLK, SB_BLK), 1)
        tri_su = (r > c).astype(BF16)
        tri_pre = (r < c).astype(BF16)
        nh = range(SB_HP)
        hs = [slice(h * HD, (h + 1) * HD) for h in nh]
        qs = [(q_ref[0, :, hs[h]] * scale).astype(BF16) for h in nh]
        dob = [do_ref[0, :, hs[h]].astype(BF16) for h in nh]
        quads = lax.shift_right_logical(i, 2)
        rem = jnp.bitwise_and(i, 3)
        pair = lax.shift_right_logical(rem, 1)
        odd = jnp.bitwise_and(rem, 1)

        def span_a(jblk, nb, mask, runs):
            rows = pl.ds(_blk_off(jblk), nb * SB_BLK)
            ks = [k_ref[0, rows, hs[h]].astype(BF16) for h in nh]
            vs = [v_ref[0, rows, hs[h]].astype(BF16) for h in nh]
            dws = [lax.dot_general(dob[h], vs[h], _dims("nt"), preferred_element_type=F32) for h in nh]
            zs, _, ws, runs = _sb_span(qs, ks, mask, runs, tri_su)
            dvs = [lax.dot_general(ws[h].astype(BF16), dob[h], _dims("tn"), preferred_element_type=F32) for h in nh]
            for h in nh:
                dl = dws[h] * ws[h]
                for b in range(nb):
                    dl_pan[h, jblk + b] = dl[:, b * SB_BLK:(b + 1) * SB_BLK]
                    z_pan[h, jblk + b] = zs[h][:, b * SB_BLK:(b + 1) * SB_BLK]
                dv_acc[rows, hs[h]] += dvs[h]
            return tuple(runs)

        runs = tuple(jnp.zeros((SB_BLK, 1), F32) for _ in nh)
        runs = span_a(i, 1, c < r, runs)
        runs = lax.fori_loop(0, quads, lambda p, rn: span_a(i - 4 - 4 * p, 4, None, rn), runs)
        runs = lax.fori_loop(0, pair, lambda _, rn: span_a(odd, 2, None, rn), runs)
        lax.fori_loop(0, odd, lambda _, rn: span_a(0, 1, None, rn), runs)

        def span_b(jblk, nb, mask, carry):
            rows = pl.ds(_blk_off(jblk), nb * SB_BLK)
            ks = [k_ref[0, rows, hs[h]].astype(BF16) for h in nh]
            units = [dl_pan[h, jblk + b] for h in nh for b in range(nb)]
            sgs = [_sigmoid(z_pan[h, jblk + b]) for h in nh for b in range(nb)]
            cums = _heads_cumsum(units, tri_pre)
            dzs, pres = [], []
            for h in nh:
                pre, parts = carry[h][1], []
                for b in range(nb):
                    u, sg = units[h * nb + b], sgs[h * nb + b]
                    parts.append(u * (1.0 - sg) - sg * (cums[h * nb + b] + pre))
                    pre = pre + jnp.sum(u, axis=1, keepdims=True)
                dz = jnp.concatenate(parts, axis=1) if nb > 1 else parts[0]
                if mask is not None:
                    dz = jnp.where(mask, dz, 0.0)
                dzs.append(dz.astype(BF16))
                pres.append(pre)
            dqs = [lax.dot_general(dzs[h], ks[h], _dims("nn"), preferred_element_type=F32) for h in nh]
            dks = [lax.dot_general(dzs[h], qs[h], _dims("tn"), preferred_element_type=F32) for h in nh]
            for h in nh:
                dk_acc[rows, hs[h]] += dks[h]
            return tuple((carry[h][0] + dqs[h], pres[h]) for h in nh)

        carry = tuple((jnp.zeros((SB_BLK, HD), F32), jnp.zeros((SB_BLK, 1), F32)) for _ in nh)
        carry = lax.fori_loop(0, odd, lambda _, cr: span_b(0, 1, None, cr), carry)
        carry = lax.fori_loop(0, pair, lambda _, cr: span_b(odd, 2, None, cr), carry)
        carry = lax.fori_loop(0, quads, lambda p, cr: span_b(rem + 4 * p, 4, None, cr), carry)
        carry = span_b(i, 1, c < r, carry)
        for h in nh:
            dq_ref[0, :, hs[h]] = (carry[h][0] * scale).astype(dq_ref.dtype)

        @pl.when(i == nblk - 1)
        def _():
            dk_ref[0] = dk_acc[...].astype(dk_ref.dtype)
            dv_ref[0] = dv_acc[...].astype(dv_ref.dtype)

        @pl.when((pl.program_id(0) == bsz - 1) & (pl.program_id(1) == ng - 1) & (i == nblk - 1))
        def _():
            for cp in copies:
                cp.wait_recv()
            for cp in copies:
                cp.wait_send()

    out = jax.ShapeDtypeStruct((bsz, s, CFG.sbh * HD), BF16)
    hbm = pl.BlockSpec(memory_space=pl.ANY)
    return pl.pallas_call(
        body, name="sb_bwd", grid=(bsz, ng, nblk),
        in_specs=[qb(0), full(ng), full(2 * ng), qb(0), hbm],
        out_specs=[qb(0), full(0), full(0), hbm],
        out_shape=[out, out, out, jax.ShapeDtypeStruct((3,) + s1.shape[1:], s1.dtype)],
        scratch_shapes=[pltpu.VMEM((s, w), F32), pltpu.VMEM((s, w), F32),
                        pltpu.VMEM((SB_HP, nblk, SB_BLK, SB_BLK), F32), pltpu.VMEM((SB_HP, nblk, SB_BLK, SB_BLK), F32),
                        pltpu.SemaphoreType.DMA((3,)), pltpu.SemaphoreType.DMA((3,))],
        compiler_params=_cp(("arbitrary", "arbitrary", "arbitrary")),
    )(qkv, qkv, qkv, do, s1)


def _xattn_fn(q_raw, kv, qn, kn):
    d = q_raw.shape[1]
    dh = d // CFG.xh
    outs = []
    for h in range(CFG.xh):
        qh = _rms(q_raw[:, h * dh:(h + 1) * dh], qn)
        kh = _rms(kv[:, h * dh:(h + 1) * dh], kn)
        vh = kv[:, d + h * dh:d + (h + 1) * dh]
        sc = _bdot(qh, kh, "nt") * (dh ** -0.5)
        sc = sc - lax.stop_gradient(jnp.max(sc, axis=-1, keepdims=True))
        e = jnp.exp(sc)
        p = e / jnp.sum(e, axis=-1, keepdims=True)
        outs.append(_bdot(p, vh, "nn"))
    return jnp.concatenate(outs, axis=1)


def _xattn_fwd(q_raw, kv, qn, kn):
    bsz, s, d = q_raw.shape
    m = kv.shape[1]
    tq = _tile(s, (XQ_TILE, 128))

    def body(q_ref, kv_ref, qn_ref, kn_ref, o_ref):
        o_ref[0] = _xattn_fn(q_ref[0], kv_ref[0], qn_ref[...], kn_ref[...]).astype(o_ref.dtype)

    return pl.pallas_call(
        body, name="xattn_fwd", grid=(bsz, s // tq),
        in_specs=[pl.BlockSpec((1, tq, d), lambda b, i: (b, i, 0)), pl.BlockSpec((1, m, 2 * d), lambda b, i: (b, 0, 0)),
                  pl.BlockSpec(qn.shape, lambda b, i: (0, 0)), pl.BlockSpec(kn.shape, lambda b, i: (0, 0))],
        out_specs=pl.BlockSpec((1, tq, d), lambda b, i: (b, i, 0)),
        out_shape=jax.ShapeDtypeStruct((bsz, s, d), BF16),
        compiler_params=_cp(("parallel", "parallel")),
    )(q_raw, kv, qn, kn)


def _xattn_bwd(q_raw, kv, qn, kn, do):
    bsz, s, d = q_raw.shape
    m = kv.shape[1]
    tq = _tile(s, (XQ_TILE, 128))

    def body(q_ref, kv_ref, qn_ref, kn_ref, do_ref, dq_ref, dkv_ref, dqn_ref, dkn_ref):
        b, i = pl.program_id(0), pl.program_id(1)

        @pl.when((b == 0) & (i == 0))
        def _():
            dqn_ref[...] = jnp.zeros_like(dqn_ref)
            dkn_ref[...] = jnp.zeros_like(dkn_ref)

        @pl.when(i == 0)
        def _():
            dkv_ref[...] = jnp.zeros_like(dkv_ref)

        _, f = jax.vjp(_xattn_fn, q_ref[0], kv_ref[0], qn_ref[...], kn_ref[...])
        dq, dkv, dqn, dkn = f(do_ref[0].astype(F32))
        dq_ref[0] = dq.astype(dq_ref.dtype)
        dkv_ref[0] += dkv
        dqn_ref[...] += dqn
        dkn_ref[...] += dkn

    return pl.pallas_call(
        body, name="xattn_bwd", grid=(bsz, s // tq),
        in_specs=[pl.BlockSpec((1, tq, d), lambda b, i: (b, i, 0)), pl.BlockSpec((1, m, 2 * d), lambda b, i: (b, 0, 0)),
                  pl.BlockSpec(qn.shape, lambda b, i: (0, 0)), pl.BlockSpec(kn.shape, lambda b, i: (0, 0)),
                  pl.BlockSpec((1, tq, d), lambda b, i: (b, i, 0))],
        out_specs=[pl.BlockSpec((1, tq, d), lambda b, i: (b, i, 0)), pl.BlockSpec((1, m, 2 * d), lambda b, i: (b, 0, 0)),
                   pl.BlockSpec(qn.shape, lambda b, i: (0, 0)), pl.BlockSpec(kn.shape, lambda b, i: (0, 0))],
        out_shape=[jax.ShapeDtypeStruct((bsz, s, d), BF16), jax.ShapeDtypeStruct(kv.shape, F32),
                   jax.ShapeDtypeStruct(qn.shape, F32), jax.ShapeDtypeStruct(kn.shape, F32)],
        compiler_params=_cp(("arbitrary", "arbitrary")),
    )(q_raw, kv, qn, kn, do)


def _my_pos():
    return lax.axis_index("x"), lax.axis_index("y"), lax.axis_index("c")


def _gather_stages(x_ref, out_ref, send_sems, recv_sems, local_sem):
    x, y, c = _my_pos()
    me, sibling = (x, y, c), (x, y, 1 - c)
    chips = [(1 - x, y), (x, 1 - y), (1 - x, 1 - y)]

    def slot(px, py, pc):
        return out_ref.at[4 * px + 2 * py + pc]

    def copy(k, block, to, src=None):
        return pltpu.make_async_remote_copy(
            src_ref=slot(*block) if src is None else src, dst_ref=slot(*block),
            send_sem=send_sems.at[k], recv_sem=recv_sems.at[k], device_id=to, device_id_type=MESH)

    mine = pltpu.make_async_copy(x_ref, slot(*me), local_sem)
    first = [copy(0, me, sibling, src=x_ref)]
    first += [copy(1 + j, me, (*chip, c), src=x_ref) for j, chip in enumerate(chips)]
    passed = [copy(4 + j, (*chip, c), sibling) for j, chip in enumerate(chips)]

    def begin():
        mine.start()
        for cp in first:
            cp.start()

    def relay():
        for j, chip in enumerate(chips):
            copy(1 + j, (*chip, c), me).wait_recv()
            passed[j].start()

    def finish():
        copy(0, sibling, me).wait_recv()
        for j, chip in enumerate(chips):
            copy(4 + j, (*chip, 1 - c), me).wait_recv()
        for cp in first + passed:
            cp.wait_send()
        mine.wait()

    return begin, relay, finish


_GATHER_SEMS = [pltpu.SemaphoreType.DMA((7,)), pltpu.SemaphoreType.DMA((7,)), pltpu.SemaphoreType.DMA]


def _all_gather_big(shard, name):
    r, d = shard.shape

    def body(x_ref, out_ref, send_sems, recv_sems, local_sem):
        begin, relay, finish = _gather_stages(x_ref, out_ref, send_sems, recv_sems, local_sem)
        begin()
        relay()
        finish()

    return pl.pallas_call(
        body, name=name,
        out_shape=jax.ShapeDtypeStruct((N_DEV, r, d), shard.dtype),
        in_specs=[pl.BlockSpec(memory_space=pl.ANY)], out_specs=pl.BlockSpec(memory_space=pl.ANY),
        scratch_shapes=list(_GATHER_SEMS),
    )(shard)


def _exchange_sibling(g, name):
    _, r, d = g.shape

    def body(g_ref, out_ref, send_sems, recv_sems):
        x, y, c = _my_pos()
        copies = [pltpu.make_async_remote_copy(
            src_ref=g_ref.at[2 * k + (1 - c)], dst_ref=out_ref.at[k],
            send_sem=send_sems.at[k], recv_sem=recv_sems.at[k], device_id=(x, y, 1 - c), device_id_type=MESH)
            for k in range(4)]
        for cp in copies:
            cp.start()
        for cp in copies:
            cp.wait_recv()
        for cp in copies:
            cp.wait_send()

    return pl.pallas_call(
        body, name=name,
        out_shape=jax.ShapeDtypeStruct((4, r, d), g.dtype),
        in_specs=[pl.BlockSpec(memory_space=pl.ANY)], out_specs=pl.BlockSpec(memory_space=pl.ANY),
        scratch_shapes=[pltpu.SemaphoreType.DMA((4,)), pltpu.SemaphoreType.DMA((4,))],
    )(g)


def _chip_copies(s_ref, out_ref, send_sems, recv_sems):
    x, y, c = _my_pos()
    copies = []
    for rel in (1, 2, 3):
        px = jnp.bitwise_xor(x, rel >> 1)
        py = jnp.bitwise_xor(y, rel & 1)
        copies.append(pltpu.make_async_remote_copy(
            src_ref=s_ref.at[2 * px + py], dst_ref=out_ref.at[rel - 1],
            send_sem=send_sems.at[rel - 1], recv_sem=recv_sems.at[rel - 1],
            device_id=(px, py, c), device_id_type=MESH))
    return copies


def _all_reduce_small(blk, name):
    rows, d = blk.shape

    def body(x_ref, out_ref, land, send_sems, recv_sems):
        x, y, c = _my_pos()
        me = 4 * x + 2 * y + c
        copies = []
        for rel in range(1, N_DEV):
            peer = (jnp.bitwise_xor(x, rel >> 2), jnp.bitwise_xor(y, (rel >> 1) & 1), jnp.bitwise_xor(c, rel & 1))
            copies.append(pltpu.make_async_remote_copy(
                src_ref=x_ref, dst_ref=land.at[rel - 1], send_sem=send_sems.at[rel - 1], recv_sem=recv_sems.at[rel - 1],
                device_id=peer, device_id_type=MESH))
        for cp in copies:
            cp.start()
        for cp in copies:
            cp.wait_recv()
        acc = jnp.zeros((rows, d), F32)
        for dev in range(N_DEV):
            rel = jnp.bitwise_xor(me, dev)
            got = land[jnp.maximum(rel - 1, 0)]
            acc = acc + jnp.where(rel == 0, x_ref[...], got)
        out_ref[...] = acc
        for cp in copies:
            cp.wait_send()

    return pl.pallas_call(
        body, name=name,
        out_shape=jax.ShapeDtypeStruct((rows, d), F32),
        in_specs=[pl.BlockSpec(memory_space=pltpu.VMEM)], out_specs=pl.BlockSpec(memory_space=pltpu.VMEM),
        scratch_shapes=[pltpu.VMEM((N_DEV - 1, rows, d), F32), pltpu.SemaphoreType.DMA((N_DEV - 1,)),
                        pltpu.SemaphoreType.DMA((N_DEV - 1,))],
    )(blk)


def _dxn_fused(parts, s1):
    t, d = parts[0][0].shape[0], parts[0][1].shape[1]
    tm = _tile(t, MM_TILES)
    nm = t // tm
    segs, k0 = [], 0
    for a, _ in parts:
        tk = _tile(a.shape[1], (512, 256, 128))
        segs.append((k0, a.shape[1] // tk, tk))
        k0 += a.shape[1] // tk
    ktot = k0
    npart = len(parts)

    def body(*refs):
        ab = refs[:2 * npart]
        s1_ref, o_ref, r2_ref, acc, send_sems, recv_sems = refs[2 * npart:]
        i, k = pl.program_id(0), pl.program_id(1)
        copies = _chip_copies(s1_ref, r2_ref, send_sems, recv_sems)

        @pl.when((i == 0) & (k == 0))
        def _():
            for cp in copies:
                cp.start()

        @pl.when(k == 0)
        def _():
            acc[...] = jnp.zeros_like(acc)

        for p, (p0, nk, _) in enumerate(segs):
            @pl.when((k >= p0) & (k < p0 + nk))
            def _(p=p):
                acc[...] += lax.dot_general(ab[2 * p][...], ab[2 * p + 1][...], _dims("nn"), preferred_element_type=F32)

        @pl.when(k == ktot - 1)
        def _():
            o_ref[...] = acc[...]

        @pl.when((i == nm - 1) & (k == ktot - 1))
        def _():
            for cp in copies:
                cp.wait_recv()
            for cp in copies:
                cp.wait_send()

    in_specs, args = [], []
    for (a, b), (p0, nk, tk) in zip(parts, segs):
        def kk(k, p0=p0, nk=nk):
            return jnp.clip(k - p0, 0, nk - 1)
        in_specs.append(pl.BlockSpec((tm, tk), lambda i, k, kk=kk: (i, kk(k))))
        in_specs.append(pl.BlockSpec((tk, d), lambda i, k, kk=kk: (kk(k), 0)))
        args += [a, b]
    hbm = pl.BlockSpec(memory_space=pl.ANY)
    return pl.pallas_call(
        body, name="d_xn", grid=(nm, ktot),
        in_specs=in_specs + [hbm], out_specs=[pl.BlockSpec((tm, d), lambda i, k: (i, 0)), hbm],
        out_shape=[jax.ShapeDtypeStruct((t, d), F32), jax.ShapeDtypeStruct((3,) + s1.shape[1:], s1.dtype)],
        scratch_shapes=[pltpu.VMEM((tm, d), F32), pltpu.SemaphoreType.DMA((3,)), pltpu.SemaphoreType.DMA((3,))],
        compiler_params=_cp(("arbitrary", "arbitrary")),
    )(*args, s1)


def _cast_rows(x, dtype, name):
    return _rowwise(lambda v: v, [x], [], [(x.shape[1], dtype)], [], name=name, tm=CFG.pack_tile)[0]


def _sum_sibling(g, recv1, c_idx, name):
    _, r, d = g.shape
    tm = CFG.pack_tile

    def body(c_ref, g_ref, r_ref, o_ref):
        o_ref[0] = (g_ref[0] + r_ref[0]).astype(o_ref.dtype)

    grid_spec = pltpu.PrefetchScalarGridSpec(
        num_scalar_prefetch=1, grid=(4, r // tm),
        in_specs=[pl.BlockSpec((1, tm, d), lambda k, i, c_ref: (2 * k + c_ref[0], i, 0)),
                  pl.BlockSpec((1, tm, d), lambda k, i, c_ref: (k, i, 0))],
        out_specs=pl.BlockSpec((1, tm, d), lambda k, i, c_ref: (k, i, 0)))
    return pl.pallas_call(
        body, name=name, grid_spec=grid_spec,
        out_shape=jax.ShapeDtypeStruct((4, r, d), BF16),
        compiler_params=_cp(("parallel", "parallel")),
    )(c_idx, g, recv1)


def _adamw_math(w, g, m, v):
    m2 = ADAM_B1 * m + (1.0 - ADAM_B1) * g
    v2 = ADAM_B2 * v + (1.0 - ADAM_B2) * (g * g)
    m_hat = m2 / (1.0 - ADAM_B1 ** ADAM_STEP)
    v_hat = v2 / (1.0 - ADAM_B2 ** ADAM_STEP)
    delta = -ADAM_LR * (m_hat / (jnp.sqrt(v_hat) + ADAM_EPS) + ADAM_WD * w)
    return delta, m2, v2


def _adamw_big(g, recv1, recv2, w, m, v, idx, row0, name):
    _, r, d = g.shape
    tm = CFG.pack_tile
    t0 = row0 // tm

    def body(idx_ref, g_ref, r1_ref, ra_ref, rb_ref, rc_ref, w_ref, m_ref, v_ref, og, od, om, ov):
        grad = (g_ref[0] + r1_ref[0]) + ra_ref[0].astype(F32) + rb_ref[0].astype(F32) + rc_ref[0].astype(F32)
        delta, m2, v2 = _adamw_math(w_ref[...], grad, m_ref[...], v_ref[...])
        og[...] = grad
        od[...] = delta
        om[...] = m2
        ov[...] = v2

    flat = pl.BlockSpec((tm, d), lambda i, idx_ref: (i, 0))
    shifted = pl.BlockSpec((tm, d), lambda i, idx_ref: (i + t0, 0))
    grid_spec = pltpu.PrefetchScalarGridSpec(
        num_scalar_prefetch=1, grid=(r // tm,),
        in_specs=[pl.BlockSpec((1, tm, d), lambda i, idx_ref: (idx_ref[0], i, 0)),
                  pl.BlockSpec((1, tm, d), lambda i, idx_ref: (idx_ref[1], i, 0)),
                  pl.BlockSpec((1, tm, d), lambda i, idx_ref: (0, i, 0)),
                  pl.BlockSpec((1, tm, d), lambda i, idx_ref: (1, i, 0)),
                  pl.BlockSpec((1, tm, d), lambda i, idx_ref: (2, i, 0)),
                  shifted, shifted, shifted],
        out_specs=[flat, flat, flat, flat])
    shp = jax.ShapeDtypeStruct((r, d), F32)
    return pl.pallas_call(
        body, name=name, grid_spec=grid_spec, out_shape=[shp, shp, shp, shp],
        compiler_params=_cp(("parallel",)),
    )(idx, g, recv1, recv2, recv2, recv2, w, m, v)


def _rows_of(v, d):
    flat = v.reshape(-1)
    rows = -(-flat.shape[0] // d)
    rows += (-rows) % SUBLANE
    return jnp.pad(flat, (0, rows * d - flat.shape[0])).reshape(rows, d)


def _pad_rows(a, mult):
    pad = (-a.shape[0]) % mult
    if pad:
        a = jnp.pad(a, ((0, pad),) + ((0, 0),) * (a.ndim - 1))
    return a


_BIG = ("w_in", "w_xkv", "w_up", "w_proj_gdn", "w_proj_sb", "w_out", "w_xq", "w_xo", "w_down")
_COL_SHARDED = ("w_in", "w_xkv", "w_up")
_SMALL_REP = ("norm_mix", "norm_x", "norm_mem", "norm_ffn", "a_log", "dt_bias", "gdn_out_norm", "xq_norm", "xk_norm")
_SMALL_CONV = ("conv_gdn", "conv_ffn")


def _part_rows(shapes):
    out = []
    for n in _BIG:
        rows, cols = shapes[n]
        cnt = cols if n in _COL_SHARDED else rows
        out.append((cnt, cnt + (-cnt) % (CFG.pack_tile if n == _BIG[0] else PACK_ROW_ALIGN)))
    return out


def _pack_big_shards(shards, shapes):
    parts = []
    for n, (_, padded) in zip(_BIG, _part_rows(shapes)):
        parts.append(_pad_rows(shards[n].T if n in _COL_SHARDED else shards[n], padded))
    return _pad_rows(jnp.concatenate(parts, axis=0), CFG.pack_tile)


def _unpack_gathered(gath, shapes, names):
    out, r0 = {}, 0
    for n, (cnt, padded) in zip(_BIG, _part_rows(shapes)):
        if n not in names:
            continue
        out[n] = gath[:, r0:r0 + cnt, :].reshape(N_DEV * cnt, gath.shape[2])
        r0 += padded
    return out


def _pack_full_grads(grads, shapes, names):
    d = CFG.d
    parts = []
    for n, (cnt, padded) in zip(_BIG, _part_rows(shapes)):
        if n not in names:
            continue
        g = grads[n].reshape(N_DEV, cnt, d)
        if padded > cnt:
            g = jnp.pad(g, ((0, 0), (0, padded - cnt), (0, 0)))
        parts.append(g)
    full = jnp.concatenate(parts, axis=1)
    pad = (-full.shape[1]) % CFG.pack_tile
    if pad:
        full = jnp.pad(full, ((0, 0), (0, pad), (0, 0)))
    return full


def _unpack_shard(packed, shapes, names):
    out, r0 = {}, 0
    for n, (cnt, padded) in zip(_BIG, _part_rows(shapes)):
        if n not in names:
            continue
        part = packed[r0:r0 + cnt]
        out[n] = (part.T if n in _COL_SHARDED else part).reshape((1,) + tuple(shapes[n]))
        r0 += padded
    return out


def kernel(x, mem, norm_mix, w_in, conv_gdn, a_log, dt_bias, gdn_out_norm, w_proj_gdn, w_proj_sb, w_out, norm_x, norm_mem, w_xq, w_xkv, xq_norm, xk_norm, w_xo, norm_ffn, w_up, conv_ffn, w_down, loss_target, m_norm_mix, m_w_in, m_conv_gdn, m_a_log, m_dt_bias, m_gdn_out_norm, m_w_proj_gdn, m_w_proj_sb, m_w_out, m_norm_x, m_norm_mem, m_w_xq, m_w_xkv, m_xq_norm, m_xk_norm, m_w_xo, m_norm_ffn, m_w_up, m_conv_ffn, m_w_down, v_norm_mix, v_w_in, v_conv_gdn, v_a_log, v_dt_bias, v_gdn_out_norm, v_w_proj_gdn, v_w_proj_sb, v_w_out, v_norm_x, v_norm_mem, v_w_xq, v_w_xkv, v_xq_norm, v_xk_norm, v_w_xo, v_norm_ffn, v_w_up, v_conv_ffn, v_w_down):
    names = ("norm_mix", "w_in", "conv_gdn", "a_log", "dt_bias", "gdn_out_norm", "w_proj_gdn", "w_proj_sb", "w_out",
             "norm_x", "norm_mem", "w_xq", "w_xkv", "xq_norm", "xk_norm", "w_xo", "norm_ffn", "w_up", "conv_ffn", "w_down")
    wts = dict(zip(names, (norm_mix, w_in, conv_gdn, a_log, dt_bias, gdn_out_norm, w_proj_gdn, w_proj_sb, w_out,
                           norm_x, norm_mem, w_xq, w_xkv, xq_norm, xk_norm, w_xo, norm_ffn, w_up, conv_ffn, w_down)))
    mom = dict(zip(names, (m_norm_mix, m_w_in, m_conv_gdn, m_a_log, m_dt_bias, m_gdn_out_norm, m_w_proj_gdn, m_w_proj_sb,
                           m_w_out, m_norm_x, m_norm_mem, m_w_xq, m_w_xkv, m_xq_norm, m_xk_norm, m_w_xo, m_norm_ffn, m_w_up,
                           m_conv_ffn, m_w_down)))
    vel = dict(zip(names, (v_norm_mix, v_w_in, v_conv_gdn, v_a_log, v_dt_bias, v_gdn_out_norm, v_w_proj_gdn, v_w_proj_sb,
                           v_w_out, v_norm_x, v_norm_mem, v_w_xq, v_w_xkv, v_xq_norm, v_xk_norm, v_w_xo, v_norm_ffn, v_w_up,
                           v_conv_ffn, v_w_down)))
    cfg = CFG
    d, bsz, s = cfg.d, cfg.b, cfg.s
    t = bsz * s
    gh, sbh = cfg.gh, cfg.sbh
    gw, sw = gh * HD, sbh * HD
    nchunk = s // cfg.gch
    mx, my, mc = _my_pos()
    me = 4 * mx + 2 * my + mc

    shard_shapes = {n: tuple(wts[n].shape[1:]) for n in _BIG}

    packed_w = _pack_big_shards({n: wts[n][0] for n in _BIG}, shard_shapes)
    packed_wb = _cast_rows(packed_w, BF16, "cast_weights")
    rows_in = _part_rows(shard_shapes)[0][1]
    full = _unpack_gathered(_all_gather_big(packed_wb[:rows_in], "all_gather_w_in"), shard_shapes, _BIG[:1])

    conv_rows = {n: _rows_of(wts[n][0], d) for n in _SMALL_CONV}
    conv_cnt = {n: conv_rows[n].shape[0] for n in _SMALL_CONV}
    conv_blk = _pad_rows(jnp.concatenate([conv_rows[n] for n in _SMALL_CONV], axis=0), SUBLANE)
    conv_all = jnp.zeros((N_DEV,) + conv_blk.shape, F32)
    conv_all = lax.dynamic_update_slice(conv_all, conv_blk[None], (me, 0, 0))
    conv_all = _all_reduce_small(conv_all.reshape(-1, d), "gather_conv_taps").reshape((N_DEV,) + conv_blk.shape)

    def full_conv(n, r0):
        k, cols = wts[n].shape[1], wts[n].shape[2]
        part = conv_all[:, r0:r0 + conv_cnt[n], :].reshape(N_DEV, -1)[:, :k * cols].reshape(N_DEV, k, cols)
        return part.transpose(1, 0, 2).reshape(k, N_DEV * cols)

    cgdn = full_conv("conv_gdn", 0)
    cffn = full_conv("conv_ffn", conv_cnt["conv_gdn"])

    win = full["w_in"]
    o_ab = 3 * gw
    o_z = o_ab + 2 * gh
    o_sb = o_z + gw
    o_gate = o_sb + 3 * sw
    w_qkv = win[:o_ab]
    w_ab = jnp.concatenate([win[o_ab:o_z], jnp.zeros((LANE - 2 * gh, d), win.dtype)], axis=0)
    w_z = win[o_z:o_sb]
    w_sb = win[o_sb:o_gate]
    w_gate = win[o_gate:]

    alog_p = jnp.concatenate([a_log.reshape(1, -1), jnp.zeros((1, LANE - gh), F32)], axis=1)
    dtb_p = jnp.concatenate([dt_bias.reshape(1, -1), jnp.zeros((1, LANE - gh), F32)], axis=1)
    onorm = gdn_out_norm.reshape(1, HD)

    x2 = x.reshape(t, d)
    tgt2 = loss_target.reshape(t, d)
    mem2 = mem.reshape(bsz * cfg.mem, d)

    (xn,) = _rowwise(_rms, [x2], [norm_mix], [(d, BF16)], [], name="norm_mix_fwd")
    p_qkv = _mm(xn, w_qkv, tb=True, name="proj_qkv")
    p_ab = _mm(xn, w_ab, tb=True, name="proj_ab")
    p_z = _mm(xn, w_z, tb=True, name="proj_z")
    p_sb = _mm(xn, w_sb, tb=True, name="proj_sb")
    p_gate = _mm(xn, w_gate, tb=True, name="proj_gate")

    qkv_c = _gdn_conv_fwd(p_qkv.reshape(bsz, s, 3 * gw), cgdn)
    (gb,) = _rowwise(_gates_fn, [p_ab], [alog_p, dtb_p], [(LANE, F32)], [], name="gdn_gates_fwd")
    gbt = gb.reshape(bsz, s, LANE)[:, :, :2 * gh].transpose(0, 2, 1).reshape(bsz, 2 * gh, nchunk, 1, cfg.gch)
    o_a, states, tinvs = _gdn_fwd(qkv_c, p_z.reshape(bsz, s, gw), gbt, onorm)
    o_b, gathered = _sb_fwd(p_sb.reshape(bsz, s, 3 * sw), packed_wb[rows_in:])
    full.update(_unpack_gathered(gathered, shard_shapes, _BIG[1:]))

    pa = _mm(o_a.reshape(t, gw), full["w_proj_gdn"], name="proj_gdn_out")
    pb = _mm(o_b.reshape(t, sw), full["w_proj_sb"], name="proj_sb_out")

    def merge_fn(pa_, pb_, gate_):
        return _sigmoid(gate_[:, :d]) * pa_ + _sigmoid(gate_[:, d:]) * pb_

    (merged,) = _rowwise(merge_fn, [pa, pb, p_gate], [], [(d, BF16)], [], name="merge_fwd")
    def with_norm(r, g):
        return r, _rms(r, g)

    h1, hn_x = _mm(merged, full["w_out"], add=x2, name="mixer_out", epilogue=(with_norm, [], [norm_x], [F32, BF16], 0, True))
    (mn,) = _rowwise(_rms, [mem2], [norm_mem], [(d, BF16)], [], name="norm_mem_fwd")
    q_raw = _mm(hn_x, full["w_xq"], name="xattn_q")
    kv = _mm(mn, full["w_xkv"], tb=True, name="xattn_kv")
    xo = _xattn_fwd(q_raw.reshape(bsz, s, d), kv.reshape(bsz, cfg.mem, 2 * d), xq_norm, xk_norm)
    h2, hn_f = _mm(xo.reshape(t, d), full["w_xo"], add=h1, name="xattn_out",
                   epilogue=(with_norm, [], [norm_ffn], [F32, BF16], 0, True))
    up = _mm(hn_f, full["w_up"], tb=True, name="ffn_up")
    act = _ffn_conv_fwd(up.reshape(bsz, s, 2 * cfg.dff), cffn)
    def loss_fn(y_, tg_):
        err = y_ - tg_
        part = 0.5 * jnp.sum(err * err) / d
        return err / d, err / d, jnp.full((1, LANE), part, F32)

    dy, dy_b, loss_part = _mm(act.reshape(t, cfg.dff), full["w_down"], add=h2, name="ffn_down_loss",
                              epilogue=(loss_fn, [tgt2], [], [F32, BF16], 1, False))

    grads = {}
    dact = _mm(dy_b, full["w_down"], tb=True, name="d_act")
    grads["w_down"] = _mm(act.reshape(t, cfg.dff), dy_b, ta=True, name="dw_down")
    dup1, dup2, dcf1, dcf2 = _ffn_conv_bwd(up.reshape(bsz, s, 2 * cfg.dff), cffn, dact.reshape(bsz, s, cfg.dff))
    dup = jnp.concatenate([dup1, dup2], axis=2).reshape(t, 2 * cfg.dff)
    g_conv_ffn = jnp.concatenate([dcf1, dcf2], axis=1)
    dhn_f = _mm(dup, full["w_up"], name="d_hn_ffn")
    grads["w_up"] = _mm(dup, hn_f, ta=True, name="dw_up")

    def norm_bwd_fn(h_, res_, dn_, g_):
        _, f = jax.vjp(_rms, h_, g_)
        dh, dg = f(dn_)
        return res_ + dh, dg

    def norm_bwd_copy_fn(h_, res_, dn_, g_):
        dres, dg = norm_bwd_fn(h_, res_, dn_, g_)
        return dres, dres, dg

    dh2, dh2_b, g_norm_ffn = _rowwise(norm_bwd_copy_fn, [h2, dy, dhn_f], [norm_ffn], [(d, F32), (d, BF16)], [(1, d)],
                                      name="norm_ffn_bwd")

    dxo = _mm(dh2_b, full["w_xo"], tb=True, out_dtype=BF16, name="d_xo")
    grads["w_xo"] = _mm(xo.reshape(t, d), dh2_b, ta=True, name="dw_xo")
    dq_raw, dkv, g_xq_norm, g_xk_norm = _xattn_bwd(q_raw.reshape(bsz, s, d), kv.reshape(bsz, cfg.mem, 2 * d),
                                                   xq_norm, xk_norm, dxo.reshape(bsz, s, d))
    dq_raw2 = dq_raw.reshape(t, d)
    dkv2 = dkv.reshape(bsz * cfg.mem, 2 * d)
    dhn_x = _mm(dq_raw2, full["w_xq"], tb=True, name="d_hn_x")
    grads["w_xq"] = _mm(hn_x, dq_raw2, ta=True, name="dw_xq")
    dmn = _mm(dkv2, full["w_xkv"], name="d_mn")
    grads["w_xkv"] = _mm(dkv2, mn, ta=True, name="dw_xkv")

    def norm_w_bwd_fn(h_, dn_, g_):
        _, f = jax.vjp(lambda gg: _rms(h_, gg), g_)
        return f(dn_)[0]

    (g_norm_mem,) = _rowwise(norm_w_bwd_fn, [mem2, dmn], [norm_mem], [], [(1, d)], name="norm_mem_bwd")
    dh1, dh1_b, g_norm_x = _rowwise(norm_bwd_copy_fn, [h1, dh2, dhn_x], [norm_x], [(d, F32), (d, BF16)], [(1, d)],
                                    name="norm_x_bwd")

    dmerged = _mm(dh1_b, full["w_out"], tb=True, name="d_merged")
    grads["w_out"] = _mm(merged, dh1_b, ta=True, name="dw_out")

    def merge_bwd_fn(pa_, pb_, gate_, dm_):
        _, f = jax.vjp(merge_fn, pa_, pb_, gate_)
        return f(dm_)

    dpa, dpb, dgate = _rowwise(merge_bwd_fn, [pa, pb, p_gate, dmerged], [], [(d, BF16), (d, BF16), (2 * d, BF16)], [],
                               name="merge_bwd")
    do_a = _mm(dpa, full["w_proj_gdn"], tb=True, name="d_o_gdn")
    grads["w_proj_gdn"] = _mm(o_a.reshape(t, gw), dpa, ta=True, name="dw_proj_gdn")
    do_b = _mm(dpb, full["w_proj_sb"], tb=True, name="d_o_sb")
    grads["w_proj_sb"] = _mm(o_b.reshape(t, sw), dpb, ta=True, name="dw_proj_sb")

    c_idx = jnp.reshape(mc, (1,)).astype(jnp.int32)
    early = _BIG[1:]
    g_early = _pack_full_grads(grads, shard_shapes, early)
    r1_early = _exchange_sibling(g_early, "grads_to_sibling_early")
    s1_early = _sum_sibling(g_early, r1_early, c_idx, "sum_sibling_early")
    dsq, dsk, dsv, r2_early = _sb_bwd(p_sb.reshape(bsz, s, 3 * sw), do_b.reshape(bsz, s, sw), s1_early)
    dp_sb = [a.reshape(t, sw) for a in (dsq, dsk, dsv)]
    w_sb3 = [w_sb[i * sw:(i + 1) * sw] for i in range(3)]

    dqkv_c, dz, dg, dbeta, g_onorm = _gdn_bwd(qkv_c, p_z.reshape(bsz, s, gw), gbt, onorm, states, tinvs,
                                              do_a.reshape(bsz, s, gw))
    dgb = jnp.concatenate([dg, dbeta], axis=1).reshape(bsz, 2 * gh, s).transpose(0, 2, 1)
    dgb = jnp.concatenate([dgb, jnp.zeros((bsz, s, LANE - 2 * gh), F32)], axis=2).reshape(t, LANE)

    def gates_bwd_fn(ab_, dgb_, alog_, dtb_):
        _, f = jax.vjp(_gates_fn, ab_, alog_, dtb_)
        return f(dgb_)

    dp_ab, g_alog, g_dtb = _rowwise(gates_bwd_fn, [p_ab, dgb], [alog_p, dtb_p], [(LANE, BF16)], [(1, LANE), (1, LANE)],
                                    name="gdn_gates_bwd")
    dp_qkv, g_conv_gdn = _gdn_conv_bwd(p_qkv.reshape(bsz, s, 3 * gw), cgdn, dqkv_c)
    dp_qkv = dp_qkv.reshape(t, 3 * gw)
    dp_z = dz.reshape(t, gw)

    grads["w_in"] = jnp.concatenate([
        _mm(dp_qkv, xn, ta=True, name="dw_in_qkv"),
        _mm(dp_ab, xn, ta=True, name="dw_in_ab")[:2 * gh],
        _mm(dp_z, xn, ta=True, name="dw_in_z"),
        _mm(dp_sb[0], xn, ta=True, name="dw_in_sbq"),
        _mm(dp_sb[1], xn, ta=True, name="dw_in_sbk"),
        _mm(dp_sb[2], xn, ta=True, name="dw_in_sbv"),
        _mm(dgate, xn, ta=True, name="dw_in_gate")], axis=0)
    g_late = _pack_full_grads(grads, shard_shapes, _BIG[:1])
    r1_late = _exchange_sibling(g_late, "grads_to_sibling_late")
    s1_late = _sum_sibling(g_late, r1_late, c_idx, "sum_sibling_late")
    dxn, r2_late = _dxn_fused([(dp_qkv, w_qkv), (dp_ab, w_ab), (dp_z, w_z), *zip(dp_sb, w_sb3), (dgate, w_gate)], s1_late)
    grad_x, g_norm_mix = _rowwise(norm_bwd_fn, [x2, dh1, dxn], [norm_mix], [(d, F32)], [(1, d)], name="norm_mix_bwd")

    small_g = {"norm_mix": g_norm_mix, "norm_x": g_norm_x, "norm_mem": g_norm_mem, "norm_ffn": g_norm_ffn,
               "a_log": g_alog[:, :gh], "dt_bias": g_dtb[:, :gh], "gdn_out_norm": g_onorm,
               "xq_norm": g_xq_norm, "xk_norm": g_xk_norm}
    sm_rows = [_rows_of(small_g[n], d) for n in _SMALL_REP] + [_rows_of(loss_part, d)]
    sm_rows += [_rows_of(g_conv_gdn, d), _rows_of(g_conv_ffn, d)]
    sm_cnt = [r.shape[0] for r in sm_rows]
    sm_sum = _all_reduce_small(_pad_rows(jnp.concatenate(sm_rows, axis=0), SUBLANE), "all_reduce_small_grads")
    offs = [0]
    for cnt in sm_cnt:
        offs.append(offs[-1] + cnt)
    small_grad = {}
    for i, n in enumerate(_SMALL_REP):
        small_grad[n] = sm_sum[offs[i]:offs[i + 1]].reshape(-1)[:wts[n].size].reshape(wts[n].shape)
    loss = sm_sum[offs[len(_SMALL_REP)], 0]
    for i, n in enumerate(_SMALL_CONV):
        k, cols = wts[n].shape[1], wts[n].shape[2]
        o = offs[len(_SMALL_REP) + 1 + i]
        fullg = sm_sum[o:o + sm_cnt[len(_SMALL_REP) + 1 + i]].reshape(-1)[:k * cols * N_DEV].reshape(k, N_DEV * cols)
        small_grad[n] = lax.dynamic_slice(fullg, (0, me * cols), (k, cols)).reshape(wts[n].shape)

    small_names = _SMALL_REP + _SMALL_CONV

    def pack_small(src):
        return _pad_rows(jnp.concatenate([_rows_of(src[n], d) for n in small_names], axis=0), SUBLANE)

    sw_, sg_, sm_, sv_ = pack_small(wts), pack_small(small_grad), pack_small(mom), pack_small(vel)
    sd_, snm_, snv_ = _rowwise(_adamw_math, [sw_, sg_, sm_, sv_], [], [(d, F32)] * 3, [], name="adamw_small", tm=sw_.shape[0])

    def unpack_small(packed):
        out, r0 = {}, 0
        for n in small_names:
            cnt = _rows_of(wts[n], d).shape[0]
            out[n] = packed[r0:r0 + cnt].reshape(-1)[:wts[n].size].reshape(wts[n].shape)
            r0 += cnt
        return out

    small_delta, small_m, small_v = unpack_small(sd_), unpack_small(snm_), unpack_small(snv_)

    idx = jnp.stack([me, 2 * mx + my]).astype(jnp.int32)
    pm = _pack_big_shards({n: mom[n][0] for n in _BIG}, shard_shapes)
    pv = _pack_big_shards({n: vel[n][0] for n in _BIG}, shard_shapes)
    upd_late = _adamw_big(g_late, r1_late, r2_late, packed_w, pm, pv, idx, 0, "adamw_late")
    upd_early = _adamw_big(g_early, r1_early, r2_early, packed_w, pm, pv, idx, g_late.shape[1], "adamw_early")
    big_grad, big_delta, big_m, big_v = (
        {**_unpack_shard(a, shard_shapes, _BIG[:1]), **_unpack_shard(b, shard_shapes, early)}
        for a, b in zip(upd_late, upd_early))

    def pick(big, small, n):
        return big[n] if n in big else small[n]

    outs = [loss, grad_x.reshape(bsz, s, d)]
    outs += [pick(big_grad, small_grad, n) for n in names]
    outs += [pick(big_delta, small_delta, n) for n in names]
    outs += [pick(big_m, small_m, n) for n in names]
    outs += [pick(big_v, small_v, n) for n in names]
    return tuple(outs)
```

```python
import functools

import jax
import jax.numpy as jnp
from jax import lax
from jax.experimental import pallas as pl
from jax.experimental.pallas import tpu as pltpu

F32 = jnp.float32
BF16 = jnp.bfloat16
FDOT_PRECISION = lax.Precision.HIGH

LANE = 128
SUBLANE = 8
PACK_ROW_ALIGN = 16
VMEM_LIMIT = 56 * 2 ** 20
N_DEV = 8
MESH = pl.DeviceIdType.MESH

EPS = 1e-6
ADAM_LR = 0.001
ADAM_B1 = 0.9
ADAM_B2 = 0.999
ADAM_EPS = 1e-08
ADAM_WD = 0.01
ADAM_STEP = 10


class _Cfg:
    d = 1024
    b = 4
    s = 2048
    mem = 256
    gh = 8
    gch = 64
    sbh = 8
    xh = 4
    dff = 2816
    pack_tile = 256


CFG = _Cfg()
HD = 128
SB_BLK = 128
SB_HP = 4
SB_HP_FWD = 8
GDN_HP = 8
MM_TILES = (1024, 1408, 704, 512, 256, 128)
MM_K_TILES = (1024, 1408, 704, 512, 256, 128)
MM_K_TILES_F32 = (512, 704, 256, 128)
CONV_CB = 256
XQ_TILE = 256


def _tile(n, prefs):
    for t in prefs:
        if n % t == 0:
            return t
    raise ValueError(f"no tile for {n}")


def _cp(sem, **kw):
    return pltpu.CompilerParams(dimension_semantics=sem, vmem_limit_bytes=VMEM_LIMIT, **kw)


def _dims(kind):
    return {"nn": (((1,), (0,)), ((), ())), "nt": (((1,), (1,)), ((), ())), "tn": (((0,), (0,)), ((), ()))}[kind]


def _raw_bdot(a, b, kind):
    return lax.dot_general(a.astype(BF16), b.astype(BF16), _dims(kind), preferred_element_type=F32)


def _raw_fdot(a, b, kind):
    return lax.dot_general(a.astype(F32), b.astype(F32), _dims(kind), precision=FDOT_PRECISION,
                           preferred_element_type=F32)


def _make_dot(raw):
    @functools.partial(jax.custom_vjp, nondiff_argnums=(2,))
    def dot(a, b, kind):
        return raw(a, b, kind)

    def fwd(a, b, kind):
        return raw(a, b, kind), (a, b)

    def bwd(kind, res, g):
        a, b = res
        if kind == "nn":
            return raw(g, b, "nt").astype(a.dtype), raw(a, g, "tn").astype(b.dtype)
        if kind == "nt":
            return raw(g, b, "nn").astype(a.dtype), raw(g, a, "tn").astype(b.dtype)
        return raw(b, g, "nt").astype(a.dtype), raw(a, g, "nn").astype(b.dtype)

    dot.defvjp(fwd, bwd)
    return dot


_bdot = _make_dot(_raw_bdot)
_fdot = _make_dot(_raw_fdot)


def _split_dot(x, m01):
    hi = x.astype(BF16)
    lo = (x - hi.astype(F32)).astype(BF16)
    return (lax.dot_general(hi, m01, _dims("nn"), preferred_element_type=F32)
            + lax.dot_general(lo, m01, _dims("nn"), preferred_element_type=F32))


_sigmoid = jax.nn.sigmoid


def _silu(x):
    return x * _sigmoid(x)


def _softplus(x):
    return jnp.maximum(x, 0.0) + jnp.log1p(jnp.exp(-jnp.abs(x)))


def _rms(x, g):
    return x * lax.rsqrt(jnp.mean(x * x, axis=-1, keepdims=True) + EPS) * g


def _iota2(shape, dim):
    return lax.broadcasted_iota(jnp.int32, shape, dim)


def _mm(a, b, *, ta=False, tb=False, add=None, out_dtype=F32, name, epilogue=None):
    if ta:
        kd, m = a.shape
    else:
        m, kd = a.shape
    if tb:
        n, kb = b.shape
    else:
        kb, n = b.shape
    assert kd == kb, (a.shape, b.shape, ta, tb)
    fn, e_rows, e_pars, e_dtypes, n_acc, full_rows = epilogue or (None, [], [], [out_dtype], 0, False)
    tm = _tile(m, MM_TILES)
    tn = n if full_rows else _tile(n, MM_TILES)
    wide = max(a.dtype.itemsize, b.dtype.itemsize) > 2
    tk = _tile(kd, MM_K_TILES_F32 if wide else MM_K_TILES)
    nk = kd // tk
    kind_dims = (((0 if ta else 1,), (1 if tb else 0,)), ((), ()))
    n_in = 2 + (add is not None) + len(e_rows) + len(e_pars)
    n_out = len(e_dtypes)

    def body(*refs):
        a_ref, b_ref = refs[:2]
        add_ref = refs[2] if add is not None else None
        extra = refs[2 + (add is not None):n_in]
        o_refs = refs[n_in:n_in + n_out]
        acc_refs = refs[n_in + n_out:n_in + n_out + n_acc]
        part = lax.dot_general(a_ref[...].astype(BF16), b_ref[...].astype(BF16), kind_dims,
                               preferred_element_type=F32)

        def finish(r):
            if add is not None:
                r = r + add_ref[...].astype(F32)
            outs = (r,) if fn is None else fn(r, *[e[...] for e in extra])
            for o_ref, val in zip(o_refs, outs[:n_out]):
                o_ref[...] = val.astype(o_ref.dtype)
            if n_acc:
                first = (pl.program_id(0) == 0) & (pl.program_id(1) == 0)

                @pl.when(first)
                def _():
                    for ref in acc_refs:
                        ref[...] = jnp.zeros_like(ref)

                for ref, val in zip(acc_refs, outs[n_out:]):
                    ref[...] += val

        if nk == 1:
            finish(part)
            return
        acc = refs[-1]
        k = pl.program_id(2)

        @pl.when(k == 0)
        def _():
            acc[...] = part

        @pl.when((k > 0) & (k < nk - 1))
        def _():
            acc[...] += part

        @pl.when(k == nk - 1)
        def _():
            finish(acc[...] + part)

    a_spec = pl.BlockSpec((tk, tm), lambda i, j, k: (k, i)) if ta else pl.BlockSpec((tm, tk), lambda i, j, k: (i, k))
    b_spec = pl.BlockSpec((tn, tk), lambda i, j, k: (j, k)) if tb else pl.BlockSpec((tk, tn), lambda i, j, k: (k, j))
    tile = pl.BlockSpec((tm, tn), lambda i, j, k: (i, j))
    in_specs = [a_spec, b_spec] + [tile] * ((add is not None) + len(e_rows))
    in_specs += [pl.BlockSpec((1, tn), lambda i, j, k: (0, j)) for _ in e_pars]
    args = [a, b] + ([add] if add is not None else []) + list(e_rows) + list(e_pars)
    out_specs = [tile] * n_out + [pl.BlockSpec((1, LANE), lambda i, j, k: (0, 0))] * n_acc
    out_shape = [jax.ShapeDtypeStruct((m, n), dt) for dt in e_dtypes] + [jax.ShapeDtypeStruct((1, LANE), F32)] * n_acc
    res = pl.pallas_call(
        body, name=name, grid=(m // tm, n // tn, nk),
        in_specs=in_specs, out_specs=out_specs, out_shape=out_shape,
        scratch_shapes=[pltpu.VMEM((tm, tn), F32)] if nk > 1 else [],
        compiler_params=_cp(("arbitrary", "arbitrary", "arbitrary") if n_acc else ("parallel", "parallel", "arbitrary")),
    )(*args)
    return res[0] if epilogue is None else res


def _rowwise(fn, rows, pars, out_rows, out_accs, *, name, tm=None):
    t = rows[0].shape[0]
    if tm is None:
        tm = _tile(t, (256, 128, 64, 32, 16))
    assert t % tm == 0, (t, tm)
    n_r, n_p, n_or, n_oa = len(rows), len(pars), len(out_rows), len(out_accs)

    def body(*refs):
        r_in = refs[:n_r]
        p_in = refs[n_r:n_r + n_p]
        o_r = refs[n_r + n_p:n_r + n_p + n_or]
        o_a = refs[n_r + n_p + n_or:]
        outs = fn(*[r[...] for r in r_in], *[p[...] for p in p_in])
        if not isinstance(outs, (tuple, list)):
            outs = (outs,)
        assert len(outs) == n_or + n_oa, (name, len(outs))
        for ref, val in zip(o_r, outs[:n_or]):
            ref[...] = val.astype(ref.dtype)
        if n_oa:
            @pl.when(pl.program_id(0) == 0)
            def _():
                for ref in o_a:
                    ref[...] = jnp.zeros_like(ref)

            for ref, val in zip(o_a, outs[n_or:]):
                ref[...] += val.astype(F32)

    in_specs = [pl.BlockSpec((tm, r.shape[1]), lambda i: (i, 0)) for r in rows]
    in_specs += [pl.BlockSpec(p.shape, lambda i: (0, 0)) for p in pars]
    out_specs = [pl.BlockSpec((tm, c), lambda i: (i, 0)) for c, _ in out_rows]
    out_specs += [pl.BlockSpec(s, lambda i: (0, 0)) for s in out_accs]
    out_shape = [jax.ShapeDtypeStruct((t, c), dt) for c, dt in out_rows]
    out_shape += [jax.ShapeDtypeStruct(s, F32) for s in out_accs]
    return pl.pallas_call(
        body, name=name, grid=(t // tm,), in_specs=in_specs, out_specs=out_specs, out_shape=out_shape,
        compiler_params=_cp(("arbitrary",)),
    )(*rows, *pars)


def _shift_down(x, sh):
    rolled = pltpu.roll(x, sh, 0)
    top = rolled[:SUBLANE]
    top = jnp.where(_iota2(top.shape, 0) >= sh, top, 0.0)
    return jnp.concatenate([top, rolled[SUBLANE:]], axis=0)


def _shift_up(x, sh):
    s = x.shape[0]
    rolled = pltpu.roll(x, s - sh, 0)
    bottom = rolled[s - SUBLANE:]
    bottom = jnp.where(_iota2(bottom.shape, 0) < SUBLANE - sh, bottom, 0.0)
    return jnp.concatenate([rolled[:s - SUBLANE], bottom], axis=0)


def _conv(x, w):
    k = w.shape[0]
    y = x * w[k - 1:k, :]
    for i in range(k - 1):
        y = y + _shift_down(x, k - 1 - i) * w[i:i + 1, :]
    return y


def _conv_bwd(x, w, dy):
    k = w.shape[0]
    dx = dy * w[k - 1:k, :]
    dws = []
    for i in range(k - 1):
        dx = dx + _shift_up(dy, k - 1 - i) * w[i:i + 1, :]
        dws.append(jnp.sum(dy * _shift_down(x, k - 1 - i), axis=0, keepdims=True))
    dws.append(jnp.sum(dy * x, axis=0, keepdims=True))
    return dx, dws


def _gdn_post(y, j, nqb):
    a = _silu(y)
    sc = jnp.where(j < nqb, HD ** -0.5, 1.0).astype(F32)
    outs = []
    for h in range(y.shape[1] // HD):
        ah = a[:, h * HD:(h + 1) * HD]
        l2 = ah * lax.rsqrt(jnp.sum(ah * ah, axis=-1, keepdims=True) + EPS)
        outs.append(jnp.where(j < 2 * nqb, l2 * sc, ah))
    return jnp.concatenate(outs, axis=1) if len(outs) > 1 else outs[0]


def _gdn_conv_fwd(x, w):
    bsz, s, c3 = x.shape
    k = w.shape[0]
    nb = c3 // CONV_CB
    nqb = nb // 3

    def body(x_ref, w_ref, o_ref):
        j = pl.program_id(1)
        o_ref[0] = _gdn_post(_conv(x_ref[0], w_ref[...]), j, nqb)

    return pl.pallas_call(
        body, name="gdn_conv_fwd", grid=(bsz, nb),
        in_specs=[pl.BlockSpec((1, s, CONV_CB), lambda b, j: (b, 0, j)), pl.BlockSpec((k, CONV_CB), lambda b, j: (0, j))],
        out_specs=pl.BlockSpec((1, s, CONV_CB), lambda b, j: (b, 0, j)),
        out_shape=jax.ShapeDtypeStruct(x.shape, F32),
        compiler_params=_cp(("parallel", "parallel")),
    )(x, w)


def _gdn_conv_bwd(x, w, dout):
    bsz, s, c3 = x.shape
    k = w.shape[0]
    nb = c3 // CONV_CB
    nqb = nb // 3

    def body(x_ref, w_ref, d_ref, dx_ref, dw_ref):
        j = pl.program_id(0)
        b = pl.program_id(1)
        xv, wv = x_ref[0], w_ref[...]
        y = _conv(xv, wv)
        _, f = jax.vjp(lambda yy: _gdn_post(yy, j, nqb), y)
        (dy,) = f(d_ref[0])
        dx, dws = _conv_bwd(xv, wv, dy)
        dx_ref[0] = dx.astype(dx_ref.dtype)

        @pl.when(b == 0)
        def _():
            dw_ref[...] = jnp.zeros_like(dw_ref)

        for i in range(k):
            dw_ref[i:i + 1, :] += dws[i]

    return pl.pallas_call(
        body, name="gdn_conv_bwd", grid=(nb, bsz),
        in_specs=[pl.BlockSpec((1, s, CONV_CB), lambda j, b: (b, 0, j)), pl.BlockSpec((k, CONV_CB), lambda j, b: (0, j)),
                  pl.BlockSpec((1, s, CONV_CB), lambda j, b: (b, 0, j))],
        out_specs=[pl.BlockSpec((1, s, CONV_CB), lambda j, b: (b, 0, j)), pl.BlockSpec((k, CONV_CB), lambda j, b: (0, j))],
        out_shape=[jax.ShapeDtypeStruct(x.shape, BF16), jax.ShapeDtypeStruct(w.shape, F32)],
        compiler_params=_cp(("parallel", "arbitrary")),
    )(x, w, dout)


def _ffn_conv_fwd(up, w):
    bsz, s, c2 = up.shape
    k = w.shape[0]
    nb = (c2 // 2) // CONV_CB

    def body(x1_ref, x2_ref, w1_ref, w2_ref, o_ref):
        u1 = _conv(x1_ref[0], w1_ref[...])
        u2 = _conv(x2_ref[0], w2_ref[...])
        o_ref[0] = (_silu(u1) * u2).astype(o_ref.dtype)

    return pl.pallas_call(
        body, name="ffn_conv_fwd", grid=(bsz, nb),
        in_specs=[pl.BlockSpec((1, s, CONV_CB), lambda b, j: (b, 0, j)), pl.BlockSpec((1, s, CONV_CB), lambda b, j: (b, 0, j + nb)),
                  pl.BlockSpec((k, CONV_CB), lambda b, j: (0, j)), pl.BlockSpec((k, CONV_CB), lambda b, j: (0, j + nb))],
        out_specs=pl.BlockSpec((1, s, CONV_CB), lambda b, j: (b, 0, j)),
        out_shape=jax.ShapeDtypeStruct((bsz, s, c2 // 2), BF16),
        compiler_params=_cp(("parallel", "parallel")),
    )(up, up, w, w)


def _ffn_conv_bwd(up, w, dact):
    bsz, s, c2 = up.shape
    k = w.shape[0]
    half = c2 // 2
    nb = half // CONV_CB

    def body(x1_ref, x2_ref, w1_ref, w2_ref, d_ref, dx1_ref, dx2_ref, dw1_ref, dw2_ref):
        b = pl.program_id(1)
        x1, x2, w1, w2 = x1_ref[0], x2_ref[0], w1_ref[...], w2_ref[...]
        u1 = _conv(x1, w1)
        u2 = _conv(x2, w2)
        _, f = jax.vjp(lambda p, q: _silu(p) * q, u1, u2)
        du1, du2 = f(d_ref[0])
        dx1, dws1 = _conv_bwd(x1, w1, du1)
        dx2, dws2 = _conv_bwd(x2, w2, du2)
        dx1_ref[0] = dx1.astype(dx1_ref.dtype)
        dx2_ref[0] = dx2.astype(dx2_ref.dtype)

        @pl.when(b == 0)
        def _():
            dw1_ref[...] = jnp.zeros_like(dw1_ref)
            dw2_ref[...] = jnp.zeros_like(dw2_ref)

        for i in range(k):
            dw1_ref[i:i + 1, :] += dws1[i]
            dw2_ref[i:i + 1, :] += dws2[i]

    def blk(off):
        return pl.BlockSpec((1, s, CONV_CB), lambda j, b: (b, 0, j + off))

    def wblk(off):
        return pl.BlockSpec((k, CONV_CB), lambda j, b: (0, j + off))

    return pl.pallas_call(
        body, name="ffn_conv_bwd", grid=(nb, bsz),
        in_specs=[blk(0), blk(nb), wblk(0), wblk(nb), blk(0)],
        out_specs=[blk(0), blk(0), wblk(0), wblk(0)],
        out_shape=[jax.ShapeDtypeStruct((bsz, s, half), BF16), jax.ShapeDtypeStruct((bsz, s, half), BF16),
                   jax.ShapeDtypeStruct((k, half), F32), jax.ShapeDtypeStruct((k, half), F32)],
        compiler_params=_cp(("parallel", "arbitrary")),
    )(up, up, w, w, dact)


@jax.custom_vjp
def _inv_unit_lower(mats):
    c = mats[0].shape[0]
    eye = (_iota2((c, c), 0) == _iota2((c, c), 1)).astype(F32)
    ps = [-a for a in mats]
    ts = [eye + p for p in ps]
    n = 2
    while n < c:
        ps = [_raw_fdot(p, p, "nn") for p in ps]
        ts = [t + _raw_fdot(t, p, "nn") for t, p in zip(ts, ps)]
        n *= 2
    return ts


def _inv_fwd(mats):
    ts = _inv_unit_lower(mats)
    return ts, ts


def _inv_bwd(ts, gs):
    xs = [_raw_fdot(g, t, "nt") for g, t in zip(gs, ts)]
    return ([-_raw_fdot(t, x, "tn") for t, x in zip(ts, xs)],)


_inv_unit_lower.defvjp(_inv_fwd, _inv_bwd)


@jax.custom_vjp
def _inv_known(mats, ts):
    return ts


def _inv_known_fwd(mats, ts):
    return ts, ts


def _inv_known_bwd(ts, gs):
    return _inv_bwd(ts, gs)[0], [jnp.zeros_like(t) for t in ts]


_inv_known.defvjp(_inv_known_fwd, _inv_known_bwd)


def _gdn_chunk(q, k, v, z, g_row, beta_row, state, onorm, tinv_known=None):
    nh = range(len(q))
    c = q[0].shape[0]
    ii, jj = _iota2((c, c), 0), _iota2((c, c), 1)
    incl, strict, eye = ii >= jj, ii > jj, ii == jj

    def to_col(row):
        return jnp.sum(jnp.where(eye, jnp.broadcast_to(row, (c, c)), 0.0), axis=1, keepdims=True)

    gc_col = [jnp.sum(jnp.where(incl, jnp.broadcast_to(g_row[h], (c, c)), 0.0), axis=1, keepdims=True) for h in nh]
    gc_row = [jnp.sum(jnp.where(eye, jnp.broadcast_to(gc_col[h], (c, c)), 0.0), axis=0, keepdims=True) for h in nh]
    beta_col = [to_col(beta_row[h]) for h in nh]
    gc_last = [jnp.sum(g_row[h], axis=1, keepdims=True) for h in nh]
    decay = [jnp.where(incl, jnp.exp(jnp.where(incl, gc_col[h] - gc_row[h], 0.0)), 0.0) for h in nh]
    kk = [_bdot(k[h], k[h], "nt") for h in nh]
    qk = [_bdot(q[h], k[h], "nt") * decay[h] for h in nh]
    mats = [jnp.where(strict, beta_col[h] * kk[h] * decay[h], 0.0) for h in nh]
    tinv = _inv_unit_lower(mats) if tinv_known is None else _inv_known(mats, tinv_known)
    rhs =[jnp.concatenate([v[h] * beta_col[h], k[h] * (beta_col[h] * jnp.exp(gc_col[h]))], axis=1) for h in nh]
    uw = [_fdot(tinv[h], rhs[h], "nn") for h in nh]
    dv = v[0].shape[1]
    ws = [_bdot(uw[h][:, dv:], state[h], "nn") for h in nh]
    qs = [_bdot(q[h] * jnp.exp(gc_col[h]), state[h], "nn") for h in nh]
    v_new = [uw[h][:, :dv] - ws[h] for h in nh]
    o = [qs[h] + _bdot(qk[h], v_new[h], "nn") for h in nh]
    kv = [_bdot(k[h] * jnp.exp(gc_last[h] - gc_col[h]), v_new[h], "tn") for h in nh]
    new_state = [state[h] * jnp.exp(gc_last[h]) + kv[h] for h in nh]
    y = [_rms(o[h], onorm) * _silu(z[h]) for h in nh]
    return y, new_state, tinv


def _gdn_specs(s, c, reverse):
    n = s // c
    nn = (lambda i: n - 1 - i) if reverse else (lambda i: i)

    def qkv(off):
        return pl.BlockSpec((1, c, GDN_HP * HD), lambda b, h, i: (b, nn(i), h + off))

    def gate(off):
        return pl.BlockSpec((1, GDN_HP, 1, 1, c), lambda b, h, i: (b, h + off, nn(i), 0, 0))

    st = pl.BlockSpec((1, GDN_HP, 1, HD, HD), lambda b, h, i: (b, h, nn(i), 0, 0))
    ti = pl.BlockSpec((1, GDN_HP, 1, c, c), lambda b, h, i: (b, h, nn(i), 0, 0))
    onorm = pl.BlockSpec((1, HD), lambda b, h, i: (0, 0))
    return n, qkv, gate, (st, ti), onorm


def _gdn_fwd(qkv, z, gbt, onorm):
    bsz, s, _ = qkv.shape
    gh, c = CFG.gh, CFG.gch
    ng = gh // GDN_HP
    n, qs, gs, (st, ti), on = _gdn_specs(s, c, False)

    def body(q_ref, k_ref, v_ref, z_ref, g_ref, b_ref, on_ref, y_ref, st_ref, ti_ref, state):
        @pl.when(pl.program_id(2) == 0)
        def _():
            state[...] = jnp.zeros_like(state)

        nh = range(GDN_HP)
        hs = [slice(h * HD, (h + 1) * HD) for h in nh]
        s_in = [state[h] for h in nh]
        for h in nh:
            st_ref[0, h, 0] = s_in[h]
        y, s_out, tinv = _gdn_chunk([q_ref[0, :, hs[h]] for h in nh], [k_ref[0, :, hs[h]] for h in nh],
                                    [v_ref[0, :, hs[h]] for h in nh], [z_ref[0, :, hs[h]] for h in nh],
                                    [g_ref[0, h, 0] for h in nh], [b_ref[0, h, 0] for h in nh], s_in, on_ref[...])
        for h in nh:
            y_ref[0, :, hs[h]] = y[h].astype(y_ref.dtype)
            ti_ref[0, h, 0] = tinv[h]
            state[h] = s_out[h]

    return pl.pallas_call(
        body, name="gdn_fwd", grid=(bsz, ng, n),
        in_specs=[qs(0), qs(ng), qs(2 * ng), qs(0), gs(0), gs(ng), on],
        out_specs=[qs(0), st, ti],
        out_shape=[jax.ShapeDtypeStruct((bsz, s, gh * HD), BF16), jax.ShapeDtypeStruct((bsz, gh, n, HD, HD), F32),
                   jax.ShapeDtypeStruct((bsz, gh, n, c, c), F32)],
        scratch_shapes=[pltpu.VMEM((GDN_HP, HD, HD), F32)],
        compiler_params=_cp(("parallel", "parallel", "arbitrary")),
    )(qkv, qkv, qkv, z, gbt, gbt, onorm)


def _gdn_bwd(qkv, z, gbt, onorm, states, tinvs, dy):
    bsz, s, _ = qkv.shape
    gh, c = CFG.gh, CFG.gch
    ng = gh // GDN_HP
    assert ng == 1, "d(q | k | v) is written as one block of all heads"
    w = gh * HD
    n, qs, gs, (st, ti), on = _gdn_specs(s, c, True)
    dqkv_spec = pl.BlockSpec((1, c, 3 * w), lambda b, h, i: (b, n - 1 - i, 0))

    def body(q_ref, k_ref, v_ref, z_ref, g_ref, b_ref, on_ref, st_ref, ti_ref, dy_ref,
             dqkv_ref, dz_ref, dg_ref, db_ref, don_ref, dstate):
        first = (pl.program_id(0) == 0) & (pl.program_id(1) == 0) & (pl.program_id(2) == 0)

        @pl.when(first)
        def _():
            don_ref[...] = jnp.zeros_like(don_ref)

        @pl.when(pl.program_id(2) == 0)
        def _():
            dstate[...] = jnp.zeros_like(dstate)

        nh = range(GDN_HP)
        hs = [slice(h * HD, (h + 1) * HD) for h in nh]
        tinv = [ti_ref[0, h, 0] for h in nh]

        def chunk(*args):
            return _gdn_chunk(*args, tinv_known=tinv)[:2]

        _, f = jax.vjp(chunk, [q_ref[0, :, hs[h]] for h in nh], [k_ref[0, :, hs[h]] for h in nh],
                       [v_ref[0, :, hs[h]] for h in nh], [z_ref[0, :, hs[h]] for h in nh],
                       [g_ref[0, h, 0] for h in nh], [b_ref[0, h, 0] for h in nh],
                       [st_ref[0, h, 0] for h in nh], on_ref[...])
        dq, dk, dv, dz, dg, db, ds, don = f(([dy_ref[0, :, hs[h]] for h in nh], [dstate[h] for h in nh]))
        for h in nh:
            dqkv_ref[0, :, h * HD:(h + 1) * HD] = dq[h]
            dqkv_ref[0, :, w + h * HD:w + (h + 1) * HD] = dk[h]
            dqkv_ref[0, :, 2 * w + h * HD:2 * w + (h + 1) * HD] = dv[h]
            dz_ref[0, :, hs[h]] = dz[h].astype(dz_ref.dtype)
            dg_ref[0, h, 0] = dg[h]
            db_ref[0, h, 0] = db[h]
            dstate[h] = ds[h]
        don_ref[...] += don

    act = jax.ShapeDtypeStruct((bsz, s, gh * HD), F32)
    gshape = jax.ShapeDtypeStruct((bsz, gh, n, 1, c), F32)
    return pl.pallas_call(
        body, name="gdn_bwd", grid=(bsz, ng, n),
        in_specs=[qs(0), qs(ng), qs(2 * ng), qs(0), gs(0), gs(ng), on, st, ti, qs(0)],
        out_specs=[dqkv_spec, qs(0), gs(0), gs(0), on],
        out_shape=[jax.ShapeDtypeStruct((bsz, s, 3 * w), F32), jax.ShapeDtypeStruct(act.shape, BF16), gshape, gshape,
                   jax.ShapeDtypeStruct((1, HD), F32)],
        scratch_shapes=[pltpu.VMEM((GDN_HP, HD, HD), F32)],
        compiler_params=_cp(("arbitrary", "arbitrary", "arbitrary")),
    )(qkv, qkv, qkv, z, gbt, gbt, onorm, states, tinvs, dy)


def _gates_fn(ab, alog, dtb):
    lane = _iota2(ab.shape, 1)
    g = -jnp.exp(alog) * _softplus(ab + dtb)
    beta = _sigmoid(ab)
    return jnp.where(lane < CFG.gh, g, jnp.where(lane < 2 * CFG.gh, beta, 0.0))


def _heads_cumsum(xs, tri):
    n = xs[0].shape[0]
    y = _split_dot(jnp.concatenate(xs, axis=0), tri)
    return [y[h * n:(h + 1) * n] for h in range(len(xs))]


def _blk_off(jblk):
    return jblk * SB_BLK if isinstance(jblk, int) else pl.multiple_of(jblk * SB_BLK, SB_BLK)


def _sb_span(qs, k_spans, mask, runs, tri_su):
    nh = range(len(qs))
    nb = k_spans[0].shape[0] // SB_BLK
    zs = [lax.dot_general(qs[h], k_spans[h], _dims("nt"), preferred_element_type=F32) for h in nh]
    l1p = [jnp.log(1.0 + jnp.exp(-jnp.abs(z))) for z in zs]
    lss = [jnp.minimum(zs[h], 0.0) - l1p[h] for h in nh]
    lfs = [lss[h] - zs[h] for h in nh]
    if mask is not None:
        lfs = [jnp.where(mask, lf, 0.0) for lf in lfs]
    units = [lfs[h][:, b * SB_BLK:(b + 1) * SB_BLK] for h in nh for b in range(nb)]
    cums = _heads_cumsum(units, tri_su)
    sfx, new_runs = [], []
    for h in nh:
        run, parts = runs[h], [None] * nb
        for b in reversed(range(nb)):
            parts[b] = cums[h * nb + b] + run
            run = run + jnp.sum(units[h * nb + b], axis=1, keepdims=True)
        sfx.append(jnp.concatenate(parts, axis=1) if nb > 1 else parts[0])
        new_runs.append(run)
    ws = [jnp.exp(lss[h] + sfx[h]) for h in nh]
    if mask is not None:
        ws = [jnp.where(mask, w, 0.0) for w in ws]
    return zs, lfs, ws, new_runs


def _sb_specs(s, w):
    def qb(off):
        return pl.BlockSpec((1, SB_BLK, w), lambda b, h, i: (b, i, h + off))

    def full(off):
        return pl.BlockSpec((1, s, w), lambda b, h, i: (b, 0, h + off))

    return qb, full


def _sb_fwd(qkv, shard):
    bsz, s, _ = qkv.shape
    ng = CFG.sbh // SB_HP_FWD
    w = SB_HP_FWD * HD
    scale = HD ** -0.5
    qb, full = _sb_specs(s, w)

    def body(q_ref, k_ref, v_ref, x_ref, o_ref, g_ref, send_sems, recv_sems, local_sem):
        i = pl.program_id(2)
        begin, relay, finish = _gather_stages(x_ref, g_ref, send_sems, recv_sems, local_sem)
        start_of_group = (pl.program_id(1) == 0) & (i == 0)
        pl.when((pl.program_id(0) == 0) & start_of_group)(begin)
        pl.when((pl.program_id(0) == bsz // 2) & start_of_group)(relay)

        r, c = _iota2((SB_BLK, SB_BLK), 0), _iota2((SB_BLK, SB_BLK), 1)
        tri_su = (r > c).astype(BF16)
        nh = range(SB_HP_FWD)
        hs = [slice(h * HD, (h + 1) * HD) for h in nh]
        qs = [(q_ref[0, :, hs[h]] * scale).astype(BF16) for h in nh]

        def span(off, nb, mask, carry):
            ks = [k_ref[0, pl.ds(off, nb * SB_BLK), hs[h]].astype(BF16) for h in nh]
            vs = [v_ref[0, pl.ds(off, nb * SB_BLK), hs[h]].astype(BF16) for h in nh]
            _, _, ws, runs = _sb_span(qs, ks, mask, [cr[1] for cr in carry], tri_su)
            pv = [lax.dot_general(ws[h].astype(BF16), vs[h], _dims("nn"), preferred_element_type=F32) for h in nh]
            return tuple((carry[h][0] + pv[h], runs[h]) for h in nh)

        carry = tuple((jnp.zeros((SB_BLK, HD), F32), jnp.zeros((SB_BLK, 1), F32)) for _ in nh)
        carry = span(_blk_off(i), 1, c < r, carry)
        rem = jnp.bitwise_and(i, 3)
        carry = lax.fori_loop(0, lax.shift_right_logical(i, 2),
                              lambda p, cr: span(_blk_off(i - 4 - 4 * p), 4, None, cr), carry)
        carry = lax.fori_loop(0, lax.shift_right_logical(rem, 1),
                              lambda _, cr: span(_blk_off(jnp.bitwise_and(rem, 1)), 2, None, cr), carry)
        carry = lax.fori_loop(0, jnp.bitwise_and(rem, 1), lambda _, cr: span(0, 1, None, cr), carry)
        for h in nh:
            o_ref[0, :, hs[h]] = carry[h][0].astype(o_ref.dtype)

        pl.when((pl.program_id(0) == bsz - 1) & (pl.program_id(1) == ng - 1) & (i == nblk - 1))(finish)

    nblk = s // SB_BLK
    hbm = pl.BlockSpec(memory_space=pl.ANY)
    return pl.pallas_call(
        body, name="sb_fwd", grid=(bsz, ng, nblk),
        in_specs=[qb(0), full(ng), full(2 * ng), hbm], out_specs=[qb(0), hbm],
        out_shape=[jax.ShapeDtypeStruct((bsz, s, CFG.sbh * HD), BF16),
                   jax.ShapeDtypeStruct((N_DEV,) + shard.shape, shard.dtype)],
        scratch_shapes=list(_GATHER_SEMS),
        compiler_params=_cp(("arbitrary", "arbitrary", "arbitrary")),
    )(qkv, qkv, qkv, shard)


def _sb_bwd(qkv, do, s1):
    bsz, s, _ = qkv.shape
    ng = CFG.sbh // SB_HP
    w = SB_HP * HD
    nblk = s // SB_BLK
    scale = HD ** -0.5
    qb, full = _sb_specs(s, w)

    def body(q_ref, k_ref, v_ref, do_ref, s1_ref, dq_ref, dk_ref, dv_ref, r2_ref, dk_acc, dv_acc, dl_pan, z_pan,
             send_sems, recv_sems):
        i = pl.program_id(2)
        copies = _chip_copies(s1_ref, r2_ref, send_sems, recv_sems)

        @pl.when((pl.program_id(0) == 0) & (pl.program_id(1) == 0) & (i == 0))
        def _():
            for cp in copies:
                cp.start()

        @pl.when(i == 0)
        def _():
            dk_acc[...] = jnp.zeros_like(dk_acc)
            dv_acc[...] = jnp.zeros_like(dv_acc)

        r, c = _iota2((SB_BLK, SB_BLK), 0), _iota2((SB_BLK, SB_BLK), 1)
        tri_su = (r > c).astype(BF16)
        tri_pre = (r < c).astype(BF16)
        nh = range(SB_HP)
        hs = [slice(h * HD, (h + 1) * HD) for h in nh]
        qs = [(q_ref[0, :, hs[h]] * scale).astype(BF16) for h in nh]
        dob = [do_ref[0, :, hs[h]].astype(BF16) for h in nh]
        quads = lax.shift_right_logical(i, 2)
        rem = jnp.bitwise_and(i, 3)
        pair = lax.shift_right_logical(rem, 1)
        odd = jnp.bitwise_and(rem, 1)

        def span_a(jblk, nb, mask, runs):
            rows = pl.ds(_blk_off(jblk), nb * SB_BLK)
            ks = [k_ref[0, rows, hs[h]].astype(BF16) for h in nh]
            vs = [v_ref[0, rows, hs[h]].astype(BF16) for h in nh]
            dws = [lax.dot_general(dob[h], vs[h], _dims("nt"), preferred_element_type=F32) for h in nh]
            zs, _, ws, runs = _sb_span(qs, ks, mask, runs, tri_su)
            dvs = [lax.dot_general(ws[h].astype(BF16), dob[h], _dims("tn"), preferred_element_type=F32) for h in nh]
            for h in nh:
                dl = dws[h] * ws[h]
                for b in range(nb):
                    dl_pan[h, jblk + b] = dl[:, b * SB_BLK:(b + 1) * SB_BLK]
                    z_pan[h, jblk + b] = zs[h][:, b * SB_BLK:(b + 1) * SB_BLK]
                dv_acc[rows, hs[h]] += dvs[h]
            return tuple(runs)

        runs = tuple(jnp.zeros((SB_BLK, 1), F32) for _ in nh)
        runs = span_a(i, 1, c < r, runs)
        runs = lax.fori_loop(0, quads, lambda p, rn: span_a(i - 4 - 4 * p, 4, None, rn), runs)
        runs = lax.fori_loop(0, pair, lambda _, rn: span_a(odd, 2, None, rn), runs)
        lax.fori_loop(0, odd, lambda _, rn: span_a(0, 1, None, rn), runs)

        def span_b(jblk, nb, mask, carry):
            rows = pl.ds(_blk_off(jblk), nb * SB_BLK)
            ks = [k_ref[0, rows, hs[h]].astype(BF16) for h in nh]
            units = [dl_pan[h, jblk + b] for h in nh for b in range(nb)]
            sgs = [_sigmoid(z_pan[h, jblk + b]) for h in nh for b in range(nb)]
            cums = _heads_cumsum(units, tri_pre)
            dzs, pres = [], []
            for h in nh:
                pre, parts = carry[h][1], []
                for b in range(nb):
                    u, sg = units[h * nb + b], sgs[h * nb + b]
                    parts.append(u * (1.0 - sg) - sg * (cums[h * nb + b] + pre))
                    pre = pre + jnp.sum(u, axis=1, keepdims=True)
                dz = jnp.concatenate(parts, axis=1) if nb > 1 else parts[0]
                if mask is not None:
                    dz = jnp.where(mask, dz, 0.0)
                dzs.append(dz.astype(BF16))
                pres.append(pre)
            dqs = [lax.dot_general(dzs[h], ks[h], _dims("nn"), preferred_element_type=F32) for h in nh]
            dks = [lax.dot_general(dzs[h], qs[h], _dims("tn"), preferred_element_type=F32) for h in nh]
            for h in nh:
                dk_acc[rows, hs[h]] += dks[h]
            return tuple((carry[h][0] + dqs[h], pres[h]) for h in nh)

        carry = tuple((jnp.zeros((SB_BLK, HD), F32), jnp.zeros((SB_BLK, 1), F32)) for _ in nh)
        carry = lax.fori_loop(0, odd, lambda _, cr: span_b(0, 1, None, cr), carry)
        carry = lax.fori_loop(0, pair, lambda _, cr: span_b(odd, 2, None, cr), carry)
        carry = lax.fori_loop(0, quads, lambda p, cr: span_b(rem + 4 * p, 4, None, cr), carry)
        carry = span_b(i, 1, c < r, carry)
        for h in nh:
            dq_ref[0, :, hs[h]] = (carry[h][0] * scale).astype(dq_ref.dtype)

        @pl.when(i == nblk - 1)
        def _():
            dk_ref[0] = dk_acc[...].astype(dk_ref.dtype)
            dv_ref[0] = dv_acc[...].astype(dv_ref.dtype)

        @pl.when((pl.program_id(0) == bsz - 1) & (pl.program_id(1) == ng - 1) & (i == nblk - 1))
        def _():
            for cp in copies:
                cp.wait_recv()
            for cp in copies:
                cp.wait_send()

    out = jax.ShapeDtypeStruct((bsz, s, CFG.sbh * HD), BF16)
    hbm = pl.BlockSpec(memory_space=pl.ANY)
    return pl.pallas_call(
        body, name="sb_bwd", grid=(bsz, ng, nblk),
        in_specs=[qb(0), full(ng), full(2 * ng), qb(0), hbm],
        out_specs=[qb(0), full(0), full(0), hbm],
        out_shape=[out, out, out, jax.ShapeDtypeStruct((3,) + s1.shape[1:], s1.dtype)],
        scratch_shapes=[pltpu.VMEM((s, w), F32), pltpu.VMEM((s, w), F32),
                        pltpu.VMEM((SB_HP, nblk, SB_BLK, SB_BLK), F32), pltpu.VMEM((SB_HP, nblk, SB_BLK, SB_BLK), F32),
                        pltpu.SemaphoreType.DMA((3,)), pltpu.SemaphoreType.DMA((3,))],
        compiler_params=_cp(("arbitrary", "arbitrary", "arbitrary")),
    )(qkv, qkv, qkv, do, s1)


def _xattn_fn(q_raw, kv, qn, kn):
    d = q_raw.shape[1]
    dh = d // CFG.xh
    outs = []
    for h in range(CFG.xh):
        qh = _rms(q_raw[:, h * dh:(h + 1) * dh], qn)
        kh = _rms(kv[:, h * dh:(h + 1) * dh], kn)
        vh = kv[:, d + h * dh:d + (h + 1) * dh]
        sc = _bdot(qh, kh, "nt") * (dh ** -0.5)
        sc = sc - lax.stop_gradient(jnp.max(sc, axis=-1, keepdims=True))
        e = jnp.exp(sc)
        p = e / jnp.sum(e, axis=-1, keepdims=True)
        outs.append(_bdot(p, vh, "nn"))
    return jnp.concatenate(outs, axis=1)


def _xattn_fwd(q_raw, kv, qn, kn):
    bsz, s, d = q_raw.shape
    m = kv.shape[1]
    tq = _tile(s, (XQ_TILE, 128))

    def body(q_ref, kv_ref, qn_ref, kn_ref, o_ref):
        o_ref[0] = _xattn_fn(q_ref[0], kv_ref[0], qn_ref[...], kn_ref[...]).astype(o_ref.dtype)

    return pl.pallas_call(
        body, name="xattn_fwd", grid=(bsz, s // tq),
        in_specs=[pl.BlockSpec((1, tq, d), lambda b, i: (b, i, 0)), pl.BlockSpec((1, m, 2 * d), lambda b, i: (b, 0, 0)),
                  pl.BlockSpec(qn.shape, lambda b, i: (0, 0)), pl.BlockSpec(kn.shape, lambda b, i: (0, 0))],
        out_specs=pl.BlockSpec((1, tq, d), lambda b, i: (b, i, 0)),
        out_shape=jax.ShapeDtypeStruct((bsz, s, d), BF16),
        compiler_params=_cp(("parallel", "parallel")),
    )(q_raw, kv, qn, kn)


def _xattn_bwd(q_raw, kv, qn, kn, do):
    bsz, s, d = q_raw.shape
    m = kv.shape[1]
    tq = _tile(s, (XQ_TILE, 128))

    def body(q_ref, kv_ref, qn_ref, kn_ref, do_ref, dq_ref, dkv_ref, dqn_ref, dkn_ref):
        b, i = pl.program_id(0), pl.program_id(1)

        @pl.when((b == 0) & (i == 0))
        def _():
            dqn_ref[...] = jnp.zeros_like(dqn_ref)
            dkn_ref[...] = jnp.zeros_like(dkn_ref)

        @pl.when(i == 0)
        def _():
            dkv_ref[...] = jnp.zeros_like(dkv_ref)

        _, f = jax.vjp(_xattn_fn, q_ref[0], kv_ref[0], qn_ref[...], kn_ref[...])
        dq, dkv, dqn, dkn = f(do_ref[0].astype(F32))
        dq_ref[0] = dq.astype(dq_ref.dtype)
        dkv_ref[0] += dkv
        dqn_ref[...] += dqn
        dkn_ref[...] += dkn

    return pl.pallas_call(
        body, name="xattn_bwd", grid=(bsz, s // tq),
        in_specs=[pl.BlockSpec((1, tq, d), lambda b, i: (b, i, 0)), pl.BlockSpec((1, m, 2 * d), lambda b, i: (b, 0, 0)),
                  pl.BlockSpec(qn.shape, lambda b, i: (0, 0)), pl.BlockSpec(kn.shape, lambda b, i: (0, 0)),
                  pl.BlockSpec((1, tq, d), lambda b, i: (b, i, 0))],
        out_specs=[pl.BlockSpec((1, tq, d), lambda b, i: (b, i, 0)), pl.BlockSpec((1, m, 2 * d), lambda b, i: (b, 0, 0)),
                   pl.BlockSpec(qn.shape, lambda b, i: (0, 0)), pl.BlockSpec(kn.shape, lambda b, i: (0, 0))],
        out_shape=[jax.ShapeDtypeStruct((bsz, s, d), BF16), jax.ShapeDtypeStruct(kv.shape, F32),
                   jax.ShapeDtypeStruct(qn.shape, F32), jax.ShapeDtypeStruct(kn.shape, F32)],
        compiler_params=_cp(("arbitrary", "arbitrary")),
    )(q_raw, kv, qn, kn, do)


def _my_pos():
    return lax.axis_index("x"), lax.axis_index("y"), lax.axis_index("c")


def _gather_stages(x_ref, out_ref, send_sems, recv_sems, local_sem):
    x, y, c = _my_pos()
    me, sibling = (x, y, c), (x, y, 1 - c)
    chips = [(1 - x, y), (x, 1 - y), (1 - x, 1 - y)]

    def slot(px, py, pc):
        return out_ref.at[4 * px + 2 * py + pc]

    def copy(k, block, to, src=None):
        return pltpu.make_async_remote_copy(
            src_ref=slot(*block) if src is None else src, dst_ref=slot(*block),
            send_sem=send_sems.at[k], recv_sem=recv_sems.at[k], device_id=to, device_id_type=MESH)

    mine = pltpu.make_async_copy(x_ref, slot(*me), local_sem)
    first = [copy(0, me, sibling, src=x_ref)]
    first += [copy(1 + j, me, (*chip, c), src=x_ref) for j, chip in enumerate(chips)]
    passed = [copy(4 + j, (*chip, c), sibling) for j, chip in enumerate(chips)]

    def begin():
        mine.start()
        for cp in first:
            cp.start()

    def relay():
        for j, chip in enumerate(chips):
            copy(1 + j, (*chip, c), me).wait_recv()
            passed[j].start()

    def finish():
        copy(0, sibling, me).wait_recv()
        for j, chip in enumerate(chips):
            copy(4 + j, (*chip, 1 - c), me).wait_recv()
        for cp in first + passed:
            cp.wait_send()
        mine.wait()

    return begin, relay, finish


_GATHER_SEMS = [pltpu.SemaphoreType.DMA((7,)), pltpu.SemaphoreType.DMA((7,)), pltpu.SemaphoreType.DMA]


def _all_gather_big(shard, name):
    r, d = shard.shape

    def body(x_ref, out_ref, send_sems, recv_sems, local_sem):
        begin, relay, finish = _gather_stages(x_ref, out_ref, send_sems, recv_sems, local_sem)
        begin()
        relay()
        finish()

    return pl.pallas_call(
        body, name=name,
        out_shape=jax.ShapeDtypeStruct((N_DEV, r, d), shard.dtype),
        in_specs=[pl.BlockSpec(memory_space=pl.ANY)], out_specs=pl.BlockSpec(memory_space=pl.ANY),
        scratch_shapes=list(_GATHER_SEMS),
    )(shard)


def _exchange_sibling(g, name):
    _, r, d = g.shape

    def body(g_ref, out_ref, send_sems, recv_sems):
        x, y, c = _my_pos()
        copies = [pltpu.make_async_remote_copy(
            src_ref=g_ref.at[2 * k + (1 - c)], dst_ref=out_ref.at[k],
            send_sem=send_sems.at[k], recv_sem=recv_sems.at[k], device_id=(x, y, 1 - c), device_id_type=MESH)
            for k in range(4)]
        for cp in copies:
            cp.start()
        for cp in copies:
            cp.wait_recv()
        for cp in copies:
            cp.wait_send()

    return pl.pallas_call(
        body, name=name,
        out_shape=jax.ShapeDtypeStruct((4, r, d), g.dtype),
        in_specs=[pl.BlockSpec(memory_space=pl.ANY)], out_specs=pl.BlockSpec(memory_space=pl.ANY),
        scratch_shapes=[pltpu.SemaphoreType.DMA((4,)), pltpu.SemaphoreType.DMA((4,))],
    )(g)


def _chip_copies(s_ref, out_ref, send_sems, recv_sems):
    x, y, c = _my_pos()
    copies = []
    for rel in (1, 2, 3):
        px = jnp.bitwise_xor(x, rel >> 1)
        py = jnp.bitwise_xor(y, rel & 1)
        copies.append(pltpu.make_async_remote_copy(
            src_ref=s_ref.at[2 * px + py], dst_ref=out_ref.at[rel - 1],
            send_sem=send_sems.at[rel - 1], recv_sem=recv_sems.at[rel - 1],
            device_id=(px, py, c), device_id_type=MESH))
    return copies


def _all_reduce_small(blk, name):
    rows, d = blk.shape

    def body(x_ref, out_ref, land, send_sems, recv_sems):
        x, y, c = _my_pos()
        me = 4 * x + 2 * y + c
        copies = []
        for rel in range(1, N_DEV):
            peer = (jnp.bitwise_xor(x, rel >> 2), jnp.bitwise_xor(y, (rel >> 1) & 1), jnp.bitwise_xor(c, rel & 1))
            copies.append(pltpu.make_async_remote_copy(
                src_ref=x_ref, dst_ref=land.at[rel - 1], send_sem=send_sems.at[rel - 1], recv_sem=recv_sems.at[rel - 1],
                device_id=peer, device_id_type=MESH))
        for cp in copies:
            cp.start()
        for cp in copies:
            cp.wait_recv()
        acc = jnp.zeros((rows, d), F32)
        for dev in range(N_DEV):
            rel = jnp.bitwise_xor(me, dev)
            got = land[jnp.maximum(rel - 1, 0)]
            acc = acc + jnp.where(rel == 0, x_ref[...], got)
        out_ref[...] = acc
        for cp in copies:
            cp.wait_send()

    return pl.pallas_call(
        body, name=name,
        out_shape=jax.ShapeDtypeStruct((rows, d), F32),
        in_specs=[pl.BlockSpec(memory_space=pltpu.VMEM)], out_specs=pl.BlockSpec(memory_space=pltpu.VMEM),
        scratch_shapes=[pltpu.VMEM((N_DEV - 1, rows, d), F32), pltpu.SemaphoreType.DMA((N_DEV - 1,)),
                        pltpu.SemaphoreType.DMA((N_DEV - 1,))],
    )(blk)


def _dxn_fused(parts, s1):
    t, d = parts[0][0].shape[0], parts[0][1].shape[1]
    tm = _tile(t, MM_TILES)
    nm = t // tm
    segs, k0 = [], 0
    for a, _ in parts:
        tk = _tile(a.shape[1], (512, 256, 128))
        segs.append((k0, a.shape[1] // tk, tk))
        k0 += a.shape[1] // tk
    ktot = k0
    npart = len(parts)

    def body(*refs):
        ab = refs[:2 * npart]
        s1_ref, o_ref, r2_ref, acc, send_sems, recv_sems = refs[2 * npart:]
        i, k = pl.program_id(0), pl.program_id(1)
        copies = _chip_copies(s1_ref, r2_ref, send_sems, recv_sems)

        @pl.when((i == 0) & (k == 0))
        def _():
            for cp in copies:
                cp.start()

        @pl.when(k == 0)
        def _():
            acc[...] = jnp.zeros_like(acc)

        for p, (p0, nk, _) in enumerate(segs):
            @pl.when((k >= p0) & (k < p0 + nk))
            def _(p=p):
                acc[...] += lax.dot_general(ab[2 * p][...], ab[2 * p + 1][...], _dims("nn"), preferred_element_type=F32)

        @pl.when(k == ktot - 1)
        def _():
            o_ref[...] = acc[...]

        @pl.when((i == nm - 1) & (k == ktot - 1))
        def _():
            for cp in copies:
                cp.wait_recv()
            for cp in copies:
                cp.wait_send()

    in_specs, args = [], []
    for (a, b), (p0, nk, tk) in zip(parts, segs):
        def kk(k, p0=p0, nk=nk):
            return jnp.clip(k - p0, 0, nk - 1)
        in_specs.append(pl.BlockSpec((tm, tk), lambda i, k, kk=kk: (i, kk(k))))
        in_specs.append(pl.BlockSpec((tk, d), lambda i, k, kk=kk: (kk(k), 0)))
        args += [a, b]
    hbm = pl.BlockSpec(memory_space=pl.ANY)
    return pl.pallas_call(
        body, name="d_xn", grid=(nm, ktot),
        in_specs=in_specs + [hbm], out_specs=[pl.BlockSpec((tm, d), lambda i, k: (i, 0)), hbm],
        out_shape=[jax.ShapeDtypeStruct((t, d), F32), jax.ShapeDtypeStruct((3,) + s1.shape[1:], s1.dtype)],
        scratch_shapes=[pltpu.VMEM((tm, d), F32), pltpu.SemaphoreType.DMA((3,)), pltpu.SemaphoreType.DMA((3,))],
        compiler_params=_cp(("arbitrary", "arbitrary")),
    )(*args, s1)


def _cast_rows(x, dtype, name):
    return _rowwise(lambda v: v, [x], [], [(x.shape[1], dtype)], [], name=name, tm=CFG.pack_tile)[0]


def _sum_sibling(g, recv1, c_idx, name):
    _, r, d = g.shape
    tm = CFG.pack_tile

    def body(c_ref, g_ref, r_ref, o_ref):
        o_ref[0] = (g_ref[0] + r_ref[0]).astype(o_ref.dtype)

    grid_spec = pltpu.PrefetchScalarGridSpec(
        num_scalar_prefetch=1, grid=(4, r // tm),
        in_specs=[pl.BlockSpec((1, tm, d), lambda k, i, c_ref: (2 * k + c_ref[0], i, 0)),
                  pl.BlockSpec((1, tm, d), lambda k, i, c_ref: (k, i, 0))],
        out_specs=pl.BlockSpec((1, tm, d), lambda k, i, c_ref: (k, i, 0)))
    return pl.pallas_call(
        body, name=name, grid_spec=grid_spec,
        out_shape=jax.ShapeDtypeStruct((4, r, d), BF16),
        compiler_params=_cp(("parallel", "parallel")),
    )(c_idx, g, recv1)


def _adamw_math(w, g, m, v):
    m2 = ADAM_B1 * m + (1.0 - ADAM_B1) * g
    v2 = ADAM_B2 * v + (1.0 - ADAM_B2) * (g * g)
    m_hat = m2 / (1.0 - ADAM_B1 ** ADAM_STEP)
    v_hat = v2 / (1.0 - ADAM_B2 ** ADAM_STEP)
    delta = -ADAM_LR * (m_hat / (jnp.sqrt(v_hat) + ADAM_EPS) + ADAM_WD * w)
    return delta, m2, v2


def _adamw_big(g, recv1, recv2, w, m, v, idx, row0, name):
    _, r, d = g.shape
    tm = CFG.pack_tile
    t0 = row0 // tm

    def body(idx_ref, g_ref, r1_ref, ra_ref, rb_ref, rc_ref, w_ref, m_ref, v_ref, og, od, om, ov):
        grad = (g_ref[0] + r1_ref[0]) + ra_ref[0].astype(F32) + rb_ref[0].astype(F32) + rc_ref[0].astype(F32)
        delta, m2, v2 = _adamw_math(w_ref[...], grad, m_ref[...], v_ref[...])
        og[...] = grad
        od[...] = delta
        om[...] = m2
        ov[...] = v2

    flat = pl.BlockSpec((tm, d), lambda i, idx_ref: (i, 0))
    shifted = pl.BlockSpec((tm, d), lambda i, idx_ref: (i + t0, 0))
    grid_spec = pltpu.PrefetchScalarGridSpec(
        num_scalar_prefetch=1, grid=(r // tm,),
        in_specs=[pl.BlockSpec((1, tm, d), lambda i, idx_ref: (idx_ref[0], i, 0)),
                  pl.BlockSpec((1, tm, d), lambda i, idx_ref: (idx_ref[1], i, 0)),
                  pl.BlockSpec((1, tm, d), lambda i, idx_ref: (0, i, 0)),
                  pl.BlockSpec((1, tm, d), lambda i, idx_ref: (1, i, 0)),
                  pl.BlockSpec((1, tm, d), lambda i, idx_ref: (2, i, 0)),
                  shifted, shifted, shifted],
        out_specs=[flat, flat, flat, flat])
    shp = jax.ShapeDtypeStruct((r, d), F32)
    return pl.pallas_call(
        body, name=name, grid_spec=grid_spec, out_shape=[shp, shp, shp, shp],
        compiler_params=_cp(("parallel",)),
    )(idx, g, recv1, recv2, recv2, recv2, w, m, v)


def _rows_of(v, d):
    flat = v.reshape(-1)
    rows = -(-flat.shape[0] // d)
    rows += (-rows) % SUBLANE
    return jnp.pad(flat, (0, rows * d - flat.shape[0])).reshape(rows, d)


def _pad_rows(a, mult):
    pad = (-a.shape[0]) % mult
    if pad:
        a = jnp.pad(a, ((0, pad),) + ((0, 0),) * (a.ndim - 1))
    return a


_BIG = ("w_in", "w_xkv", "w_up", "w_proj_gdn", "w_proj_sb", "w_out", "w_xq", "w_xo", "w_down")
_COL_SHARDED = ("w_in", "w_xkv", "w_up")
_SMALL_REP = ("norm_mix", "norm_x", "norm_mem", "norm_ffn", "a_log", "dt_bias", "gdn_out_norm", "xq_norm", "xk_norm")
_SMALL_CONV = ("conv_gdn", "conv_ffn")


def _part_rows(shapes):
    out = []
    for n in _BIG:
        rows, cols = shapes[n]
        cnt = cols if n in _COL_SHARDED else rows
        out.append((cnt, cnt + (-cnt) % (CFG.pack_tile if n == _BIG[0] else PACK_ROW_ALIGN)))
    return out


def _pack_big_shards(shards, shapes):
    parts = []
    for n, (_, padded) in zip(_BIG, _part_rows(shapes)):
        parts.append(_pad_rows(shards[n].T if n in _COL_SHARDED else shards[n], padded))
    return _pad_rows(jnp.concatenate(parts, axis=0), CFG.pack_tile)


def _unpack_gathered(gath, shapes, names):
    out, r0 = {}, 0
    for n, (cnt, padded) in zip(_BIG, _part_rows(shapes)):
        if n not in names:
            continue
        out[n] = gath[:, r0:r0 + cnt, :].reshape(N_DEV * cnt, gath.shape[2])
        r0 += padded
    return out


def _pack_full_grads(grads, shapes, names):
    d = CFG.d
    parts = []
    for n, (cnt, padded) in zip(_BIG, _part_rows(shapes)):
        if n not in names:
            continue
        g = grads[n].reshape(N_DEV, cnt, d)
        if padded > cnt:
            g = jnp.pad(g, ((0, 0), (0, padded - cnt), (0, 0)))
        parts.append(g)
    full = jnp.concatenate(parts, axis=1)
    pad = (-full.shape[1]) % CFG.pack_tile
    if pad:
        full = jnp.pad(full, ((0, 0), (0, pad), (0, 0)))
    return full


def _unpack_shard(packed, shapes, names):
    out, r0 = {}, 0
    for n, (cnt, padded) in zip(_BIG, _part_rows(shapes)):
        if n not in names:
            continue
        part = packed[r0:r0 + cnt]
        out[n] = (part.T if n in _COL_SHARDED else part).reshape((1,) + tuple(shapes[n]))
        r0 += padded
    return out


def kernel(x, mem, norm_mix, w_in, conv_gdn, a_log, dt_bias, gdn_out_norm, w_proj_gdn, w_proj_sb, w_out, norm_x, norm_mem, w_xq, w_xkv, xq_norm, xk_norm, w_xo, norm_ffn, w_up, conv_ffn, w_down, loss_target, m_norm_mix, m_w_in, m_conv_gdn, m_a_log, m_dt_bias, m_gdn_out_norm, m_w_proj_gdn, m_w_proj_sb, m_w_out, m_norm_x, m_norm_mem, m_w_xq, m_w_xkv, m_xq_norm, m_xk_norm, m_w_xo, m_norm_ffn, m_w_up, m_conv_ffn, m_w_down, v_norm_mix, v_w_in, v_conv_gdn, v_a_log, v_dt_bias, v_gdn_out_norm, v_w_proj_gdn, v_w_proj_sb, v_w_out, v_norm_x, v_norm_mem, v_w_xq, v_w_xkv, v_xq_norm, v_xk_norm, v_w_xo, v_norm_ffn, v_w_up, v_conv_ffn, v_w_down):
    names = ("norm_mix", "w_in", "conv_gdn", "a_log", "dt_bias", "gdn_out_norm", "w_proj_gdn", "w_proj_sb", "w_out",
             "norm_x", "norm_mem", "w_xq", "w_xkv", "xq_norm", "xk_norm", "w_xo", "norm_ffn", "w_up", "conv_ffn", "w_down")
    wts = dict(zip(names, (norm_mix, w_in, conv_gdn, a_log, dt_bias, gdn_out_norm, w_proj_gdn, w_proj_sb, w_out,
                           norm_x, norm_mem, w_xq, w_xkv, xq_norm, xk_norm, w_xo, norm_ffn, w_up, conv_ffn, w_down)))
    mom = dict(zip(names, (m_norm_mix, m_w_in, m_conv_gdn, m_a_log, m_dt_bias, m_gdn_out_norm, m_w_proj_gdn, m_w_proj_sb,
                           m_w_out, m_norm_x, m_norm_mem, m_w_xq, m_w_xkv, m_xq_norm, m_xk_norm, m_w_xo, m_norm_ffn, m_w_up,
                           m_conv_ffn, m_w_down)))
    vel = dict(zip(names, (v_norm_mix, v_w_in, v_conv_gdn, v_a_log, v_dt_bias, v_gdn_out_norm, v_w_proj_gdn, v_w_proj_sb,
                           v_w_out, v_norm_x, v_norm_mem, v_w_xq, v_w_xkv, v_xq_norm, v_xk_norm, v_w_xo, v_norm_ffn, v_w_up,
                           v_conv_ffn, v_w_down)))
    cfg = CFG
    d, bsz, s = cfg.d, cfg.b, cfg.s
    t = bsz * s
    gh, sbh = cfg.gh, cfg.sbh
    gw, sw = gh * HD, sbh * HD
    nchunk = s // cfg.gch
    mx, my, mc = _my_pos()
    me = 4 * mx + 2 * my + mc

    shard_shapes = {n: tuple(wts[n].shape[1:]) for n in _BIG}

    packed_w = _pack_big_shards({n: wts[n][0] for n in _BIG}, shard_shapes)
    packed_wb = _cast_rows(packed_w, BF16, "cast_weights")
    rows_in = _part_rows(shard_shapes)[0][1]
    full = _unpack_gathered(_all_gather_big(packed_wb[:rows_in], "all_gather_w_in"), shard_shapes, _BIG[:1])

    conv_rows = {n: _rows_of(wts[n][0], d) for n in _SMALL_CONV}
    conv_cnt = {n: conv_rows[n].shape[0] for n in _SMALL_CONV}
    conv_blk = _pad_rows(jnp.concatenate([conv_rows[n] for n in _SMALL_CONV], axis=0), SUBLANE)
    conv_all = jnp.zeros((N_DEV,) + conv_blk.shape, F32)
    conv_all = lax.dynamic_update_slice(conv_all, conv_blk[None], (me, 0, 0))
    conv_all = _all_reduce_small(conv_all.reshape(-1, d), "gather_conv_taps").reshape((N_DEV,) + conv_blk.shape)

    def full_conv(n, r0):
        k, cols = wts[n].shape[1], wts[n].shape[2]
        part = conv_all[:, r0:r0 + conv_cnt[n], :].reshape(N_DEV, -1)[:, :k * cols].reshape(N_DEV, k, cols)
        return part.transpose(1, 0, 2).reshape(k, N_DEV * cols)

    cgdn = full_conv("conv_gdn", 0)
    cffn = full_conv("conv_ffn", conv_cnt["conv_gdn"])

    win = full["w_in"]
    o_ab = 3 * gw
    o_z = o_ab + 2 * gh
    o_sb = o_z + gw
    o_gate = o_sb + 3 * sw
    w_qkv = win[:o_ab]
    w_ab = jnp.concatenate([win[o_ab:o_z], jnp.zeros((LANE - 2 * gh, d), win.dtype)], axis=0)
    w_z = win[o_z:o_sb]
    w_sb = win[o_sb:o_gate]
    w_gate = win[o_gate:]

    alog_p = jnp.concatenate([a_log.reshape(1, -1), jnp.zeros((1, LANE - gh), F32)], axis=1)
    dtb_p = jnp.concatenate([dt_bias.reshape(1, -1), jnp.zeros((1, LANE - gh), F32)], axis=1)
    onorm = gdn_out_norm.reshape(1, HD)

    x2 = x.reshape(t, d)
    tgt2 = loss_target.reshape(t, d)
    mem2 = mem.reshape(bsz * cfg.mem, d)

    (xn,) = _rowwise(_rms, [x2], [norm_mix], [(d, BF16)], [], name="norm_mix_fwd")
    p_qkv = _mm(xn, w_qkv, tb=True, name="proj_qkv")
    p_ab = _mm(xn, w_ab, tb=True, name="proj_ab")
    p_z = _mm(xn, w_z, tb=True, name="proj_z")
    p_sb = _mm(xn, w_sb, tb=True, name="proj_sb")
    p_gate = _mm(xn, w_gate, tb=True, name="proj_gate")

    qkv_c = _gdn_conv_fwd(p_qkv.reshape(bsz, s, 3 * gw), cgdn)
    (gb,) = _rowwise(_gates_fn, [p_ab], [alog_p, dtb_p], [(LANE, F32)], [], name="gdn_gates_fwd")
    gbt = gb.reshape(bsz, s, LANE)[:, :, :2 * gh].transpose(0, 2, 1).reshape(bsz, 2 * gh, nchunk, 1, cfg.gch)
    o_a, states, tinvs = _gdn_fwd(qkv_c, p_z.reshape(bsz, s, gw), gbt, onorm)
    o_b, gathered = _sb_fwd(p_sb.reshape(bsz, s, 3 * sw), packed_wb[rows_in:])
    full.update(_unpack_gathered(gathered, shard_shapes, _BIG[1:]))

    pa = _mm(o_a.reshape(t, gw), full["w_proj_gdn"], name="proj_gdn_out")
    pb = _mm(o_b.reshape(t, sw), full["w_proj_sb"], name="proj_sb_out")

    def merge_fn(pa_, pb_, gate_):
        return _sigmoid(gate_[:, :d]) * pa_ + _sigmoid(gate_[:, d:]) * pb_

    (merged,) = _rowwise(merge_fn, [pa, pb, p_gate], [], [(d, BF16)], [], name="merge_fwd")
    def with_norm(r, g):
        return r, _rms(r, g)

    h1, hn_x = _mm(merged, full["w_out"], add=x2, name="mixer_out", epilogue=(with_norm, [], [norm_x], [F32, BF16], 0, True))
    (mn,) = _rowwise(_rms, [mem2], [norm_mem], [(d, BF16)], [], name="norm_mem_fwd")
    q_raw = _mm(hn_x, full["w_xq"], name="xattn_q")
    kv = _mm(mn, full["w_xkv"], tb=True, name="xattn_kv")
    xo = _xattn_fwd(q_raw.reshape(bsz, s, d), kv.reshape(bsz, cfg.mem, 2 * d), xq_norm, xk_norm)
    h2, hn_f = _mm(xo.reshape(t, d), full["w_xo"], add=h1, name="xattn_out",
                   epilogue=(with_norm, [], [norm_ffn], [F32, BF16], 0, True))
    up = _mm(hn_f, full["w_up"], tb=True, name="ffn_up")
    act = _ffn_conv_fwd(up.reshape(bsz, s, 2 * cfg.dff), cffn)
    def loss_fn(y_, tg_):
        err = y_ - tg_
        part = 0.5 * jnp.sum(err * err) / d
        return err / d, err / d, jnp.full((1, LANE), part, F32)

    dy, dy_b, loss_part = _mm(act.reshape(t, cfg.dff), full["w_down"], add=h2, name="ffn_down_loss",
                              epilogue=(loss_fn, [tgt2], [], [F32, BF16], 1, False))

    grads = {}
    dact = _mm(dy_b, full["w_down"], tb=True, name="d_act")
    grads["w_down"] = _mm(act.reshape(t, cfg.dff), dy_b, ta=True, name="dw_down")
    dup1, dup2, dcf1, dcf2 = _ffn_conv_bwd(up.reshape(bsz, s, 2 * cfg.dff), cffn, dact.reshape(bsz, s, cfg.dff))
    dup = jnp.concatenate([dup1, dup2], axis=2).reshape(t, 2 * cfg.dff)
    g_conv_ffn = jnp.concatenate([dcf1, dcf2], axis=1)
    dhn_f = _mm(dup, full["w_up"], name="d_hn_ffn")
    grads["w_up"] = _mm(dup, hn_f, ta=True, name="dw_up")

    def norm_bwd_fn(h_, res_, dn_, g_):
        _, f = jax.vjp(_rms, h_, g_)
        dh, dg = f(dn_)
        return res_ + dh, dg

    def norm_bwd_copy_fn(h_, res_, dn_, g_):
        dres, dg = norm_bwd_fn(h_, res_, dn_, g_)
        return dres, dres, dg

    dh2, dh2_b, g_norm_ffn = _rowwise(norm_bwd_copy_fn, [h2, dy, dhn_f], [norm_ffn], [(d, F32), (d, BF16)], [(1, d)],
                                      name="norm_ffn_bwd")

    dxo = _mm(dh2_b, full["w_xo"], tb=True, out_dtype=BF16, name="d_xo")
    grads["w_xo"] = _mm(xo.reshape(t, d), dh2_b, ta=True, name="dw_xo")
    dq_raw, dkv, g_xq_norm, g_xk_norm = _xattn_bwd(q_raw.reshape(bsz, s, d), kv.reshape(bsz, cfg.mem, 2 * d),
                                                   xq_norm, xk_norm, dxo.reshape(bsz, s, d))
    dq_raw2 = dq_raw.reshape(t, d)
    dkv2 = dkv.reshape(bsz * cfg.mem, 2 * d)
    dhn_x = _mm(dq_raw2, full["w_xq"], tb=True, name="d_hn_x")
    grads["w_xq"] = _mm(hn_x, dq_raw2, ta=True, name="dw_xq")
    dmn = _mm(dkv2, full["w_xkv"], name="d_mn")
    grads["w_xkv"] = _mm(dkv2, mn, ta=True, name="dw_xkv")

    def norm_w_bwd_fn(h_, dn_, g_):
        _, f = jax.vjp(lambda gg: _rms(h_, gg), g_)
        return f(dn_)[0]

    (g_norm_mem,) = _rowwise(norm_w_bwd_fn, [mem2, dmn], [norm_mem], [], [(1, d)], name="norm_mem_bwd")
    dh1, dh1_b, g_norm_x = _rowwise(norm_bwd_copy_fn, [h1, dh2, dhn_x], [norm_x], [(d, F32), (d, BF16)], [(1, d)],
                                    name="norm_x_bwd")

    dmerged = _mm(dh1_b, full["w_out"], tb=True, name="d_merged")
    grads["w_out"] = _mm(merged, dh1_b, ta=True, name="dw_out")

    def merge_bwd_fn(pa_, pb_, gate_, dm_):
        _, f = jax.vjp(merge_fn, pa_, pb_, gate_)
        return f(dm_)

    dpa, dpb, dgate = _rowwise(merge_bwd_fn, [pa, pb, p_gate, dmerged], [], [(d, BF16), (d, BF16), (2 * d, BF16)], [],
                               name="merge_bwd")
    do_a = _mm(dpa, full["w_proj_gdn"], tb=True, name="d_o_gdn")
    grads["w_proj_gdn"] = _mm(o_a.reshape(t, gw), dpa, ta=True, name="dw_proj_gdn")
    do_b = _mm(dpb, full["w_proj_sb"], tb=True, name="d_o_sb")
    grads["w_proj_sb"] = _mm(o_b.reshape(t, sw), dpb, ta=True, name="dw_proj_sb")

    c_idx = jnp.reshape(mc, (1,)).astype(jnp.int32)
    early = _BIG[1:]
    g_early = _pack_full_grads(grads, shard_shapes, early)
    r1_early = _exchange_sibling(g_early, "grads_to_sibling_early")
    s1_early = _sum_sibling(g_early, r1_early, c_idx, "sum_sibling_early")
    dsq, dsk, dsv, r2_early = _sb_bwd(p_sb.reshape(bsz, s, 3 * sw), do_b.reshape(bsz, s, sw), s1_early)
    dp_sb = [a.reshape(t, sw) for a in (dsq, dsk, dsv)]
    w_sb3 = [w_sb[i * sw:(i + 1) * sw] for i in range(3)]

    dqkv_c, dz, dg, dbeta, g_onorm = _gdn_bwd(qkv_c, p_z.reshape(bsz, s, gw), gbt, onorm, states, tinvs,
                                              do_a.reshape(bsz, s, gw))
    dgb = jnp.concatenate([dg, dbeta], axis=1).reshape(bsz, 2 * gh, s).transpose(0, 2, 1)
    dgb = jnp.concatenate([dgb, jnp.zeros((bsz, s, LANE - 2 * gh), F32)], axis=2).reshape(t, LANE)

    def gates_bwd_fn(ab_, dgb_, alog_, dtb_):
        _, f = jax.vjp(_gates_fn, ab_, alog_, dtb_)
        return f(dgb_)

    dp_ab, g_alog, g_dtb = _rowwise(gates_bwd_fn, [p_ab, dgb], [alog_p, dtb_p], [(LANE, BF16)], [(1, LANE), (1, LANE)],
                                    name="gdn_gates_bwd")
    dp_qkv, g_conv_gdn = _gdn_conv_bwd(p_qkv.reshape(bsz, s, 3 * gw), cgdn, dqkv_c)
    dp_qkv = dp_qkv.reshape(t, 3 * gw)
    dp_z = dz.reshape(t, gw)

    grads["w_in"] = jnp.concatenate([
        _mm(dp_qkv, xn, ta=True, name="dw_in_qkv"),
        _mm(dp_ab, xn, ta=True, name="dw_in_ab")[:2 * gh],
        _mm(dp_z, xn, ta=True, name="dw_in_z"),
        _mm(dp_sb[0], xn, ta=True, name="dw_in_sbq"),
        _mm(dp_sb[1], xn, ta=True, name="dw_in_sbk"),
        _mm(dp_sb[2], xn, ta=True, name="dw_in_sbv"),
        _mm(dgate, xn, ta=True, name="dw_in_gate")], axis=0)
    g_late = _pack_full_grads(grads, shard_shapes, _BIG[:1])
    r1_late = _exchange_sibling(g_late, "grads_to_sibling_late")
    s1_late = _sum_sibling(g_late, r1_late, c_idx, "sum_sibling_late")
    dxn, r2_late = _dxn_fused([(dp_qkv, w_qkv), (dp_ab, w_ab), (dp_z, w_z), *zip(dp_sb, w_sb3), (dgate, w_gate)], s1_late)
    grad_x, g_norm_mix = _rowwise(norm_bwd_fn, [x2, dh1, dxn], [norm_mix], [(d, F32)], [(1, d)], name="norm_mix_bwd")

    small_g = {"norm_mix": g_norm_mix, "norm_x": g_norm_x, "norm_mem": g_norm_mem, "norm_ffn": g_norm_ffn,
               "a_log": g_alog[:, :gh], "dt_bias": g_dtb[:, :gh], "gdn_out_norm": g_onorm,
               "xq_norm": g_xq_norm, "xk_norm": g_xk_norm}
    sm_rows = [_rows_of(small_g[n], d) for n in _SMALL_REP] + [_rows_of(loss_part, d)]
    sm_rows += [_rows_of(g_conv_gdn, d), _rows_of(g_conv_ffn, d)]
    sm_cnt = [r.shape[0] for r in sm_rows]
    sm_sum = _all_reduce_small(_pad_rows(jnp.concatenate(sm_rows, axis=0), SUBLANE), "all_reduce_small_grads")
    offs = [0]
    for cnt in sm_cnt:
        offs.append(offs[-1] + cnt)
    small_grad = {}
    for i, n in enumerate(_SMALL_REP):
        small_grad[n] = sm_sum[offs[i]:offs[i + 1]].reshape(-1)[:wts[n].size].reshape(wts[n].shape)
    loss = sm_sum[offs[len(_SMALL_REP)], 0]
    for i, n in enumerate(_SMALL_CONV):
        k, cols = wts[n].shape[1], wts[n].shape[2]
        o = offs[len(_SMALL_REP) + 1 + i]
        fullg = sm_sum[o:o + sm_cnt[len(_SMALL_REP) + 1 + i]].reshape(-1)[:k * cols * N_DEV].reshape(k, N_DEV * cols)
        small_grad[n] = lax.dynamic_slice(fullg, (0, me * cols), (k, cols)).reshape(wts[n].shape)

    small_names = _SMALL_REP + _SMALL_CONV

    def pack_small(src):
        return _pad_rows(jnp.concatenate([_rows_of(src[n], d) for n in small_names], axis=0), SUBLANE)

    sw_, sg_, sm_, sv_ = pack_small(wts), pack_small(small_grad), pack_small(mom), pack_small(vel)
    sd_, snm_, snv_ = _rowwise(_adamw_math, [sw_, sg_, sm_, sv_], [], [(d, F32)] * 3, [], name="adamw_small", tm=sw_.shape[0])

    def unpack_small(packed):
        out, r0 = {}, 0
        for n in small_names:
            cnt = _rows_of(wts[n], d).shape[0]
            out[n] = packed[r0:r0 + cnt].reshape(-1)[:wts[n].size].reshape(wts[n].shape)
            r0 += cnt
        return out

    small_delta, small_m, small_v = unpack_small(sd_), unpack_small(snm_), unpack_small(snv_)

    idx = jnp.stack([me, 2 * mx + my]).astype(jnp.int32)
    pm = _pack_big_shards({n: mom[n][0] for n in _BIG}, shard_shapes)
    pv = _pack_big_shards({n: vel[n][0] for n in _BIG}, shard_shapes)
    upd_late = _adamw_big(g_late, r1_late, r2_late, packed_w, pm, pv, idx, 0, "adamw_late")
    upd_early = _adamw_big(g_early, r1_early, r2_early, packed_w, pm, pv, idx, g_late.shape[1], "adamw_early")
    big_grad, big_delta, big_m, big_v = (
        {**_unpack_shard(a, shard_shapes, _BIG[:1]), **_unpack_shard(b, shard_shapes, early)}
        for a, b in zip(upd_late, upd_early))

    def pick(big, small, n):
        return big[n] if n in big else small[n]

    outs = [loss, grad_x.reshape(bsz, s, d)]
    outs += [pick(big_grad, small_grad, n) for n in names]
    outs += [pick(big_delta, small_delta, n) for n in names]
    outs += [pick(big_m, small_m, n) for n in names]
    outs += [pick(big_v, small_v, n) for n in names]
    return tuple(outs)
```
